```python
import math, functools
import jax, jax.numpy as jnp
from jax import lax
import numpy as np

D_MODEL = 2048
BATCH = 32
SEQ = 256
DEPTH = 2
DEC_BATCH = 2
DEC_SEQ = 2048
PAST_LEN = 512

GRID_W = 64
N_HEADS = D_MODEL // 128
HEAD_DIM = 64
ATTN_W = N_HEADS * HEAD_DIM
SSM_W = D_MODEL // 4
SSM_GROUP = 16
SSM_GROUPS = SSM_W // SSM_GROUP
SSM_STATE = 64
POOL_W = D_MODEL // 4
POOL_WINDOWS = (2, 4, 8, 16)
POOL_GROUP = POOL_W // len(POOL_WINDOWS)
IN_W = 3 * ATTN_W + SSM_W + POOL_W
SPLITS = (ATTN_W, 2 * ATTN_W, 3 * ATTN_W, 3 * ATTN_W + SSM_W)
WIN_ROWS_MAX = 8
WIN_COLS = 16
Q_BLOCK = 128
FFN_HIDDEN = ((8 * D_MODEL + 3 * 256 - 1) // (3 * 256)) * 256
RMS_EPS = 1e-6
NEG_INF = -1e30

kernel_name = 'hybrid_s5_natten_pool_diffusion_step'


def rms_norm(x, g):
    xf = x.astype(jnp.float32)
    y = xf * lax.rsqrt(jnp.mean(xf * xf, axis=-1, keepdims=True) + RMS_EPS)
    return (y * g.astype(jnp.float32)).astype(x.dtype)


def modulation(cond, w_mod, b_mod):
    m = jax.nn.silu(cond) @ w_mod + b_mod
    return [t[:, None, :] for t in jnp.split(m, 6, axis=-1)]


def to_heads(t):
    b, n, _ = t.shape
    return t.reshape(b, n, N_HEADS, HEAD_DIM).transpose(0, 2, 1, 3)


def from_heads(t):
    b, h, n, d = t.shape
    return t.transpose(0, 2, 1, 3).reshape(b, n, h * d)


def context_attention(q, k, v):
    b, h, n, d = q.shape
    qb = q.reshape(b, h, n // Q_BLOCK, Q_BLOCK, d).transpose(2, 0, 1, 3, 4)

    def block(qi):
        s = jnp.einsum('bhqd,bhkd->bhqk', qi, k).astype(jnp.float32) * HEAD_DIM ** -0.5
        p = jax.nn.softmax(s, axis=-1).astype(v.dtype)
        return jnp.einsum('bhqk,bhkd->bhqd', p, v)

    o = lax.map(block, qb)
    return o.transpose(1, 2, 0, 3, 4).reshape(b, h, n, d)


def latent_attention(q, k, v, ctx_k, ctx_v, rpb):
    f32 = jnp.float32
    b, h, n, d = q.shape
    rows = n // GRID_W
    wr = min(WIN_ROWS_MAX, rows)
    scale = HEAD_DIM ** -0.5
    qg = q.reshape(b, h, rows, GRID_W, d).transpose(2, 0, 1, 3, 4)
    kg = k.reshape(b, h, rows, GRID_W, d)
    vg = v.reshape(b, h, rows, GRID_W, d)
    cols = jnp.arange(GRID_W)
    col_start = jnp.clip(cols - WIN_COLS // 2, 0, GRID_W - WIN_COLS)
    col_mask = (cols[None, :] >= col_start[:, None]) & (cols[None, :] < col_start[:, None] + WIN_COLS)
    dc_idx = jnp.clip(cols[None, :] - cols[:, None], 1 - WIN_COLS, WIN_COLS - 1) + WIN_COLS - 1

    def row(args):
        r, qr = args
        rs = jnp.clip(r - wr // 2, 0, rows - wr)
        kb = lax.dynamic_slice_in_dim(kg, rs, wr, axis=2)
        vb = lax.dynamic_slice_in_dim(vg, rs, wr, axis=2)
        dr_idx = rs + jnp.arange(wr) - r + WIN_ROWS_MAX - 1
        bias = rpb[:, dr_idx[None, :, None], dc_idx[:, None, :]].astype(f32)
        s_loc = jnp.einsum('bhqd,bhrkd->bhqrk', qr, kb).astype(f32) * scale + bias
        s_loc = jnp.where(col_mask[:, None, :], s_loc, NEG_INF).reshape(b, h, GRID_W, wr * GRID_W)
        s_ctx = jnp.einsum('bhqd,bhld->bhql', qr, ctx_k).astype(f32) * scale
        p = jax.nn.softmax(jnp.concatenate([s_loc, s_ctx], axis=-1), axis=-1).astype(v.dtype)
        p_loc = p[..., :wr * GRID_W].reshape(b, h, GRID_W, wr, GRID_W)
        return (jnp.einsum('bhqrk,bhrkd->bhqd', p_loc, vb)
                + jnp.einsum('bhql,bhld->bhqd', p[..., wr * GRID_W:], ctx_v))

    o = lax.map(row, (jnp.arange(rows), qg))
    return o.transpose(1, 2, 0, 3, 4).reshape(b, h, n, d)


def _scan_combine(e1, e2):
    a1, b1 = e1
    a2, b2 = e2
    return a1 * a2, a2 * b1 + b2


def diag_scan(a_bar, bu, h0, reverse):
    a = jnp.broadcast_to(a_bar, bu.shape)
    a_cum, h = lax.associative_scan(_scan_combine, (a, bu), reverse=reverse, axis=1)
    return h + a_cum * h0[:, None]


def ssm_mixer(u, h0, a_re, a_im, log_dt, b_re, b_im, c_re, c_im, d_skip, w_glu, b_glu):
    f32 = jnp.float32
    bsz, n, _ = u.shape
    lam = lax.complex(a_re.astype(f32), a_im.astype(f32))
    dt = jnp.exp(log_dt.astype(f32))[..., None]
    a_bar = jnp.exp(lam * dt)
    b_bar = ((a_bar - 1.0) / lam)[..., None] * lax.complex(b_re.astype(f32), b_im.astype(f32))
    c_mat = lax.complex(c_re.astype(f32), c_im.astype(f32))
    uf = u.astype(f32)
    ug = uf.reshape(bsz, n, SSM_GROUPS, SSM_GROUP)
    y = d_skip.astype(f32) * uf
    finals = []
    for direction, rev in ((0, False), (1, True)):
        bu = jnp.einsum('btgm,gpm->btgp', ug, b_bar[direction])
        h = diag_scan(a_bar[direction], bu, h0[:, direction], rev)
        y = y + jnp.real(jnp.einsum('btgp,gmp->btgm', h, c_mat[direction])).reshape(bsz, n, SSM_W)
        finals.append(h[:, 0] if rev else h[:, -1])
    z = jax.nn.gelu(y).astype(u.dtype) @ w_glu + b_glu
    z_val, z_gate = jnp.split(z, 2, axis=-1)
    return z_val * jax.nn.sigmoid(z_gate), jnp.stack(finals, axis=1)


def pool_mixer(p, w_pool, pool_scale):
    f32 = jnp.float32
    bsz, n, _ = p.shape
    ng = len(POOL_WINDOWS)
    pf = p.astype(f32).reshape(bsz, n, ng, POOL_GROUP)
    csum = jnp.concatenate([jnp.zeros((bsz, 1, ng, POOL_GROUP), f32), jnp.cumsum(pf, axis=1)], axis=1)
    t = jnp.arange(n)[:, None]
    win = jnp.array(POOL_WINDOWS, dtype=jnp.int32)[None, :]
    lo = jnp.clip(t - win // 2, 0, n)
    hi = jnp.clip(t - win // 2 + win, 0, n)
    g_idx = jnp.arange(ng)[None, :]
    sums = csum[:, hi, g_idx] - csum[:, lo, g_idx]
    mixed = sums / (hi - lo).astype(f32)[None, :, :, None] - pf
    out = jnp.einsum('btgc,gcd->btgd', mixed, w_pool.astype(f32)).reshape(bsz, n, POOL_W)
    return (out * pool_scale.astype(f32)).astype(p.dtype)


def swiglu(h, w_in, w_out):
    g, u = jnp.split(h @ w_in, 2, axis=-1)
    return (jax.nn.silu(g) * u) @ w_out


def trunk_layer(x, cond, lp, attend, h0):
    sh1, sc1, g1, sh2, sc2, g2 = modulation(cond, lp['w_mod'], lp['b_mod'])
    h = rms_norm(x, lp['norm1_g']) * (1 + sc1) + sh1
    q, k, v, u, p = jnp.split(h @ lp['w_in'], SPLITS, axis=-1)
    q, k, v = to_heads(q), to_heads(k), to_heads(v)
    a_out = from_heads(attend(q, k, v))
    s_out, s_final = ssm_mixer(u, h0, lp['ssm_a_re'], lp['ssm_a_im'], lp['ssm_log_dt'],
                               lp['ssm_b_re'], lp['ssm_b_im'], lp['ssm_c_re'], lp['ssm_c_im'],
                               lp['ssm_d'], lp['ssm_w_glu'], lp['ssm_b_glu'])
    p_out = pool_mixer(p, lp['pool_w'], lp['pool_scale'])
    mixed = jnp.concatenate([a_out, s_out, p_out], axis=-1) @ lp['w_out']
    x = x + g1 * mixed
    h2 = rms_norm(x, lp['norm2_g']) * (1 + sc2) + sh2
    x = x + g2 * swiglu(h2, lp['ffn_w_in'], lp['ffn_w_out'])
    return x, k, v, s_final


def setup_inputs(seed: int = 0) -> dict:
    key = jax.random.key(seed)
    ks = jax.random.split(key, 32)
    f32 = jnp.float32
    L, D, G, P, M = DEPTH, D_MODEL, SSM_GROUPS, SSM_STATE, SSM_GROUP

    def nrm(k, shape, scale):
        return jax.random.normal(k, shape, f32) * scale

    n_idx = jnp.arange(P, dtype=f32)
    return {
        'x_prompt': nrm(ks[0], (BATCH, SEQ, D), 1.0),
        'x_sample': nrm(ks[1], (DEC_BATCH, DEC_SEQ, D), 1.0),
        'c': nrm(ks[2], (DEC_BATCH, D), 1.0),
        'cache_k': nrm(ks[3], (DEC_BATCH, L, N_HEADS, PAST_LEN, HEAD_DIM), 1.0),
        'cache_v': nrm(ks[4], (DEC_BATCH, L, N_HEADS, PAST_LEN, HEAD_DIM), 1.0),
        'state_ssm': nrm(ks[5], (DEC_BATCH, L, 2, G, P, 2), 0.5),
        'c_ctx': nrm(ks[6], (D,), 1.0),
        'w_mod': nrm(ks[7], (L, D, 6 * D), 0.5 * D ** -0.5),
        'b_mod': nrm(ks[8], (L, 6 * D), 0.01),
        'norm1_g': 1.0 + nrm(ks[9], (L, D), 0.1),
        'norm2_g': 1.0 + nrm(ks[10], (L, D), 0.1),
        'w_in': nrm(ks[11], (L, D, IN_W), D ** -0.5),
        'attn_rpb': nrm(ks[12], (L, N_HEADS, 2 * WIN_ROWS_MAX - 1, 2 * WIN_COLS - 1), 0.1),
        'ssm_a_re': -0.5 + nrm(ks[13], (L, 2, G, P), 0.01),
        'ssm_a_im': math.pi * n_idx + nrm(ks[14], (L, 2, G, P), 0.01),
        'ssm_log_dt': jax.random.uniform(ks[15], (L, 2, G), f32, math.log(1e-3), math.log(1e-1)),
        'ssm_b_re': nrm(ks[16], (L, 2, G, P, M), (2 * M) ** -0.5),
        'ssm_b_im': nrm(ks[17], (L, 2, G, P, M), (2 * M) ** -0.5),
        'ssm_c_re': nrm(ks[18], (L, 2, G, M, P), (2 * P) ** -0.5),
        'ssm_c_im': nrm(ks[19], (L, 2, G, M, P), (2 * P) ** -0.5),
        'ssm_d': nrm(ks[20], (L, SSM_W), 1.0),
        'ssm_w_glu': nrm(ks[21], (L, SSM_W, 2 * SSM_W), SSM_W ** -0.5),
        'ssm_b_glu': nrm(ks[22], (L, 2 * SSM_W), 0.01),
        'pool_w': nrm(ks[23], (L, len(POOL_WINDOWS), POOL_GROUP, POOL_GROUP), POOL_GROUP ** -0.5),
        'pool_scale': 1.0 + nrm(ks[24], (L, POOL_W), 0.1),
        'w_out': nrm(ks[25], (L, D, D), D ** -0.5),
        'ffn_w_in': nrm(ks[26], (L, D, 2 * FFN_HIDDEN), D ** -0.5),
        'ffn_w_out': nrm(ks[27], (L, FFN_HIDDEN, D), FFN_HIDDEN ** -0.5),
        'final_norm_g': 1.0 + nrm(ks[28], (D,), 0.1),
    }


def reference(x_prompt, x_sample, c, cache_k, cache_v, state_ssm, c_ctx, w_mod, b_mod,
              norm1_g, norm2_g, w_in, attn_rpb, ssm_a_re, ssm_a_im, ssm_log_dt,
              ssm_b_re, ssm_b_im, ssm_c_re, ssm_c_im, ssm_d, ssm_w_glu, ssm_b_glu,
              pool_w, pool_scale, w_out, ffn_w_in, ffn_w_out, final_norm_g):
    f32 = jnp.float32

    def layer_params(l):
        return {
            'w_mod': w_mod[l], 'b_mod': b_mod[l], 'norm1_g': norm1_g[l], 'norm2_g': norm2_g[l],
            'w_in': w_in[l], 'ssm_a_re': ssm_a_re[l], 'ssm_a_im': ssm_a_im[l],
            'ssm_log_dt': ssm_log_dt[l], 'ssm_b_re': ssm_b_re[l], 'ssm_b_im': ssm_b_im[l],
            'ssm_c_re': ssm_c_re[l], 'ssm_c_im': ssm_c_im[l], 'ssm_d': ssm_d[l],
            'ssm_w_glu': ssm_w_glu[l], 'ssm_b_glu': ssm_b_glu[l], 'pool_w': pool_w[l],
            'pool_scale': pool_scale[l], 'w_out': w_out[l], 'ffn_w_in': ffn_w_in[l],
            'ffn_w_out': ffn_w_out[l],
        }

    xp = x_prompt
    cond_ctx = c_ctx[None, :]
    h0_ctx = jnp.zeros((x_prompt.shape[0], 2, SSM_GROUPS, SSM_STATE), jnp.complex64)
    ks_out, vs_out, st_out = [], [], []
    for l in range(DEPTH):
        xp, k_l, v_l, fin = trunk_layer(xp, cond_ctx, layer_params(l), context_attention, h0_ctx)
        ks_out.append(k_l)
        vs_out.append(v_l)
        st_out.append(jnp.stack([jnp.real(fin), jnp.imag(fin)], axis=-1))
    y_prompt = rms_norm(xp, final_norm_g)
    new_cache_k = jnp.stack(ks_out, axis=1)
    new_cache_v = jnp.stack(vs_out, axis=1)
    new_state_ssm = jnp.stack(st_out, axis=1)

    xs = x_sample
    for l in range(DEPTH):
        st = state_ssm[:, l]
        h0 = lax.complex(st[..., 0].astype(f32), st[..., 1].astype(f32))
        attend = functools.partial(latent_attention, ctx_k=cache_k[:, l], ctx_v=cache_v[:, l], rpb=attn_rpb[l])
        xs, _, _, _ = trunk_layer(xs, c, layer_params(l), attend, h0)
    y_sample = rms_norm(xs, final_norm_g)

    return (y_prompt, y_sample, new_cache_k, new_cache_v, new_state_ssm)
```

```python
import functools
import math

import jax
import jax.numpy as jnp
from jax import lax
from jax.experimental import pallas as pl
from jax.experimental.pallas import tpu as pltpu

f32 = jnp.float32
bf16 = jnp.bfloat16

D_MODEL = 2048
N_HEADS = 16
HEAD_DIM = 64
ATTN_W = N_HEADS * HEAD_DIM
SSM_W = 512
SSM_GROUP = 16
SSM_GROUPS = 32
SSM_STATE = 64
POOL_W = 512
POOL_WINDOWS = (2, 4, 8, 16)
POOL_GROUP = 128
POOL_PAD = 16
IN_W = 3 * ATTN_W + SSM_W + POOL_W
GRID_W = 64
WIN_ROWS_MAX = 8
WIN_COLS = 16
RMS_EPS = 1e-6
NEG_INF = -1e30

SSM_CHUNK = 16
SSM_PAIRS = SSM_GROUPS // 2
SSM_SEQ_ALIGN = 8

VMEM_LIMIT = 56 * 1024 * 1024


def _cparams(sem):
    return pltpu.CompilerParams(dimension_semantics=sem, vmem_limit_bytes=VMEM_LIMIT)


def _rms(x, g):
    return x * lax.rsqrt(jnp.mean(x * x, axis=-1, keepdims=True) + RMS_EPS) * g


def _mod_kernel(c_ref, w_ref, b_ref, o_ref):
    c = c_ref[...]
    s = (c * jax.nn.sigmoid(c)).astype(bf16)
    o_ref[0] = jnp.dot(s, w_ref[0].astype(bf16), preferred_element_type=f32) + b_ref[0]


def modulation_all(cond, w_mod, b_mod, tn=1024):
    n_layers, d, n = w_mod.shape
    return pl.pallas_call(
        _mod_kernel,
        grid=(n_layers, n // tn),
        in_specs=[
            pl.BlockSpec((8, d), lambda l, j: (0, 0)),
            pl.BlockSpec((1, d, tn), lambda l, j: (l, 0, j)),
            pl.BlockSpec((1, 1, tn), lambda l, j: (l, 0, j)),
        ],
        out_specs=pl.BlockSpec((1, 8, tn), lambda l, j: (l, 0, j)),
        out_shape=jax.ShapeDtypeStruct((n_layers, 8, n), f32),
        compiler_params=_cparams(("parallel", "parallel")),
        name="modulation",
    )(cond, w_mod, b_mod.reshape(n_layers, 1, n))


def _in_kernel(x_ref, mod_ref, g_ref, w_ref, o_ref, h_ref):
    @pl.when(pl.program_id(1) == 0)
    def _():
        m = mod_ref[0]
        y = _rms(x_ref[...], g_ref[...])
        h_ref[...] = (y * (1.0 + m[1:2]) + m[0:1]).astype(bf16)

    o_ref[...] = jnp.dot(h_ref[...], w_ref[...], preferred_element_type=f32)


def in_projection(x, mod, g, w, row_of_tile, tm, tn=512):
    m_total, d = x.shape
    n = w.shape[1]
    return pl.pallas_call(
        _in_kernel,
        grid=(m_total // tm, n // tn),
        in_specs=[
            pl.BlockSpec((tm, d), lambda i, j: (i, 0)),
            pl.BlockSpec((1, 6, d), lambda i, j: (row_of_tile(i), 0, 0)),
            pl.BlockSpec((1, d), lambda i, j: (0, 0)),
            pl.BlockSpec((d, tn), lambda i, j: (0, j)),
        ],
        out_specs=pl.BlockSpec((tm, tn), lambda i, j: (i, j)),
        out_shape=jax.ShapeDtypeStruct((m_total, n), f32),
        scratch_shapes=[pltpu.VMEM((tm, d), bf16)],
        compiler_params=_cparams(("parallel", "arbitrary")),
        name="in_projection",
    )(x, mod, g, w)


def _attn_ctx_kernel(q_ref, k_ref, v_ref, o_ref, kc_ref, vc_ref):
    scale = HEAD_DIM ** -0.5
    outs = []
    for h in range(2):
        sl = slice(h * HEAD_DIM, (h + 1) * HEAD_DIM)
        q = q_ref[:, sl]
        k = k_ref[:, sl]
        v = v_ref[:, sl]
        kc_ref[0, h] = k
        vc_ref[0, h] = v
        s = lax.dot_general(q.astype(bf16), k.astype(bf16), (((1,), (1,)), ((), ())),
                            preferred_element_type=f32) * scale
        e = jnp.exp(s - jnp.max(s, axis=-1, keepdims=True))
        o = jnp.dot(e.astype(bf16), v.astype(bf16), preferred_element_type=f32)
        outs.append(o / jnp.sum(e, axis=-1, keepdims=True))
    o_ref[...] = jnp.concatenate(outs, axis=-1).astype(bf16)


def context_attention(z, batch, seq):
    lanes = 2 * HEAD_DIM
    n_pairs = N_HEADS // 2
    col = lambda off: pl.BlockSpec((seq, lanes), lambda b, p: (b, off + p))
    cache = pl.BlockSpec((1, 2, seq, HEAD_DIM), lambda b, p: (b, p, 0, 0))
    cache_shape = jax.ShapeDtypeStruct((batch, N_HEADS, seq, HEAD_DIM), f32)
    return pl.pallas_call(
        _attn_ctx_kernel,
        grid=(batch, n_pairs),
        in_specs=[col(0), col(n_pairs), col(2 * n_pairs)],
        out_specs=[pl.BlockSpec((seq, lanes), lambda b, p: (b, p)), cache, cache],
        out_shape=[jax.ShapeDtypeStruct((batch * seq, ATTN_W), bf16), cache_shape, cache_shape],
        compiler_params=_cparams(("parallel", "parallel")),
        name="context_attention",
    )(z, z, z)


def _attn_lat_kernel(q_ref, k_ref, v_ref, ck_ref, cv_ref, bias_ref, o_ref, *, rows, wr):
    scale = HEAD_DIM ** -0.5
    n_loc = wr * GRID_W

    def row(r, carry):
        rs = jnp.clip(r - wr // 2, 0, rows - wr)
        q2 = q_ref[pl.ds(pl.multiple_of(r * GRID_W, GRID_W), GRID_W), :]
        k2 = k_ref[pl.ds(pl.multiple_of(rs * GRID_W, GRID_W), n_loc), :]
        v2 = v_ref[pl.ds(pl.multiple_of(rs * GRID_W, GRID_W), n_loc), :]
        outs = []
        for h in range(2):
            sl = slice(h * HEAD_DIM, (h + 1) * HEAD_DIM)
            q = q2[:, sl].astype(bf16)
            s_loc = lax.dot_general(q, k2[:, sl].astype(bf16), (((1,), (1,)), ((), ())),
                                    preferred_element_type=f32) * scale + bias_ref[h, r - rs]
            s_ctx = lax.dot_general(q, ck_ref[0, 0, h].astype(bf16), (((1,), (1,)), ((), ())),
                                    preferred_element_type=f32) * scale
            m = jnp.maximum(jnp.max(s_loc, axis=-1, keepdims=True), jnp.max(s_ctx, axis=-1, keepdims=True))
            e_loc = jnp.exp(s_loc - m)
            e_ctx = jnp.exp(s_ctx - m)
            den = jnp.sum(e_loc, axis=-1, keepdims=True) + jnp.sum(e_ctx, axis=-1, keepdims=True)
            o = (jnp.dot(e_loc.astype(bf16), v2[:, sl].astype(bf16), preferred_element_type=f32)
                 + jnp.dot(e_ctx.astype(bf16), cv_ref[0, 0, h].astype(bf16), preferred_element_type=f32))
            outs.append(o / den)
        o_ref[pl.ds(pl.multiple_of(r * GRID_W, GRID_W), GRID_W), :] = jnp.concatenate(outs, axis=-1).astype(bf16)
        return carry

    lax.fori_loop(0, rows, row, 0)


def latent_bias_table(rpb, rows):
    wr = min(WIN_ROWS_MAX, rows)
    cols = jnp.arange(GRID_W)
    col_start = jnp.clip(cols - WIN_COLS // 2, 0, GRID_W - WIN_COLS)
    col_mask = (cols[None, :] >= col_start[:, None]) & (cols[None, :] < col_start[:, None] + WIN_COLS)
    dc_idx = jnp.clip(cols[None, :] - cols[:, None], 1 - WIN_COLS, WIN_COLS - 1) + WIN_COLS - 1
    d = jnp.arange(wr)
    j = jnp.arange(wr)
    dr_idx = j[None, :] - d[:, None] + WIN_ROWS_MAX - 1
    bias = rpb[:, dr_idx[:, None, :, None], dc_idx[None, :, None, :]]
    bias = jnp.where(col_mask[None, None, :, None, :], bias.astype(f32), NEG_INF)
    return bias.reshape(rpb.shape[0], wr, GRID_W, wr * GRID_W)


def latent_attention(z, row0, batch, seq, cache_k, cache_v, layer, bias):
    lanes = 2 * HEAD_DIM
    n_pairs = N_HEADS // 2
    rows = seq // GRID_W
    wr = min(WIN_ROWS_MAX, rows)
    past = cache_k.shape[3]
    blk0 = row0 // seq
    col = lambda off: pl.BlockSpec((seq, lanes), lambda b, p: (blk0 + b, off + p))
    ctx = pl.BlockSpec((1, 1, 2, past, HEAD_DIM), lambda b, p: (b, layer, p, 0, 0))
    return pl.pallas_call(
        functools.partial(_attn_lat_kernel, rows=rows, wr=wr),
        grid=(batch, n_pairs),
        in_specs=[col(0), col(n_pairs), col(2 * n_pairs), ctx, ctx,
                  pl.BlockSpec((2, wr, GRID_W, wr * GRID_W), lambda b, p: (p, 0, 0, 0))],
        out_specs=pl.BlockSpec((seq, lanes), lambda b, p: (b, p)),
        out_shape=jax.ShapeDtypeStruct((batch * seq, ATTN_W), bf16),
        compiler_params=_cparams(("parallel", "parallel")),
        name="latent_attention",
    )(z, z, z, cache_k, cache_v, bias)


def ssm_tables(a_re, a_im, log_dt, b_re, b_im, c_re, c_im):
    hp = lax.Precision.HIGHEST
    lc, g, p, m = SSM_CHUNK, SSM_GROUPS, SSM_STATE, SSM_GROUP
    lam = lax.complex(a_re.astype(f32), a_im.astype(f32))
    dt = jnp.exp(log_dt.astype(f32))[..., None]
    step = lam * dt
    a_bar = jnp.exp(step)
    b_bar = ((a_bar - 1.0) / lam)[..., None] * lax.complex(b_re.astype(f32), b_im.astype(f32))
    c_mat = lax.complex(c_re.astype(f32), c_im.astype(f32))
    k = jnp.arange(lc + 1, dtype=f32)
    apow = jnp.exp(step[None] * k[:, None, None, None])

    s_idx = jnp.arange(lc)[:, None]
    t_idx = jnp.arange(lc)[None, :]
    ef = jnp.where((t_idx >= s_idx)[..., None, None], apow[jnp.clip(t_idx - s_idx, 0, lc), 0], 0.0)
    eb = jnp.where((s_idx >= t_idx)[..., None, None], apow[jnp.clip(s_idx - t_idx, 0, lc), 1], 0.0)
    tz = (jnp.einsum('gmp,stgp,gpn->gsntm', c_mat[0], ef, b_bar[0], precision=hp)
          + jnp.einsum('gmp,stgp,gpn->gsntm', c_mat[1], eb, b_bar[1], precision=hp))
    tz = jnp.real(tz).reshape(SSM_PAIRS, 2, lc * m, lc * m)

    bxf = apow[lc - 1 - jnp.arange(lc), 0][..., None] * b_bar[0][None]
    bxb = apow[jnp.arange(lc), 1][..., None] * b_bar[1][None]

    def pack_bx(t):
        return t.transpose(1, 0, 3, 2).reshape(SSM_PAIRS, 2, lc * m, p)

    zero = jnp.zeros((SSM_PAIRS, lc * m, p), f32)

    def bx_block(t):
        t = pack_bx(t)
        top = jnp.concatenate([t[:, 0], zero], axis=-1)
        bot = jnp.concatenate([zero, t[:, 1]], axis=-1)
        return jnp.concatenate([top, bot], axis=1)

    bx = jnp.concatenate([bx_block(jnp.real(bxf)), bx_block(jnp.imag(bxf)),
                          bx_block(jnp.real(bxb)), bx_block(jnp.imag(bxb))], axis=-1)

    cyf = c_mat[0][None] * apow[jnp.arange(lc) + 1, 0][:, :, None, :]
    cyb = c_mat[1][None] * apow[lc - jnp.arange(lc), 1][:, :, None, :]

    def cy_block(t):
        t = t.transpose(1, 3, 0, 2).reshape(SSM_PAIRS, 2, p, lc * m)
        z = jnp.zeros((SSM_PAIRS, p, lc * m), f32)
        top = jnp.concatenate([t[:, 0], z], axis=-1)
        bot = jnp.concatenate([z, t[:, 1]], axis=-1)
        return jnp.concatenate([top, bot], axis=1)

    cy = jnp.concatenate([cy_block(jnp.real(cyf)), cy_block(-jnp.imag(cyf)),
                          cy_block(jnp.real(cyb)), cy_block(-jnp.imag(cyb))], axis=1)

    a16 = apow[lc]

    def pack_a(t):
        return t.reshape(SSM_PAIRS, 1, 2 * p)

    a16p = jnp.concatenate([pack_a(jnp.real(a16[0])), pack_a(jnp.imag(a16[0])),
                            pack_a(jnp.real(a16[1])), pack_a(jnp.imag(a16[1]))], axis=-1)
    return tz.astype(bf16), bx.astype(bf16), cy.astype(bf16), a16p


def _ssm_kernel(u_ref, tz_ref, bx_ref, cy_ref, a_ref, h0_ref, y_ref, fin_ref, x_scr, s_scr, *, n_seq, n_chunks):
    half = SSM_CHUNK * SSM_GROUP
    sw = 2 * SSM_STATE
    u = u_ref[0].astype(bf16)
    x_scr[...] = jnp.dot(u, bx_ref[0], preferred_element_type=f32)
    a = a_ref[0]
    h0 = h0_ref[0]

    def scan(direction):
        lo = 2 * direction * sw
        ar = a[:, lo:lo + sw]
        ai = a[:, lo + sw:lo + 2 * sw]

        def body(i, carry):
            sr, si = carry
            c = i if direction == 0 else n_chunks - 1 - i
            rows = pl.ds(pl.multiple_of(c * n_seq, n_seq), n_seq)
            s_scr[rows, lo:lo + sw] = sr
            s_scr[rows, lo + sw:lo + 2 * sw] = si
            xr = x_scr[rows, lo:lo + sw]
            xi = x_scr[rows, lo + sw:lo + 2 * sw]
            return ar * sr - ai * si + xr, ar * si + ai * sr + xi

        sr, si = lax.fori_loop(0, n_chunks, body, (h0[:, lo:lo + sw], h0[:, lo + sw:lo + 2 * sw]))
        fin_ref[0, :, lo:lo + sw] = sr
        fin_ref[0, :, lo + sw:lo + 2 * sw] = si

    scan(0)
    scan(1)
    y = jnp.concatenate(
        [jnp.dot(u[:, :half], tz_ref[0, 0], preferred_element_type=f32),
         jnp.dot(u[:, half:], tz_ref[0, 1], preferred_element_type=f32)], axis=-1)
    y_ref[0] = y + jnp.dot(s_scr[...].astype(bf16), cy_ref[0], preferred_element_type=f32)


def ssm_scan(u_pairs, tables, h0_pairs, n_seq, n_chunks):
    tz, bx, cy, a16p = tables
    r = n_seq * n_chunks
    w = 2 * SSM_CHUNK * SSM_GROUP
    per_pair = lambda *shape: pl.BlockSpec((1,) + shape, lambda k: (k,) + (0,) * len(shape))
    return pl.pallas_call(
        functools.partial(_ssm_kernel, n_seq=n_seq, n_chunks=n_chunks),
        grid=(SSM_PAIRS,),
        in_specs=[per_pair(r, w), per_pair(2, w // 2, w // 2), per_pair(w, w), per_pair(w, w),
                  per_pair(1, w), per_pair(n_seq, w)],
        out_specs=[per_pair(r, w), per_pair(n_seq, w)],
        out_shape=[jax.ShapeDtypeStruct((SSM_PAIRS, r, w), f32),
                   jax.ShapeDtypeStruct((SSM_PAIRS, n_seq, w), f32)],
        scratch_shapes=[pltpu.VMEM((r, w), f32), pltpu.VMEM((r, w), f32)],
        compiler_params=_cparams(("parallel",)),
        name="ssm_scan",
    )(u_pairs, tz, bx, cy, a16p, h0_pairs)


def to_pairs(u, batch, seq, n_seq):
    nc = seq // SSM_CHUNK
    t = u.reshape(batch, nc, SSM_CHUNK, SSM_PAIRS, 2, SSM_GROUP).transpose(3, 1, 0, 4, 2, 5)
    t = jnp.pad(t, ((0, 0), (0, 0), (0, n_seq - batch), (0, 0), (0, 0), (0, 0)))
    return t.reshape(SSM_PAIRS, nc * n_seq, 2 * SSM_CHUNK * SSM_GROUP)


def from_pairs(y, batch, seq, n_seq):
    nc = seq // SSM_CHUNK
    t = y.reshape(SSM_PAIRS, nc, n_seq, 2, SSM_CHUNK, SSM_GROUP)[:, :, :batch]
    return t.transpose(2, 1, 4, 0, 3, 5).reshape(batch * seq, SSM_W)


def state_to_pairs(st, n_seq):
    batch = st.shape[0]
    t = st.reshape(batch, 2, SSM_PAIRS, 2, SSM_STATE, 2).transpose(2, 0, 1, 5, 3, 4)
    t = t.reshape(SSM_PAIRS, batch, 8 * SSM_STATE)
    return jnp.pad(t, ((0, 0), (0, n_seq - batch), (0, 0)))


def state_from_pairs(fin, batch):
    t = fin[:, :batch].reshape(SSM_PAIRS, batch, 2, 2, 2, SSM_STATE)
    return t.transpose(1, 2, 0, 4, 5, 3).reshape(batch, 2, SSM_GROUPS, SSM_STATE, 2)


def _glu_kernel(y_ref, u_ref, d_ref, w_ref, b_ref, o_ref):
    y = y_ref[...] + d_ref[...] * u_ref[...]
    z = jnp.dot(jax.nn.gelu(y).astype(bf16), w_ref[...], preferred_element_type=f32) + b_ref[...]
    o_ref[...] = (z[:, :SSM_W] * jax.nn.sigmoid(z[:, SSM_W:])).astype(bf16)


def ssm_glu(y, z, d_skip, w_glu, b_glu, tm):
    m_total = y.shape[0]
    u_blk = (3 * ATTN_W) // SSM_W
    return pl.pallas_call(
        _glu_kernel,
        grid=(m_total // tm,),
        in_specs=[
            pl.BlockSpec((tm, SSM_W), lambda i: (i, 0)),
            pl.BlockSpec((tm, SSM_W), lambda i: (i, u_blk)),
            pl.BlockSpec((1, SSM_W), lambda i: (0, 0)),
            pl.BlockSpec((SSM_W, 2 * SSM_W), lambda i: (0, 0)),
            pl.BlockSpec((1, 2 * SSM_W), lambda i: (0, 0)),
        ],
        out_specs=pl.BlockSpec((tm, SSM_W), lambda i: (i, 0)),
        out_shape=jax.ShapeDtypeStruct((m_total, SSM_W), bf16),
        compiler_params=_cparams(("parallel",)),
        name="ssm_glu",
    )(y, z, d_skip, w_glu, b_glu)


def _pool_kernel(p_ref, w_ref, sc_ref, o_ref, pad_ref, *, seq):
    zeros = jnp.zeros((POOL_PAD, POOL_W), f32)
    pad_ref[0:POOL_PAD, :] = zeros
    pad_ref[POOL_PAD + seq:, :] = zeros
    pad_ref[POOL_PAD:POOL_PAD + seq, :] = p_ref[...]
    t = lax.broadcasted_iota(jnp.int32, (seq, 1), 0)
    for g, win in enumerate(POOL_WINDOWS):
        cols = slice(g * POOL_GROUP, (g + 1) * POOL_GROUP)
        total = jnp.zeros((seq, POOL_GROUP), f32)
        for d in range(-(win // 2), win - win // 2):
            total = total + pad_ref[POOL_PAD + d:POOL_PAD + d + seq, cols]
        lo = jnp.clip(t - win // 2, 0, seq)
        hi = jnp.clip(t - win // 2 + win, 0, seq)
        mixed = total / (hi - lo).astype(f32) - p_ref[:, cols]
        out = jnp.dot(mixed.astype(bf16), w_ref[g].astype(bf16), preferred_element_type=f32)
        o_ref[:, cols] = (out * sc_ref[:, cols]).astype(bf16)


def pool_mixer(z, row0, batch, seq, w_pool, pool_scale):
    p_blk = (3 * ATTN_W + SSM_W) // POOL_W
    blk0 = row0 // seq
    return pl.pallas_call(
        functools.partial(_pool_kernel, seq=seq),
        grid=(batch,),
        in_specs=[
            pl.BlockSpec((seq, POOL_W), lambda b: (blk0 + b, p_blk)),
            pl.BlockSpec((len(POOL_WINDOWS), POOL_GROUP, POOL_GROUP), lambda b: (0, 0, 0)),
            pl.BlockSpec((1, POOL_W), lambda b: (0, 0)),
        ],
        out_specs=pl.BlockSpec((seq, POOL_W), lambda b: (b, 0)),
        out_shape=jax.ShapeDtypeStruct((batch * seq, POOL_W), bf16),
        scratch_shapes=[pltpu.VMEM((seq + 2 * POOL_PAD, POOL_W), f32)],
        compiler_params=_cparams(("parallel",)),
        name="pool_mixer",
    )(z, w_pool, pool_scale)


def _out_kernel(a_ref, s_ref, p_ref, x_ref, mod_ref, w_ref, o_ref):
    acc = jnp.dot(a_ref[...], w_ref[0:ATTN_W, :], preferred_element_type=f32)
    acc = acc + jnp.dot(s_ref[...], w_ref[ATTN_W:ATTN_W + SSM_W, :], preferred_element_type=f32)
    acc = acc + jnp.dot(p_ref[...], w_ref[ATTN_W + SSM_W:, :], preferred_element_type=f32)
    o_ref[...] = x_ref[...] + mod_ref[0][2:3] * acc


def out_projection(a, s, p, x, mod, w, row_of_tile, tm, tn=512):
    m_total, d = x.shape
    return pl.pallas_call(
        _out_kernel,
        grid=(m_total // tm, d // tn),
        in_specs=[
            pl.BlockSpec((tm, ATTN_W), lambda i, j: (i, 0)),
            pl.BlockSpec((tm, SSM_W), lambda i, j: (i, 0)),
            pl.BlockSpec((tm, POOL_W), lambda i, j: (i, 0)),
            pl.BlockSpec((tm, tn), lambda i, j: (i, j)),
            pl.BlockSpec((1, 6, tn), lambda i, j: (row_of_tile(i), 0, j)),
            pl.BlockSpec((d, tn), lambda i, j: (0, j)),
        ],
        out_specs=pl.BlockSpec((tm, tn), lambda i, j: (i, j)),
        out_shape=jax.ShapeDtypeStruct((m_total, d), f32),
        compiler_params=_cparams(("parallel", "arbitrary")),
        name="out_projection",
    )(a, s, p, x, mod, w)


def _ffn_kernel(x_ref, mod_ref, g_ref, wg_ref, wu_ref, wo_ref, fg_ref, o_ref, h_ref, *, final_norm):
    j = pl.program_id(1)

    @pl.when(j == 0)
    def _():
        m = mod_ref[0]
        x = x_ref[...]
        h_ref[...] = (_rms(x, g_ref[...]) * (1.0 + m[4:5]) + m[3:4]).astype(bf16)
        o_ref[...] = x

    h = h_ref[...]
    gate = jnp.dot(h, wg_ref[...], preferred_element_type=f32)
    up = jnp.dot(h, wu_ref[...], preferred_element_type=f32)
    act = (gate * jax.nn.sigmoid(gate) * up).astype(bf16)
    o_ref[...] += mod_ref[0][5:6] * jnp.dot(act, wo_ref[...], preferred_element_type=f32)

    if final_norm:
        @pl.when(j == pl.num_programs(1) - 1)
        def _():
            o_ref[...] = _rms(o_ref[...], fg_ref[...])


def ffn(x, mod, g, w_in, w_out, final_g, row_of_tile, tm, th, final_norm):
    m_total, d = x.shape
    hidden = w_out.shape[0]
    nh = hidden // th
    return pl.pallas_call(
        functools.partial(_ffn_kernel, final_norm=final_norm),
        grid=(m_total // tm, nh),
        in_specs=[
            pl.BlockSpec((tm, d), lambda i, j: (i, 0)),
            pl.BlockSpec((1, 6, d), lambda i, j: (row_of_tile(i), 0, 0)),
            pl.BlockSpec((1, d), lambda i, j: (0, 0)),
            pl.BlockSpec((d, th), lambda i, j: (0, j)),
            pl.BlockSpec((d, th), lambda i, j: (0, nh + j)),
            pl.BlockSpec((th, d), lambda i, j: (j, 0)),
            pl.BlockSpec((1, d), lambda i, j: (0, 0)),
        ],
        out_specs=pl.BlockSpec((tm, d), lambda i, j: (i, 0)),
        out_shape=jax.ShapeDtypeStruct((m_total, d), f32),
        scratch_shapes=[pltpu.VMEM((tm, d), bf16)],
        compiler_params=_cparams(("parallel", "arbitrary")),
        name="ffn",
    )(x, mod, g, w_in, w_in, w_out, final_g)


def kernel(x_prompt, x_sample, c, cache_k, cache_v, state_ssm, c_ctx, w_mod, b_mod, norm1_g, norm2_g, w_in,
           attn_rpb, ssm_a_re, ssm_a_im, ssm_log_dt, ssm_b_re, ssm_b_im, ssm_c_re, ssm_c_im, ssm_d, ssm_w_glu,
           ssm_b_glu, pool_w, pool_scale, w_out, ffn_w_in, ffn_w_out, final_norm_g):
    batch, seq, d = x_prompt.shape
    dec_batch, dec_seq, _ = x_sample.shape
    depth = w_in.shape[0]
    m_ctx = batch * seq
    m_total = m_ctx + dec_batch * dec_seq
    tm = 512
    assert m_ctx % tm == 0 and dec_seq % tm == 0 and dec_batch + 1 <= 8

    def row_of_tile(i):
        return jnp.where(i < m_ctx // tm, 0, 1 + (i - m_ctx // tm) // (dec_seq // tm))

    cond = jnp.concatenate([c_ctx[None, :], c, jnp.zeros((8 - 1 - dec_batch, d), f32)], axis=0)
    mod = modulation_all(cond, w_mod, b_mod)
    mod = mod[:, :1 + dec_batch].reshape(depth, 1 + dec_batch, 6, d)

    x = jnp.concatenate([x_prompt.reshape(m_ctx, d), x_sample.reshape(dec_batch * dec_seq, d)], axis=0)
    rows = dec_seq // GRID_W
    lat_seqs = -(-dec_batch // SSM_SEQ_ALIGN) * SSM_SEQ_ALIGN
    ctx_seqs = -(-batch // SSM_SEQ_ALIGN) * SSM_SEQ_ALIGN
    h0_ctx = jnp.zeros((SSM_PAIRS, ctx_seqs, 8 * SSM_STATE), f32)
    u_cols = slice(3 * ATTN_W, 3 * ATTN_W + SSM_W)

    ks_out, vs_out, st_out = [], [], []
    for l in range(depth):
        z = in_projection(x, mod[l], norm1_g[l][None], w_in[l].astype(bf16), row_of_tile, tm)

        a_ctx, k_l, v_l = context_attention(z, batch, seq)
        bias = latent_bias_table(attn_rpb[l], rows)
        a_lat = latent_attention(z, m_ctx, dec_batch, dec_seq, cache_k, cache_v, l, bias)
        a_out = jnp.concatenate([a_ctx, a_lat], axis=0)
        ks_out.append(k_l)
        vs_out.append(v_l)

        tables = ssm_tables(ssm_a_re[l], ssm_a_im[l], ssm_log_dt[l], ssm_b_re[l], ssm_b_im[l],
                            ssm_c_re[l], ssm_c_im[l])
        u = z[:, u_cols]
        y_ctx, fin = ssm_scan(to_pairs(u[:m_ctx], batch, seq, ctx_seqs), tables, h0_ctx,
                              ctx_seqs, seq // SSM_CHUNK)
        y_lat, _ = ssm_scan(to_pairs(u[m_ctx:], dec_batch, dec_seq, lat_seqs), tables,
                            state_to_pairs(state_ssm[:, l], lat_seqs), lat_seqs, dec_seq // SSM_CHUNK)
        st_out.append(state_from_pairs(fin, batch))
        y = jnp.concatenate([from_pairs(y_ctx, batch, seq, ctx_seqs),
                             from_pairs(y_lat, dec_batch, dec_seq, lat_seqs)], axis=0)
        s_out = ssm_glu(y, z, ssm_d[l][None], ssm_w_glu[l].astype(bf16), ssm_b_glu[l][None], tm)

        p_out = jnp.concatenate([pool_mixer(z, 0, batch, seq, pool_w[l], pool_scale[l][None]),
                                 pool_mixer(z, m_ctx, dec_batch, dec_seq, pool_w[l], pool_scale[l][None])], axis=0)

        x = out_projection(a_out, s_out, p_out, x, mod[l], w_out[l].astype(bf16), row_of_tile, tm)
        x = ffn(x, mod[l], norm2_g[l][None], ffn_w_in[l].astype(bf16), ffn_w_out[l].astype(bf16),
                final_norm_g[None], row_of_tile, tm, 512, final_norm=(l == depth - 1))

    y_prompt = x[:m_ctx].reshape(batch, seq, d)
    y_sample = x[m_ctx:].reshape(dec_batch, dec_seq, d)
    return (y_prompt, y_sample, jnp.stack(ks_out, axis=1), jnp.stack(vs_out, axis=1), jnp.stack(st_out, axis=1))
```

```python
import functools

import jax
import jax.numpy as jnp
import numpy as np
from jax import lax
from jax.experimental import pallas as pl
from jax.experimental.pallas import tpu as pltpu

f32 = jnp.float32
bf16 = jnp.bfloat16

D_MODEL = 2048
N_HEADS = 16
HEAD_DIM = 64
ATTN_W = N_HEADS * HEAD_DIM
SSM_W = 512
SSM_GROUP = 16
SSM_GROUPS = 32
SSM_STATE = 64
POOL_W = 512
POOL_WINDOWS = (2, 4, 8, 16)
POOL_GROUP = 128
POOL_PAD = 16
IN_W = 3 * ATTN_W + SSM_W + POOL_W
GRID_W = 64
WIN_ROWS_MAX = 8
WIN_COLS = 16
RMS_EPS = 1e-6
NEG_INF = -1e30

SSM_CHUNK = 16
SSM_PAIRS = SSM_GROUPS // 2
SSM_QUAD = 4
SSM_PW = 2 * SSM_CHUNK * SSM_GROUP

VMEM_LIMIT = 56 * 1024 * 1024


def _cparams(sem):
    return pltpu.CompilerParams(dimension_semantics=sem, vmem_limit_bytes=VMEM_LIMIT)


def _rms(x, g):
    return x * lax.rsqrt(jnp.mean(x * x, axis=-1, keepdims=True) + RMS_EPS) * g


def _two_part(tm, width, tiles_a, tiles_b):
    return (pl.BlockSpec((tm, width), lambda i, *_: (jnp.minimum(i, tiles_a - 1), 0)),
            pl.BlockSpec((tm, width), lambda i, *_: (jnp.clip(i - tiles_a, 0, tiles_b - 1), 0)))


def _pick(first, a_ref, b_ref):
    return jnp.where(first, a_ref[...], b_ref[...])


def _mod_kernel(c_ref, w_ref, b_ref, o_ref):
    c = c_ref[...]
    s = (c * jax.nn.sigmoid(c)).astype(bf16)
    o_ref[0] = jnp.dot(s, w_ref[0].astype(bf16), preferred_element_type=f32) + b_ref[0]


def modulation_all(cond, w_mod, b_mod, tn=1024):
    n_layers, d, n = w_mod.shape
    return pl.pallas_call(
        _mod_kernel,
        grid=(n_layers, n // tn),
        in_specs=[
            pl.BlockSpec((8, d), lambda l, j: (0, 0)),
            pl.BlockSpec((1, d, tn), lambda l, j: (l, 0, j)),
            pl.BlockSpec((1, 1, tn), lambda l, j: (l, 0, j)),
        ],
        out_specs=pl.BlockSpec((1, 8, tn), lambda l, j: (l, 0, j)),
        out_shape=jax.ShapeDtypeStruct((n_layers, 8, n), f32),
        compiler_params=_cparams(("parallel", "parallel")),
        name="modulation",
    )(cond, w_mod, b_mod.reshape(n_layers, 1, n))


def _in_kernel(x_ref, mod_ref, g_ref, w_ref, o_ref, h_ref):
    @pl.when(pl.program_id(1) == 0)
    def _():
        m = mod_ref[0]
        y = _rms(x_ref[...], g_ref[...])
        h_ref[...] = (y * (1.0 + m[1:2]) + m[0:1]).astype(bf16)

    o_ref[...] = jnp.dot(h_ref[...], w_ref[...], preferred_element_type=f32)


def in_projection(x, mod, g, w, row_of_tile, tm, tn=512):
    m_total, d = x.shape
    n = w.shape[1]
    return pl.pallas_call(
        _in_kernel,
        grid=(m_total // tm, n // tn),
        in_specs=[
            pl.BlockSpec((tm, d), lambda i, j: (i, 0)),
            pl.BlockSpec((1, 6, d), lambda i, j: (row_of_tile(i), 0, 0)),
            pl.BlockSpec((1, d), lambda i, j: (0, 0)),
            pl.BlockSpec((d, tn), lambda i, j: (0, j)),
        ],
        out_specs=pl.BlockSpec((tm, tn), lambda i, j: (i, j)),
        out_shape=jax.ShapeDtypeStruct((m_total, n), f32),
        scratch_shapes=[pltpu.VMEM((tm, d), bf16)],
        compiler_params=_cparams(("parallel", "arbitrary")),
        name="in_projection",
    )(x, mod, g, w)


def _attn_ctx_kernel(q_ref, k_ref, v_ref, o_ref, kc_ref, vc_ref):
    scale = HEAD_DIM ** -0.5
    outs = []
    for h in range(2):
        sl = slice(h * HEAD_DIM, (h + 1) * HEAD_DIM)
        q = q_ref[:, sl]
        k = k_ref[:, sl]
        v = v_ref[:, sl]
        kc_ref[0, h] = k
        vc_ref[0, h] = v
        s = lax.dot_general(q.astype(bf16), k.astype(bf16), (((1,), (1,)), ((), ())),
                            preferred_element_type=f32) * scale
        e = jnp.exp(s - jnp.max(s, axis=-1, keepdims=True))
        o = jnp.dot(e.astype(bf16), v.astype(bf16), preferred_element_type=f32)
        outs.append(o / jnp.sum(e, axis=-1, keepdims=True))
    o_ref[...] = jnp.concatenate(outs, axis=-1).astype(bf16)


def context_attention(z, batch, seq):
    lanes = 2 * HEAD_DIM
    n_pairs = N_HEADS // 2
    col = lambda off: pl.BlockSpec((seq, lanes), lambda b, p: (b, off + p))
    cache = pl.BlockSpec((1, 2, seq, HEAD_DIM), lambda b, p: (b, p, 0, 0))
    cache_shape = jax.ShapeDtypeStruct((batch, N_HEADS, seq, HEAD_DIM), f32)
    return pl.pallas_call(
        _attn_ctx_kernel,
        grid=(batch, n_pairs),
        in_specs=[col(0), col(n_pairs), col(2 * n_pairs)],
        out_specs=[pl.BlockSpec((seq, lanes), lambda b, p: (b, p)), cache, cache],
        out_shape=[jax.ShapeDtypeStruct((batch * seq, ATTN_W), bf16), cache_shape, cache_shape],
        compiler_params=_cparams(("parallel", "parallel")),
        name="context_attention",
    )(z, z, z)


def _attn_lat_kernel(q_ref, k_ref, v_ref, ck_ref, cv_ref, t_ref, o_ref, bias_scr, *, rows, wr):
    scale = HEAD_DIM ** -0.5
    n_loc = wr * GRID_W
    for h in range(2):
        for d in range(wr):
            for j in range(wr):
                bias_scr[h, d, :, j * GRID_W:(j + 1) * GRID_W] = t_ref[h, j - d + WIN_ROWS_MAX - 1]

    def row(r, carry):
        rs = jnp.clip(r - wr // 2, 0, rows - wr)
        q2 = q_ref[pl.ds(pl.multiple_of(r * GRID_W, GRID_W), GRID_W), :]
        k2 = k_ref[pl.ds(pl.multiple_of(rs * GRID_W, GRID_W), n_loc), :]
        v2 = v_ref[pl.ds(pl.multiple_of(rs * GRID_W, GRID_W), n_loc), :]
        outs = []
        for h in range(2):
            sl = slice(h * HEAD_DIM, (h + 1) * HEAD_DIM)
            q = q2[:, sl].astype(bf16)
            s_loc = lax.dot_general(q, k2[:, sl].astype(bf16), (((1,), (1,)), ((), ())),
                                    preferred_element_type=f32) * scale + bias_scr[h, r - rs]
            s_ctx = lax.dot_general(q, ck_ref[0, 0, h].astype(bf16), (((1,), (1,)), ((), ())),
                                    preferred_element_type=f32) * scale
            m = jnp.maximum(jnp.max(s_loc, axis=-1, keepdims=True), jnp.max(s_ctx, axis=-1, keepdims=True))
            e_loc = jnp.exp(s_loc - m)
            e_ctx = jnp.exp(s_ctx - m)
            den = jnp.sum(e_loc, axis=-1, keepdims=True) + jnp.sum(e_ctx, axis=-1, keepdims=True)
            o = (jnp.dot(e_loc.astype(bf16), v2[:, sl].astype(bf16), preferred_element_type=f32)
                 + jnp.dot(e_ctx.astype(bf16), cv_ref[0, 0, h].astype(bf16), preferred_element_type=f32))
            outs.append(o / den)
        o_ref[pl.ds(pl.multiple_of(r * GRID_W, GRID_W), GRID_W), :] = jnp.concatenate(outs, axis=-1).astype(bf16)
        return carry

    lax.fori_loop(0, rows, row, 0)


def latent_bias_table(rpb):
    cols = np.arange(GRID_W)
    col_start = np.clip(cols - WIN_COLS // 2, 0, GRID_W - WIN_COLS)
    valid = (cols[None, :] >= col_start[:, None]) & (cols[None, :] < col_start[:, None] + WIN_COLS)
    dc = cols[None, :] - cols[:, None] + WIN_COLS - 1
    assert np.all((dc[valid] >= 0) & (dc[valid] < 2 * WIN_COLS - 1))
    onehot = ((np.arange(2 * WIN_COLS - 1)[:, None, None] == dc[None]) & valid[None]).astype(np.float32)
    t = jnp.einsum('hrd,dqk->hrqk', rpb.astype(f32), jnp.asarray(onehot), precision=lax.Precision.HIGHEST)
    return jnp.where(jnp.asarray(valid)[None, None], t, NEG_INF)


def latent_attention(z, row0, batch, seq, cache_k, cache_v, layer, table):
    lanes = 2 * HEAD_DIM
    n_pairs = N_HEADS // 2
    rows = seq // GRID_W
    wr = min(WIN_ROWS_MAX, rows)
    past = cache_k.shape[3]
    blk0 = row0 // seq
    n_dr = 2 * WIN_ROWS_MAX - 1
    col = lambda off: pl.BlockSpec((seq, lanes), lambda b, p: (blk0 + b, off + p))
    ctx = pl.BlockSpec((1, 1, 2, past, HEAD_DIM), lambda b, p: (b, layer, p, 0, 0))
    return pl.pallas_call(
        functools.partial(_attn_lat_kernel, rows=rows, wr=wr),
        grid=(batch, n_pairs),
        in_specs=[col(0), col(n_pairs), col(2 * n_pairs), ctx, ctx,
                  pl.BlockSpec((2, n_dr, GRID_W, GRID_W), lambda b, p: (p, 0, 0, 0))],
        out_specs=pl.BlockSpec((seq, lanes), lambda b, p: (b, p)),
        out_shape=jax.ShapeDtypeStruct((batch * seq, ATTN_W), bf16),
        scratch_shapes=[pltpu.VMEM((2, wr, GRID_W, wr * GRID_W), f32)],
        compiler_params=_cparams(("parallel", "parallel")),
        name="latent_attention",
    )(z, z, z, cache_k, cache_v, table)


def _cmul(a, b):
    return a[0] * b[0] - a[1] * b[1], a[0] * b[1] + a[1] * b[0]


def ssm_tables(a_re, a_im, log_dt, b_re, b_im, c_re, c_im):
    hp = lax.Precision.HIGHEST
    lc, p, m = SSM_CHUNK, SSM_STATE, SSM_GROUP
    lr, li = a_re.astype(f32), a_im.astype(f32)
    dt = jnp.exp(log_dt.astype(f32))[..., None]
    sr, si = lr * dt, li * dt
    k = jnp.arange(lc + 1, dtype=f32)[:, None, None, None]
    mag = jnp.exp(k * sr[None])
    apow = (mag * jnp.cos(k * si[None]), mag * jnp.sin(k * si[None]))
    a_bar = (apow[0][1], apow[1][1])
    den = lr * lr + li * li
    q = (((a_bar[0] - 1.0) * lr + a_bar[1] * li) / den, (a_bar[1] * lr - (a_bar[0] - 1.0) * li) / den)
    b_bar = _cmul((q[0][..., None], q[1][..., None]), (b_re.astype(f32), b_im.astype(f32)))
    c_mat = (c_re.astype(f32), c_im.astype(f32))
    eye2 = jnp.eye(2, dtype=f32)

    def kern(d):
        ab = _cmul((apow[0][:lc, d][..., None], apow[1][:lc, d][..., None]), (b_bar[0][d][None], b_bar[1][d][None]))
        return (jnp.einsum('gmp,kgpn->kgmn', c_mat[0][d], ab[0], precision=hp)
                - jnp.einsum('gmp,kgpn->kgmn', c_mat[1][d], ab[1], precision=hp))

    s_idx = jnp.arange(lc)[:, None]
    t_idx = jnp.arange(lc)[None, :]
    tau = jnp.arange(lc)[:, None, None]
    onehot_f = (t_idx - s_idx == tau).astype(f32)
    onehot_b = (s_idx - t_idx == tau).astype(f32)
    tz = (jnp.einsum('kst,kgmn->gsntm', onehot_f, kern(0), precision=hp)
          + jnp.einsum('kst,kgmn->gsntm', onehot_b, kern(1), precision=hp))
    tz = tz.reshape(SSM_PAIRS, 2, lc, m, lc, m).transpose(0, 2, 1, 3, 4, 5)
    tz = tz[:, :, :, :, :, None, :] * eye2[None, None, :, None, None, :, None]
    tz = tz.reshape(SSM_PAIRS, SSM_PW, SSM_PW)

    def bx_block(t):
        t = t.reshape(lc, SSM_PAIRS, 2, p, m).transpose(1, 0, 2, 4, 3)
        t = t[:, :, :, :, None, :] * eye2[None, None, :, None, :, None]
        return t.reshape(SSM_PAIRS, SSM_PW, 2 * p)

    def cy_block(t):
        t = t.reshape(lc, SSM_PAIRS, 2, m, p).transpose(1, 2, 4, 0, 3)
        t = t[:, :, :, :, None, :] * eye2[None, :, None, None, :, None]
        return t.reshape(SSM_PAIRS, 2 * p, SSM_PW)

    rev = lc - 1 - jnp.arange(lc)
    fwd = jnp.arange(lc)
    bxf = _cmul((apow[0][rev, 0][..., None], apow[1][rev, 0][..., None]), (b_bar[0][0][None], b_bar[1][0][None]))
    bxb = _cmul((apow[0][fwd, 1][..., None], apow[1][fwd, 1][..., None]), (b_bar[0][1][None], b_bar[1][1][None]))
    bx = jnp.concatenate([bx_block(bxf[0]), bx_block(bxf[1]), bx_block(bxb[0]), bx_block(bxb[1])], axis=-1)
    cyf = _cmul((c_mat[0][0][None], c_mat[1][0][None]),
                (apow[0][fwd + 1, 0][:, :, None, :], apow[1][fwd + 1, 0][:, :, None, :]))
    cyb = _cmul((c_mat[0][1][None], c_mat[1][1][None]),
                (apow[0][lc - fwd, 1][:, :, None, :], apow[1][lc - fwd, 1][:, :, None, :]))
    cy = jnp.concatenate([cy_block(cyf[0]), cy_block(-cyf[1]), cy_block(cyb[0]), cy_block(-cyb[1])], axis=1)

    def pack_a(t):
        return t.reshape(SSM_PAIRS, 1, 2 * p)

    a16p = jnp.concatenate([pack_a(apow[0][lc, 0]), pack_a(apow[1][lc, 0]),
                            pack_a(apow[0][lc, 1]), pack_a(apow[1][lc, 1])], axis=-1)
    return tz.astype(bf16), bx.astype(bf16), cy.astype(bf16), a16p


def _ssm_kernel(u_ref, tz_ref, bx_ref, cy_ref, a_ref, h0_ref, y_ref, fin_ref, x_scr, s_scr, y_scr, *, n_seq, n_chunks):
    r = n_seq * n_chunks
    pc = 2 * SSM_GROUP
    sw = 2 * SSM_STATE
    xs = [u_ref[pl.ds(s, r, stride=SSM_CHUNK), :] for s in range(SSM_CHUNK)]
    for kq in range(SSM_QUAD):
        u = jnp.concatenate([x[:, kq * pc:(kq + 1) * pc] for x in xs], axis=-1).astype(bf16)
        x = jnp.dot(u, bx_ref[kq], preferred_element_type=f32)
        for i in range(4):
            x_scr[4 * kq + i] = x[:, i * sw:(i + 1) * sw]
        y_scr[:, kq * SSM_PW:(kq + 1) * SSM_PW] = jnp.dot(u, tz_ref[kq], preferred_element_type=f32)

    def scan(direction):
        offs = [kq * SSM_PW + 2 * direction * sw for kq in range(SSM_QUAD)]
        slabs = [4 * kq + 2 * direction for kq in range(SSM_QUAD)]
        coef = [(a_ref[kq][:, 2 * direction * sw:2 * direction * sw + sw],
                 a_ref[kq][:, 2 * direction * sw + sw:2 * direction * sw + 2 * sw]) for kq in range(SSM_QUAD)]

        def body(i, carry):
            c = i if direction == 0 else n_chunks - 1 - i
            rows = pl.ds(c, n_seq, stride=n_chunks)
            out = []
            for kq in range(SSM_QUAD):
                sr, si = carry[2 * kq], carry[2 * kq + 1]
                ar, ai = coef[kq]
                re, im = slabs[kq], slabs[kq] + 1
                s_scr[re, rows, :] = sr
                s_scr[im, rows, :] = si
                xr = x_scr[re, rows, :]
                xi = x_scr[im, rows, :]
                out += [ar * sr - ai * si + xr, ar * si + ai * sr + xi]
            return tuple(out)

        init = []
        for lo in offs:
            init += [h0_ref[:, lo:lo + sw], h0_ref[:, lo + sw:lo + 2 * sw]]
        fin = lax.fori_loop(0, n_chunks, body, tuple(init))
        for kq, lo in enumerate(offs):
            fin_ref[:, lo:lo + sw] = fin[2 * kq]
            fin_ref[:, lo + sw:lo + 2 * sw] = fin[2 * kq + 1]

    scan(0)
    scan(1)
    ys = []
    for kq in range(SSM_QUAD):
        cols = slice(kq * SSM_PW, (kq + 1) * SSM_PW)
        s_in = jnp.concatenate([s_scr[4 * kq + i] for i in range(4)], axis=-1).astype(bf16)
        ys.append(y_scr[:, cols] + jnp.dot(s_in, cy_ref[kq], preferred_element_type=f32))
    for t in range(SSM_CHUNK):
        y_ref[pl.ds(t, r, stride=SSM_CHUNK), :] = jnp.concatenate([y[:, t * pc:(t + 1) * pc] for y in ys], axis=-1)


def ssm_scan(z, row0, n_seq, seq, tables, h0):
    tz, bx, cy, a16p = tables
    n_chunks = seq // SSM_CHUNK
    rows = n_seq * seq
    r = n_seq * n_chunks
    lanes = SSM_QUAD * 2 * SSM_GROUP
    u_blk = (3 * ATTN_W) // lanes
    qw = SSM_QUAD * SSM_PW
    table = pl.BlockSpec((SSM_QUAD, SSM_PW, SSM_PW), lambda q: (q, 0, 0))
    return pl.pallas_call(
        functools.partial(_ssm_kernel, n_seq=n_seq, n_chunks=n_chunks),
        grid=(SSM_PAIRS // SSM_QUAD,),
        in_specs=[pl.BlockSpec((rows, lanes), lambda q: (row0 // rows, u_blk + q)),
                  table, table, table,
                  pl.BlockSpec((SSM_QUAD, 1, SSM_PW), lambda q: (q, 0, 0)),
                  pl.BlockSpec((n_seq, qw), lambda q: (0, q))],
        out_specs=[pl.BlockSpec((rows, lanes), lambda q: (0, q)),
                   pl.BlockSpec((n_seq, qw), lambda q: (0, q))],
        out_shape=[jax.ShapeDtypeStruct((rows, SSM_W), f32),
                   jax.ShapeDtypeStruct((n_seq, SSM_PAIRS * SSM_PW), f32)],
        scratch_shapes=[pltpu.VMEM((4 * SSM_QUAD, r, 2 * SSM_STATE), f32),
                        pltpu.VMEM((4 * SSM_QUAD, r, 2 * SSM_STATE), f32), pltpu.VMEM((r, qw), f32)],
        compiler_params=_cparams(("parallel",)),
        name="ssm_scan",
    )(z, tz, bx, cy, a16p, h0)


def state_to_slabs(st):
    batch = st.shape[0]
    t = st.reshape(batch, 2, SSM_PAIRS, 2, SSM_STATE, 2).transpose(0, 2, 1, 5, 3, 4)
    return t.reshape(batch, SSM_PAIRS * SSM_PW)


def state_from_slabs(fin):
    batch = fin.shape[0]
    t = fin.reshape(batch, SSM_PAIRS, 2, 2, 2, SSM_STATE)
    return t.transpose(0, 2, 1, 4, 5, 3).reshape(batch, 2, SSM_GROUPS, SSM_STATE, 2)


def _glu_kernel(ya_ref, yb_ref, u_ref, d_ref, w_ref, b_ref, o_ref, *, tiles_a):
    y = _pick(pl.program_id(0) < tiles_a, ya_ref, yb_ref) + d_ref[...] * u_ref[...]
    z = jnp.dot(jax.nn.gelu(y).astype(bf16), w_ref[...], preferred_element_type=f32) + b_ref[...]
    o_ref[...] = (z[:, :SSM_W] * jax.nn.sigmoid(z[:, SSM_W:])).astype(bf16)


def ssm_glu(y_a, y_b, z, d_skip, w_glu, b_glu, tm):
    tiles_a, tiles_b = y_a.shape[0] // tm, y_b.shape[0] // tm
    m_total = y_a.shape[0] + y_b.shape[0]
    u_blk = (3 * ATTN_W) // SSM_W
    return pl.pallas_call(
        functools.partial(_glu_kernel, tiles_a=tiles_a),
        grid=(tiles_a + tiles_b,),
        in_specs=[
            *_two_part(tm, SSM_W, tiles_a, tiles_b),
            pl.BlockSpec((tm, SSM_W), lambda i: (i, u_blk)),
            pl.BlockSpec((1, SSM_W), lambda i: (0, 0)),
            pl.BlockSpec((SSM_W, 2 * SSM_W), lambda i: (0, 0)),
            pl.BlockSpec((1, 2 * SSM_W), lambda i: (0, 0)),
        ],
        out_specs=pl.BlockSpec((tm, SSM_W), lambda i: (i, 0)),
        out_shape=jax.ShapeDtypeStruct((m_total, SSM_W), bf16),
        compiler_params=_cparams(("arbitrary",)),
        name="ssm_glu",
    )(y_a, y_b, z, d_skip, w_glu, b_glu)


def _pool_kernel(p_ref, w_ref, sc_ref, o_ref, pad_ref, *, seq):
    zeros = jnp.zeros((POOL_PAD, POOL_W), f32)
    pad_ref[0:POOL_PAD, :] = zeros
    pad_ref[POOL_PAD + seq:, :] = zeros
    pad_ref[POOL_PAD:POOL_PAD + seq, :] = p_ref[...]
    t = lax.broadcasted_iota(jnp.int32, (seq, 1), 0)
    for g, win in enumerate(POOL_WINDOWS):
        cols = slice(g * POOL_GROUP, (g + 1) * POOL_GROUP)
        total = jnp.zeros((seq, POOL_GROUP), f32)
        for d in range(-(win // 2), win - win // 2):
            total = total + pad_ref[POOL_PAD + d:POOL_PAD + d + seq, cols]
        lo = jnp.clip(t - win // 2, 0, seq)
        hi = jnp.clip(t - win // 2 + win, 0, seq)
        mixed = total / (hi - lo).astype(f32) - p_ref[:, cols]
        out = jnp.dot(mixed.astype(bf16), w_ref[g].astype(bf16), preferred_element_type=f32)
        o_ref[:, cols] = (out * sc_ref[:, cols]).astype(bf16)


def pool_mixer(z, row0, batch, seq, w_pool, pool_scale):
    p_blk = (3 * ATTN_W + SSM_W) // POOL_W
    blk0 = row0 // seq
    return pl.pallas_call(
        functools.partial(_pool_kernel, seq=seq),
        grid=(batch,),
        in_specs=[
            pl.BlockSpec((seq, POOL_W), lambda b: (blk0 + b, p_blk)),
            pl.BlockSpec((len(POOL_WINDOWS), POOL_GROUP, POOL_GROUP), lambda b: (0, 0, 0)),
            pl.BlockSpec((1, POOL_W), lambda b: (0, 0)),
        ],
        out_specs=pl.BlockSpec((seq, POOL_W), lambda b: (b, 0)),
        out_shape=jax.ShapeDtypeStruct((batch * seq, POOL_W), bf16),
        scratch_shapes=[pltpu.VMEM((seq + 2 * POOL_PAD, POOL_W), f32)],
        compiler_params=_cparams(("parallel",)),
        name="pool_mixer",
    )(z, w_pool, pool_scale)


def _out_kernel(aa_ref, ab_ref, s_ref, pa_ref, pb_ref, x_ref, mod_ref, w_ref, o_ref, *, tiles_a):
    first = pl.program_id(0) < tiles_a
    acc = jnp.dot(_pick(first, aa_ref, ab_ref), w_ref[0:ATTN_W, :], preferred_element_type=f32)
    acc = acc + jnp.dot(s_ref[...], w_ref[ATTN_W:ATTN_W + SSM_W, :], preferred_element_type=f32)
    acc = acc + jnp.dot(_pick(first, pa_ref, pb_ref), w_ref[ATTN_W + SSM_W:, :], preferred_element_type=f32)
    o_ref[...] = x_ref[...] + mod_ref[0][2:3] * acc


def out_projection(a_a, a_b, s, p_a, p_b, x, mod, w, row_of_tile, tm, tn=512):
    m_total, d = x.shape
    tiles_a, tiles_b = a_a.shape[0] // tm, a_b.shape[0] // tm
    return pl.pallas_call(
        functools.partial(_out_kernel, tiles_a=tiles_a),
        grid=(m_total // tm, d // tn),
        in_specs=[
            *_two_part(tm, ATTN_W, tiles_a, tiles_b),
            pl.BlockSpec((tm, SSM_W), lambda i, j: (i, 0)),
            *_two_part(tm, POOL_W, tiles_a, tiles_b),
            pl.BlockSpec((tm, tn), lambda i, j: (i, j)),
            pl.BlockSpec((1, 6, tn), lambda i, j: (row_of_tile(i), 0, j)),
            pl.BlockSpec((d, tn), lambda i, j: (0, j)),
        ],
        out_specs=pl.BlockSpec((tm, tn), lambda i, j: (i, j)),
        out_shape=jax.ShapeDtypeStruct((m_total, d), f32),
        compiler_params=_cparams(("arbitrary", "arbitrary")),
        name="out_projection",
    )(a_a, a_b, s, p_a, p_b, x, mod, w)


def _ffn_kernel(x_ref, mod_ref, g_ref, wg_ref, wu_ref, wo_ref, fg_ref, o_ref, h_ref, *, final_norm):
    j = pl.program_id(1)

    @pl.when(j == 0)
    def _():
        m = mod_ref[0]
        x = x_ref[...]
        h_ref[...] = (_rms(x, g_ref[...]) * (1.0 + m[4:5]) + m[3:4]).astype(bf16)
        o_ref[...] = x

    h = h_ref[...]
    gate = jnp.dot(h, wg_ref[...], preferred_element_type=f32)
    up = jnp.dot(h, wu_ref[...], preferred_element_type=f32)
    act = (gate * jax.nn.sigmoid(gate) * up).astype(bf16)
    o_ref[...] += mod_ref[0][5:6] * jnp.dot(act, wo_ref[...], preferred_element_type=f32)

    if final_norm:
        @pl.when(j == pl.num_programs(1) - 1)
        def _():
            o_ref[...] = _rms(o_ref[...], fg_ref[...])


def ffn(x, mod, g, w_in, w_out, final_g, row_of_tile, tm, th, final_norm):
    m_total, d = x.shape
    hidden = w_out.shape[0]
    nh = hidden // th
    return pl.pallas_call(
        functools.partial(_ffn_kernel, final_norm=final_norm),
        grid=(m_total // tm, nh),
        in_specs=[
            pl.BlockSpec((tm, d), lambda i, j: (i, 0)),
            pl.BlockSpec((1, 6, d), lambda i, j: (row_of_tile(i), 0, 0)),
            pl.BlockSpec((1, d), lambda i, j: (0, 0)),
            pl.BlockSpec((d, th), lambda i, j: (0, j)),
            pl.BlockSpec((d, th), lambda i, j: (0, nh + j)),
            pl.BlockSpec((th, d), lambda i, j: (j, 0)),
            pl.BlockSpec((1, d), lambda i, j: (0, 0)),
        ],
        out_specs=pl.BlockSpec((tm, d), lambda i, j: (i, 0)),
        out_shape=jax.ShapeDtypeStruct((m_total, d), f32),
        scratch_shapes=[pltpu.VMEM((tm, d), bf16)],
        compiler_params=_cparams(("parallel", "arbitrary")),
        name="ffn",
    )(x, mod, g, w_in, w_in, w_out, final_g)


def kernel(x_prompt, x_sample, c, cache_k, cache_v, state_ssm, c_ctx, w_mod, b_mod, norm1_g, norm2_g, w_in,
           attn_rpb, ssm_a_re, ssm_a_im, ssm_log_dt, ssm_b_re, ssm_b_im, ssm_c_re, ssm_c_im, ssm_d, ssm_w_glu,
           ssm_b_glu, pool_w, pool_scale, w_out, ffn_w_in, ffn_w_out, final_norm_g):
    batch, seq, d = x_prompt.shape
    dec_batch, dec_seq, _ = x_sample.shape
    depth = w_in.shape[0]
    m_ctx = batch * seq
    tm = 512
    assert m_ctx % tm == 0 and dec_seq % tm == 0 and dec_batch + 1 <= 8

    def row_of_tile(i):
        return jnp.where(i < m_ctx // tm, 0, 1 + (i - m_ctx // tm) // (dec_seq // tm))

    cond = jnp.concatenate([c_ctx[None, :], c, jnp.zeros((8 - 1 - dec_batch, d), f32)], axis=0)
    mod = modulation_all(cond, w_mod, b_mod)
    mod = mod[:, :1 + dec_batch].reshape(depth, 1 + dec_batch, 6, d)

    x = jnp.concatenate([x_prompt.reshape(m_ctx, d), x_sample.reshape(dec_batch * dec_seq, d)], axis=0)
    h0_ctx = jnp.zeros((batch, SSM_PAIRS * SSM_PW), f32)

    ks_out, vs_out, st_out = [], [], []
    for l in range(depth):
        z = in_projection(x, mod[l], norm1_g[l][None], w_in[l].astype(bf16), row_of_tile, tm)

        a_ctx, k_l, v_l = context_attention(z, batch, seq)
        a_lat = latent_attention(z, m_ctx, dec_batch, dec_seq, cache_k, cache_v, l, latent_bias_table(attn_rpb[l]))
        ks_out.append(k_l)
        vs_out.append(v_l)

        tables = ssm_tables(ssm_a_re[l], ssm_a_im[l], ssm_log_dt[l], ssm_b_re[l], ssm_b_im[l],
                            ssm_c_re[l], ssm_c_im[l])
        y_ctx, fin = ssm_scan(z, 0, batch, seq, tables, h0_ctx)
        y_lat, _ = ssm_scan(z, m_ctx, dec_batch, dec_seq, tables, state_to_slabs(state_ssm[:, l]))
        st_out.append(state_from_slabs(fin))
        s_out = ssm_glu(y_ctx, y_lat, z, ssm_d[l][None], ssm_w_glu[l].astype(bf16), ssm_b_glu[l][None], tm)

        p_ctx = pool_mixer(z, 0, batch, seq, pool_w[l], pool_scale[l][None])
        p_lat = pool_mixer(z, m_ctx, dec_batch, dec_seq, pool_w[l], pool_scale[l][None])

        x = out_projection(a_ctx, a_lat, s_out, p_ctx, p_lat, x, mod[l], w_out[l].astype(bf16), row_of_tile, tm)
        x = ffn(x, mod[l], norm2_g[l][None], ffn_w_in[l].astype(bf16), ffn_w_out[l].astype(bf16),
                final_norm_g[None], row_of_tile, tm, 512, final_norm=(l == depth - 1))

    y_prompt = x[:m_ctx].reshape(batch, seq, d)
    y_sample = x[m_ctx:].reshape(dec_batch, dec_seq, d)
    return (y_prompt, y_sample, jnp.stack(ks_out, axis=1), jnp.stack(vs_out, axis=1), jnp.stack(st_out, axis=1))
```

```python
import functools

import jax
import jax.numpy as jnp
import numpy as np
from jax import lax
from jax.experimental import pallas as pl
from jax.experimental.pallas import tpu as pltpu

f32 = jnp.float32
bf16 = jnp.bfloat16

D_MODEL = 2048
N_HEADS = 16
HEAD_DIM = 64
ATTN_W = N_HEADS * HEAD_DIM
SSM_W = 512
SSM_GROUP = 16
SSM_GROUPS = 32
SSM_STATE = 64
POOL_W = 512
POOL_WINDOWS = (2, 4, 8, 16)
POOL_GROUP = 128
POOL_PAD = 16
IN_W = 3 * ATTN_W + SSM_W + POOL_W
GRID_W = 64
WIN_ROWS_MAX = 8
WIN_COLS = 16
RMS_EPS = 1e-6
NEG_INF = -1e30

SSM_CHUNK = 16
SSM_PAIRS = SSM_GROUPS // 2
SSM_QUAD = 4
SSM_PW = 2 * SSM_CHUNK * SSM_GROUP

VMEM_LIMIT = 56 * 1024 * 1024


def _cparams(sem):
    return pltpu.CompilerParams(dimension_semantics=sem, vmem_limit_bytes=VMEM_LIMIT)


def _rms(x, g):
    return x * lax.rsqrt(jnp.mean(x * x, axis=-1, keepdims=True) + RMS_EPS) * g


def _two_part(tm, width, tiles_a, tiles_b, col=lambda *_: 0):
    return (pl.BlockSpec((tm, width), lambda i, *r: (jnp.minimum(i, tiles_a - 1), col(*r))),
            pl.BlockSpec((tm, width), lambda i, *r: (jnp.clip(i - tiles_a, 0, tiles_b - 1), col(*r))))


def _row_specs(parts, tm, width, col=lambda *_: 0):
    if len(parts) == 1:
        return (pl.BlockSpec((tm, width), lambda i, *r: (i, col(*r))),)
    return _two_part(tm, width, parts[0].shape[0] // tm, parts[1].shape[0] // tm, col)


def _pick(first, a_ref, b_ref):
    return jnp.where(first, a_ref[...], b_ref[...])


def _rows(refs, tiles_a):
    if len(refs) == 1:
        return refs[0][...]
    return _pick(pl.program_id(0) < tiles_a, *refs)


def _mod_kernel(c_ref, w_ref, b_ref, o_ref):
    c = c_ref[...]
    s = (c * jax.nn.sigmoid(c)).astype(bf16)
    o_ref[0] = jnp.dot(s, w_ref[0].astype(bf16), preferred_element_type=f32) + b_ref[0]


def modulation_all(cond, w_mod, b_mod, tn=1024):
    n_layers, d, n = w_mod.shape
    return pl.pallas_call(
        _mod_kernel,
        grid=(n_layers, n // tn),
        in_specs=[
            pl.BlockSpec((8, d), lambda l, j: (0, 0)),
            pl.BlockSpec((1, d, tn), lambda l, j: (l, 0, j)),
            pl.BlockSpec((1, 1, tn), lambda l, j: (l, 0, j)),
        ],
        out_specs=pl.BlockSpec((1, 8, tn), lambda l, j: (l, 0, j)),
        out_shape=jax.ShapeDtypeStruct((n_layers, 8, n), f32),
        compiler_params=_cparams(("parallel", "parallel")),
        name="modulation",
    )(cond, w_mod, b_mod.reshape(n_layers, 1, n))


def _in_kernel(*refs, n_x, tiles_a):
    x_refs = refs[:n_x]
    mod_ref, g_ref, w_ref, o_ref, h_ref = refs[n_x:]

    @pl.when(pl.program_id(1) == 0)
    def _():
        m = mod_ref[0]
        y = _rms(_rows(x_refs, tiles_a), g_ref[...])
        h_ref[...] = (y * (1.0 + m[1:2]) + m[0:1]).astype(bf16)

    o_ref[...] = jnp.dot(h_ref[...], w_ref[...], preferred_element_type=f32)


def in_projection(xs, mod, g, w, row_of_tile, tm, tn=512):
    m_total = sum(x.shape[0] for x in xs)
    d = xs[0].shape[1]
    n = w.shape[1]
    return pl.pallas_call(
        functools.partial(_in_kernel, n_x=len(xs), tiles_a=xs[0].shape[0] // tm),
        grid=(m_total // tm, n // tn),
        in_specs=[
            *_row_specs(xs, tm, d),
            pl.BlockSpec((1, 6, d), lambda i, j: (row_of_tile(i), 0, 0)),
            pl.BlockSpec((1, d), lambda i, j: (0, 0)),
            pl.BlockSpec((d, tn), lambda i, j: (0, j)),
        ],
        out_specs=pl.BlockSpec((tm, tn), lambda i, j: (i, j)),
        out_shape=jax.ShapeDtypeStruct((m_total, n), f32),
        scratch_shapes=[pltpu.VMEM((tm, d), bf16)],
        compiler_params=_cparams(("arbitrary", "arbitrary")),
        name="in_projection",
    )(*xs, mod, g, w)


def _attn_ctx_kernel(q_ref, k_ref, v_ref, *refs):
    o_ref, kc_ref, vc_ref = refs[-3:]
    scale = HEAD_DIM ** -0.5
    outs = []
    for h in range(2):
        sl = slice(h * HEAD_DIM, (h + 1) * HEAD_DIM)
        q = q_ref[:, sl]
        k = k_ref[:, sl]
        v = v_ref[:, sl]
        kc_ref[0, 0, h] = k
        vc_ref[0, 0, h] = v
        s = lax.dot_general(q.astype(bf16), k.astype(bf16), (((1,), (1,)), ((), ())),
                            preferred_element_type=f32) * scale
        e = jnp.exp(s - jnp.max(s, axis=-1, keepdims=True))
        o = jnp.dot(e.astype(bf16), v.astype(bf16), preferred_element_type=f32)
        outs.append(o / jnp.sum(e, axis=-1, keepdims=True))
    o_ref[...] = jnp.concatenate(outs, axis=-1).astype(bf16)


def context_attention(z, batch, seq, layer, depth, caches=None):
    lanes = 2 * HEAD_DIM
    n_pairs = N_HEADS // 2
    col = lambda off: pl.BlockSpec((seq, lanes), lambda b, p: (b, off + p))
    cache = pl.BlockSpec((1, 1, 2, seq, HEAD_DIM), lambda b, p: (b, layer, p, 0, 0))
    cache_shape = jax.ShapeDtypeStruct((batch, depth, N_HEADS, seq, HEAD_DIM), f32)
    prev = () if caches is None else tuple(caches)
    return pl.pallas_call(
        _attn_ctx_kernel,
        grid=(batch, n_pairs),
        in_specs=[col(0), col(n_pairs), col(2 * n_pairs)] + [pl.BlockSpec(memory_space=pl.ANY)] * len(prev),
        out_specs=[pl.BlockSpec((seq, lanes), lambda b, p: (b, p)), cache, cache],
        out_shape=[jax.ShapeDtypeStruct((batch * seq, ATTN_W), bf16), cache_shape, cache_shape],
        input_output_aliases={3 + i: 1 + i for i in range(len(prev))},
        compiler_params=_cparams(("parallel", "parallel")),
        name="context_attention",
    )(z, z, z, *prev)


def _attn_lat_kernel(q_ref, k_ref, v_ref, ck_ref, cv_ref, t_ref, o_ref, bias_scr, *, rows, wr):
    scale = HEAD_DIM ** -0.5
    n_loc = wr * GRID_W
    for h in range(2):
        for d in range(wr):
            for j in range(wr):
                bias_scr[h, d, :, j * GRID_W:(j + 1) * GRID_W] = t_ref[h, j - d + WIN_ROWS_MAX - 1]

    def row(r, carry):
        rs = jnp.clip(r - wr // 2, 0, rows - wr)
        q2 = q_ref[pl.ds(pl.multiple_of(r * GRID_W, GRID_W), GRID_W), :]
        k2 = k_ref[pl.ds(pl.multiple_of(rs * GRID_W, GRID_W), n_loc), :]
        v2 = v_ref[pl.ds(pl.multiple_of(rs * GRID_W, GRID_W), n_loc), :]
        outs = []
        for h in range(2):
            sl = slice(h * HEAD_DIM, (h + 1) * HEAD_DIM)
            q = q2[:, sl].astype(bf16)
            s_loc = lax.dot_general(q, k2[:, sl].astype(bf16), (((1,), (1,)), ((), ())),
                                    preferred_element_type=f32) * scale + bias_scr[h, r - rs]
            s_ctx = lax.dot_general(q, ck_ref[0, 0, h].astype(bf16), (((1,), (1,)), ((), ())),
                                    preferred_element_type=f32) * scale
            m = jnp.maximum(jnp.max(s_loc, axis=-1, keepdims=True), jnp.max(s_ctx, axis=-1, keepdims=True))
            e_loc = jnp.exp(s_loc - m)
            e_ctx = jnp.exp(s_ctx - m)
            den = jnp.sum(e_loc, axis=-1, keepdims=True) + jnp.sum(e_ctx, axis=-1, keepdims=True)
            o = (jnp.dot(e_loc.astype(bf16), v2[:, sl].astype(bf16), preferred_element_type=f32)
                 + jnp.dot(e_ctx.astype(bf16), cv_ref[0, 0, h].astype(bf16), preferred_element_type=f32))
            outs.append(o / den)
        o_ref[pl.ds(pl.multiple_of(r * GRID_W, GRID_W), GRID_W), :] = jnp.concatenate(outs, axis=-1).astype(bf16)
        return carry

    lax.fori_loop(0, rows, row, 0)


def latent_bias_table(rpb):
    cols = np.arange(GRID_W)
    col_start = np.clip(cols - WIN_COLS // 2, 0, GRID_W - WIN_COLS)
    valid = (cols[None, :] >= col_start[:, None]) & (cols[None, :] < col_start[:, None] + WIN_COLS)
    dc = cols[None, :] - cols[:, None] + WIN_COLS - 1
    assert np.all((dc[valid] >= 0) & (dc[valid] < 2 * WIN_COLS - 1))
    onehot = ((np.arange(2 * WIN_COLS - 1)[:, None, None] == dc[None]) & valid[None]).astype(np.float32)
    t = jnp.einsum('hrd,dqk->hrqk', rpb.astype(f32), jnp.asarray(onehot), precision=lax.Precision.HIGHEST)
    return jnp.where(jnp.asarray(valid)[None, None], t, NEG_INF)


def latent_attention(z, row0, batch, seq, cache_k, cache_v, layer, table):
    lanes = 2 * HEAD_DIM
    n_pairs = N_HEADS // 2
    rows = seq // GRID_W
    wr = min(WIN_ROWS_MAX, rows)
    past = cache_k.shape[3]
    blk0 = row0 // seq
    n_dr = 2 * WIN_ROWS_MAX - 1
    col = lambda off: pl.BlockSpec((seq, lanes), lambda b, p: (blk0 + b, off + p))
    ctx = pl.BlockSpec((1, 1, 2, past, HEAD_DIM), lambda b, p: (b, layer, p, 0, 0))
    return pl.pallas_call(
        functools.partial(_attn_lat_kernel, rows=rows, wr=wr),
        grid=(batch, n_pairs),
        in_specs=[col(0), col(n_pairs), col(2 * n_pairs), ctx, ctx,
                  pl.BlockSpec((2, n_dr, GRID_W, GRID_W), lambda b, p: (p, 0, 0, 0))],
        out_specs=pl.BlockSpec((seq, lanes), lambda b, p: (b, p)),
        out_shape=jax.ShapeDtypeStruct((batch * seq, ATTN_W), bf16),
        scratch_shapes=[pltpu.VMEM((2, wr, GRID_W, wr * GRID_W), f32)],
        compiler_params=_cparams(("parallel", "parallel")),
        name="latent_attention",
    )(z, z, z, cache_k, cache_v, table)


def _cmul(a, b):
    return a[0] * b[0] - a[1] * b[1], a[0] * b[1] + a[1] * b[0]


def ssm_tables(a_re, a_im, log_dt, b_re, b_im, c_re, c_im):
    hp = lax.Precision.HIGHEST
    lc, p, m = SSM_CHUNK, SSM_STATE, SSM_GROUP
    lr, li = a_re.astype(f32), a_im.astype(f32)
    dt = jnp.exp(log_dt.astype(f32))[..., None]
    sr, si = lr * dt, li * dt
    k = jnp.arange(lc + 1, dtype=f32)[:, None, None, None]
    mag = jnp.exp(k * sr[None])
    apow = (mag * jnp.cos(k * si[None]), mag * jnp.sin(k * si[None]))
    a_bar = (apow[0][1], apow[1][1])
    den = lr * lr + li * li
    q = (((a_bar[0] - 1.0) * lr + a_bar[1] * li) / den, (a_bar[1] * lr - (a_bar[0] - 1.0) * li) / den)
    b_bar = _cmul((q[0][..., None], q[1][..., None]), (b_re.astype(f32), b_im.astype(f32)))
    c_mat = (c_re.astype(f32), c_im.astype(f32))
    eye2 = jnp.eye(2, dtype=f32)
    half = SSM_PW // 2

    def slab_rows(first, second):
        t = jnp.stack([first, second], axis=-3)
        lead = t.shape[:-4]
        t = t.reshape(*lead, 2, 2, SSM_PAIRS, 2, p)
        nl = len(lead)
        t = t.transpose(*range(nl), nl + 2, nl, nl + 1, nl + 3, nl + 4)
        return t.reshape(*lead, SSM_PAIRS, SSM_PW)

    def embed(first, second):
        t = jnp.stack([first, second], axis=1)
        x = t.shape[-1]
        t = t.reshape(2, 2, SSM_PAIRS, 2, p, x).transpose(2, 3, 5, 0, 1, 4)
        t = t[:, :, :, :, :, None, :] * eye2[None, :, None, None, None, :, None]
        return t.reshape(SSM_PAIRS, 2 * x, SSM_PW)

    ar = slab_rows(apow[0], apow[0])
    ai = slab_rows(apow[1], apow[1])
    rev = lc - 1 - jnp.arange(lc)
    fwd = jnp.arange(lc)

    def per_step(t, k_fwd, k_bwd):
        t = jnp.concatenate([t[k_fwd][..., :half], t[k_bwd][..., half:]], axis=-1)
        return t.transpose(1, 0, 2)[:, :, None, :]

    b4 = embed(b_bar[0], b_bar[1])[:, None]
    b4s = embed(-b_bar[1], b_bar[0])[:, None]
    bx = b4 * per_step(ar, rev, fwd) + b4s * per_step(ai, rev, fwd)
    bx = bx.reshape(SSM_PAIRS, SSM_PW, SSM_PW)
    ct = (c_mat[0].transpose(0, 1, 3, 2), c_mat[1].transpose(0, 1, 3, 2))
    c4 = embed(ct[0], -ct[1])[:, None]
    c4s = embed(-ct[1], -ct[0])[:, None]
    cyt = c4 * per_step(ar, fwd + 1, lc - fwd) + c4s * per_step(ai, fwd + 1, lc - fwd)
    cyt = cyt.reshape(SSM_PAIRS, SSM_PW, SSM_PW)

    def kern(d):
        ab = _cmul((apow[0][:lc, d][..., None], apow[1][:lc, d][..., None]), (b_bar[0][d][None], b_bar[1][d][None]))
        return (jnp.einsum('gmp,kgpn->kgmn', c_mat[0][d], ab[0], precision=hp)
                - jnp.einsum('gmp,kgpn->kgmn', c_mat[1][d], ab[1], precision=hp))

    kf, kb = kern(0), kern(1)
    strip = jnp.concatenate([kb[:0:-1], (kf[0] + kb[0])[None], kf[1:]], axis=0)
    strip = strip.reshape(2 * lc - 1, SSM_PAIRS, 2, m, m).transpose(1, 2, 4, 0, 3)
    strip = strip[:, :, :, :, None, :] * eye2[None, :, None, None, :, None]
    strip = strip.reshape(SSM_PAIRS, 2 * m, (2 * lc - 1) * 2 * m)
    tz = jnp.stack([strip[:, :, (lc - 1 - s) * 2 * m:(lc - 1 - s) * 2 * m + SSM_PW] for s in range(lc)], axis=1)
    tz = tz.reshape(SSM_PAIRS, SSM_PW, SSM_PW)

    a16p = slab_rows(apow[0][lc], apow[1][lc])[:, None, :]
    return tz.astype(bf16), bx.astype(bf16), cyt.astype(bf16), a16p


def _ssm_kernel(u_ref, tz_ref, bx_ref, cyt_ref, a_ref, h0_ref, y_ref, fin_ref, x_scr, s_scr, y_scr, *, n_seq, n_chunks):
    r = n_seq * n_chunks
    pc = 2 * SSM_GROUP
    sw = 2 * SSM_STATE
    xs = [u_ref[pl.ds(s, r, stride=SSM_CHUNK), :] for s in range(SSM_CHUNK)]
    for kq in range(SSM_QUAD):
        u = jnp.concatenate([x[:, kq * pc:(kq + 1) * pc] for x in xs], axis=-1).astype(bf16)
        x = jnp.dot(u, bx_ref[kq], preferred_element_type=f32)
        for i in range(4):
            x_scr[4 * kq + i] = x[:, i * sw:(i + 1) * sw]
        y_scr[:, kq * SSM_PW:(kq + 1) * SSM_PW] = jnp.dot(u, tz_ref[kq], preferred_element_type=f32)

    def scan(direction):
        offs = [kq * SSM_PW + 2 * direction * sw for kq in range(SSM_QUAD)]
        slabs = [4 * kq + 2 * direction for kq in range(SSM_QUAD)]
        coef = [(a_ref[kq][:, 2 * direction * sw:2 * direction * sw + sw],
                 a_ref[kq][:, 2 * direction * sw + sw:2 * direction * sw + 2 * sw]) for kq in range(SSM_QUAD)]

        def body(i, carry):
            c = i if direction == 0 else n_chunks - 1 - i
            rows = pl.ds(c, n_seq, stride=n_chunks)
            out = []
            for kq in range(SSM_QUAD):
                sr, si = carry[2 * kq], carry[2 * kq + 1]
                ar, ai = coef[kq]
                re, im = slabs[kq], slabs[kq] + 1
                s_scr[re, rows, :] = sr
                s_scr[im, rows, :] = si
                xr = x_scr[re, rows, :]
                xi = x_scr[im, rows, :]
                out += [ar * sr - ai * si + xr, ar * si + ai * sr + xi]
            return tuple(out)

        init = []
        for lo in offs:
            init += [h0_ref[:, lo:lo + sw], h0_ref[:, lo + sw:lo + 2 * sw]]
        fin = lax.fori_loop(0, n_chunks, body, tuple(init))
        for kq, lo in enumerate(offs):
            fin_ref[:, lo:lo + sw] = fin[2 * kq]
            fin_ref[:, lo + sw:lo + 2 * sw] = fin[2 * kq + 1]

    scan(0)
    scan(1)
    ys = []
    for kq in range(SSM_QUAD):
        cols = slice(kq * SSM_PW, (kq + 1) * SSM_PW)
        s_in = jnp.concatenate([s_scr[4 * kq + i] for i in range(4)], axis=-1).astype(bf16)
        ys.append(y_scr[:, cols] + lax.dot_general(s_in, cyt_ref[kq], (((1,), (1,)), ((), ())),
                                                   preferred_element_type=f32))
    for t in range(SSM_CHUNK):
        y_ref[pl.ds(t, r, stride=SSM_CHUNK), :] = jnp.concatenate([y[:, t * pc:(t + 1) * pc] for y in ys], axis=-1)


def ssm_scan(z, row0, n_seq, seq, tables, h0):
    tz, bx, cy, a16p = tables
    n_chunks = seq // SSM_CHUNK
    rows = n_seq * seq
    r = n_seq * n_chunks
    lanes = SSM_QUAD * 2 * SSM_GROUP
    u_blk = (3 * ATTN_W) // lanes
    qw = SSM_QUAD * SSM_PW
    table = pl.BlockSpec((SSM_QUAD, SSM_PW, SSM_PW), lambda q: (q, 0, 0))
    return pl.pallas_call(
        functools.partial(_ssm_kernel, n_seq=n_seq, n_chunks=n_chunks),
        grid=(SSM_PAIRS // SSM_QUAD,),
        in_specs=[pl.BlockSpec((rows, lanes), lambda q: (row0 // rows, u_blk + q)),
                  table, table, table,
                  pl.BlockSpec((SSM_QUAD, 1, SSM_PW), lambda q: (q, 0, 0)),
                  pl.BlockSpec((n_seq, qw), lambda q: (0, q))],
        out_specs=[pl.BlockSpec((rows, lanes), lambda q: (0, q)),
                   pl.BlockSpec((n_seq, qw), lambda q: (0, q))],
        out_shape=[jax.ShapeDtypeStruct((rows, SSM_W), f32),
                   jax.ShapeDtypeStruct((n_seq, SSM_PAIRS * SSM_PW), f32)],
        scratch_shapes=[pltpu.VMEM((4 * SSM_QUAD, r, 2 * SSM_STATE), f32),
                        pltpu.VMEM((4 * SSM_QUAD, r, 2 * SSM_STATE), f32), pltpu.VMEM((r, qw), f32)],
        compiler_params=_cparams(("parallel",)),
        name="ssm_scan",
    )(z, tz, bx, cy, a16p, h0)


def state_to_slabs(st):
    batch = st.shape[0]
    t = st.reshape(batch, 2, SSM_PAIRS, 2, SSM_STATE, 2).transpose(0, 2, 1, 5, 3, 4)
    return t.reshape(batch, SSM_PAIRS * SSM_PW)


def state_from_slabs(fin):
    batch = fin.shape[0]
    t = fin.reshape(batch, SSM_PAIRS, 2, 2, 2, SSM_STATE)
    return t.transpose(0, 2, 1, 4, 5, 3).reshape(batch, 2, SSM_GROUPS, SSM_STATE, 2)


def _glu_kernel(ya_ref, yb_ref, u_ref, d_ref, w_ref, b_ref, o_ref, *, tiles_a):
    y = _pick(pl.program_id(0) < tiles_a, ya_ref, yb_ref) + d_ref[...] * u_ref[...]
    z = jnp.dot(jax.nn.gelu(y).astype(bf16), w_ref[...], preferred_element_type=f32) + b_ref[...]
    o_ref[...] = (z[:, :SSM_W] * jax.nn.sigmoid(z[:, SSM_W:])).astype(bf16)


def ssm_glu(y_a, y_b, z, d_skip, w_glu, b_glu, tm):
    tiles_a, tiles_b = y_a.shape[0] // tm, y_b.shape[0] // tm
    m_total = y_a.shape[0] + y_b.shape[0]
    u_blk = (3 * ATTN_W) // SSM_W
    return pl.pallas_call(
        functools.partial(_glu_kernel, tiles_a=tiles_a),
        grid=(tiles_a + tiles_b,),
        in_specs=[
            *_two_part(tm, SSM_W, tiles_a, tiles_b),
            pl.BlockSpec((tm, SSM_W), lambda i: (i, u_blk)),
            pl.BlockSpec((1, SSM_W), lambda i: (0, 0)),
            pl.BlockSpec((SSM_W, 2 * SSM_W), lambda i: (0, 0)),
            pl.BlockSpec((1, 2 * SSM_W), lambda i: (0, 0)),
        ],
        out_specs=pl.BlockSpec((tm, SSM_W), lambda i: (i, 0)),
        out_shape=jax.ShapeDtypeStruct((m_total, SSM_W), bf16),
        compiler_params=_cparams(("arbitrary",)),
        name="ssm_glu",
    )(y_a, y_b, z, d_skip, w_glu, b_glu)


def _pool_kernel(p_ref, w_ref, sc_ref, o_ref, pad_ref, *, seq):
    zeros = jnp.zeros((POOL_PAD, POOL_W), f32)
    pad_ref[0:POOL_PAD, :] = zeros
    pad_ref[POOL_PAD + seq:, :] = zeros
    pad_ref[POOL_PAD:POOL_PAD + seq, :] = p_ref[...]
    t = lax.broadcasted_iota(jnp.int32, (seq, 1), 0)
    for g, win in enumerate(POOL_WINDOWS):
        cols = slice(g * POOL_GROUP, (g + 1) * POOL_GROUP)
        total = jnp.zeros((seq, POOL_GROUP), f32)
        for d in range(-(win // 2), win - win // 2):
            total = total + pad_ref[POOL_PAD + d:POOL_PAD + d + seq, cols]
        lo = jnp.clip(t - win // 2, 0, seq)
        hi = jnp.clip(t - win // 2 + win, 0, seq)
        mixed = total / (hi - lo).astype(f32) - p_ref[:, cols]
        out = jnp.dot(mixed.astype(bf16), w_ref[g].astype(bf16), preferred_element_type=f32)
        o_ref[:, cols] = (out * sc_ref[:, cols]).astype(bf16)


def pool_mixer(z, row0, batch, seq, w_pool, pool_scale):
    p_blk = (3 * ATTN_W + SSM_W) // POOL_W
    blk0 = row0 // seq
    return pl.pallas_call(
        functools.partial(_pool_kernel, seq=seq),
        grid=(batch,),
        in_specs=[
            pl.BlockSpec((seq, POOL_W), lambda b: (blk0 + b, p_blk)),
            pl.BlockSpec((len(POOL_WINDOWS), POOL_GROUP, POOL_GROUP), lambda b: (0, 0, 0)),
            pl.BlockSpec((1, POOL_W), lambda b: (0, 0)),
        ],
        out_specs=pl.BlockSpec((seq, POOL_W), lambda b: (b, 0)),
        out_shape=jax.ShapeDtypeStruct((batch * seq, POOL_W), bf16),
        scratch_shapes=[pltpu.VMEM((seq + 2 * POOL_PAD, POOL_W), f32)],
        compiler_params=_cparams(("parallel",)),
        name="pool_mixer",
    )(z, w_pool, pool_scale)


def _out_kernel(aa_ref, ab_ref, s_ref, pa_ref, pb_ref, *refs, n_x, tiles_a):
    x_refs = refs[:n_x]
    mod_ref, w_ref, o_ref = refs[n_x:]
    first = pl.program_id(0) < tiles_a
    acc = jnp.dot(_pick(first, aa_ref, ab_ref), w_ref[0:ATTN_W, :], preferred_element_type=f32)
    acc = acc + jnp.dot(s_ref[...], w_ref[ATTN_W:ATTN_W + SSM_W, :], preferred_element_type=f32)
    acc = acc + jnp.dot(_pick(first, pa_ref, pb_ref), w_ref[ATTN_W + SSM_W:, :], preferred_element_type=f32)
    o_ref[...] = _rows(x_refs, tiles_a) + mod_ref[0][2:3] * acc


def out_projection(a_a, a_b, s, p_a, p_b, xs, mod, w, row_of_tile, tm, tn=512):
    m_total = sum(x.shape[0] for x in xs)
    d = xs[0].shape[1]
    tiles_a, tiles_b = a_a.shape[0] // tm, a_b.shape[0] // tm
    assert len(xs) == 1 or xs[0].shape[0] == a_a.shape[0]
    return pl.pallas_call(
        functools.partial(_out_kernel, n_x=len(xs), tiles_a=tiles_a),
        grid=(m_total // tm, d // tn),
        in_specs=[
            *_two_part(tm, ATTN_W, tiles_a, tiles_b),
            pl.BlockSpec((tm, SSM_W), lambda i, j: (i, 0)),
            *_two_part(tm, POOL_W, tiles_a, tiles_b),
            *_row_specs(xs, tm, tn, lambda j: j),
            pl.BlockSpec((1, 6, tn), lambda i, j: (row_of_tile(i), 0, j)),
            pl.BlockSpec((d, tn), lambda i, j: (0, j)),
        ],
        out_specs=pl.BlockSpec((tm, tn), lambda i, j: (i, j)),
        out_shape=jax.ShapeDtypeStruct((m_total, d), f32),
        compiler_params=_cparams(("arbitrary", "arbitrary")),
        name="out_projection",
    )(a_a, a_b, s, p_a, p_b, *xs, mod, w)


def _ffn_kernel(x_ref, mod_ref, g_ref, wg_ref, wu_ref, wo_ref, fg_ref, *refs, n_out, tiles_a, final_norm):
    o_refs = refs[:n_out]
    h_ref, acc_ref = refs[n_out:]
    i, j = pl.program_id(0), pl.program_id(1)
    last = j == pl.num_programs(1) - 1

    @pl.when(j == 0)
    def _():
        m = mod_ref[0]
        x = x_ref[...]
        h_ref[...] = (_rms(x, g_ref[...]) * (1.0 + m[4:5]) + m[3:4]).astype(bf16)
        acc_ref[...] = x

    h = h_ref[...]
    gate = jnp.dot(h, wg_ref[...], preferred_element_type=f32)
    up = jnp.dot(h, wu_ref[...], preferred_element_type=f32)
    act = (gate * jax.nn.sigmoid(gate) * up).astype(bf16)
    acc_ref[...] += mod_ref[0][5:6] * jnp.dot(act, wo_ref[...], preferred_element_type=f32)

    def result():
        return _rms(acc_ref[...], fg_ref[...]) if final_norm else acc_ref[...]

    if n_out == 1:
        @pl.when(last)
        def _():
            o_refs[0][...] = result()
    else:
        @pl.when(last & (i < tiles_a))
        def _():
            o_refs[0][...] = result()

        @pl.when(last & (i >= tiles_a))
        def _():
            o_refs[1][...] = result()


def ffn(x, mod, g, w_in, w_out, final_g, row_of_tile, tm, th, final_norm, split_rows=None):
    m_total, d = x.shape
    hidden = w_out.shape[0]
    nh = hidden // th
    if split_rows is None:
        tiles_a = m_total // tm
        out_specs = [pl.BlockSpec((tm, d), lambda i, j: (i, 0))]
        out_shape = [jax.ShapeDtypeStruct((m_total, d), f32)]
    else:
        tiles_a = split_rows // tm
        out_specs = list(_two_part(tm, d, tiles_a, m_total // tm - tiles_a))
        out_shape = [jax.ShapeDtypeStruct((split_rows, d), f32), jax.ShapeDtypeStruct((m_total - split_rows, d), f32)]
    return pl.pallas_call(
        functools.partial(_ffn_kernel, n_out=len(out_specs), tiles_a=tiles_a, final_norm=final_norm),
        grid=(m_total // tm, nh),
        in_specs=[
            pl.BlockSpec((tm, d), lambda i, j: (i, 0)),
            pl.BlockSpec((1, 6, d), lambda i, j: (row_of_tile(i), 0, 0)),
            pl.BlockSpec((1, d), lambda i, j: (0, 0)),
            pl.BlockSpec((d, th), lambda i, j: (0, j)),
            pl.BlockSpec((d, th), lambda i, j: (0, nh + j)),
            pl.BlockSpec((th, d), lambda i, j: (j, 0)),
            pl.BlockSpec((1, d), lambda i, j: (0, 0)),
        ],
        out_specs=out_specs,
        out_shape=out_shape,
        scratch_shapes=[pltpu.VMEM((tm, d), bf16), pltpu.VMEM((tm, d), f32)],
        compiler_params=_cparams(("arbitrary", "arbitrary")),
        name="ffn",
    )(x, mod, g, w_in, w_in, w_out, final_g)


def kernel(x_prompt, x_sample, c, cache_k, cache_v, state_ssm, c_ctx, w_mod, b_mod, norm1_g, norm2_g, w_in,
           attn_rpb, ssm_a_re, ssm_a_im, ssm_log_dt, ssm_b_re, ssm_b_im, ssm_c_re, ssm_c_im, ssm_d, ssm_w_glu,
           ssm_b_glu, pool_w, pool_scale, w_out, ffn_w_in, ffn_w_out, final_norm_g):
    batch, seq, d = x_prompt.shape
    dec_batch, dec_seq, _ = x_sample.shape
    depth = w_in.shape[0]
    m_ctx = batch * seq
    tm = 512
    assert m_ctx % tm == 0 and dec_seq % tm == 0 and dec_batch + 1 <= 8

    def row_of_tile(i):
        return jnp.where(i < m_ctx // tm, 0, 1 + (i - m_ctx // tm) // (dec_seq // tm))

    cond = jnp.concatenate([c_ctx[None, :], c, jnp.zeros((8 - 1 - dec_batch, d), f32)], axis=0)
    mod = modulation_all(cond, w_mod, b_mod)
    mod = mod[:, :1 + dec_batch].reshape(depth, 1 + dec_batch, 6, d)

    xs = (x_prompt.reshape(m_ctx, d), x_sample.reshape(dec_batch * dec_seq, d))
    h0_ctx = jnp.zeros((batch, SSM_PAIRS * SSM_PW), f32)

    caches, st_out = None, []
    for l in range(depth):
        last = l == depth - 1
        z = in_projection(xs, mod[l], norm1_g[l][None], w_in[l].astype(bf16), row_of_tile, tm)

        a_ctx, *caches = context_attention(z, batch, seq, l, depth, caches)
        a_lat = latent_attention(z, m_ctx, dec_batch, dec_seq, cache_k, cache_v, l, latent_bias_table(attn_rpb[l]))

        tables = ssm_tables(ssm_a_re[l], ssm_a_im[l], ssm_log_dt[l], ssm_b_re[l], ssm_b_im[l],
                            ssm_c_re[l], ssm_c_im[l])
        y_ctx, fin = ssm_scan(z, 0, batch, seq, tables, h0_ctx)
        y_lat, _ = ssm_scan(z, m_ctx, dec_batch, dec_seq, tables, state_to_slabs(state_ssm[:, l]))
        st_out.append(state_from_slabs(fin))
        s_out = ssm_glu(y_ctx, y_lat, z, ssm_d[l][None], ssm_w_glu[l].astype(bf16), ssm_b_glu[l][None], tm)

        p_ctx = pool_mixer(z, 0, batch, seq, pool_w[l], pool_scale[l][None])
        p_lat = pool_mixer(z, m_ctx, dec_batch, dec_seq, pool_w[l], pool_scale[l][None])

        x = out_projection(a_ctx, a_lat, s_out, p_ctx, p_lat, xs, mod[l], w_out[l].astype(bf16), row_of_tile, tm)
        xs = ffn(x, mod[l], norm2_g[l][None], ffn_w_in[l].astype(bf16), ffn_w_out[l].astype(bf16),
                 final_norm_g[None], row_of_tile, tm, 512, final_norm=last, split_rows=m_ctx if last else None)

    y_prompt = xs[0].reshape(batch, seq, d)
    y_sample = xs[1].reshape(dec_batch, dec_seq, d)
    return (y_prompt, y_sample, caches[0], caches[1], jnp.stack(st_out, axis=1))
```

```python
import functools

import jax
import jax.numpy as jnp
import numpy as np
from jax import lax
from jax.experimental import pallas as pl
from jax.experimental.pallas import tpu as pltpu

f32 = jnp.float32
bf16 = jnp.bfloat16

D_MODEL = 2048
N_HEADS = 16
HEAD_DIM = 64
ATTN_W = N_HEADS * HEAD_DIM
SSM_W = 512
SSM_GROUP = 16
SSM_GROUPS = 32
SSM_STATE = 64
POOL_W = 512
POOL_WINDOWS = (2, 4, 8, 16)
POOL_GROUP = 128
POOL_PAD = 16
IN_W = 3 * ATTN_W + SSM_W + POOL_W
GRID_W = 64
WIN_ROWS_MAX = 8
WIN_COLS = 16
RMS_EPS = 1e-6
NEG_INF = -1e30

LAT_BLOCK_ROWS = 4
SSM_CHUNK = 16
SSM_PAIRS = SSM_GROUPS // 2
SSM_QUAD = 4
SSM_PW = 2 * SSM_CHUNK * SSM_GROUP

VMEM_LIMIT = 56 * 1024 * 1024


def _cparams(sem):
    return pltpu.CompilerParams(dimension_semantics=sem, vmem_limit_bytes=VMEM_LIMIT)


def _rms(x, g):
    return x * lax.rsqrt(jnp.mean(x * x, axis=-1, keepdims=True) + RMS_EPS) * g


def _two_part(tm, width, tiles_a, tiles_b, col=lambda *_: 0):
    return (pl.BlockSpec((tm, width), lambda i, *r: (jnp.minimum(i, tiles_a - 1), col(*r))),
            pl.BlockSpec((tm, width), lambda i, *r: (jnp.clip(i - tiles_a, 0, tiles_b - 1), col(*r))))


def _row_specs(parts, tm, width, col=lambda *_: 0):
    if len(parts) == 1:
        return (pl.BlockSpec((tm, width), lambda i, *r: (i, col(*r))),)
    return _two_part(tm, width, parts[0].shape[0] // tm, parts[1].shape[0] // tm, col)


def _pick(first, a_ref, b_ref):
    return jnp.where(first, a_ref[...], b_ref[...])


def _rows(refs, tiles_a):
    if len(refs) == 1:
        return refs[0][...]
    return _pick(pl.program_id(0) < tiles_a, *refs)


def _mod_kernel(c_ref, w_ref, b_ref, o_ref):
    c = c_ref[...]
    s = (c * jax.nn.sigmoid(c)).astype(bf16)
    o_ref[0] = jnp.dot(s, w_ref[0].astype(bf16), preferred_element_type=f32) + b_ref[0]


def modulation_all(cond, w_mod, b_mod, tn=1024):
    n_layers, d, n = w_mod.shape
    return pl.pallas_call(
        _mod_kernel,
        grid=(n_layers, n // tn),
        in_specs=[
            pl.BlockSpec((8, d), lambda l, j: (0, 0)),
            pl.BlockSpec((1, d, tn), lambda l, j: (l, 0, j)),
            pl.BlockSpec((1, 1, tn), lambda l, j: (l, 0, j)),
        ],
        out_specs=pl.BlockSpec((1, 8, tn), lambda l, j: (l, 0, j)),
        out_shape=jax.ShapeDtypeStruct((n_layers, 8, n), f32),
        compiler_params=_cparams(("parallel", "parallel")),
        name="modulation",
    )(cond, w_mod, b_mod.reshape(n_layers, 1, n))


def _in_kernel(*refs, n_x, tiles_a, tn):
    x_refs = refs[:n_x]
    mod_ref, g_ref, w_ref, o_ref = refs[n_x:]
    m = mod_ref[0]
    y = _rms(_rows(x_refs, tiles_a), g_ref[...])
    h = (y * (1.0 + m[1:2]) + m[0:1]).astype(bf16)
    for c in range(w_ref.shape[1] // tn):
        cols = slice(c * tn, (c + 1) * tn)
        o_ref[:, cols] = jnp.dot(h, w_ref[:, cols], preferred_element_type=f32)


def in_projection(xs, mod, g, w, row_of_tile, tm, tn=512):
    m_total = sum(x.shape[0] for x in xs)
    d = xs[0].shape[1]
    n = w.shape[1]
    return pl.pallas_call(
        functools.partial(_in_kernel, n_x=len(xs), tiles_a=xs[0].shape[0] // tm, tn=tn),
        grid=(m_total // tm,),
        in_specs=[
            *_row_specs(xs, tm, d),
            pl.BlockSpec((1, 6, d), lambda i: (row_of_tile(i), 0, 0)),
            pl.BlockSpec((1, d), lambda i: (0, 0)),
            pl.BlockSpec((d, n), lambda i: (0, 0), pipeline_mode=pl.Buffered(1)),
        ],
        out_specs=pl.BlockSpec((tm, n), lambda i: (i, 0)),
        out_shape=jax.ShapeDtypeStruct((m_total, n), f32),
        compiler_params=_cparams(("arbitrary",)),
        name="in_projection",
    )(*xs, mod, g, w)


def _attn_ctx_kernel(q_ref, k_ref, v_ref, *refs):
    o_ref, kc_ref, vc_ref = refs[-3:]
    scale = HEAD_DIM ** -0.5
    outs = []
    for h in range(N_HEADS):
        sl = slice(h * HEAD_DIM, (h + 1) * HEAD_DIM)
        q = q_ref[:, sl]
        k = k_ref[:, sl]
        v = v_ref[:, sl]
        kc_ref[0, 0, h] = k
        vc_ref[0, 0, h] = v
        s = lax.dot_general(q.astype(bf16), k.astype(bf16), (((1,), (1,)), ((), ())),
                            preferred_element_type=f32) * scale
        e = jnp.exp(s - jnp.max(s, axis=-1, keepdims=True))
        o = jnp.dot(e.astype(bf16), v.astype(bf16), preferred_element_type=f32)
        outs.append(o / jnp.sum(e, axis=-1, keepdims=True))
        if h % 2 == 1:
            o_ref[:, (h - 1) * HEAD_DIM:(h + 1) * HEAD_DIM] = jnp.concatenate(outs[-2:], axis=-1).astype(bf16)


def context_attention(z, batch, seq, layer, depth, caches=None):
    col = lambda blk: pl.BlockSpec((seq, ATTN_W), lambda b: (b, blk))
    cache = pl.BlockSpec((1, 1, N_HEADS, seq, HEAD_DIM), lambda b: (b, layer, 0, 0, 0))
    cache_shape = jax.ShapeDtypeStruct((batch, depth, N_HEADS, seq, HEAD_DIM), f32)
    prev = () if caches is None else tuple(caches)
    return pl.pallas_call(
        _attn_ctx_kernel,
        grid=(batch,),
        in_specs=[col(0), col(1), col(2)] + [pl.BlockSpec(memory_space=pl.ANY)] * len(prev),
        out_specs=[pl.BlockSpec((seq, ATTN_W), lambda b: (b, 0)), cache, cache],
        out_shape=[jax.ShapeDtypeStruct((batch * seq, ATTN_W), bf16), cache_shape, cache_shape],
        input_output_aliases={3 + i: 1 + i for i in range(len(prev))},
        compiler_params=_cparams(("parallel",)),
        name="context_attention",
    )(z, z, z, *prev)


def _attn_lat_kernel(q_ref, k_ref, v_ref, ck_ref, cv_ref, t_ref, o_ref, bias_scr, *, rows, wr):
    scale = HEAD_DIM ** -0.5
    n_loc = wr * GRID_W
    for h in range(2):
        for d in range(wr):
            for j in range(wr):
                bias_scr[h, d, :, j * GRID_W:(j + 1) * GRID_W] = t_ref[h, j - d + WIN_ROWS_MAX - 1]

    nt = (((1,), (1,)), ((), ()))
    br = LAT_BLOCK_ROWS
    nq = br * GRID_W

    def block(blk, carry):
        q_rows = pl.ds(pl.multiple_of(blk * nq, nq), nq)
        outs = []
        for h in range(2):
            sl = slice(h * HEAD_DIM, (h + 1) * HEAD_DIM)
            q = q_ref[q_rows, sl].astype(bf16)
            s_ctx = lax.dot_general(q, ck_ref[0, 0, h].astype(bf16), nt, preferred_element_type=f32) * scale
            s_loc, windows = [], []
            for i in range(br):
                r = blk * br + i
                rs = jnp.clip(r - wr // 2, 0, rows - wr)
                win = pl.ds(pl.multiple_of(rs * GRID_W, GRID_W), n_loc)
                windows.append(win)
                s = lax.dot_general(q[i * GRID_W:(i + 1) * GRID_W], k_ref[win, sl].astype(bf16), nt,
                                    preferred_element_type=f32)
                s_loc.append(s * scale + bias_scr[h, r - rs])
            s_loc = jnp.concatenate(s_loc, axis=0)
            m = jnp.maximum(jnp.max(s_loc, axis=-1, keepdims=True), jnp.max(s_ctx, axis=-1, keepdims=True))
            e_loc = jnp.exp(s_loc - m)
            e_ctx = jnp.exp(s_ctx - m)
            den = jnp.sum(e_loc, axis=-1, keepdims=True) + jnp.sum(e_ctx, axis=-1, keepdims=True)
            e_loc = e_loc.astype(bf16)
            o_loc = [jnp.dot(e_loc[i * GRID_W:(i + 1) * GRID_W], v_ref[windows[i], sl].astype(bf16),
                             preferred_element_type=f32) for i in range(br)]
            o = jnp.dot(e_ctx.astype(bf16), cv_ref[0, 0, h].astype(bf16), preferred_element_type=f32)
            outs.append((o + jnp.concatenate(o_loc, axis=0)) / den)
        o_ref[q_rows, :] = jnp.concatenate(outs, axis=-1).astype(bf16)
        return carry

    lax.fori_loop(0, rows // br, block, 0)


def latent_bias_table(rpb):
    cols = np.arange(GRID_W)
    col_start = np.clip(cols - WIN_COLS // 2, 0, GRID_W - WIN_COLS)
    valid = (cols[None, :] >= col_start[:, None]) & (cols[None, :] < col_start[:, None] + WIN_COLS)
    dc = cols[None, :] - cols[:, None] + WIN_COLS - 1
    assert np.all((dc[valid] >= 0) & (dc[valid] < 2 * WIN_COLS - 1))
    onehot = ((np.arange(2 * WIN_COLS - 1)[:, None, None] == dc[None]) & valid[None]).astype(np.float32)
    t = jnp.einsum('hrd,dqk->hrqk', rpb.astype(f32), jnp.asarray(onehot), precision=lax.Precision.HIGHEST)
    return jnp.where(jnp.asarray(valid)[None, None], t, NEG_INF)


def latent_attention(z, row0, batch, seq, cache_k, cache_v, layer, table):
    lanes = 2 * HEAD_DIM
    n_pairs = N_HEADS // 2
    rows = seq // GRID_W
    wr = min(WIN_ROWS_MAX, rows)
    past = cache_k.shape[3]
    blk0 = row0 // seq
    n_dr = 2 * WIN_ROWS_MAX - 1
    col = lambda off: pl.BlockSpec((seq, lanes), lambda b, p: (blk0 + b, off + p))
    ctx = pl.BlockSpec((1, 1, 2, past, HEAD_DIM), lambda b, p: (b, layer, p, 0, 0))
    return pl.pallas_call(
        functools.partial(_attn_lat_kernel, rows=rows, wr=wr),
        grid=(batch, n_pairs),
        in_specs=[col(0), col(n_pairs), col(2 * n_pairs), ctx, ctx,
                  pl.BlockSpec((2, n_dr, GRID_W, GRID_W), lambda b, p: (p, 0, 0, 0))],
        out_specs=pl.BlockSpec((seq, lanes), lambda b, p: (b, p)),
        out_shape=jax.ShapeDtypeStruct((batch * seq, ATTN_W), bf16),
        scratch_shapes=[pltpu.VMEM((2, wr, GRID_W, wr * GRID_W), f32)],
        compiler_params=_cparams(("parallel", "parallel")),
        name="latent_attention",
    )(z, z, z, cache_k, cache_v, table)


def _cmul(a, b):
    return a[0] * b[0] - a[1] * b[1], a[0] * b[1] + a[1] * b[0]


def ssm_tables(a_re, a_im, log_dt, b_re, b_im, c_re, c_im):
    hp = lax.Precision.HIGHEST
    lc, p, m = SSM_CHUNK, SSM_STATE, SSM_GROUP
    lr, li = a_re.astype(f32), a_im.astype(f32)
    dt = jnp.exp(log_dt.astype(f32))[..., None]
    sr, si = lr * dt, li * dt
    k = jnp.arange(lc + 1, dtype=f32)[:, None, None, None]
    mag = jnp.exp(k * sr[None])
    apow = (mag * jnp.cos(k * si[None]), mag * jnp.sin(k * si[None]))
    a_bar = (apow[0][1], apow[1][1])
    den = lr * lr + li * li
    q = (((a_bar[0] - 1.0) * lr + a_bar[1] * li) / den, (a_bar[1] * lr - (a_bar[0] - 1.0) * li) / den)
    b_bar = _cmul((q[0][..., None], q[1][..., None]), (b_re.astype(f32), b_im.astype(f32)))
    c_mat = (c_re.astype(f32), c_im.astype(f32))
    eye2 = jnp.eye(2, dtype=f32)
    half = SSM_PW // 2

    def slab_rows(first, second):
        t = jnp.stack([first, second], axis=-3)
        lead = t.shape[:-4]
        t = t.reshape(*lead, 2, 2, SSM_PAIRS, 2, p)
        nl = len(lead)
        t = t.transpose(*range(nl), nl + 2, nl, nl + 1, nl + 3, nl + 4)
        return t.reshape(*lead, SSM_PAIRS, SSM_PW)

    def embed(first, second):
        t = jnp.stack([first, second], axis=1)
        x = t.shape[-1]
        t = t.reshape(2, 2, SSM_PAIRS, 2, p, x).transpose(2, 3, 5, 0, 1, 4)
        t = t[:, :, :, :, :, None, :] * eye2[None, :, None, None, None, :, None]
        return t.reshape(SSM_PAIRS, 2 * x, SSM_PW)

    ar = slab_rows(apow[0], apow[0])
    ai = slab_rows(apow[1], apow[1])
    rev = lc - 1 - jnp.arange(lc)
    fwd = jnp.arange(lc)

    def per_step(t, k_fwd, k_bwd):
        t = jnp.concatenate([t[k_fwd][..., :half], t[k_bwd][..., half:]], axis=-1)
        return t.transpose(1, 0, 2)[:, :, None, :]

    b4 = embed(b_bar[0], b_bar[1])[:, None]
    b4s = embed(-b_bar[1], b_bar[0])[:, None]
    bx = b4 * per_step(ar, rev, fwd) + b4s * per_step(ai, rev, fwd)
    bx = bx.reshape(SSM_PAIRS, SSM_PW, SSM_PW)
    ct = (c_mat[0].transpose(0, 1, 3, 2), c_mat[1].transpose(0, 1, 3, 2))
    c4 = embed(ct[0], -ct[1])[:, None]
    c4s = embed(-ct[1], -ct[0])[:, None]
    cyt = c4 * per_step(ar, fwd + 1, lc - fwd) + c4s * per_step(ai, fwd + 1, lc - fwd)
    cyt = cyt.reshape(SSM_PAIRS, SSM_PW, SSM_PW)

    def kern(d):
        ab = _cmul((apow[0][:lc, d][..., None], apow[1][:lc, d][..., None]), (b_bar[0][d][None], b_bar[1][d][None]))
        return (jnp.einsum('gmp,kgpn->kgmn', c_mat[0][d], ab[0], precision=hp)
                - jnp.einsum('gmp,kgpn->kgmn', c_mat[1][d], ab[1], precision=hp))

    kf, kb = kern(0), kern(1)
    strip = jnp.concatenate([kb[:0:-1], (kf[0] + kb[0])[None], kf[1:]], axis=0)
    strip = strip.reshape(2 * lc - 1, SSM_PAIRS, 2, m, m).transpose(1, 2, 4, 0, 3)
    strip = strip[:, :, :, :, None, :] * eye2[None, :, None, None, :, None]
    strip = strip.reshape(SSM_PAIRS, 2 * m, (2 * lc - 1) * 2 * m)
    tz = jnp.stack([strip[:, :, (lc - 1 - s) * 2 * m:(lc - 1 - s) * 2 * m + SSM_PW] for s in range(lc)], axis=1)
    tz = tz.reshape(SSM_PAIRS, SSM_PW, SSM_PW)

    a16p = slab_rows(apow[0][lc], apow[1][lc])[:, None, :]
    return tz.astype(bf16), bx.astype(bf16), cyt.astype(bf16), a16p


def _ssm_kernel(u_ref, tz_ref, bx_ref, cyt_ref, a_ref, h0_ref, y_ref, fin_ref, x_scr, s_scr, y_scr, *, n_seq, n_chunks):
    r = n_seq * n_chunks
    pc = 2 * SSM_GROUP
    sw = 2 * SSM_STATE
    xs = [u_ref[pl.ds(s, r, stride=SSM_CHUNK), :] for s in range(SSM_CHUNK)]
    for kq in range(SSM_QUAD):
        u = jnp.concatenate([x[:, kq * pc:(kq + 1) * pc] for x in xs], axis=-1).astype(bf16)
        x = jnp.dot(u, bx_ref[kq], preferred_element_type=f32)
        for i in range(4):
            x_scr[4 * kq + i] = x[:, i * sw:(i + 1) * sw]
        y_scr[:, kq * SSM_PW:(kq + 1) * SSM_PW] = jnp.dot(u, tz_ref[kq], preferred_element_type=f32)

    def scan(direction):
        offs = [kq * SSM_PW + 2 * direction * sw for kq in range(SSM_QUAD)]
        slabs = [4 * kq + 2 * direction for kq in range(SSM_QUAD)]
        coef = [(a_ref[kq][:, 2 * direction * sw:2 * direction * sw + sw],
                 a_ref[kq][:, 2 * direction * sw + sw:2 * direction * sw + 2 * sw]) for kq in range(SSM_QUAD)]

        def body(i, carry):
            c = i if direction == 0 else n_chunks - 1 - i
            rows = pl.ds(c, n_seq, stride=n_chunks)
            out = []
            for kq in range(SSM_QUAD):
                sr, si = carry[2 * kq], carry[2 * kq + 1]
                ar, ai = coef[kq]
                re, im = slabs[kq], slabs[kq] + 1
                s_scr[re, rows, :] = sr
                s_scr[im, rows, :] = si
                xr = x_scr[re, rows, :]
                xi = x_scr[im, rows, :]
                out += [ar * sr - ai * si + xr, ar * si + ai * sr + xi]
            return tuple(out)

        init = []
        for lo in offs:
            init += [h0_ref[:, lo:lo + sw], h0_ref[:, lo + sw:lo + 2 * sw]]
        fin = lax.fori_loop(0, n_chunks, body, tuple(init))
        for kq, lo in enumerate(offs):
            fin_ref[:, lo:lo + sw] = fin[2 * kq]
            fin_ref[:, lo + sw:lo + 2 * sw] = fin[2 * kq + 1]

    scan(0)
    scan(1)
    ys = []
    for kq in range(SSM_QUAD):
        cols = slice(kq * SSM_PW, (kq + 1) * SSM_PW)
        s_in = jnp.concatenate([s_scr[4 * kq + i] for i in range(4)], axis=-1).astype(bf16)
        ys.append(y_scr[:, cols] + lax.dot_general(s_in, cyt_ref[kq], (((1,), (1,)), ((), ())),
                                                   preferred_element_type=f32))
    for t in range(SSM_CHUNK):
        y_ref[pl.ds(t, r, stride=SSM_CHUNK), :] = jnp.concatenate([y[:, t * pc:(t + 1) * pc] for y in ys], axis=-1)


def ssm_scan(z, row0, n_seq, seq, tables, h0):
    tz, bx, cy, a16p = tables
    n_chunks = seq // SSM_CHUNK
    rows = n_seq * seq
    r = n_seq * n_chunks
    lanes = SSM_QUAD * 2 * SSM_GROUP
    u_blk = (3 * ATTN_W) // lanes
    qw = SSM_QUAD * SSM_PW
    table = pl.BlockSpec((SSM_QUAD, SSM_PW, SSM_PW), lambda q: (q, 0, 0))
    return pl.pallas_call(
        functools.partial(_ssm_kernel, n_seq=n_seq, n_chunks=n_chunks),
        grid=(SSM_PAIRS // SSM_QUAD,),
        in_specs=[pl.BlockSpec((rows, lanes), lambda q: (row0 // rows, u_blk + q)),
                  table, table, table,
                  pl.BlockSpec((SSM_QUAD, 1, SSM_PW), lambda q: (q, 0, 0)),
                  pl.BlockSpec((n_seq, qw), lambda q: (0, q))],
        out_specs=[pl.BlockSpec((rows, lanes), lambda q: (0, q)),
                   pl.BlockSpec((n_seq, qw), lambda q: (0, q))],
        out_shape=[jax.ShapeDtypeStruct((rows, SSM_W), f32),
                   jax.ShapeDtypeStruct((n_seq, SSM_PAIRS * SSM_PW), f32)],
        scratch_shapes=[pltpu.VMEM((4 * SSM_QUAD, r, 2 * SSM_STATE), f32),
                        pltpu.VMEM((4 * SSM_QUAD, r, 2 * SSM_STATE), f32), pltpu.VMEM((r, qw), f32)],
        compiler_params=_cparams(("parallel",)),
        name="ssm_scan",
    )(z, tz, bx, cy, a16p, h0)


def state_to_slabs(st):
    batch = st.shape[0]
    t = st.reshape(batch, 2, SSM_PAIRS, 2, SSM_STATE, 2).transpose(0, 2, 1, 5, 3, 4)
    return t.reshape(batch, SSM_PAIRS * SSM_PW)


def state_from_slabs(fin):
    batch = fin.shape[0]
    t = fin.reshape(batch, SSM_PAIRS, 2, 2, 2, SSM_STATE)
    return t.transpose(0, 2, 1, 4, 5, 3).reshape(batch, 2, SSM_GROUPS, SSM_STATE, 2)


def _glu_kernel(ya_ref, yb_ref, u_ref, d_ref, w_ref, b_ref, o_ref, *, tiles_a):
    y = _pick(pl.program_id(0) < tiles_a, ya_ref, yb_ref) + d_ref[...] * u_ref[...]
    z = jnp.dot(jax.nn.gelu(y).astype(bf16), w_ref[...], preferred_element_type=f32) + b_ref[...]
    o_ref[...] = (z[:, :SSM_W] * jax.nn.sigmoid(z[:, SSM_W:])).astype(bf16)


def ssm_glu(y_a, y_b, z, d_skip, w_glu, b_glu, tm):
    tiles_a, tiles_b = y_a.shape[0] // tm, y_b.shape[0] // tm
    m_total = y_a.shape[0] + y_b.shape[0]
    u_blk = (3 * ATTN_W) // SSM_W
    return pl.pallas_call(
        functools.partial(_glu_kernel, tiles_a=tiles_a),
        grid=(tiles_a + tiles_b,),
        in_specs=[
            *_two_part(tm, SSM_W, tiles_a, tiles_b),
            pl.BlockSpec((tm, SSM_W), lambda i: (i, u_blk)),
            pl.BlockSpec((1, SSM_W), lambda i: (0, 0)),
            pl.BlockSpec((SSM_W, 2 * SSM_W), lambda i: (0, 0)),
            pl.BlockSpec((1, 2 * SSM_W), lambda i: (0, 0)),
        ],
        out_specs=pl.BlockSpec((tm, SSM_W), lambda i: (i, 0)),
        out_shape=jax.ShapeDtypeStruct((m_total, SSM_W), bf16),
        compiler_params=_cparams(("arbitrary",)),
        name="ssm_glu",
    )(y_a, y_b, z, d_skip, w_glu, b_glu)


def _pool_kernel(p_ref, w_ref, sc_ref, o_ref, pad_ref, *, seq):
    zeros = jnp.zeros((POOL_PAD, POOL_W), f32)
    pad_ref[0:POOL_PAD, :] = zeros
    pad_ref[POOL_PAD + seq:, :] = zeros
    pad_ref[POOL_PAD:POOL_PAD + seq, :] = p_ref[...]
    t = lax.broadcasted_iota(jnp.int32, (seq, 1), 0)
    for g, win in enumerate(POOL_WINDOWS):
        cols = slice(g * POOL_GROUP, (g + 1) * POOL_GROUP)
        total = jnp.zeros((seq, POOL_GROUP), f32)
        for d in range(-(win // 2), win - win // 2):
            total = total + pad_ref[POOL_PAD + d:POOL_PAD + d + seq, cols]
        lo = jnp.clip(t - win // 2, 0, seq)
        hi = jnp.clip(t - win // 2 + win, 0, seq)
        mixed = total / (hi - lo).astype(f32) - p_ref[:, cols]
        out = jnp.dot(mixed.astype(bf16), w_ref[g].astype(bf16), preferred_element_type=f32)
        o_ref[:, cols] = (out * sc_ref[:, cols]).astype(bf16)


def pool_mixer(z, row0, batch, seq, w_pool, pool_scale):
    p_blk = (3 * ATTN_W + SSM_W) // POOL_W
    blk0 = row0 // seq
    return pl.pallas_call(
        functools.partial(_pool_kernel, seq=seq),
        grid=(batch,),
        in_specs=[
            pl.BlockSpec((seq, POOL_W), lambda b: (blk0 + b, p_blk)),
            pl.BlockSpec((len(POOL_WINDOWS), POOL_GROUP, POOL_GROUP), lambda b: (0, 0, 0)),
            pl.BlockSpec((1, POOL_W), lambda b: (0, 0)),
        ],
        out_specs=pl.BlockSpec((seq, POOL_W), lambda b: (b, 0)),
        out_shape=jax.ShapeDtypeStruct((batch * seq, POOL_W), bf16),
        scratch_shapes=[pltpu.VMEM((seq + 2 * POOL_PAD, POOL_W), f32)],
        compiler_params=_cparams(("parallel",)),
        name="pool_mixer",
    )(z, w_pool, pool_scale)


def _out_kernel(aa_ref, ab_ref, s_ref, pa_ref, pb_ref, *refs, n_x, tiles_a, tn):
    x_refs = refs[:n_x]
    mod_ref, w_ref, o_ref = refs[n_x:]
    first = pl.program_id(0) < tiles_a
    a = _pick(first, aa_ref, ab_ref)
    s = s_ref[...]
    p = _pick(first, pa_ref, pb_ref)
    x = _rows(x_refs, tiles_a)
    gate = mod_ref[0][2:3]
    for c in range(w_ref.shape[1] // tn):
        cols = slice(c * tn, (c + 1) * tn)
        acc = jnp.dot(a, w_ref[0:ATTN_W, cols], preferred_element_type=f32)
        acc = acc + jnp.dot(s, w_ref[ATTN_W:ATTN_W + SSM_W, cols], preferred_element_type=f32)
        acc = acc + jnp.dot(p, w_ref[ATTN_W + SSM_W:, cols], preferred_element_type=f32)
        o_ref[:, cols] = x[:, cols] + gate[:, cols] * acc


def out_projection(a_a, a_b, s, p_a, p_b, xs, mod, w, row_of_tile, tm, tn=512):
    m_total = sum(x.shape[0] for x in xs)
    d = xs[0].shape[1]
    tiles_a, tiles_b = a_a.shape[0] // tm, a_b.shape[0] // tm
    assert len(xs) == 1 or xs[0].shape[0] == a_a.shape[0]
    return pl.pallas_call(
        functools.partial(_out_kernel, n_x=len(xs), tiles_a=tiles_a, tn=tn),
        grid=(m_total // tm,),
        in_specs=[
            *_two_part(tm, ATTN_W, tiles_a, tiles_b),
            pl.BlockSpec((tm, SSM_W), lambda i: (i, 0)),
            *_two_part(tm, POOL_W, tiles_a, tiles_b),
            *_row_specs(xs, tm, d),
            pl.BlockSpec((1, 6, d), lambda i: (row_of_tile(i), 0, 0)),
            pl.BlockSpec((d, d), lambda i: (0, 0), pipeline_mode=pl.Buffered(1)),
        ],
        out_specs=pl.BlockSpec((tm, d), lambda i: (i, 0)),
        out_shape=jax.ShapeDtypeStruct((m_total, d), f32),
        compiler_params=_cparams(("arbitrary",)),
        name="out_projection",
    )(a_a, a_b, s, p_a, p_b, *xs, mod, w)


def _ffn_kernel(x_ref, mod_ref, g_ref, wg_ref, wu_ref, wo_ref, fg_ref, *refs, n_out, tiles_a, final_norm):
    o_refs = refs[:n_out]
    h_ref, acc_ref = refs[n_out:]
    i, j = pl.program_id(0), pl.program_id(1)
    last = j == pl.num_programs(1) - 1

    @pl.when(j == 0)
    def _():
        m = mod_ref[0]
        x = x_ref[...]
        h_ref[...] = (_rms(x, g_ref[...]) * (1.0 + m[4:5]) + m[3:4]).astype(bf16)
        acc_ref[...] = x

    h = h_ref[...]
    gate = jnp.dot(h, wg_ref[...], preferred_element_type=f32)
    up = jnp.dot(h, wu_ref[...], preferred_element_type=f32)
    act = (gate * jax.nn.sigmoid(gate) * up).astype(bf16)
    acc_ref[...] += mod_ref[0][5:6] * jnp.dot(act, wo_ref[...], preferred_element_type=f32)

    def result():
        return _rms(acc_ref[...], fg_ref[...]) if final_norm else acc_ref[...]

    if n_out == 1:
        @pl.when(last)
        def _():
            o_refs[0][...] = result()
    else:
        @pl.when(last & (i < tiles_a))
        def _():
            o_refs[0][...] = result()

        @pl.when(last & (i >= tiles_a))
        def _():
            o_refs[1][...] = result()


def ffn(x, mod, g, w_in, w_out, final_g, row_of_tile, tm, th, final_norm, split_rows=None):
    m_total, d = x.shape
    hidden = w_out.shape[0]
    nh = hidden // th
    if split_rows is None:
        tiles_a = m_total // tm
        out_specs = [pl.BlockSpec((tm, d), lambda i, j: (i, 0))]
        out_shape = [jax.ShapeDtypeStruct((m_total, d), f32)]
    else:
        tiles_a = split_rows // tm
        out_specs = list(_two_part(tm, d, tiles_a, m_total // tm - tiles_a))
        out_shape = [jax.ShapeDtypeStruct((split_rows, d), f32), jax.ShapeDtypeStruct((m_total - split_rows, d), f32)]
    return pl.pallas_call(
        functools.partial(_ffn_kernel, n_out=len(out_specs), tiles_a=tiles_a, final_norm=final_norm),
        grid=(m_total // tm, nh),
        in_specs=[
            pl.BlockSpec((tm, d), lambda i, j: (i, 0)),
            pl.BlockSpec((1, 6, d), lambda i, j: (row_of_tile(i), 0, 0)),
            pl.BlockSpec((1, d), lambda i, j: (0, 0)),
            pl.BlockSpec((d, th), lambda i, j: (0, j)),
            pl.BlockSpec((d, th), lambda i, j: (0, nh + j)),
            pl.BlockSpec((th, d), lambda i, j: (j, 0)),
            pl.BlockSpec((1, d), lambda i, j: (0, 0)),
        ],
        out_specs=out_specs,
        out_shape=out_shape,
        scratch_shapes=[pltpu.VMEM((tm, d), bf16), pltpu.VMEM((tm, d), f32)],
        compiler_params=_cparams(("arbitrary", "arbitrary")),
        name="ffn",
    )(x, mod, g, w_in, w_in, w_out, final_g)


def kernel(x_prompt, x_sample, c, cache_k, cache_v, state_ssm, c_ctx, w_mod, b_mod, norm1_g, norm2_g, w_in,
           attn_rpb, ssm_a_re, ssm_a_im, ssm_log_dt, ssm_b_re, ssm_b_im, ssm_c_re, ssm_c_im, ssm_d, ssm_w_glu,
           ssm_b_glu, pool_w, pool_scale, w_out, ffn_w_in, ffn_w_out, final_norm_g):
    batch, seq, d = x_prompt.shape
    dec_batch, dec_seq, _ = x_sample.shape
    depth = w_in.shape[0]
    m_ctx = batch * seq
    tm = 512
    assert m_ctx % tm == 0 and dec_seq % tm == 0 and dec_batch + 1 <= 8

    def row_of_tile(i):
        return jnp.where(i < m_ctx // tm, 0, 1 + (i - m_ctx // tm) // (dec_seq // tm))

    cond = jnp.concatenate([c_ctx[None, :], c, jnp.zeros((8 - 1 - dec_batch, d), f32)], axis=0)
    mod = modulation_all(cond, w_mod, b_mod)
    mod = mod[:, :1 + dec_batch].reshape(depth, 1 + dec_batch, 6, d)

    xs = (x_prompt.reshape(m_ctx, d), x_sample.reshape(dec_batch * dec_seq, d))
    h0_ctx = jnp.zeros((batch, SSM_PAIRS * SSM_PW), f32)

    caches, st_out = None, []
    for l in range(depth):
        last = l == depth - 1
        z = in_projection(xs, mod[l], norm1_g[l][None], w_in[l].astype(bf16), row_of_tile, tm)

        a_ctx, *caches = context_attention(z, batch, seq, l, depth, caches)
        a_lat = latent_attention(z, m_ctx, dec_batch, dec_seq, cache_k, cache_v, l, latent_bias_table(attn_rpb[l]))

        tables = ssm_tables(ssm_a_re[l], ssm_a_im[l], ssm_log_dt[l], ssm_b_re[l], ssm_b_im[l],
                            ssm_c_re[l], ssm_c_im[l])
        y_ctx, fin = ssm_scan(z, 0, batch, seq, tables, h0_ctx)
        y_lat, _ = ssm_scan(z, m_ctx, dec_batch, dec_seq, tables, state_to_slabs(state_ssm[:, l]))
        st_out.append(state_from_slabs(fin))
        s_out = ssm_glu(y_ctx, y_lat, z, ssm_d[l][None], ssm_w_glu[l].astype(bf16), ssm_b_glu[l][None], tm)

        p_ctx = pool_mixer(z, 0, batch, seq, pool_w[l], pool_scale[l][None])
        p_lat = pool_mixer(z, m_ctx, dec_batch, dec_seq, pool_w[l], pool_scale[l][None])

        x = out_projection(a_ctx, a_lat, s_out, p_ctx, p_lat, xs, mod[l], w_out[l].astype(bf16), row_of_tile, tm)
        xs = ffn(x, mod[l], norm2_g[l][None], ffn_w_in[l].astype(bf16), ffn_w_out[l].astype(bf16),
                 final_norm_g[None], row_of_tile, tm, 512, final_norm=last, split_rows=m_ctx if last else None)

    y_prompt = xs[0].reshape(batch, seq, d)
    y_sample = xs[1].reshape(dec_batch, dec_seq, d)
    return (y_prompt, y_sample, caches[0], caches[1], jnp.stack(st_out, axis=1))
```

```python
import functools

import jax
import jax.numpy as jnp
import numpy as np
from jax import lax
from jax.experimental import pallas as pl
from jax.experimental.pallas import tpu as pltpu

f32 = jnp.float32
bf16 = jnp.bfloat16

D_MODEL = 2048
N_HEADS = 16
HEAD_DIM = 64
ATTN_W = N_HEADS * HEAD_DIM
SSM_W = 512
SSM_GROUP = 16
SSM_GROUPS = 32
SSM_STATE = 64
POOL_W = 512
POOL_WINDOWS = (2, 4, 8, 16)
POOL_GROUP = 128
POOL_PAD = 16
IN_W = 3 * ATTN_W + SSM_W + POOL_W
GRID_W = 64
WIN_ROWS_MAX = 8
WIN_COLS = 16
RMS_EPS = 1e-6
NEG_INF = -1e30

LAT_BLOCK_ROWS = 4
SSM_CHUNK = 16
SSM_PAIRS = SSM_GROUPS // 2
SSM_QUAD = 4
SSM_PW = 2 * SSM_CHUNK * SSM_GROUP

VMEM_LIMIT = 56 * 1024 * 1024


def _cparams(sem):
    return pltpu.CompilerParams(dimension_semantics=sem, vmem_limit_bytes=VMEM_LIMIT)


def _rms(x, g):
    return x * lax.rsqrt(jnp.mean(x * x, axis=-1, keepdims=True) + RMS_EPS) * g


def _two_part(tm, width, tiles_a, tiles_b, col=lambda *_: 0):
    return (pl.BlockSpec((tm, width), lambda i, *r: (jnp.minimum(i, tiles_a - 1), col(*r))),
            pl.BlockSpec((tm, width), lambda i, *r: (jnp.clip(i - tiles_a, 0, tiles_b - 1), col(*r))))


def _row_specs(parts, tm, width, col=lambda *_: 0):
    if len(parts) == 1:
        return (pl.BlockSpec((tm, width), lambda i, *r: (i, col(*r))),)
    return _two_part(tm, width, parts[0].shape[0] // tm, parts[1].shape[0] // tm, col)


def _pick(first, a_ref, b_ref):
    return jnp.where(first, a_ref[...], b_ref[...])


def _rows(refs, tiles_a):
    if len(refs) == 1:
        return refs[0][...]
    return _pick(pl.program_id(0) < tiles_a, *refs)


def _mod_kernel(c_ref, w_ref, b_ref, o_ref):
    c = c_ref[...]
    s = (c * jax.nn.sigmoid(c)).astype(bf16)
    o_ref[0] = jnp.dot(s, w_ref[0].astype(bf16), preferred_element_type=f32) + b_ref[0]


def modulation_all(cond, w_mod, b_mod, tn=1024):
    n_layers, d, n = w_mod.shape
    return pl.pallas_call(
        _mod_kernel,
        grid=(n_layers, n // tn),
        in_specs=[
            pl.BlockSpec((8, d), lambda l, j: (0, 0)),
            pl.BlockSpec((1, d, tn), lambda l, j: (l, 0, j)),
            pl.BlockSpec((1, 1, tn), lambda l, j: (l, 0, j)),
        ],
        out_specs=pl.BlockSpec((1, 8, tn), lambda l, j: (l, 0, j)),
        out_shape=jax.ShapeDtypeStruct((n_layers, 8, n), f32),
        compiler_params=_cparams(("parallel", "parallel")),
        name="modulation",
    )(cond, w_mod, b_mod.reshape(n_layers, 1, n))


def _in_kernel(*refs, n_x, tiles_a, tn):
    x_refs = refs[:n_x]
    mod_ref, g_ref, w_ref, o_ref = refs[n_x:]
    m = mod_ref[0]
    y = _rms(_rows(x_refs, tiles_a), g_ref[...])
    h = (y * (1.0 + m[1:2]) + m[0:1]).astype(bf16)
    for c in range(w_ref.shape[1] // tn):
        cols = slice(c * tn, (c + 1) * tn)
        o_ref[:, cols] = jnp.dot(h, w_ref[:, cols], preferred_element_type=f32)


def in_projection(xs, mod, g, w, layer, row_of_tile, tm, tn=512):
    m_total = sum(x.shape[0] for x in xs)
    d = xs[0].shape[1]
    n = w.shape[2]
    return pl.pallas_call(
        functools.partial(_in_kernel, n_x=len(xs), tiles_a=xs[0].shape[0] // tm, tn=tn),
        grid=(m_total // tm,),
        in_specs=[
            *_row_specs(xs, tm, d),
            pl.BlockSpec((None, 1, 6, d), lambda i: (layer, row_of_tile(i), 0, 0)),
            pl.BlockSpec((None, 1, d), lambda i: (layer, 0, 0)),
            pl.BlockSpec((None, d, n), lambda i: (layer, 0, 0), pipeline_mode=pl.Buffered(1)),
        ],
        out_specs=pl.BlockSpec((tm, n), lambda i: (i, 0)),
        out_shape=jax.ShapeDtypeStruct((m_total, n), f32),
        compiler_params=_cparams(("arbitrary",)),
        name="in_projection",
    )(*xs, mod, g, w)


def _first_head(shape):
    return lax.broadcasted_iota(jnp.int32, shape, len(shape) - 1) < HEAD_DIM


def _attn_ctx_kernel(*refs, n_prev, write_caches):
    q_ref, k_ref, v_ref = refs[:3]
    prev = refs[3:3 + 2 * n_prev]
    outs = refs[3 + 2 * n_prev:]
    o_ref = outs[0]
    scale = HEAD_DIM ** -0.5
    nt = (((1,), (1,)), ((), ()))
    seq = q_ref.shape[0]
    lanes = 2 * HEAD_DIM
    first = _first_head((seq, lanes))
    for p in range(N_HEADS // 2):
        cols = slice(p * lanes, (p + 1) * lanes)
        q = q_ref[:, cols].astype(bf16)
        k = k_ref[:, cols]
        v = v_ref[:, cols]
        if write_caches:
            kc_ref, vc_ref = outs[1:]
            layers = [(prev[2 * l][:, cols], prev[2 * l + 1][:, cols]) for l in range(n_prev)] + [(k, v)]
            for l, (kl, vl) in enumerate(layers):
                for h in range(2):
                    kc_ref[0, l, 2 * p + h] = kl[:, h * HEAD_DIM:(h + 1) * HEAD_DIM]
                    vc_ref[0, l, 2 * p + h] = vl[:, h * HEAD_DIM:(h + 1) * HEAD_DIM]
        kb = k.astype(bf16)
        vb = v.astype(bf16)
        zero = jnp.zeros_like(kb)
        num, inv = None, []
        for h in range(2):
            keep = first if h == 0 else jnp.logical_not(first)
            s = lax.dot_general(q, jnp.where(keep, kb, zero), nt, preferred_element_type=f32) * scale
            e = jnp.exp(s - jnp.max(s, axis=-1, keepdims=True))
            inv.append(1.0 / jnp.sum(e, axis=-1, keepdims=True))
            part = jnp.dot(e.astype(bf16), jnp.where(keep, vb, zero), preferred_element_type=f32)
            num = part if num is None else num + part
        o_ref[:, cols] = (num * jnp.where(first, inv[0], inv[1])).astype(bf16)


def context_attention(z, batch, seq, prev_zs=(), write_caches=False):
    col = lambda blk: pl.BlockSpec((seq, ATTN_W), lambda b: (b, blk))
    out_specs = [pl.BlockSpec((seq, ATTN_W), lambda b: (b, 0))]
    out_shape = [jax.ShapeDtypeStruct((batch * seq, ATTN_W), bf16)]
    if write_caches:
        depth = len(prev_zs) + 1
        cache = pl.BlockSpec((1, depth, N_HEADS, seq, HEAD_DIM), lambda b: (b, 0, 0, 0, 0))
        cache_shape = jax.ShapeDtypeStruct((batch, depth, N_HEADS, seq, HEAD_DIM), f32)
        out_specs += [cache, cache]
        out_shape += [cache_shape, cache_shape]
    else:
        prev_zs = ()
    prev_args = [a for zp in prev_zs for a in (zp, zp)]
    return pl.pallas_call(
        functools.partial(_attn_ctx_kernel, n_prev=len(prev_zs), write_caches=write_caches),
        grid=(batch,),
        in_specs=[col(0), col(1), col(2)] + [col(1), col(2)] * len(prev_zs),
        out_specs=out_specs,
        out_shape=out_shape,
        compiler_params=_cparams(("parallel",)),
        name="context_attention",
    )(z, z, z, *prev_args)


def _attn_lat_kernel(q_ref, k_ref, v_ref, ck_ref, cv_ref, t_ref, o_ref, bias_scr, k_scr, v_scr, ck_scr, cv_scr,
                     *, rows, wr):
    scale = HEAD_DIM ** -0.5
    n_loc = wr * GRID_W
    for h in range(2):
        for d in range(wr):
            for j in range(wr):
                bias_scr[h, d, :, j * GRID_W:(j + 1) * GRID_W] = t_ref[h, j - d + WIN_ROWS_MAX - 1]

    kb = k_ref[...].astype(bf16)
    vb = v_ref[...].astype(bf16)
    first = _first_head(kb.shape)
    zero = jnp.zeros_like(kb)
    pad = jnp.zeros(ck_ref.shape[3:], bf16)
    for h in range(2):
        keep = first if h == 0 else jnp.logical_not(first)
        k_scr[h] = jnp.where(keep, kb, zero)
        v_scr[h] = jnp.where(keep, vb, zero)
        ck, cv = ck_ref[0, 0, h].astype(bf16), cv_ref[0, 0, h].astype(bf16)
        ck_scr[h] = jnp.concatenate([ck, pad] if h == 0 else [pad, ck], axis=-1)
        cv_scr[h] = jnp.concatenate([cv, pad] if h == 0 else [pad, cv], axis=-1)

    nt = (((1,), (1,)), ((), ()))
    br = LAT_BLOCK_ROWS
    nq = br * GRID_W
    first_q = _first_head((nq, 2 * HEAD_DIM))

    def block(blk, carry):
        q_rows = pl.ds(pl.multiple_of(blk * nq, nq), nq)
        q = q_ref[q_rows, :].astype(bf16)
        windows, offsets = [], []
        for i in range(br):
            r = blk * br + i
            rs = jnp.clip(r - wr // 2, 0, rows - wr)
            windows.append(pl.ds(pl.multiple_of(rs * GRID_W, GRID_W), n_loc))
            offsets.append(r - rs)
        num, inv = None, []
        for h in range(2):
            s_ctx = lax.dot_general(q, ck_scr[h], nt, preferred_element_type=f32) * scale
            s_loc = [lax.dot_general(q[i * GRID_W:(i + 1) * GRID_W], k_scr[h, windows[i], :], nt,
                                     preferred_element_type=f32) * scale + bias_scr[h, offsets[i]]
                     for i in range(br)]
            s_loc = jnp.concatenate(s_loc, axis=0)
            m = jnp.maximum(jnp.max(s_loc, axis=-1, keepdims=True), jnp.max(s_ctx, axis=-1, keepdims=True))
            e_loc = jnp.exp(s_loc - m)
            e_ctx = jnp.exp(s_ctx - m)
            inv.append(1.0 / (jnp.sum(e_loc, axis=-1, keepdims=True) + jnp.sum(e_ctx, axis=-1, keepdims=True)))
            e_loc = e_loc.astype(bf16)
            o_loc = [jnp.dot(e_loc[i * GRID_W:(i + 1) * GRID_W], v_scr[h, windows[i], :],
                             preferred_element_type=f32) for i in range(br)]
            part = (jnp.dot(e_ctx.astype(bf16), cv_scr[h], preferred_element_type=f32)
                    + jnp.concatenate(o_loc, axis=0))
            num = part if num is None else num + part
        o_ref[q_rows, :] = (num * jnp.where(first_q, inv[0], inv[1])).astype(bf16)
        return carry

    lax.fori_loop(0, rows // br, block, 0)


def latent_bias_table(rpb):
    cols = np.arange(GRID_W)
    col_start = np.clip(cols - WIN_COLS // 2, 0, GRID_W - WIN_COLS)
    valid = (cols[None, :] >= col_start[:, None]) & (cols[None, :] < col_start[:, None] + WIN_COLS)
    dc = cols[None, :] - cols[:, None] + WIN_COLS - 1
    assert np.all((dc[valid] >= 0) & (dc[valid] < 2 * WIN_COLS - 1))
    onehot = ((np.arange(2 * WIN_COLS - 1)[:, None, None] == dc[None]) & valid[None]).astype(np.float32)
    t = jnp.einsum('hrd,dqk->hrqk', rpb.astype(f32), jnp.asarray(onehot), precision=lax.Precision.HIGHEST)
    return jnp.where(jnp.asarray(valid)[None, None], t, NEG_INF)


def latent_attention(z, row0, batch, seq, cache_k, cache_v, layer, tables):
    lanes = 2 * HEAD_DIM
    n_pairs = N_HEADS // 2
    rows = seq // GRID_W
    wr = min(WIN_ROWS_MAX, rows)
    past = cache_k.shape[3]
    blk0 = row0 // seq
    n_dr = 2 * WIN_ROWS_MAX - 1
    assert rows % LAT_BLOCK_ROWS == 0
    col = lambda off: pl.BlockSpec((seq, lanes), lambda b, p: (blk0 + b, off + p))
    ctx = pl.BlockSpec((1, 1, 2, past, HEAD_DIM), lambda b, p: (b, layer, p, 0, 0))
    return pl.pallas_call(
        functools.partial(_attn_lat_kernel, rows=rows, wr=wr),
        grid=(batch, n_pairs),
        in_specs=[col(0), col(n_pairs), col(2 * n_pairs), ctx, ctx,
                  pl.BlockSpec((None, 2, n_dr, GRID_W, GRID_W), lambda b, p: (layer, p, 0, 0, 0))],
        out_specs=pl.BlockSpec((seq, lanes), lambda b, p: (b, p)),
        out_shape=jax.ShapeDtypeStruct((batch * seq, ATTN_W), bf16),
        scratch_shapes=[pltpu.VMEM((2, wr, GRID_W, wr * GRID_W), f32),
                        pltpu.VMEM((2, seq, lanes), bf16), pltpu.VMEM((2, seq, lanes), bf16),
                        pltpu.VMEM((2, past, lanes), bf16), pltpu.VMEM((2, past, lanes), bf16)],
        compiler_params=_cparams(("parallel", "parallel")),
        name="latent_attention",
    )(z, z, z, cache_k, cache_v, tables)


def _cmul(a, b):
    return a[0] * b[0] - a[1] * b[1], a[0] * b[1] + a[1] * b[0]


def ssm_tables(a_re, a_im, log_dt, b_re, b_im, c_re, c_im):
    hp = lax.Precision.HIGHEST
    lc, p, m = SSM_CHUNK, SSM_STATE, SSM_GROUP
    lr, li = a_re.astype(f32), a_im.astype(f32)
    dt = jnp.exp(log_dt.astype(f32))[..., None]
    sr, si = lr * dt, li * dt
    k = jnp.arange(lc + 1, dtype=f32)[:, None, None, None]
    mag = jnp.exp(k * sr[None])
    apow = (mag * jnp.cos(k * si[None]), mag * jnp.sin(k * si[None]))
    a_bar = (apow[0][1], apow[1][1])
    den = lr * lr + li * li
    q = (((a_bar[0] - 1.0) * lr + a_bar[1] * li) / den, (a_bar[1] * lr - (a_bar[0] - 1.0) * li) / den)
    b_bar = _cmul((q[0][..., None], q[1][..., None]), (b_re.astype(f32), b_im.astype(f32)))
    c_mat = (c_re.astype(f32), c_im.astype(f32))
    eye2 = jnp.eye(2, dtype=f32)
    half = SSM_PW // 2

    def slab_rows(first, second):
        t = jnp.stack([first, second], axis=-3)
        lead = t.shape[:-4]
        t = t.reshape(*lead, 2, 2, SSM_PAIRS, 2, p)
        nl = len(lead)
        t = t.transpose(*range(nl), nl + 2, nl, nl + 1, nl + 3, nl + 4)
        return t.reshape(*lead, SSM_PAIRS, SSM_PW)

    def embed(first, second):
        t = jnp.stack([first, second], axis=1)
        x = t.shape[-1]
        t = t.reshape(2, 2, SSM_PAIRS, 2, p, x).transpose(2, 3, 5, 0, 1, 4)
        t = t[:, :, :, :, :, None, :] * eye2[None, :, None, None, None, :, None]
        return t.reshape(SSM_PAIRS, 2 * x, SSM_PW)

    ar = slab_rows(apow[0], apow[0])
    ai = slab_rows(apow[1], apow[1])
    rev = lc - 1 - jnp.arange(lc)
    fwd = jnp.arange(lc)

    def per_step(t, k_fwd, k_bwd):
        t = jnp.concatenate([t[k_fwd][..., :half], t[k_bwd][..., half:]], axis=-1)
        return t.transpose(1, 0, 2)[:, :, None, :]

    b4 = embed(b_bar[0], b_bar[1])[:, None]
    b4s = embed(-b_bar[1], b_bar[0])[:, None]
    bx = b4 * per_step(ar, rev, fwd) + b4s * per_step(ai, rev, fwd)
    bx = bx.reshape(SSM_PAIRS, SSM_PW, SSM_PW)
    ct = (c_mat[0].transpose(0, 1, 3, 2), c_mat[1].transpose(0, 1, 3, 2))
    c4 = embed(ct[0], -ct[1])[:, None]
    c4s = embed(-ct[1], -ct[0])[:, None]
    cyt = c4 * per_step(ar, fwd + 1, lc - fwd) + c4s * per_step(ai, fwd + 1, lc - fwd)
    cyt = cyt.reshape(SSM_PAIRS, SSM_PW, SSM_PW)

    def kern(d):
        ab = _cmul((apow[0][:lc, d][..., None], apow[1][:lc, d][..., None]), (b_bar[0][d][None], b_bar[1][d][None]))
        return (jnp.einsum('gmp,kgpn->kgmn', c_mat[0][d], ab[0], precision=hp)
                - jnp.einsum('gmp,kgpn->kgmn', c_mat[1][d], ab[1], precision=hp))

    kf, kb = kern(0), kern(1)
    strip = jnp.concatenate([kb[:0:-1], (kf[0] + kb[0])[None], kf[1:]], axis=0)
    strip = strip.reshape(2 * lc - 1, SSM_PAIRS, 2, m, m).transpose(1, 2, 4, 0, 3)
    strip = strip[:, :, :, :, None, :] * eye2[None, :, None, None, :, None]
    strip = strip.reshape(SSM_PAIRS, 2 * m, (2 * lc - 1) * 2 * m)
    tz = jnp.stack([strip[:, :, (lc - 1 - s) * 2 * m:(lc - 1 - s) * 2 * m + SSM_PW] for s in range(lc)], axis=1)
    tz = tz.reshape(SSM_PAIRS, SSM_PW, SSM_PW)

    a16p = slab_rows(apow[0][lc], apow[1][lc])[:, None, :]
    return tz.astype(bf16), bx.astype(bf16), cyt.astype(bf16), a16p


def _ssm_kernel(u_ref, tz_ref, bx_ref, cyt_ref, a_ref, h0_ref, y_ref, fin_ref, x_scr, s_scr, y_scr, *, n_seq, n_chunks):
    r = n_seq * n_chunks
    pc = 2 * SSM_GROUP
    sw = 2 * SSM_STATE
    xs = [u_ref[pl.ds(s, r, stride=SSM_CHUNK), :] for s in range(SSM_CHUNK)]
    for kq in range(SSM_QUAD):
        u = jnp.concatenate([x[:, kq * pc:(kq + 1) * pc] for x in xs], axis=-1).astype(bf16)
        x = jnp.dot(u, bx_ref[kq], preferred_element_type=f32)
        for i in range(4):
            x_scr[4 * kq + i] = x[:, i * sw:(i + 1) * sw]
        y_scr[:, kq * SSM_PW:(kq + 1) * SSM_PW] = jnp.dot(u, tz_ref[kq], preferred_element_type=f32)

    def scan(direction):
        offs = [kq * SSM_PW + 2 * direction * sw for kq in range(SSM_QUAD)]
        slabs = [4 * kq + 2 * direction for kq in range(SSM_QUAD)]
        coef = [(a_ref[kq][:, 2 * direction * sw:2 * direction * sw + sw],
                 a_ref[kq][:, 2 * direction * sw + sw:2 * direction * sw + 2 * sw]) for kq in range(SSM_QUAD)]

        def body(i, carry):
            c = i if direction == 0 else n_chunks - 1 - i
            rows = pl.ds(c, n_seq, stride=n_chunks)
            out = []
            for kq in range(SSM_QUAD):
                sr, si = carry[2 * kq], carry[2 * kq + 1]
                ar, ai = coef[kq]
                re, im = slabs[kq], slabs[kq] + 1
                s_scr[re, rows, :] = sr
                s_scr[im, rows, :] = si
                xr = x_scr[re, rows, :]
                xi = x_scr[im, rows, :]
                out += [ar * sr - ai * si + xr, ar * si + ai * sr + xi]
            return tuple(out)

        init = []
        for lo in offs:
            init += [h0_ref[:, lo:lo + sw], h0_ref[:, lo + sw:lo + 2 * sw]]
        fin = lax.fori_loop(0, n_chunks, body, tuple(init))
        for kq, lo in enumerate(offs):
            fin_ref[:, lo:lo + sw] = fin[2 * kq]
            fin_ref[:, lo + sw:lo + 2 * sw] = fin[2 * kq + 1]

    scan(0)
    scan(1)
    ys = []
    for kq in range(SSM_QUAD):
        cols = slice(kq * SSM_PW, (kq + 1) * SSM_PW)
        s_in = jnp.concatenate([s_scr[4 * kq + i] for i in range(4)], axis=-1).astype(bf16)
        ys.append(y_scr[:, cols] + lax.dot_general(s_in, cyt_ref[kq], (((1,), (1,)), ((), ())),
                                                   preferred_element_type=f32))
    for t in range(SSM_CHUNK):
        y_ref[pl.ds(t, r, stride=SSM_CHUNK), :] = jnp.concatenate([y[:, t * pc:(t + 1) * pc] for y in ys], axis=-1)


def ssm_scan(z, row0, n_seq, seq, tables, layer, h0):
    tz, bx, cyt, a16p = tables
    n_chunks = seq // SSM_CHUNK
    rows = n_seq * seq
    r = n_seq * n_chunks
    lanes = SSM_QUAD * 2 * SSM_GROUP
    u_blk = (3 * ATTN_W) // lanes
    qw = SSM_QUAD * SSM_PW
    table = pl.BlockSpec((None, SSM_QUAD, SSM_PW, SSM_PW), lambda q: (layer, q, 0, 0))
    return pl.pallas_call(
        functools.partial(_ssm_kernel, n_seq=n_seq, n_chunks=n_chunks),
        grid=(SSM_PAIRS // SSM_QUAD,),
        in_specs=[pl.BlockSpec((rows, lanes), lambda q: (row0 // rows, u_blk + q)),
                  table, table, table,
                  pl.BlockSpec((None, SSM_QUAD, 1, SSM_PW), lambda q: (layer, q, 0, 0)),
                  pl.BlockSpec((n_seq, qw), lambda q: (0, q))],
        out_specs=[pl.BlockSpec((rows, lanes), lambda q: (0, q)),
                   pl.BlockSpec((n_seq, qw), lambda q: (0, q))],
        out_shape=[jax.ShapeDtypeStruct((rows, SSM_W), f32),
                   jax.ShapeDtypeStruct((n_seq, SSM_PAIRS * SSM_PW), f32)],
        scratch_shapes=[pltpu.VMEM((4 * SSM_QUAD, r, 2 * SSM_STATE), f32),
                        pltpu.VMEM((4 * SSM_QUAD, r, 2 * SSM_STATE), f32), pltpu.VMEM((r, qw), f32)],
        compiler_params=_cparams(("parallel",)),
        name="ssm_scan",
    )(z, tz, bx, cyt, a16p, h0)


def state_to_slabs(st):
    batch = st.shape[0]
    t = st.reshape(batch, 2, SSM_PAIRS, 2, SSM_STATE, 2).transpose(0, 2, 1, 5, 3, 4)
    return t.reshape(batch, SSM_PAIRS * SSM_PW)


def state_from_slabs(fin):
    batch = fin.shape[0]
    t = fin.reshape(batch, SSM_PAIRS, 2, 2, 2, SSM_STATE)
    return t.transpose(0, 2, 1, 4, 5, 3).reshape(batch, 2, SSM_GROUPS, SSM_STATE, 2)


def _glu_kernel(ya_ref, yb_ref, u_ref, d_ref, w_ref, b_ref, o_ref, *, tiles_a):
    y = _pick(pl.program_id(0) < tiles_a, ya_ref, yb_ref) + d_ref[...] * u_ref[...]
    z = jnp.dot(jax.nn.gelu(y).astype(bf16), w_ref[...], preferred_element_type=f32) + b_ref[...]
    o_ref[...] = (z[:, :SSM_W] * jax.nn.sigmoid(z[:, SSM_W:])).astype(bf16)


def ssm_glu(y_a, y_b, z, d_skip, w_glu, b_glu, layer, tm):
    tiles_a, tiles_b = y_a.shape[0] // tm, y_b.shape[0] // tm
    m_total = y_a.shape[0] + y_b.shape[0]
    u_blk = (3 * ATTN_W) // SSM_W
    return pl.pallas_call(
        functools.partial(_glu_kernel, tiles_a=tiles_a),
        grid=(tiles_a + tiles_b,),
        in_specs=[
            *_two_part(tm, SSM_W, tiles_a, tiles_b),
            pl.BlockSpec((tm, SSM_W), lambda i: (i, u_blk)),
            pl.BlockSpec((None, 1, SSM_W), lambda i: (layer, 0, 0)),
            pl.BlockSpec((None, SSM_W, 2 * SSM_W), lambda i: (layer, 0, 0)),
            pl.BlockSpec((None, 1, 2 * SSM_W), lambda i: (layer, 0, 0)),
        ],
        out_specs=pl.BlockSpec((tm, SSM_W), lambda i: (i, 0)),
        out_shape=jax.ShapeDtypeStruct((m_total, SSM_W), bf16),
        compiler_params=_cparams(("arbitrary",)),
        name="ssm_glu",
    )(y_a, y_b, z, d_skip, w_glu, b_glu)


def _pool_kernel(p_ref, w_ref, sc_ref, o_ref, pad_ref, *, seq):
    zeros = jnp.zeros((POOL_PAD, POOL_W), f32)
    pad_ref[0:POOL_PAD, :] = zeros
    pad_ref[POOL_PAD + seq:, :] = zeros
    pad_ref[POOL_PAD:POOL_PAD + seq, :] = p_ref[...]
    t = lax.broadcasted_iota(jnp.int32, (seq, 1), 0)
    for g, win in enumerate(POOL_WINDOWS):
        cols = slice(g * POOL_GROUP, (g + 1) * POOL_GROUP)
        total = jnp.zeros((seq, POOL_GROUP), f32)
        for d in range(-(win // 2), win - win // 2):
            total = total + pad_ref[POOL_PAD + d:POOL_PAD + d + seq, cols]
        lo = jnp.clip(t - win // 2, 0, seq)
        hi = jnp.clip(t - win // 2 + win, 0, seq)
        mixed = total / (hi - lo).astype(f32) - p_ref[:, cols]
        out = jnp.dot(mixed.astype(bf16), w_ref[g].astype(bf16), preferred_element_type=f32)
        o_ref[:, cols] = (out * sc_ref[:, cols]).astype(bf16)


def pool_mixer(z, row0, batch, seq, w_pool, pool_scale, layer):
    p_blk = (3 * ATTN_W + SSM_W) // POOL_W
    blk0 = row0 // seq
    return pl.pallas_call(
        functools.partial(_pool_kernel, seq=seq),
        grid=(batch,),
        in_specs=[
            pl.BlockSpec((seq, POOL_W), lambda b: (blk0 + b, p_blk)),
            pl.BlockSpec((None, len(POOL_WINDOWS), POOL_GROUP, POOL_GROUP), lambda b: (layer, 0, 0, 0)),
            pl.BlockSpec((None, 1, POOL_W), lambda b: (layer, 0, 0)),
        ],
        out_specs=pl.BlockSpec((seq, POOL_W), lambda b: (b, 0)),
        out_shape=jax.ShapeDtypeStruct((batch * seq, POOL_W), bf16),
        scratch_shapes=[pltpu.VMEM((seq + 2 * POOL_PAD, POOL_W), f32)],
        compiler_params=_cparams(("parallel",)),
        name="pool_mixer",
    )(z, w_pool, pool_scale)


def _out_kernel(aa_ref, ab_ref, s_ref, pa_ref, pb_ref, *refs, n_x, tiles_a, tn):
    x_refs = refs[:n_x]
    mod_ref, w_ref, o_ref = refs[n_x:]
    first = pl.program_id(0) < tiles_a
    a = _pick(first, aa_ref, ab_ref)
    s = s_ref[...]
    p = _pick(first, pa_ref, pb_ref)
    x = _rows(x_refs, tiles_a)
    gate = mod_ref[0][2:3]
    for c in range(w_ref.shape[1] // tn):
        cols = slice(c * tn, (c + 1) * tn)
        acc = jnp.dot(a, w_ref[0:ATTN_W, cols], preferred_element_type=f32)
        acc = acc + jnp.dot(s, w_ref[ATTN_W:ATTN_W + SSM_W, cols], preferred_element_type=f32)
        acc = acc + jnp.dot(p, w_ref[ATTN_W + SSM_W:, cols], preferred_element_type=f32)
        o_ref[:, cols] = x[:, cols] + gate[:, cols] * acc


def out_projection(a_a, a_b, s, p_a, p_b, xs, mod, w, layer, row_of_tile, tm, tn=512):
    m_total = sum(x.shape[0] for x in xs)
    d = xs[0].shape[1]
    tiles_a, tiles_b = a_a.shape[0] // tm, a_b.shape[0] // tm
    assert len(xs) == 1 or xs[0].shape[0] == a_a.shape[0]
    return pl.pallas_call(
        functools.partial(_out_kernel, n_x=len(xs), tiles_a=tiles_a, tn=tn),
        grid=(m_total // tm,),
        in_specs=[
            *_two_part(tm, ATTN_W, tiles_a, tiles_b),
            pl.BlockSpec((tm, SSM_W), lambda i: (i, 0)),
            *_two_part(tm, POOL_W, tiles_a, tiles_b),
            *_row_specs(xs, tm, d),
            pl.BlockSpec((None, 1, 6, d), lambda i: (layer, row_of_tile(i), 0, 0)),
            pl.BlockSpec((None, d, d), lambda i: (layer, 0, 0), pipeline_mode=pl.Buffered(1)),
        ],
        out_specs=pl.BlockSpec((tm, d), lambda i: (i, 0)),
        out_shape=jax.ShapeDtypeStruct((m_total, d), f32),
        compiler_params=_cparams(("arbitrary",)),
        name="out_projection",
    )(a_a, a_b, s, p_a, p_b, *xs, mod, w)


def _ffn_kernel(x_ref, mod_ref, g_ref, wg_ref, wu_ref, wo_ref, fg_ref, *refs, n_out, tiles_a, final_norm):
    o_refs = refs[:n_out]
    h_ref, acc_ref = refs[n_out:]
    i, j = pl.program_id(0), pl.program_id(1)
    last = j == pl.num_programs(1) - 1

    @pl.when(j == 0)
    def _():
        m = mod_ref[0]
        x = x_ref[...]
        h_ref[...] = (_rms(x, g_ref[...]) * (1.0 + m[4:5]) + m[3:4]).astype(bf16)
        acc_ref[...] = x

    h = h_ref[...]
    gate = jnp.dot(h, wg_ref[...], preferred_element_type=f32)
    up = jnp.dot(h, wu_ref[...], preferred_element_type=f32)
    act = (gate * jax.nn.sigmoid(gate) * up).astype(bf16)
    acc_ref[...] += mod_ref[0][5:6] * jnp.dot(act, wo_ref[...], preferred_element_type=f32)

    def result():
        return _rms(acc_ref[...], fg_ref[...]) if final_norm else acc_ref[...]

    if n_out == 1:
        @pl.when(last)
        def _():
            o_refs[0][...] = result()
    else:
        @pl.when(last & (i < tiles_a))
        def _():
            o_refs[0][...] = result()

        @pl.when(last & (i >= tiles_a))
        def _():
            o_refs[1][...] = result()


def ffn(x, mod, g, w_in, w_out, final_g, layer, row_of_tile, tm, th, final_norm, split_rows=None):
    m_total, d = x.shape
    hidden = w_out.shape[1]
    nh = hidden // th
    if split_rows is None:
        tiles_a = m_total // tm
        out_specs = [pl.BlockSpec((tm, d), lambda i, j: (i, 0))]
        out_shape = [jax.ShapeDtypeStruct((m_total, d), f32)]
    else:
        tiles_a = split_rows // tm
        out_specs = list(_two_part(tm, d, tiles_a, m_total // tm - tiles_a))
        out_shape = [jax.ShapeDtypeStruct((split_rows, d), f32), jax.ShapeDtypeStruct((m_total - split_rows, d), f32)]
    return pl.pallas_call(
        functools.partial(_ffn_kernel, n_out=len(out_specs), tiles_a=tiles_a, final_norm=final_norm),
        grid=(m_total // tm, nh),
        in_specs=[
            pl.BlockSpec((tm, d), lambda i, j: (i, 0)),
            pl.BlockSpec((None, 1, 6, d), lambda i, j: (layer, row_of_tile(i), 0, 0)),
            pl.BlockSpec((None, 1, d), lambda i, j: (layer, 0, 0)),
            pl.BlockSpec((None, d, th), lambda i, j: (layer, 0, j)),
            pl.BlockSpec((None, d, th), lambda i, j: (layer, 0, nh + j)),
            pl.BlockSpec((None, th, d), lambda i, j: (layer, j, 0)),
            pl.BlockSpec((1, d), lambda i, j: (0, 0)),
        ],
        out_specs=out_specs,
        out_shape=out_shape,
        scratch_shapes=[pltpu.VMEM((tm, d), bf16), pltpu.VMEM((tm, d), f32)],
        compiler_params=_cparams(("arbitrary", "arbitrary")),
        name="ffn",
    )(x, mod, g, w_in, w_in, w_out, final_g)


def kernel(x_prompt, x_sample, c, cache_k, cache_v, state_ssm, c_ctx, w_mod, b_mod, norm1_g, norm2_g, w_in,
           attn_rpb, ssm_a_re, ssm_a_im, ssm_log_dt, ssm_b_re, ssm_b_im, ssm_c_re, ssm_c_im, ssm_d, ssm_w_glu,
           ssm_b_glu, pool_w, pool_scale, w_out, ffn_w_in, ffn_w_out, final_norm_g):
    batch, seq, d = x_prompt.shape
    dec_batch, dec_seq, _ = x_sample.shape
    depth = w_in.shape[0]
    m_ctx = batch * seq
    tm = 512
    assert m_ctx % tm == 0 and dec_seq % tm == 0 and dec_batch + 1 <= 8

    def row_of_tile(i):
        return jnp.where(i < m_ctx // tm, 0, 1 + (i - m_ctx // tm) // (dec_seq // tm))

    cond = jnp.concatenate([c_ctx[None, :], c, jnp.zeros((8 - 1 - dec_batch, d), f32)], axis=0)
    mod = modulation_all(cond, w_mod, b_mod)
    mod = mod[:, :1 + dec_batch].reshape(depth, 1 + dec_batch, 6, d)

    w_in_b, w_out_b = w_in.astype(bf16), w_out.astype(bf16)
    ffn_w_in_b, ffn_w_out_b, w_glu_b = ffn_w_in.astype(bf16), ffn_w_out.astype(bf16), ssm_w_glu.astype(bf16)
    row = lambda t: t[:, None, :]
    tables = jax.vmap(ssm_tables)(ssm_a_re, ssm_a_im, ssm_log_dt, ssm_b_re, ssm_b_im, ssm_c_re, ssm_c_im)
    bias_tables = jax.vmap(latent_bias_table)(attn_rpb)

    xs = (x_prompt.reshape(m_ctx, d), x_sample.reshape(dec_batch * dec_seq, d))
    h0_ctx = jnp.zeros((batch, SSM_PAIRS * SSM_PW), f32)

    zs, st_out = [], []
    for l in range(depth):
        last = l == depth - 1
        z = in_projection(xs, mod, row(norm1_g), w_in_b, l, row_of_tile, tm)

        a_ctx, *caches = context_attention(z, batch, seq, zs, write_caches=last)
        zs.append(z)
        a_lat = latent_attention(z, m_ctx, dec_batch, dec_seq, cache_k, cache_v, l, bias_tables)

        y_ctx, fin = ssm_scan(z, 0, batch, seq, tables, l, h0_ctx)
        y_lat, _ = ssm_scan(z, m_ctx, dec_batch, dec_seq, tables, l, state_to_slabs(state_ssm[:, l]))
        st_out.append(state_from_slabs(fin))
        s_out = ssm_glu(y_ctx, y_lat, z, row(ssm_d), w_glu_b, row(ssm_b_glu), l, tm)

        p_ctx = pool_mixer(z, 0, batch, seq, pool_w, row(pool_scale), l)
        p_lat = pool_mixer(z, m_ctx, dec_batch, dec_seq, pool_w, row(pool_scale), l)

        x = out_projection(a_ctx, a_lat, s_out, p_ctx, p_lat, xs, mod, w_out_b, l, row_of_tile, tm)
        xs = ffn(x, mod, row(norm2_g), ffn_w_in_b, ffn_w_out_b, final_norm_g[None], l, row_of_tile, tm, 512,
                 final_norm=last, split_rows=m_ctx if last else None)

    y_prompt = xs[0].reshape(batch, seq, d)
    y_sample = xs[1].reshape(dec_batch, dec_seq, d)
    return (y_prompt, y_sample, caches[0], caches[1], jnp.stack(st_out, axis=1))
```

```python
import functools

import jax
import jax.numpy as jnp
import numpy as np
from jax import lax
from jax.experimental import pallas as pl
from jax.experimental.pallas import tpu as pltpu

f32 = jnp.float32
bf16 = jnp.bfloat16

D_MODEL = 2048
N_HEADS = 16
HEAD_DIM = 64
ATTN_W = N_HEADS * HEAD_DIM
SSM_W = 512
SSM_GROUP = 16
SSM_GROUPS = 32
SSM_STATE = 64
POOL_W = 512
POOL_WINDOWS = (2, 4, 8, 16)
POOL_GROUP = 128
POOL_PAD = 16
IN_W = 3 * ATTN_W + SSM_W + POOL_W
GRID_W = 64
WIN_ROWS_MAX = 8
WIN_COLS = 16
RMS_EPS = 1e-6
NEG_INF = -1e30

LAT_BLOCK_ROWS = 4
SSM_CHUNK = 16
SSM_PAIRS = SSM_GROUPS // 2
SSM_QUAD = 4
SSM_PW = 2 * SSM_CHUNK * SSM_GROUP

VMEM_LIMIT = 56 * 1024 * 1024


def _cparams(sem):
    return pltpu.CompilerParams(dimension_semantics=sem, vmem_limit_bytes=VMEM_LIMIT)


def _rms(x, g):
    return x * lax.rsqrt(jnp.mean(x * x, axis=-1, keepdims=True) + RMS_EPS) * g


def _two_part(tm, width, tiles_a, tiles_b, col=lambda *_: 0):
    return (pl.BlockSpec((tm, width), lambda i, *r: (jnp.minimum(i, tiles_a - 1), col(*r))),
            pl.BlockSpec((tm, width), lambda i, *r: (jnp.clip(i - tiles_a, 0, tiles_b - 1), col(*r))))


def _row_specs(parts, tm, width, col=lambda *_: 0):
    if len(parts) == 1:
        return (pl.BlockSpec((tm, width), lambda i, *r: (i, col(*r))),)
    return _two_part(tm, width, parts[0].shape[0] // tm, parts[1].shape[0] // tm, col)


def _pick(first, a_ref, b_ref):
    return jnp.where(first, a_ref[...], b_ref[...])


def _rows(refs, tiles_a):
    if len(refs) == 1:
        return refs[0][...]
    return _pick(pl.program_id(0) < tiles_a, *refs)


def _mod_kernel(c_ref, w_ref, b_ref, o_ref):
    c = c_ref[...]
    s = (c * jax.nn.sigmoid(c)).astype(bf16)
    o_ref[0] = jnp.dot(s, w_ref[0].astype(bf16), preferred_element_type=f32) + b_ref[0]


def modulation_all(cond, w_mod, b_mod, tn=1024):
    n_layers, d, n = w_mod.shape
    return pl.pallas_call(
        _mod_kernel,
        grid=(n_layers, n // tn),
        in_specs=[
            pl.BlockSpec((8, d), lambda l, j: (0, 0)),
            pl.BlockSpec((1, d, tn), lambda l, j: (l, 0, j)),
            pl.BlockSpec((1, 1, tn), lambda l, j: (l, 0, j)),
        ],
        out_specs=pl.BlockSpec((1, 8, tn), lambda l, j: (l, 0, j)),
        out_shape=jax.ShapeDtypeStruct((n_layers, 8, n), f32),
        compiler_params=_cparams(("parallel", "parallel")),
        name="modulation",
    )(cond, w_mod, b_mod.reshape(n_layers, 1, n))


def _in_kernel(*refs, n_x, tiles_a, tn):
    x_refs = refs[:n_x]
    mod_ref, g_ref, w_ref, o_ref = refs[n_x:]
    m = mod_ref[0]
    y = _rms(_rows(x_refs, tiles_a), g_ref[...])
    h = (y * (1.0 + m[1:2]) + m[0:1]).astype(bf16)
    for c in range(w_ref.shape[1] // tn):
        cols = slice(c * tn, (c + 1) * tn)
        o_ref[:, cols] = jnp.dot(h, w_ref[:, cols], preferred_element_type=f32)


def in_projection(xs, mod, g, w, layer, row_of_tile, tm, tn=512):
    m_total = sum(x.shape[0] for x in xs)
    d = xs[0].shape[1]
    n = w.shape[2]
    return pl.pallas_call(
        functools.partial(_in_kernel, n_x=len(xs), tiles_a=xs[0].shape[0] // tm, tn=tn),
        grid=(m_total // tm,),
        in_specs=[
            *_row_specs(xs, tm, d),
            pl.BlockSpec((None, 1, 6, d), lambda i: (layer, row_of_tile(i), 0, 0)),
            pl.BlockSpec((None, 1, d), lambda i: (layer, 0, 0)),
            pl.BlockSpec((None, d, n), lambda i: (layer, 0, 0), pipeline_mode=pl.Buffered(1)),
        ],
        out_specs=pl.BlockSpec((tm, n), lambda i: (i, 0)),
        out_shape=jax.ShapeDtypeStruct((m_total, n), f32),
        compiler_params=_cparams(("arbitrary",)),
        name="in_projection",
    )(*xs, mod, g, w)


def _first_head(shape):
    return lax.broadcasted_iota(jnp.int32, shape, len(shape) - 1) < HEAD_DIM


def _attn_ctx_kernel(*refs, n_prev, write_caches):
    q_ref, k_ref, v_ref = refs[:3]
    prev = refs[3:3 + 2 * n_prev]
    outs = refs[3 + 2 * n_prev:]
    o_ref = outs[0]
    scale = HEAD_DIM ** -0.5
    nt = (((1,), (1,)), ((), ()))
    seq = q_ref.shape[0]
    lanes = 2 * HEAD_DIM
    first = _first_head((seq, lanes))
    for p in range(N_HEADS // 2):
        cols = slice(p * lanes, (p + 1) * lanes)
        q = q_ref[:, cols].astype(bf16)
        k = k_ref[:, cols]
        v = v_ref[:, cols]
        if write_caches:
            kc_ref, vc_ref = outs[1:]
            layers = [(prev[2 * l][:, cols], prev[2 * l + 1][:, cols]) for l in range(n_prev)] + [(k, v)]
            for l, (kl, vl) in enumerate(layers):
                for h in range(2):
                    kc_ref[0, l, 2 * p + h] = kl[:, h * HEAD_DIM:(h + 1) * HEAD_DIM]
                    vc_ref[0, l, 2 * p + h] = vl[:, h * HEAD_DIM:(h + 1) * HEAD_DIM]
        kb = k.astype(bf16)
        vb = v.astype(bf16)
        zero = jnp.zeros_like(kb)
        num, inv = None, []
        for h in range(2):
            keep = first if h == 0 else jnp.logical_not(first)
            s = lax.dot_general(q, jnp.where(keep, kb, zero), nt, preferred_element_type=f32) * scale
            e = jnp.exp(s - jnp.max(s, axis=-1, keepdims=True))
            inv.append(1.0 / jnp.sum(e, axis=-1, keepdims=True))
            part = jnp.dot(e.astype(bf16), jnp.where(keep, vb, zero), preferred_element_type=f32)
            num = part if num is None else num + part
        o_ref[:, cols] = (num * jnp.where(first, inv[0], inv[1])).astype(bf16)


def context_attention(z, batch, seq, prev_zs=(), write_caches=False):
    col = lambda blk: pl.BlockSpec((seq, ATTN_W), lambda b: (b, blk))
    out_specs = [pl.BlockSpec((seq, ATTN_W), lambda b: (b, 0))]
    out_shape = [jax.ShapeDtypeStruct((batch * seq, ATTN_W), bf16)]
    if write_caches:
        depth = len(prev_zs) + 1
        cache = pl.BlockSpec((1, depth, N_HEADS, seq, HEAD_DIM), lambda b: (b, 0, 0, 0, 0))
        cache_shape = jax.ShapeDtypeStruct((batch, depth, N_HEADS, seq, HEAD_DIM), f32)
        out_specs += [cache, cache]
        out_shape += [cache_shape, cache_shape]
    else:
        prev_zs = ()
    prev_args = [a for zp in prev_zs for a in (zp, zp)]
    return pl.pallas_call(
        functools.partial(_attn_ctx_kernel, n_prev=len(prev_zs), write_caches=write_caches),
        grid=(batch,),
        in_specs=[col(0), col(1), col(2)] + [col(1), col(2)] * len(prev_zs),
        out_specs=out_specs,
        out_shape=out_shape,
        compiler_params=_cparams(("parallel",)),
        name="context_attention",
    )(z, z, z, *prev_args)


def _attn_lat_kernel(q_ref, k_ref, v_ref, ck_ref, cv_ref, t_ref, o_ref, bias_scr, k_scr, v_scr, ck_scr, cv_scr,
                     *, rows, wr):
    scale = HEAD_DIM ** -0.5
    n_loc = wr * GRID_W
    for h in range(2):
        for d in range(wr):
            for j in range(wr):
                bias_scr[h, d, :, j * GRID_W:(j + 1) * GRID_W] = t_ref[h, j - d + WIN_ROWS_MAX - 1]

    kb = k_ref[...].astype(bf16)
    vb = v_ref[...].astype(bf16)
    first = _first_head(kb.shape)
    zero = jnp.zeros_like(kb)
    pad = jnp.zeros(ck_ref.shape[3:], bf16)
    for h in range(2):
        keep = first if h == 0 else jnp.logical_not(first)
        k_scr[h] = jnp.where(keep, kb, zero)
        v_scr[h] = jnp.where(keep, vb, zero)
        ck, cv = ck_ref[0, 0, h].astype(bf16), cv_ref[0, 0, h].astype(bf16)
        ck_scr[h] = jnp.concatenate([ck, pad] if h == 0 else [pad, ck], axis=-1)
        cv_scr[h] = jnp.concatenate([cv, pad] if h == 0 else [pad, cv], axis=-1)

    nt = (((1,), (1,)), ((), ()))
    br = LAT_BLOCK_ROWS
    nq = br * GRID_W
    first_q = _first_head((nq, 2 * HEAD_DIM))

    def block(blk, carry):
        q_rows = pl.ds(pl.multiple_of(blk * nq, nq), nq)
        q = q_ref[q_rows, :].astype(bf16)
        windows, offsets = [], []
        for i in range(br):
            r = blk * br + i
            rs = jnp.clip(r - wr // 2, 0, rows - wr)
            windows.append(pl.ds(pl.multiple_of(rs * GRID_W, GRID_W), n_loc))
            offsets.append(r - rs)
        num, inv = None, []
        for h in range(2):
            s_ctx = lax.dot_general(q, ck_scr[h], nt, preferred_element_type=f32) * scale
            s_loc = [lax.dot_general(q[i * GRID_W:(i + 1) * GRID_W], k_scr[h, windows[i], :], nt,
                                     preferred_element_type=f32) * scale + bias_scr[h, offsets[i]]
                     for i in range(br)]
            s_loc = jnp.concatenate(s_loc, axis=0)
            m = jnp.maximum(jnp.max(s_loc, axis=-1, keepdims=True), jnp.max(s_ctx, axis=-1, keepdims=True))
            e_loc = jnp.exp(s_loc - m)
            e_ctx = jnp.exp(s_ctx - m)
            inv.append(1.0 / (jnp.sum(e_loc, axis=-1, keepdims=True) + jnp.sum(e_ctx, axis=-1, keepdims=True)))
            e_loc = e_loc.astype(bf16)
            o_loc = [jnp.dot(e_loc[i * GRID_W:(i + 1) * GRID_W], v_scr[h, windows[i], :],
                             preferred_element_type=f32) for i in range(br)]
            part = (jnp.dot(e_ctx.astype(bf16), cv_scr[h], preferred_element_type=f32)
                    + jnp.concatenate(o_loc, axis=0))
            num = part if num is None else num + part
        o_ref[q_rows, :] = (num * jnp.where(first_q, inv[0], inv[1])).astype(bf16)
        return carry

    lax.fori_loop(0, rows // br, block, 0)


def latent_bias_table(rpb):
    cols = np.arange(GRID_W)
    col_start = np.clip(cols - WIN_COLS // 2, 0, GRID_W - WIN_COLS)
    valid = (cols[None, :] >= col_start[:, None]) & (cols[None, :] < col_start[:, None] + WIN_COLS)
    dc = cols[None, :] - cols[:, None] + WIN_COLS - 1
    assert np.all((dc[valid] >= 0) & (dc[valid] < 2 * WIN_COLS - 1))
    onehot = ((np.arange(2 * WIN_COLS - 1)[:, None, None] == dc[None]) & valid[None]).astype(np.float32)
    t = jnp.einsum('hrd,dqk->hrqk', rpb.astype(f32), jnp.asarray(onehot), precision=lax.Precision.HIGHEST)
    return jnp.where(jnp.asarray(valid)[None, None], t, NEG_INF)


def latent_attention(z, row0, batch, seq, cache_k, cache_v, layer, tables):
    lanes = 2 * HEAD_DIM
    n_pairs = N_HEADS // 2
    rows = seq // GRID_W
    wr = min(WIN_ROWS_MAX, rows)
    past = cache_k.shape[3]
    blk0 = row0 // seq
    n_dr = 2 * WIN_ROWS_MAX - 1
    assert rows % LAT_BLOCK_ROWS == 0
    col = lambda off: pl.BlockSpec((seq, lanes), lambda b, p: (blk0 + b, off + p))
    ctx = pl.BlockSpec((1, 1, 2, past, HEAD_DIM), lambda b, p: (b, layer, p, 0, 0))
    return pl.pallas_call(
        functools.partial(_attn_lat_kernel, rows=rows, wr=wr),
        grid=(batch, n_pairs),
        in_specs=[col(0), col(n_pairs), col(2 * n_pairs), ctx, ctx,
                  pl.BlockSpec((None, 2, n_dr, GRID_W, GRID_W), lambda b, p: (layer, p, 0, 0, 0))],
        out_specs=pl.BlockSpec((seq, lanes), lambda b, p: (b, p)),
        out_shape=jax.ShapeDtypeStruct((batch * seq, ATTN_W), bf16),
        scratch_shapes=[pltpu.VMEM((2, wr, GRID_W, wr * GRID_W), f32),
                        pltpu.VMEM((2, seq, lanes), bf16), pltpu.VMEM((2, seq, lanes), bf16),
                        pltpu.VMEM((2, past, lanes), bf16), pltpu.VMEM((2, past, lanes), bf16)],
        compiler_params=_cparams(("parallel", "parallel")),
        name="latent_attention",
    )(z, z, z, cache_k, cache_v, tables)


def ssm_tables(a_re, a_im, log_dt, b_re, b_im, c_re, c_im):
    hp = lax.Precision.HIGHEST
    lc, p, m = SSM_CHUNK, SSM_STATE, SSM_GROUP
    eye2 = jnp.eye(2, dtype=f32)
    half = SSM_PW // 2

    def slab_row(t):
        t = jnp.broadcast_to(t.astype(f32).reshape(2, 1, SSM_PAIRS, 2, p), (2, 2, SSM_PAIRS, 2, p))
        return t.transpose(2, 0, 1, 3, 4).reshape(SSM_PAIRS, SSM_PW)

    def embed(first, second):
        t = jnp.stack([first, second], axis=1).astype(f32)
        x = t.shape[-1]
        t = t.reshape(2, 2, SSM_PAIRS, 2, p, x).transpose(2, 3, 5, 0, 1, 4)
        t = t[:, :, :, :, :, None, :] * eye2[None, :, None, None, None, :, None]
        return t.reshape(SSM_PAIRS, 2 * x, SSM_PW)

    lr, li = slab_row(a_re), slab_row(a_im)
    dt = slab_row(jnp.broadcast_to(jnp.exp(log_dt.astype(f32))[..., None], a_re.shape))
    sr, si = lr * dt, li * dt
    k = jnp.arange(lc + 1, dtype=f32)[:, None, None]
    mag = jnp.exp(k * sr[None])
    ar, ai = mag * jnp.cos(k * si[None]), mag * jnp.sin(k * si[None])
    den = lr * lr + li * li
    qr = ((ar[1] - 1.0) * lr + ai[1] * li) / den
    qi = (ai[1] * lr - (ar[1] - 1.0) * li) / den
    b4, b4s = embed(b_re, b_im), embed(-b_im, b_re)
    bb4 = (qr[:, None] * b4 + qi[:, None] * b4s)[:, None]
    bb4s = (qr[:, None] * b4s - qi[:, None] * b4)[:, None]
    rev = lc - 1 - jnp.arange(lc)
    fwd = jnp.arange(lc)

    def per_step(t, k_fwd, k_bwd):
        t = jnp.concatenate([t[k_fwd][..., :half], t[k_bwd][..., half:]], axis=-1)
        return t.transpose(1, 0, 2)[:, :, None, :]

    bx = bb4 * per_step(ar, rev, fwd) + bb4s * per_step(ai, rev, fwd)
    bx = bx.reshape(SSM_PAIRS, SSM_PW, SSM_PW)
    ct_re, ct_im = c_re.transpose(0, 1, 3, 2), c_im.transpose(0, 1, 3, 2)
    c4 = embed(ct_re, -ct_im)[:, None]
    c4s = embed(-ct_im, -ct_re)[:, None]
    cyt = c4 * per_step(ar, fwd + 1, lc - fwd) + c4s * per_step(ai, fwd + 1, lc - fwd)
    cyt = cyt.reshape(SSM_PAIRS, SSM_PW, SSM_PW)

    j = jnp.arange(2 * lc - 1)
    live = jnp.concatenate([jnp.broadcast_to((j >= lc - 1)[:, None], (2 * lc - 1, half)),
                            jnp.broadcast_to((j <= lc - 1)[:, None], (2 * lc - 1, half))], axis=-1).astype(f32)
    kt = (c4 * per_step(ar, jnp.maximum(j - (lc - 1), 0), jnp.maximum(lc - 1 - j, 0))
          + c4s * per_step(ai, jnp.maximum(j - (lc - 1), 0), jnp.maximum(lc - 1 - j, 0))) * live[None, :, None, :]
    kt = kt.reshape(SSM_PAIRS, (2 * lc - 1) * 2 * m, SSM_PW)
    strip = lax.dot_general(bb4[:, 0], kt, (((2,), (2,)), ((0,), (0,))), precision=hp)
    tz = jnp.stack([strip[:, :, (lc - 1 - s) * 2 * m:(lc - 1 - s) * 2 * m + SSM_PW] for s in range(lc)], axis=1)
    tz = tz.reshape(SSM_PAIRS, SSM_PW, SSM_PW)

    sw = 2 * p
    a16p = jnp.concatenate([ar[lc][:, :sw], ai[lc][:, sw:half], ar[lc][:, half:half + sw], ai[lc][:, half + sw:]],
                           axis=-1)[:, None, :]
    return tz.astype(bf16), bx.astype(bf16), cyt.astype(bf16), a16p


def _ssm_kernel(u_ref, tz_ref, bx_ref, cyt_ref, a_ref, h0_ref, y_ref, fin_ref, x_scr, s_scr, y_scr, *, n_seq, n_chunks):
    r = n_seq * n_chunks
    pc = 2 * SSM_GROUP
    sw = 2 * SSM_STATE
    xs = [u_ref[pl.ds(s, r, stride=SSM_CHUNK), :] for s in range(SSM_CHUNK)]
    for kq in range(SSM_QUAD):
        u = jnp.concatenate([x[:, kq * pc:(kq + 1) * pc] for x in xs], axis=-1).astype(bf16)
        x = jnp.dot(u, bx_ref[kq], preferred_element_type=f32)
        for i in range(4):
            x_scr[4 * kq + i] = x[:, i * sw:(i + 1) * sw]
        y_scr[:, kq * SSM_PW:(kq + 1) * SSM_PW] = jnp.dot(u, tz_ref[kq], preferred_element_type=f32)

    def scan(direction):
        offs = [kq * SSM_PW + 2 * direction * sw for kq in range(SSM_QUAD)]
        slabs = [4 * kq + 2 * direction for kq in range(SSM_QUAD)]
        coef = [(a_ref[kq][:, 2 * direction * sw:2 * direction * sw + sw],
                 a_ref[kq][:, 2 * direction * sw + sw:2 * direction * sw + 2 * sw]) for kq in range(SSM_QUAD)]

        def body(i, carry):
            c = i if direction == 0 else n_chunks - 1 - i
            rows = pl.ds(c, n_seq, stride=n_chunks)
            out = []
            for kq in range(SSM_QUAD):
                sr, si = carry[2 * kq], carry[2 * kq + 1]
                ar, ai = coef[kq]
                re, im = slabs[kq], slabs[kq] + 1
                s_scr[re, rows, :] = sr
                s_scr[im, rows, :] = si
                xr = x_scr[re, rows, :]
                xi = x_scr[im, rows, :]
                out += [ar * sr - ai * si + xr, ar * si + ai * sr + xi]
            return tuple(out)

        init = []
        for lo in offs:
            init += [h0_ref[:, lo:lo + sw], h0_ref[:, lo + sw:lo + 2 * sw]]
        fin = lax.fori_loop(0, n_chunks, body, tuple(init))
        for kq, lo in enumerate(offs):
            fin_ref[:, lo:lo + sw] = fin[2 * kq]
            fin_ref[:, lo + sw:lo + 2 * sw] = fin[2 * kq + 1]

    scan(0)
    scan(1)
    ys = []
    for kq in range(SSM_QUAD):
        cols = slice(kq * SSM_PW, (kq + 1) * SSM_PW)
        s_in = jnp.concatenate([s_scr[4 * kq + i] for i in range(4)], axis=-1).astype(bf16)
        ys.append(y_scr[:, cols] + lax.dot_general(s_in, cyt_ref[kq], (((1,), (1,)), ((), ())),
                                                   preferred_element_type=f32))
    for t in range(SSM_CHUNK):
        y_ref[pl.ds(t, r, stride=SSM_CHUNK), :] = jnp.concatenate([y[:, t * pc:(t + 1) * pc] for y in ys], axis=-1)


def ssm_scan(z, row0, n_seq, seq, tables, layer, h0):
    tz, bx, cyt, a16p = tables
    n_chunks = seq // SSM_CHUNK
    rows = n_seq * seq
    r = n_seq * n_chunks
    lanes = SSM_QUAD * 2 * SSM_GROUP
    u_blk = (3 * ATTN_W) // lanes
    qw = SSM_QUAD * SSM_PW
    table = pl.BlockSpec((None, SSM_QUAD, SSM_PW, SSM_PW), lambda q: (layer, q, 0, 0))
    return pl.pallas_call(
        functools.partial(_ssm_kernel, n_seq=n_seq, n_chunks=n_chunks),
        grid=(SSM_PAIRS // SSM_QUAD,),
        in_specs=[pl.BlockSpec((rows, lanes), lambda q: (row0 // rows, u_blk + q)),
                  table, table, table,
                  pl.BlockSpec((None, SSM_QUAD, 1, SSM_PW), lambda q: (layer, q, 0, 0)),
                  pl.BlockSpec((n_seq, qw), lambda q: (0, q))],
        out_specs=[pl.BlockSpec((rows, lanes), lambda q: (0, q)),
                   pl.BlockSpec((n_seq, qw), lambda q: (0, q))],
        out_shape=[jax.ShapeDtypeStruct((rows, SSM_W), f32),
                   jax.ShapeDtypeStruct((n_seq, SSM_PAIRS * SSM_PW), f32)],
        scratch_shapes=[pltpu.VMEM((4 * SSM_QUAD, r, 2 * SSM_STATE), f32),
                        pltpu.VMEM((4 * SSM_QUAD, r, 2 * SSM_STATE), f32), pltpu.VMEM((r, qw), f32)],
        compiler_params=_cparams(("parallel",)),
        name="ssm_scan",
    )(z, tz, bx, cyt, a16p, h0)


def state_to_slabs(st):
    batch = st.shape[0]
    t = st.reshape(batch, 2, SSM_PAIRS, 2, SSM_STATE, 2).transpose(0, 2, 1, 5, 3, 4)
    return t.reshape(batch, SSM_PAIRS * SSM_PW)


def state_from_slabs(fin):
    batch = fin.shape[0]
    t = fin.reshape(batch, SSM_PAIRS, 2, 2, 2, SSM_STATE)
    return t.transpose(0, 2, 1, 4, 5, 3).reshape(batch, 2, SSM_GROUPS, SSM_STATE, 2)


def _glu_kernel(ya_ref, yb_ref, u_ref, d_ref, w_ref, b_ref, o_ref, *, tiles_a):
    y = _pick(pl.program_id(0) < tiles_a, ya_ref, yb_ref) + d_ref[...] * u_ref[...]
    z = jnp.dot(jax.nn.gelu(y).astype(bf16), w_ref[...], preferred_element_type=f32) + b_ref[...]
    o_ref[...] = (z[:, :SSM_W] * jax.nn.sigmoid(z[:, SSM_W:])).astype(bf16)


def ssm_glu(y_a, y_b, z, d_skip, w_glu, b_glu, layer, tm):
    tiles_a, tiles_b = y_a.shape[0] // tm, y_b.shape[0] // tm
    m_total = y_a.shape[0] + y_b.shape[0]
    u_blk = (3 * ATTN_W) // SSM_W
    return pl.pallas_call(
        functools.partial(_glu_kernel, tiles_a=tiles_a),
        grid=(tiles_a + tiles_b,),
        in_specs=[
            *_two_part(tm, SSM_W, tiles_a, tiles_b),
            pl.BlockSpec((tm, SSM_W), lambda i: (i, u_blk)),
            pl.BlockSpec((None, 1, SSM_W), lambda i: (layer, 0, 0)),
            pl.BlockSpec((None, SSM_W, 2 * SSM_W), lambda i: (layer, 0, 0)),
            pl.BlockSpec((None, 1, 2 * SSM_W), lambda i: (layer, 0, 0)),
        ],
        out_specs=pl.BlockSpec((tm, SSM_W), lambda i: (i, 0)),
        out_shape=jax.ShapeDtypeStruct((m_total, SSM_W), bf16),
        compiler_params=_cparams(("arbitrary",)),
        name="ssm_glu",
    )(y_a, y_b, z, d_skip, w_glu, b_glu)


def _pool_kernel(p_ref, w_ref, sc_ref, o_ref, pad_ref, *, seq):
    zeros = jnp.zeros((POOL_PAD, POOL_W), f32)
    pad_ref[0:POOL_PAD, :] = zeros
    pad_ref[POOL_PAD + seq:, :] = zeros
    pad_ref[POOL_PAD:POOL_PAD + seq, :] = p_ref[...]
    t = lax.broadcasted_iota(jnp.int32, (seq, 1), 0)
    for g, win in enumerate(POOL_WINDOWS):
        cols = slice(g * POOL_GROUP, (g + 1) * POOL_GROUP)
        total = jnp.zeros((seq, POOL_GROUP), f32)
        for d in range(-(win // 2), win - win // 2):
            total = total + pad_ref[POOL_PAD + d:POOL_PAD + d + seq, cols]
        lo = jnp.clip(t - win // 2, 0, seq)
        hi = jnp.clip(t - win // 2 + win, 0, seq)
        mixed = total / (hi - lo).astype(f32) - p_ref[:, cols]
        out = jnp.dot(mixed.astype(bf16), w_ref[g].astype(bf16), preferred_element_type=f32)
        o_ref[:, cols] = (out * sc_ref[:, cols]).astype(bf16)


def pool_mixer(z, row0, batch, seq, w_pool, pool_scale, layer):
    p_blk = (3 * ATTN_W + SSM_W) // POOL_W
    blk0 = row0 // seq
    return pl.pallas_call(
        functools.partial(_pool_kernel, seq=seq),
        grid=(batch,),
        in_specs=[
            pl.BlockSpec((seq, POOL_W), lambda b: (blk0 + b, p_blk)),
            pl.BlockSpec((None, len(POOL_WINDOWS), POOL_GROUP, POOL_GROUP), lambda b: (layer, 0, 0, 0)),
            pl.BlockSpec((None, 1, POOL_W), lambda b: (layer, 0, 0)),
        ],
        out_specs=pl.BlockSpec((seq, POOL_W), lambda b: (b, 0)),
        out_shape=jax.ShapeDtypeStruct((batch * seq, POOL_W), bf16),
        scratch_shapes=[pltpu.VMEM((seq + 2 * POOL_PAD, POOL_W), f32)],
        compiler_params=_cparams(("parallel",)),
        name="pool_mixer",
    )(z, w_pool, pool_scale)


def _out_kernel(aa_ref, ab_ref, s_ref, pa_ref, pb_ref, *refs, n_x, tiles_a, tn):
    x_refs = refs[:n_x]
    mod_ref, w_ref, o_ref = refs[n_x:]
    first = pl.program_id(0) < tiles_a
    a = _pick(first, aa_ref, ab_ref)
    s = s_ref[...]
    p = _pick(first, pa_ref, pb_ref)
    x = _rows(x_refs, tiles_a)
    gate = mod_ref[0][2:3]
    for c in range(w_ref.shape[1] // tn):
        cols = slice(c * tn, (c + 1) * tn)
        acc = jnp.dot(a, w_ref[0:ATTN_W, cols], preferred_element_type=f32)
        acc = acc + jnp.dot(s, w_ref[ATTN_W:ATTN_W + SSM_W, cols], preferred_element_type=f32)
        acc = acc + jnp.dot(p, w_ref[ATTN_W + SSM_W:, cols], preferred_element_type=f32)
        o_ref[:, cols] = x[:, cols] + gate[:, cols] * acc


def out_projection(a_a, a_b, s, p_a, p_b, xs, mod, w, layer, row_of_tile, tm, tn=512):
    m_total = sum(x.shape[0] for x in xs)
    d = xs[0].shape[1]
    tiles_a, tiles_b = a_a.shape[0] // tm, a_b.shape[0] // tm
    assert len(xs) == 1 or xs[0].shape[0] == a_a.shape[0]
    return pl.pallas_call(
        functools.partial(_out_kernel, n_x=len(xs), tiles_a=tiles_a, tn=tn),
        grid=(m_total // tm,),
        in_specs=[
            *_two_part(tm, ATTN_W, tiles_a, tiles_b),
            pl.BlockSpec((tm, SSM_W), lambda i: (i, 0)),
            *_two_part(tm, POOL_W, tiles_a, tiles_b),
            *_row_specs(xs, tm, d),
            pl.BlockSpec((None, 1, 6, d), lambda i: (layer, row_of_tile(i), 0, 0)),
            pl.BlockSpec((None, d, d), lambda i: (layer, 0, 0), pipeline_mode=pl.Buffered(1)),
        ],
        out_specs=pl.BlockSpec((tm, d), lambda i: (i, 0)),
        out_shape=jax.ShapeDtypeStruct((m_total, d), f32),
        compiler_params=_cparams(("arbitrary",)),
        name="out_projection",
    )(a_a, a_b, s, p_a, p_b, *xs, mod, w)


def _ffn_kernel(x_ref, mod_ref, g_ref, wg_ref, wu_ref, wo_ref, fg_ref, *refs, n_out, tiles_a, final_norm):
    o_refs = refs[:n_out]
    h_ref, acc_ref = refs[n_out:]
    i, j = pl.program_id(0), pl.program_id(1)
    last = j == pl.num_programs(1) - 1

    @pl.when(j == 0)
    def _():
        m = mod_ref[0]
        x = x_ref[...]
        h_ref[...] = (_rms(x, g_ref[...]) * (1.0 + m[4:5]) + m[3:4]).astype(bf16)
        acc_ref[...] = x

    h = h_ref[...]
    gate = jnp.dot(h, wg_ref[...], preferred_element_type=f32)
    up = jnp.dot(h, wu_ref[...], preferred_element_type=f32)
    act = (gate * jax.nn.sigmoid(gate) * up).astype(bf16)
    acc_ref[...] += mod_ref[0][5:6] * jnp.dot(act, wo_ref[...], preferred_element_type=f32)

    def result():
        return _rms(acc_ref[...], fg_ref[...]) if final_norm else acc_ref[...]

    if n_out == 1:
        @pl.when(last)
        def _():
            o_refs[0][...] = result()
    else:
        @pl.when(last & (i < tiles_a))
        def _():
            o_refs[0][...] = result()

        @pl.when(last & (i >= tiles_a))
        def _():
            o_refs[1][...] = result()


def ffn(x, mod, g, w_in, w_out, final_g, layer, row_of_tile, tm, th, final_norm, split_rows=None):
    m_total, d = x.shape
    hidden = w_out.shape[1]
    nh = hidden // th
    if split_rows is None:
        tiles_a = m_total // tm
        out_specs = [pl.BlockSpec((tm, d), lambda i, j: (i, 0))]
        out_shape = [jax.ShapeDtypeStruct((m_total, d), f32)]
    else:
        tiles_a = split_rows // tm
        out_specs = list(_two_part(tm, d, tiles_a, m_total // tm - tiles_a))
        out_shape = [jax.ShapeDtypeStruct((split_rows, d), f32), jax.ShapeDtypeStruct((m_total - split_rows, d), f32)]
    return pl.pallas_call(
        functools.partial(_ffn_kernel, n_out=len(out_specs), tiles_a=tiles_a, final_norm=final_norm),
        grid=(m_total // tm, nh),
        in_specs=[
            pl.BlockSpec((tm, d), lambda i, j: (i, 0)),
            pl.BlockSpec((None, 1, 6, d), lambda i, j: (layer, row_of_tile(i), 0, 0)),
            pl.BlockSpec((None, 1, d), lambda i, j: (layer, 0, 0)),
            pl.BlockSpec((None, d, th), lambda i, j: (layer, 0, j)),
            pl.BlockSpec((None, d, th), lambda i, j: (layer, 0, nh + j)),
            pl.BlockSpec((None, th, d), lambda i, j: (layer, j, 0)),
            pl.BlockSpec((1, d), lambda i, j: (0, 0)),
        ],
        out_specs=out_specs,
        out_shape=out_shape,
        scratch_shapes=[pltpu.VMEM((tm, d), bf16), pltpu.VMEM((tm, d), f32)],
        compiler_params=_cparams(("arbitrary", "arbitrary")),
        name="ffn",
    )(x, mod, g, w_in, w_in, w_out, final_g)


def kernel(x_prompt, x_sample, c, cache_k, cache_v, state_ssm, c_ctx, w_mod, b_mod, norm1_g, norm2_g, w_in,
           attn_rpb, ssm_a_re, ssm_a_im, ssm_log_dt, ssm_b_re, ssm_b_im, ssm_c_re, ssm_c_im, ssm_d, ssm_w_glu,
           ssm_b_glu, pool_w, pool_scale, w_out, ffn_w_in, ffn_w_out, final_norm_g):
    batch, seq, d = x_prompt.shape
    dec_batch, dec_seq, _ = x_sample.shape
    depth = w_in.shape[0]
    m_ctx = batch * seq
    tm = 512
    assert m_ctx % tm == 0 and dec_seq % tm == 0 and dec_batch + 1 <= 8

    def row_of_tile(i):
        return jnp.where(i < m_ctx // tm, 0, 1 + (i - m_ctx // tm) // (dec_seq // tm))

    cond = jnp.concatenate([c_ctx[None, :], c, jnp.zeros((8 - 1 - dec_batch, d), f32)], axis=0)
    mod = modulation_all(cond, w_mod, b_mod)
    mod = mod[:, :1 + dec_batch].reshape(depth, 1 + dec_batch, 6, d)

    w_in_b, w_out_b = w_in.astype(bf16), w_out.astype(bf16)
    ffn_w_in_b, ffn_w_out_b, w_glu_b = ffn_w_in.astype(bf16), ffn_w_out.astype(bf16), ssm_w_glu.astype(bf16)
    row = lambda t: t[:, None, :]
    tables = jax.vmap(ssm_tables)(ssm_a_re, ssm_a_im, ssm_log_dt, ssm_b_re, ssm_b_im, ssm_c_re, ssm_c_im)
    bias_tables = jax.vmap(latent_bias_table)(attn_rpb)

    xs = (x_prompt.reshape(m_ctx, d), x_sample.reshape(dec_batch * dec_seq, d))
    h0_ctx = jnp.zeros((batch, SSM_PAIRS * SSM_PW), f32)

    zs, st_out = [], []
    for l in range(depth):
        last = l == depth - 1
        z = in_projection(xs, mod, row(norm1_g), w_in_b, l, row_of_tile, tm)

        a_ctx, *caches = context_attention(z, batch, seq, zs, write_caches=last)
        zs.append(z)
        a_lat = latent_attention(z, m_ctx, dec_batch, dec_seq, cache_k, cache_v, l, bias_tables)

        y_ctx, fin = ssm_scan(z, 0, batch, seq, tables, l, h0_ctx)
        y_lat, _ = ssm_scan(z, m_ctx, dec_batch, dec_seq, tables, l, state_to_slabs(state_ssm[:, l]))
        st_out.append(state_from_slabs(fin))
        s_out = ssm_glu(y_ctx, y_lat, z, row(ssm_d), w_glu_b, row(ssm_b_glu), l, tm)

        p_ctx = pool_mixer(z, 0, batch, seq, pool_w, row(pool_scale), l)
        p_lat = pool_mixer(z, m_ctx, dec_batch, dec_seq, pool_w, row(pool_scale), l)

        x = out_projection(a_ctx, a_lat, s_out, p_ctx, p_lat, xs, mod, w_out_b, l, row_of_tile, tm)
        xs = ffn(x, mod, row(norm2_g), ffn_w_in_b, ffn_w_out_b, final_norm_g[None], l, row_of_tile, tm, 512,
                 final_norm=last, split_rows=m_ctx if last else None)

    y_prompt = xs[0].reshape(batch, seq, d)
    y_sample = xs[1].reshape(dec_batch, dec_seq, d)
    return (y_prompt, y_sample, caches[0], caches[1], jnp.stack(st_out, axis=1))
```

```python
import functools

import jax
import jax.numpy as jnp
import numpy as np
from jax import lax
from jax.experimental import pallas as pl
from jax.experimental.pallas import tpu as pltpu

f32 = jnp.float32
bf16 = jnp.bfloat16

D_MODEL = 2048
N_HEADS = 16
HEAD_DIM = 64
ATTN_W = N_HEADS * HEAD_DIM
SSM_W = 512
SSM_GROUP = 16
SSM_GROUPS = 32
SSM_STATE = 64
POOL_W = 512
POOL_WINDOWS = (2, 4, 8, 16)
POOL_GROUP = 128
POOL_PAD = 16
IN_W = 3 * ATTN_W + SSM_W + POOL_W
GRID_W = 64
WIN_ROWS_MAX = 8
WIN_COLS = 16
RMS_EPS = 1e-6
NEG_INF = -1e30

FFN_SUB_ROWS = 512
LAT_BLOCK_ROWS = 4
SSM_CHUNK = 16
SSM_PAIRS = SSM_GROUPS // 2
SSM_QUAD = 4
SSM_PW = 2 * SSM_CHUNK * SSM_GROUP

VMEM_LIMIT = 56 * 1024 * 1024


def _cparams(sem):
    return pltpu.CompilerParams(dimension_semantics=sem, vmem_limit_bytes=VMEM_LIMIT)


def _rms(x, g):
    return x * lax.rsqrt(jnp.mean(x * x, axis=-1, keepdims=True) + RMS_EPS) * g


def _two_part(tm, width, tiles_a, tiles_b, col=lambda *_: 0):
    return (pl.BlockSpec((tm, width), lambda i, *r: (jnp.minimum(i, tiles_a - 1), col(*r))),
            pl.BlockSpec((tm, width), lambda i, *r: (jnp.clip(i - tiles_a, 0, tiles_b - 1), col(*r))))


def _row_specs(parts, tm, width, col=lambda *_: 0):
    if len(parts) == 1:
        return (pl.BlockSpec((tm, width), lambda i, *r: (i, col(*r))),)
    return _two_part(tm, width, parts[0].shape[0] // tm, parts[1].shape[0] // tm, col)


def _pick(first, a_ref, b_ref):
    return jnp.where(first, a_ref[...], b_ref[...])


def _rows(refs, tiles_a):
    if len(refs) == 1:
        return refs[0][...]
    return _pick(pl.program_id(0) < tiles_a, *refs)


def _mod_kernel(c_ref, w_ref, b_ref, o_ref):
    c = c_ref[...]
    s = (c * jax.nn.sigmoid(c)).astype(bf16)
    o_ref[0] = jnp.dot(s, w_ref[0].astype(bf16), preferred_element_type=f32) + b_ref[0]


def modulation_all(cond, w_mod, b_mod, tn=1024):
    n_layers, d, n = w_mod.shape
    return pl.pallas_call(
        _mod_kernel,
        grid=(n_layers, n // tn),
        in_specs=[
            pl.BlockSpec((8, d), lambda l, j: (0, 0)),
            pl.BlockSpec((1, d, tn), lambda l, j: (l, 0, j)),
            pl.BlockSpec((1, 1, tn), lambda l, j: (l, 0, j)),
        ],
        out_specs=pl.BlockSpec((1, 8, tn), lambda l, j: (l, 0, j)),
        out_shape=jax.ShapeDtypeStruct((n_layers, 8, n), f32),
        compiler_params=_cparams(("parallel", "parallel")),
        name="modulation",
    )(cond, w_mod, b_mod.reshape(n_layers, 1, n))


def _in_kernel(*refs, n_x, tiles_a, tn):
    x_refs = refs[:n_x]
    mod_ref, g_ref, w_ref, o_ref = refs[n_x:]
    m = mod_ref[0]
    y = _rms(_rows(x_refs, tiles_a), g_ref[...])
    h = (y * (1.0 + m[1:2]) + m[0:1]).astype(bf16)
    for c in range(w_ref.shape[1] // tn):
        cols = slice(c * tn, (c + 1) * tn)
        o_ref[:, cols] = jnp.dot(h, w_ref[:, cols], preferred_element_type=f32)


def in_projection(xs, mod, g, w, layer, row_of_tile, tm, tn=512):
    m_total = sum(x.shape[0] for x in xs)
    d = xs[0].shape[1]
    n = w.shape[2]
    return pl.pallas_call(
        functools.partial(_in_kernel, n_x=len(xs), tiles_a=xs[0].shape[0] // tm, tn=tn),
        grid=(m_total // tm,),
        in_specs=[
            *_row_specs(xs, tm, d),
            pl.BlockSpec((None, 1, 6, d), lambda i: (layer, row_of_tile(i), 0, 0)),
            pl.BlockSpec((None, 1, d), lambda i: (layer, 0, 0)),
            pl.BlockSpec((None, d, n), lambda i: (layer, 0, 0), pipeline_mode=pl.Buffered(1)),
        ],
        out_specs=pl.BlockSpec((tm, n), lambda i: (i, 0)),
        out_shape=jax.ShapeDtypeStruct((m_total, n), f32),
        compiler_params=_cparams(("arbitrary",)),
        name="in_projection",
    )(*xs, mod, g, w)


def _first_head(shape):
    return lax.broadcasted_iota(jnp.int32, shape, len(shape) - 1) < HEAD_DIM


def _attn_ctx_kernel(*refs, n_prev, write_caches):
    q_ref, k_ref, v_ref = refs[:3]
    prev = refs[3:3 + 2 * n_prev]
    outs = refs[3 + 2 * n_prev:]
    o_ref = outs[0]
    scale = HEAD_DIM ** -0.5
    nt = (((1,), (1,)), ((), ()))
    seq = q_ref.shape[0]
    lanes = 2 * HEAD_DIM
    first = _first_head((seq, lanes))
    for p in range(N_HEADS // 2):
        cols = slice(p * lanes, (p + 1) * lanes)
        q = q_ref[:, cols].astype(bf16)
        k = k_ref[:, cols]
        v = v_ref[:, cols]
        if write_caches:
            kc_ref, vc_ref = outs[1:]
            layers = [(prev[2 * l][:, cols], prev[2 * l + 1][:, cols]) for l in range(n_prev)] + [(k, v)]
            for l, (kl, vl) in enumerate(layers):
                for h in range(2):
                    kc_ref[0, l, 2 * p + h] = kl[:, h * HEAD_DIM:(h + 1) * HEAD_DIM]
                    vc_ref[0, l, 2 * p + h] = vl[:, h * HEAD_DIM:(h + 1) * HEAD_DIM]
        kb = k.astype(bf16)
        vb = v.astype(bf16)
        zero = jnp.zeros_like(kb)
        num, inv = None, []
        for h in range(2):
            keep = first if h == 0 else jnp.logical_not(first)
            s = lax.dot_general(q, jnp.where(keep, kb, zero), nt, preferred_element_type=f32) * scale
            e = jnp.exp(s - jnp.max(s, axis=-1, keepdims=True))
            inv.append(1.0 / jnp.sum(e, axis=-1, keepdims=True))
            part = jnp.dot(e.astype(bf16), jnp.where(keep, vb, zero), preferred_element_type=f32)
            num = part if num is None else num + part
        o_ref[:, cols] = (num * jnp.where(first, inv[0], inv[1])).astype(bf16)


def context_attention(z, batch, seq, prev_zs=(), write_caches=False):
    col = lambda blk: pl.BlockSpec((seq, ATTN_W), lambda b: (b, blk))
    out_specs = [pl.BlockSpec((seq, ATTN_W), lambda b: (b, 0))]
    out_shape = [jax.ShapeDtypeStruct((batch * seq, ATTN_W), bf16)]
    if write_caches:
        depth = len(prev_zs) + 1
        cache = pl.BlockSpec((1, depth, N_HEADS, seq, HEAD_DIM), lambda b: (b, 0, 0, 0, 0))
        cache_shape = jax.ShapeDtypeStruct((batch, depth, N_HEADS, seq, HEAD_DIM), f32)
        out_specs += [cache, cache]
        out_shape += [cache_shape, cache_shape]
    else:
        prev_zs = ()
    prev_args = [a for zp in prev_zs for a in (zp, zp)]
    return pl.pallas_call(
        functools.partial(_attn_ctx_kernel, n_prev=len(prev_zs), write_caches=write_caches),
        grid=(batch,),
        in_specs=[col(0), col(1), col(2)] + [col(1), col(2)] * len(prev_zs),
        out_specs=out_specs,
        out_shape=out_shape,
        compiler_params=_cparams(("parallel",)),
        name="context_attention",
    )(z, z, z, *prev_args)


def _attn_lat_kernel(q_ref, k_ref, v_ref, ck_ref, cv_ref, t_ref, o_ref, bias_scr, k_scr, v_scr, ck_scr, cv_scr,
                     *, rows, wr):
    scale = HEAD_DIM ** -0.5
    n_loc = wr * GRID_W
    for h in range(2):
        for d in range(wr):
            for j in range(wr):
                bias_scr[h, d, :, j * GRID_W:(j + 1) * GRID_W] = t_ref[h, j - d + WIN_ROWS_MAX - 1]

    kb = k_ref[...].astype(bf16)
    vb = v_ref[...].astype(bf16)
    first = _first_head(kb.shape)
    zero = jnp.zeros_like(kb)
    pad = jnp.zeros(ck_ref.shape[3:], bf16)
    for h in range(2):
        keep = first if h == 0 else jnp.logical_not(first)
        k_scr[h] = jnp.where(keep, kb, zero)
        v_scr[h] = jnp.where(keep, vb, zero)
        ck, cv = ck_ref[0, 0, h].astype(bf16), cv_ref[0, 0, h].astype(bf16)
        ck_scr[h] = jnp.concatenate([ck, pad] if h == 0 else [pad, ck], axis=-1)
        cv_scr[h] = jnp.concatenate([cv, pad] if h == 0 else [pad, cv], axis=-1)

    nt = (((1,), (1,)), ((), ()))
    br = LAT_BLOCK_ROWS
    nq = br * GRID_W
    first_q = _first_head((nq, 2 * HEAD_DIM))

    def block(blk, carry):
        q_rows = pl.ds(pl.multiple_of(blk * nq, nq), nq)
        q = q_ref[q_rows, :].astype(bf16)
        windows, offsets = [], []
        for i in range(br):
            r = blk * br + i
            rs = jnp.clip(r - wr // 2, 0, rows - wr)
            windows.append(pl.ds(pl.multiple_of(rs * GRID_W, GRID_W), n_loc))
            offsets.append(r - rs)
        num, inv = None, []
        for h in range(2):
            s_ctx = lax.dot_general(q, ck_scr[h], nt, preferred_element_type=f32) * scale
            s_loc = [lax.dot_general(q[i * GRID_W:(i + 1) * GRID_W], k_scr[h, windows[i], :], nt,
                                     preferred_element_type=f32) * scale + bias_scr[h, offsets[i]]
                     for i in range(br)]
            s_loc = jnp.concatenate(s_loc, axis=0)
            m = jnp.maximum(jnp.max(s_loc, axis=-1, keepdims=True), jnp.max(s_ctx, axis=-1, keepdims=True))
            e_loc = jnp.exp(s_loc - m)
            e_ctx = jnp.exp(s_ctx - m)
            inv.append(1.0 / (jnp.sum(e_loc, axis=-1, keepdims=True) + jnp.sum(e_ctx, axis=-1, keepdims=True)))
            e_loc = e_loc.astype(bf16)
            o_loc = [jnp.dot(e_loc[i * GRID_W:(i + 1) * GRID_W], v_scr[h, windows[i], :],
                             preferred_element_type=f32) for i in range(br)]
            part = (jnp.dot(e_ctx.astype(bf16), cv_scr[h], preferred_element_type=f32)
                    + jnp.concatenate(o_loc, axis=0))
            num = part if num is None else num + part
        o_ref[q_rows, :] = (num * jnp.where(first_q, inv[0], inv[1])).astype(bf16)
        return carry

    lax.fori_loop(0, rows // br, block, 0)


def latent_bias_table(rpb):
    cols = np.arange(GRID_W)
    col_start = np.clip(cols - WIN_COLS // 2, 0, GRID_W - WIN_COLS)
    valid = (cols[None, :] >= col_start[:, None]) & (cols[None, :] < col_start[:, None] + WIN_COLS)
    dc = cols[None, :] - cols[:, None] + WIN_COLS - 1
    assert np.all((dc[valid] >= 0) & (dc[valid] < 2 * WIN_COLS - 1))
    onehot = ((np.arange(2 * WIN_COLS - 1)[:, None, None] == dc[None]) & valid[None]).astype(np.float32)
    t = jnp.einsum('hrd,dqk->hrqk', rpb.astype(f32), jnp.asarray(onehot), precision=lax.Precision.HIGHEST)
    return jnp.where(jnp.asarray(valid)[None, None], t, NEG_INF)


def latent_attention(z, row0, batch, seq, cache_k, cache_v, layer, tables):
    lanes = 2 * HEAD_DIM
    n_pairs = N_HEADS // 2
    rows = seq // GRID_W
    wr = min(WIN_ROWS_MAX, rows)
    past = cache_k.shape[3]
    blk0 = row0 // seq
    n_dr = 2 * WIN_ROWS_MAX - 1
    assert rows % LAT_BLOCK_ROWS == 0
    col = lambda off: pl.BlockSpec((seq, lanes), lambda b, p: (blk0 + b, off + p))
    ctx = pl.BlockSpec((1, 1, 2, past, HEAD_DIM), lambda b, p: (b, layer, p, 0, 0))
    return pl.pallas_call(
        functools.partial(_attn_lat_kernel, rows=rows, wr=wr),
        grid=(batch, n_pairs),
        in_specs=[col(0), col(n_pairs), col(2 * n_pairs), ctx, ctx,
                  pl.BlockSpec((None, 2, n_dr, GRID_W, GRID_W), lambda b, p: (layer, p, 0, 0, 0))],
        out_specs=pl.BlockSpec((seq, lanes), lambda b, p: (b, p)),
        out_shape=jax.ShapeDtypeStruct((batch * seq, ATTN_W), bf16),
        scratch_shapes=[pltpu.VMEM((2, wr, GRID_W, wr * GRID_W), f32),
                        pltpu.VMEM((2, seq, lanes), bf16), pltpu.VMEM((2, seq, lanes), bf16),
                        pltpu.VMEM((2, past, lanes), bf16), pltpu.VMEM((2, past, lanes), bf16)],
        compiler_params=_cparams(("parallel", "parallel")),
        name="latent_attention",
    )(z, z, z, cache_k, cache_v, tables)


def ssm_tables(a_re, a_im, log_dt, b_re, b_im, c_re, c_im):
    lc, p, m = SSM_CHUNK, SSM_STATE, SSM_GROUP
    eye2 = jnp.eye(2, dtype=f32)
    half = SSM_PW // 2

    def slab_row(t):
        t = jnp.broadcast_to(t.astype(f32).reshape(2, 1, SSM_PAIRS, 2, p), (2, 2, SSM_PAIRS, 2, p))
        return t.transpose(2, 0, 1, 3, 4).reshape(SSM_PAIRS, SSM_PW)

    def embed(first, second):
        t = jnp.stack([first, second], axis=1).astype(f32)
        x = t.shape[-1]
        t = t.reshape(2, 2, SSM_PAIRS, 2, p, x).transpose(2, 3, 5, 0, 1, 4)
        t = t[:, :, :, :, :, None, :] * eye2[None, :, None, None, None, :, None]
        return t.reshape(SSM_PAIRS, 2 * x, SSM_PW)

    lr, li = slab_row(a_re), slab_row(a_im)
    dt = slab_row(jnp.broadcast_to(jnp.exp(log_dt.astype(f32))[..., None], a_re.shape))
    sr, si = lr * dt, li * dt
    k = jnp.arange(lc + 1, dtype=f32)[:, None, None]
    mag = jnp.exp(k * sr[None])
    ar, ai = mag * jnp.cos(k * si[None]), mag * jnp.sin(k * si[None])
    den = lr * lr + li * li
    qr = ((ar[1] - 1.0) * lr + ai[1] * li) / den
    qi = (ai[1] * lr - (ar[1] - 1.0) * li) / den
    b4, b4s = embed(b_re, b_im), embed(-b_im, b_re)
    bb4 = (qr[:, None] * b4 + qi[:, None] * b4s)[:, None]
    bb4s = (qr[:, None] * b4s - qi[:, None] * b4)[:, None]
    rev = np.arange(lc)[::-1]
    fwd = np.arange(lc)

    def powers(t, ks):
        return jnp.stack([t[int(k)] for k in ks], axis=0)

    def per_step(t, k_fwd, k_bwd):
        t = jnp.concatenate([powers(t, k_fwd)[..., :half], powers(t, k_bwd)[..., half:]], axis=-1)
        return t.transpose(1, 0, 2)[:, :, None, :]

    bx = bb4 * per_step(ar, rev, fwd) + bb4s * per_step(ai, rev, fwd)
    bx = bx.reshape(SSM_PAIRS, SSM_PW, SSM_PW)
    ct_re, ct_im = c_re.transpose(0, 1, 3, 2), c_im.transpose(0, 1, 3, 2)
    c4 = embed(ct_re, -ct_im)[:, None]
    c4s = embed(-ct_im, -ct_re)[:, None]
    cyt = c4 * per_step(ar, fwd + 1, lc - fwd) + c4s * per_step(ai, fwd + 1, lc - fwd)
    cyt = cyt.reshape(SSM_PAIRS, SSM_PW, SSM_PW)

    j = np.arange(2 * lc - 1)
    live = jnp.asarray(np.concatenate([np.broadcast_to((j >= lc - 1)[:, None], (2 * lc - 1, half)),
                                       np.broadcast_to((j <= lc - 1)[:, None], (2 * lc - 1, half))], axis=-1), f32)
    kf, kb = np.maximum(j - (lc - 1), 0), np.maximum(lc - 1 - j, 0)
    kt = (c4 * per_step(ar, kf, kb) + c4s * per_step(ai, kf, kb)) * live[None, :, None, :]
    kt = kt.reshape(SSM_PAIRS, (2 * lc - 1) * 2 * m, SSM_PW)
    strip = lax.dot_general(bb4[:, 0], kt, (((2,), (2,)), ((0,), (0,))),
                            precision=lax.Precision.HIGH)
    tz = jnp.stack([strip[:, :, (lc - 1 - s) * 2 * m:(lc - 1 - s) * 2 * m + SSM_PW] for s in range(lc)], axis=1)
    tz = tz.reshape(SSM_PAIRS, SSM_PW, SSM_PW)

    sw = 2 * p
    a16p = jnp.concatenate([ar[lc][:, :sw], ai[lc][:, sw:half], ar[lc][:, half:half + sw], ai[lc][:, half + sw:]],
                           axis=-1)[:, None, :]
    return tz.astype(bf16), bx.astype(bf16), cyt.astype(bf16), a16p


def _ssm_kernel(u_ref, tz_ref, bx_ref, cyt_ref, a_ref, h0_ref, y_ref, fin_ref, x_scr, s_scr, y_scr, *, n_seq, n_chunks):
    r = n_seq * n_chunks
    pc = 2 * SSM_GROUP
    sw = 2 * SSM_STATE
    xs = [u_ref[pl.ds(s, r, stride=SSM_CHUNK), :] for s in range(SSM_CHUNK)]
    for kq in range(SSM_QUAD):
        u = jnp.concatenate([x[:, kq * pc:(kq + 1) * pc] for x in xs], axis=-1).astype(bf16)
        x = jnp.dot(u, bx_ref[kq], preferred_element_type=f32)
        for i in range(4):
            x_scr[4 * kq + i] = x[:, i * sw:(i + 1) * sw]
        y_scr[:, kq * SSM_PW:(kq + 1) * SSM_PW] = jnp.dot(u, tz_ref[kq], preferred_element_type=f32)

    def scan(direction):
        offs = [kq * SSM_PW + 2 * direction * sw for kq in range(SSM_QUAD)]
        slabs = [4 * kq + 2 * direction for kq in range(SSM_QUAD)]
        coef = [(a_ref[kq][:, 2 * direction * sw:2 * direction * sw + sw],
                 a_ref[kq][:, 2 * direction * sw + sw:2 * direction * sw + 2 * sw]) for kq in range(SSM_QUAD)]

        def body(i, carry):
            c = i if direction == 0 else n_chunks - 1 - i
            rows = pl.ds(c, n_seq, stride=n_chunks)
            out = []
            for kq in range(SSM_QUAD):
                sr, si = carry[2 * kq], carry[2 * kq + 1]
                ar, ai = coef[kq]
                re, im = slabs[kq], slabs[kq] + 1
                s_scr[re, rows, :] = sr
                s_scr[im, rows, :] = si
                xr = x_scr[re, rows, :]
                xi = x_scr[im, rows, :]
                out += [ar * sr - ai * si + xr, ar * si + ai * sr + xi]
            return tuple(out)

        init = []
        for lo in offs:
            init += [h0_ref[:, lo:lo + sw], h0_ref[:, lo + sw:lo + 2 * sw]]
        fin = lax.fori_loop(0, n_chunks, body, tuple(init))
        for kq, lo in enumerate(offs):
            fin_ref[:, lo:lo + sw] = fin[2 * kq]
            fin_ref[:, lo + sw:lo + 2 * sw] = fin[2 * kq + 1]

    scan(0)
    scan(1)
    ys = []
    for kq in range(SSM_QUAD):
        cols = slice(kq * SSM_PW, (kq + 1) * SSM_PW)
        s_in = jnp.concatenate([s_scr[4 * kq + i] for i in range(4)], axis=-1).astype(bf16)
        ys.append(y_scr[:, cols] + lax.dot_general(s_in, cyt_ref[kq], (((1,), (1,)), ((), ())),
                                                   preferred_element_type=f32))
    for t in range(SSM_CHUNK):
        y_ref[pl.ds(t, r, stride=SSM_CHUNK), :] = jnp.concatenate([y[:, t * pc:(t + 1) * pc] for y in ys], axis=-1)


def ssm_scan(z, row0, n_seq, seq, tables, layer, h0):
    tz, bx, cyt, a16p = tables
    n_chunks = seq // SSM_CHUNK
    rows = n_seq * seq
    r = n_seq * n_chunks
    lanes = SSM_QUAD * 2 * SSM_GROUP
    u_blk = (3 * ATTN_W) // lanes
    qw = SSM_QUAD * SSM_PW
    table = pl.BlockSpec((None, SSM_QUAD, SSM_PW, SSM_PW), lambda q: (layer, q, 0, 0))
    return pl.pallas_call(
        functools.partial(_ssm_kernel, n_seq=n_seq, n_chunks=n_chunks),
        grid=(SSM_PAIRS // SSM_QUAD,),
        in_specs=[pl.BlockSpec((rows, lanes), lambda q: (row0 // rows, u_blk + q)),
                  table, table, table,
                  pl.BlockSpec((None, SSM_QUAD, 1, SSM_PW), lambda q: (layer, q, 0, 0)),
                  pl.BlockSpec((n_seq, qw), lambda q: (0, q))],
        out_specs=[pl.BlockSpec((rows, lanes), lambda q: (0, q)),
                   pl.BlockSpec((n_seq, qw), lambda q: (0, q))],
        out_shape=[jax.ShapeDtypeStruct((rows, SSM_W), f32),
                   jax.ShapeDtypeStruct((n_seq, SSM_PAIRS * SSM_PW), f32)],
        scratch_shapes=[pltpu.VMEM((4 * SSM_QUAD, r, 2 * SSM_STATE), f32),
                        pltpu.VMEM((4 * SSM_QUAD, r, 2 * SSM_STATE), f32), pltpu.VMEM((r, qw), f32)],
        compiler_params=_cparams(("parallel",)),
        name="ssm_scan",
    )(z, tz, bx, cyt, a16p, h0)


def state_to_slabs(st):
    batch = st.shape[0]
    t = st.reshape(batch, 2, SSM_PAIRS, 2, SSM_STATE, 2).transpose(0, 2, 1, 5, 3, 4)
    return t.reshape(batch, SSM_PAIRS * SSM_PW)


def state_from_slabs(fin):
    batch = fin.shape[0]
    t = fin.reshape(batch, SSM_PAIRS, 2, 2, 2, SSM_STATE)
    return t.transpose(0, 2, 1, 4, 5, 3).reshape(batch, 2, SSM_GROUPS, SSM_STATE, 2)


def _glu_kernel(ya_ref, yb_ref, u_ref, d_ref, w_ref, b_ref, o_ref, *, tiles_a):
    y = _pick(pl.program_id(0) < tiles_a, ya_ref, yb_ref) + d_ref[...] * u_ref[...]
    z = jnp.dot(jax.nn.gelu(y).astype(bf16), w_ref[...], preferred_element_type=f32) + b_ref[...]
    o_ref[...] = (z[:, :SSM_W] * jax.nn.sigmoid(z[:, SSM_W:])).astype(bf16)


def ssm_glu(y_a, y_b, z, d_skip, w_glu, b_glu, layer, tm):
    tiles_a, tiles_b = y_a.shape[0] // tm, y_b.shape[0] // tm
    m_total = y_a.shape[0] + y_b.shape[0]
    u_blk = (3 * ATTN_W) // SSM_W
    return pl.pallas_call(
        functools.partial(_glu_kernel, tiles_a=tiles_a),
        grid=(tiles_a + tiles_b,),
        in_specs=[
            *_two_part(tm, SSM_W, tiles_a, tiles_b),
            pl.BlockSpec((tm, SSM_W), lambda i: (i, u_blk)),
            pl.BlockSpec((None, 1, SSM_W), lambda i: (layer, 0, 0)),
            pl.BlockSpec((None, SSM_W, 2 * SSM_W), lambda i: (layer, 0, 0)),
            pl.BlockSpec((None, 1, 2 * SSM_W), lambda i: (layer, 0, 0)),
        ],
        out_specs=pl.BlockSpec((tm, SSM_W), lambda i: (i, 0)),
        out_shape=jax.ShapeDtypeStruct((m_total, SSM_W), bf16),
        compiler_params=_cparams(("arbitrary",)),
        name="ssm_glu",
    )(y_a, y_b, z, d_skip, w_glu, b_glu)


def _pool_kernel(p_ref, w_ref, sc_ref, o_ref, pad_ref, *, seq):
    zeros = jnp.zeros((POOL_PAD, POOL_W), f32)
    pad_ref[0:POOL_PAD, :] = zeros
    pad_ref[POOL_PAD + seq:, :] = zeros
    pad_ref[POOL_PAD:POOL_PAD + seq, :] = p_ref[...]
    t = lax.broadcasted_iota(jnp.int32, (seq, 1), 0)
    for g, win in enumerate(POOL_WINDOWS):
        cols = slice(g * POOL_GROUP, (g + 1) * POOL_GROUP)
        total = jnp.zeros((seq, POOL_GROUP), f32)
        for d in range(-(win // 2), win - win // 2):
            total = total + pad_ref[POOL_PAD + d:POOL_PAD + d + seq, cols]
        lo = jnp.clip(t - win // 2, 0, seq)
        hi = jnp.clip(t - win // 2 + win, 0, seq)
        mixed = total / (hi - lo).astype(f32) - p_ref[:, cols]
        out = jnp.dot(mixed.astype(bf16), w_ref[g].astype(bf16), preferred_element_type=f32)
        o_ref[:, cols] = (out * sc_ref[:, cols]).astype(bf16)


def pool_mixer(z, row0, batch, seq, w_pool, pool_scale, layer):
    p_blk = (3 * ATTN_W + SSM_W) // POOL_W
    blk0 = row0 // seq
    return pl.pallas_call(
        functools.partial(_pool_kernel, seq=seq),
        grid=(batch,),
        in_specs=[
            pl.BlockSpec((seq, POOL_W), lambda b: (blk0 + b, p_blk)),
            pl.BlockSpec((None, len(POOL_WINDOWS), POOL_GROUP, POOL_GROUP), lambda b: (layer, 0, 0, 0)),
            pl.BlockSpec((None, 1, POOL_W), lambda b: (layer, 0, 0)),
        ],
        out_specs=pl.BlockSpec((seq, POOL_W), lambda b: (b, 0)),
        out_shape=jax.ShapeDtypeStruct((batch * seq, POOL_W), bf16),
        scratch_shapes=[pltpu.VMEM((seq + 2 * POOL_PAD, POOL_W), f32)],
        compiler_params=_cparams(("parallel",)),
        name="pool_mixer",
    )(z, w_pool, pool_scale)


def _out_kernel(aa_ref, ab_ref, s_ref, pa_ref, pb_ref, *refs, n_x, tiles_a, tn):
    x_refs = refs[:n_x]
    mod_ref, w_ref, o_ref = refs[n_x:]
    first = pl.program_id(0) < tiles_a
    a = _pick(first, aa_ref, ab_ref)
    s = s_ref[...]
    p = _pick(first, pa_ref, pb_ref)
    x = _rows(x_refs, tiles_a)
    gate = mod_ref[0][2:3]
    for c in range(w_ref.shape[1] // tn):
        cols = slice(c * tn, (c + 1) * tn)
        acc = jnp.dot(a, w_ref[0:ATTN_W, cols], preferred_element_type=f32)
        acc = acc + jnp.dot(s, w_ref[ATTN_W:ATTN_W + SSM_W, cols], preferred_element_type=f32)
        acc = acc + jnp.dot(p, w_ref[ATTN_W + SSM_W:, cols], preferred_element_type=f32)
        o_ref[:, cols] = x[:, cols] + gate[:, cols] * acc


def out_projection(a_a, a_b, s, p_a, p_b, xs, mod, w, layer, row_of_tile, tm, tn=512):
    m_total = sum(x.shape[0] for x in xs)
    d = xs[0].shape[1]
    tiles_a, tiles_b = a_a.shape[0] // tm, a_b.shape[0] // tm
    assert len(xs) == 1 or xs[0].shape[0] == a_a.shape[0]
    return pl.pallas_call(
        functools.partial(_out_kernel, n_x=len(xs), tiles_a=tiles_a, tn=tn),
        grid=(m_total // tm,),
        in_specs=[
            *_two_part(tm, ATTN_W, tiles_a, tiles_b),
            pl.BlockSpec((tm, SSM_W), lambda i: (i, 0)),
            *_two_part(tm, POOL_W, tiles_a, tiles_b),
            *_row_specs(xs, tm, d),
            pl.BlockSpec((None, 1, 6, d), lambda i: (layer, row_of_tile(i), 0, 0)),
            pl.BlockSpec((None, d, d), lambda i: (layer, 0, 0), pipeline_mode=pl.Buffered(1)),
        ],
        out_specs=pl.BlockSpec((tm, d), lambda i: (i, 0)),
        out_shape=jax.ShapeDtypeStruct((m_total, d), f32),
        compiler_params=_cparams(("arbitrary",)),
        name="out_projection",
    )(a_a, a_b, s, p_a, p_b, *xs, mod, w)


def _ffn_kernel(x_ref, mod_ref, g_ref, wg_ref, wu_ref, wo_ref, fg_ref, *refs, n_out, tiles_a, final_norm, sub_tiles):
    o_refs = refs[:n_out]
    h_ref, acc_ref = refs[n_out:]
    i, j = pl.program_id(0), pl.program_id(1)
    last = j == pl.num_programs(1) - 1

    @pl.when(j == 0)
    def _():
        m = mod_ref[0]
        x = x_ref[...]
        h_ref[...] = (_rms(x, g_ref[...]) * (1.0 + m[4:5]) + m[3:4]).astype(bf16)
        acc_ref[...] = x

    g2 = mod_ref[0][5:6]
    sub = h_ref.shape[0] // sub_tiles
    for r in range(sub_tiles):
        rows = slice(r * sub, (r + 1) * sub)
        h = h_ref[rows, :]
        gate = jnp.dot(h, wg_ref[...], preferred_element_type=f32)
        up = jnp.dot(h, wu_ref[...], preferred_element_type=f32)
        act = (gate * jax.nn.sigmoid(gate) * up).astype(bf16)
        acc_ref[rows, :] += g2 * jnp.dot(act, wo_ref[...], preferred_element_type=f32)

    def result():
        return _rms(acc_ref[...], fg_ref[...]) if final_norm else acc_ref[...]

    if n_out == 1:
        @pl.when(last)
        def _():
            o_refs[0][...] = result()
    else:
        @pl.when(last & (i < tiles_a))
        def _():
            o_refs[0][...] = result()

        @pl.when(last & (i >= tiles_a))
        def _():
            o_refs[1][...] = result()


def ffn(x, mod, g, w_in, w_out, final_g, layer, row_of_tile, tm, th, final_norm, split_rows=None):
    m_total, d = x.shape
    hidden = w_out.shape[1]
    nh = hidden // th
    sub_tiles = max(1, tm // FFN_SUB_ROWS)
    big = pl.Buffered(1) if sub_tiles > 1 else None
    if split_rows is None:
        tiles_a = m_total // tm
        out_specs = [pl.BlockSpec((tm, d), lambda i, j: (i, 0), pipeline_mode=big)]
        out_shape = [jax.ShapeDtypeStruct((m_total, d), f32)]
    else:
        tiles_a = split_rows // tm
        out_specs = list(_two_part(tm, d, tiles_a, m_total // tm - tiles_a))
        out_shape = [jax.ShapeDtypeStruct((split_rows, d), f32), jax.ShapeDtypeStruct((m_total - split_rows, d), f32)]
    return pl.pallas_call(
        functools.partial(_ffn_kernel, n_out=len(out_specs), tiles_a=tiles_a, final_norm=final_norm,
                          sub_tiles=sub_tiles),
        grid=(m_total // tm, nh),
        in_specs=[
            pl.BlockSpec((tm, d), lambda i, j: (i, 0), pipeline_mode=big),
            pl.BlockSpec((None, 1, 6, d), lambda i, j: (layer, row_of_tile(i), 0, 0)),
            pl.BlockSpec((None, 1, d), lambda i, j: (layer, 0, 0)),
            pl.BlockSpec((None, d, th), lambda i, j: (layer, 0, j)),
            pl.BlockSpec((None, d, th), lambda i, j: (layer, 0, nh + j)),
            pl.BlockSpec((None, th, d), lambda i, j: (layer, j, 0)),
            pl.BlockSpec((1, d), lambda i, j: (0, 0)),
        ],
        out_specs=out_specs,
        out_shape=out_shape,
        scratch_shapes=[pltpu.VMEM((tm, d), bf16), pltpu.VMEM((tm, d), f32)],
        compiler_params=_cparams(("arbitrary", "arbitrary")),
        name="ffn",
    )(x, mod, g, w_in, w_in, w_out, final_g)


def kernel(x_prompt, x_sample, c, cache_k, cache_v, state_ssm, c_ctx, w_mod, b_mod, norm1_g, norm2_g, w_in,
           attn_rpb, ssm_a_re, ssm_a_im, ssm_log_dt, ssm_b_re, ssm_b_im, ssm_c_re, ssm_c_im, ssm_d, ssm_w_glu,
           ssm_b_glu, pool_w, pool_scale, w_out, ffn_w_in, ffn_w_out, final_norm_g):
    batch, seq, d = x_prompt.shape
    dec_batch, dec_seq, _ = x_sample.shape
    depth = w_in.shape[0]
    m_ctx = batch * seq
    tm, tm_big = 512, 1024
    assert m_ctx % tm_big == 0 and dec_seq % tm_big == 0 and dec_batch + 1 <= 8

    def cond_row(rows):
        return lambda i: jnp.where(i < m_ctx // rows, 0, 1 + (i - m_ctx // rows) // (dec_seq // rows))

    row_of_tile = cond_row(tm)

    cond = jnp.concatenate([c_ctx[None, :], c, jnp.zeros((8 - 1 - dec_batch, d), f32)], axis=0)
    mod = modulation_all(cond, w_mod, b_mod)
    mod = mod[:, :1 + dec_batch].reshape(depth, 1 + dec_batch, 6, d)

    w_in_b, w_out_b = w_in.astype(bf16), w_out.astype(bf16)
    ffn_w_in_b, ffn_w_out_b, w_glu_b = ffn_w_in.astype(bf16), ffn_w_out.astype(bf16), ssm_w_glu.astype(bf16)
    row = lambda t: t[:, None, :]
    tables = jax.vmap(ssm_tables)(ssm_a_re, ssm_a_im, ssm_log_dt, ssm_b_re, ssm_b_im, ssm_c_re, ssm_c_im)
    bias_tables = jax.vmap(latent_bias_table)(attn_rpb)

    xs = (x_prompt.reshape(m_ctx, d), x_sample.reshape(dec_batch * dec_seq, d))
    h0_ctx = jnp.zeros((batch, SSM_PAIRS * SSM_PW), f32)

    zs, st_out = [], []
    for l in range(depth):
        last = l == depth - 1
        z = in_projection(xs, mod, row(norm1_g), w_in_b, l, row_of_tile, tm)

        a_ctx, *caches = context_attention(z, batch, seq, zs, write_caches=last)
        zs.append(z)
        a_lat = latent_attention(z, m_ctx, dec_batch, dec_seq, cache_k, cache_v, l, bias_tables)

        y_ctx, fin = ssm_scan(z, 0, batch, seq, tables, l, h0_ctx)
        y_lat, _ = ssm_scan(z, m_ctx, dec_batch, dec_seq, tables, l, state_to_slabs(state_ssm[:, l]))
        st_out.append(state_from_slabs(fin))
        s_out = ssm_glu(y_ctx, y_lat, z, row(ssm_d), w_glu_b, row(ssm_b_glu), l, tm)

        p_ctx = pool_mixer(z, 0, batch, seq, pool_w, row(pool_scale), l)
        p_lat = pool_mixer(z, m_ctx, dec_batch, dec_seq, pool_w, row(pool_scale), l)

        x = out_projection(a_ctx, a_lat, s_out, p_ctx, p_lat, xs, mod, w_out_b, l, row_of_tile, tm)
        tm_ffn = tm if last else tm_big
        xs = ffn(x, mod, row(norm2_g), ffn_w_in_b, ffn_w_out_b, final_norm_g[None], l, cond_row(tm_ffn), tm_ffn, 512,
                 final_norm=last, split_rows=m_ctx if last else None)

    y_prompt = xs[0].reshape(batch, seq, d)
    y_sample = xs[1].reshape(dec_batch, dec_seq, d)
    return (y_prompt, y_sample, caches[0], caches[1], jnp.stack(st_out, axis=1))
```

```python
import functools

import jax
import jax.numpy as jnp
import numpy as np
from jax import lax
from jax.experimental import pallas as pl
from jax.experimental.pallas import tpu as pltpu

f32 = jnp.float32
bf16 = jnp.bfloat16

D_MODEL = 2048
N_HEADS = 16
HEAD_DIM = 64
ATTN_W = N_HEADS * HEAD_DIM
SSM_W = 512
SSM_GROUP = 16
SSM_GROUPS = 32
SSM_STATE = 64
POOL_W = 512
POOL_WINDOWS = (2, 4, 8, 16)
POOL_GROUP = 128
POOL_PAD = 16
IN_W = 3 * ATTN_W + SSM_W + POOL_W
GRID_W = 64
WIN_ROWS_MAX = 8
WIN_COLS = 16
RMS_EPS = 1e-6
NEG_INF = -1e30

LAT_BLOCK_ROWS = 4
SSM_CHUNK = 16
SSM_PAIRS = SSM_GROUPS // 2
SSM_QUAD = 4
SSM_PW = 2 * SSM_CHUNK * SSM_GROUP

VMEM_LIMIT = 56 * 1024 * 1024


def _cparams(sem):
    return pltpu.CompilerParams(dimension_semantics=sem, vmem_limit_bytes=VMEM_LIMIT)


def _rms(x, g):
    return x * lax.rsqrt(jnp.mean(x * x, axis=-1, keepdims=True) + RMS_EPS) * g


def _two_part(tm, width, tiles_a, tiles_b, col=lambda *_: 0):
    return (pl.BlockSpec((tm, width), lambda i, *r: (jnp.minimum(i, tiles_a - 1), col(*r))),
            pl.BlockSpec((tm, width), lambda i, *r: (jnp.clip(i - tiles_a, 0, tiles_b - 1), col(*r))))


def _row_specs(parts, tm, width, col=lambda *_: 0):
    if len(parts) == 1:
        return (pl.BlockSpec((tm, width), lambda i, *r: (i, col(*r))),)
    return _two_part(tm, width, parts[0].shape[0] // tm, parts[1].shape[0] // tm, col)


def _pick(first, a_ref, b_ref):
    return jnp.where(first, a_ref[...], b_ref[...])


def _rows(refs, tiles_a):
    if len(refs) == 1:
        return refs[0][...]
    return _pick(pl.program_id(0) < tiles_a, *refs)


def _mod_kernel(c_ref, w_ref, b_ref, o_ref):
    c = c_ref[...]
    s = (c * jax.nn.sigmoid(c)).astype(bf16)
    o_ref[0] = jnp.dot(s, w_ref[0].astype(bf16), preferred_element_type=f32) + b_ref[0]


def modulation_all(cond, w_mod, b_mod, tn=1024):
    n_layers, d, n = w_mod.shape
    return pl.pallas_call(
        _mod_kernel,
        grid=(n_layers, n // tn),
        in_specs=[
            pl.BlockSpec((8, d), lambda l, j: (0, 0)),
            pl.BlockSpec((1, d, tn), lambda l, j: (l, 0, j)),
            pl.BlockSpec((1, 1, tn), lambda l, j: (l, 0, j)),
        ],
        out_specs=pl.BlockSpec((1, 8, tn), lambda l, j: (l, 0, j)),
        out_shape=jax.ShapeDtypeStruct((n_layers, 8, n), f32),
        compiler_params=_cparams(("parallel", "parallel")),
        name="modulation",
    )(cond, w_mod, b_mod.reshape(n_layers, 1, n))


def _in_kernel(*refs, n_x, tiles_a, tn):
    x_refs = refs[:n_x]
    mod_ref, g_ref, w_ref, o_ref = refs[n_x:]
    m = mod_ref[0]
    y = _rms(_rows(x_refs, tiles_a), g_ref[...])
    h = (y * (1.0 + m[1:2]) + m[0:1]).astype(bf16)
    for c in range(w_ref.shape[1] // tn):
        cols = slice(c * tn, (c + 1) * tn)
        o_ref[:, cols] = jnp.dot(h, w_ref[:, cols], preferred_element_type=f32)


def in_projection(xs, mod, g, w, layer, row_of_tile, tm, tn=512):
    m_total = sum(x.shape[0] for x in xs)
    d = xs[0].shape[1]
    n = w.shape[2]
    return pl.pallas_call(
        functools.partial(_in_kernel, n_x=len(xs), tiles_a=xs[0].shape[0] // tm, tn=tn),
        grid=(m_total // tm,),
        in_specs=[
            *_row_specs(xs, tm, d),
            pl.BlockSpec((None, 1, 6, d), lambda i: (layer, row_of_tile(i), 0, 0)),
            pl.BlockSpec((None, 1, d), lambda i: (layer, 0, 0)),
            pl.BlockSpec((None, d, n), lambda i: (layer, 0, 0), pipeline_mode=pl.Buffered(1)),
        ],
        out_specs=pl.BlockSpec((tm, n), lambda i: (i, 0)),
        out_shape=jax.ShapeDtypeStruct((m_total, n), f32),
        compiler_params=_cparams(("arbitrary",)),
        name="in_projection",
    )(*xs, mod, g, w)


def _first_head(shape):
    return lax.broadcasted_iota(jnp.int32, shape, len(shape) - 1) < HEAD_DIM


def _attn_ctx_kernel(*refs, n_prev, write_caches):
    q_ref, k_ref, v_ref = refs[:3]
    prev = refs[3:3 + 2 * n_prev]
    outs = refs[3 + 2 * n_prev:]
    o_ref = outs[0]
    scale = HEAD_DIM ** -0.5
    nt = (((1,), (1,)), ((), ()))
    seq = q_ref.shape[0]
    lanes = 2 * HEAD_DIM
    first = _first_head((seq, lanes))
    for p in range(N_HEADS // 2):
        cols = slice(p * lanes, (p + 1) * lanes)
        q = q_ref[:, cols].astype(bf16)
        k = k_ref[:, cols]
        v = v_ref[:, cols]
        if write_caches:
            kc_ref, vc_ref = outs[1:]
            layers = [(prev[2 * l][:, cols], prev[2 * l + 1][:, cols]) for l in range(n_prev)] + [(k, v)]
            for l, (kl, vl) in enumerate(layers):
                for h in range(2):
                    kc_ref[0, l, 2 * p + h] = kl[:, h * HEAD_DIM:(h + 1) * HEAD_DIM]
                    vc_ref[0, l, 2 * p + h] = vl[:, h * HEAD_DIM:(h + 1) * HEAD_DIM]
        kb = k.astype(bf16)
        vb = v.astype(bf16)
        zero = jnp.zeros_like(kb)
        num, inv = None, []
        for h in range(2):
            keep = first if h == 0 else jnp.logical_not(first)
            s = lax.dot_general(q, jnp.where(keep, kb, zero), nt, preferred_element_type=f32) * scale
            e = jnp.exp(s - jnp.max(s, axis=-1, keepdims=True))
            inv.append(1.0 / jnp.sum(e, axis=-1, keepdims=True))
            part = jnp.dot(e.astype(bf16), jnp.where(keep, vb, zero), preferred_element_type=f32)
            num = part if num is None else num + part
        o_ref[:, cols] = (num * jnp.where(first, inv[0], inv[1])).astype(bf16)


def context_attention(z, batch, seq, prev_zs=(), write_caches=False):
    col = lambda blk: pl.BlockSpec((seq, ATTN_W), lambda b: (b, blk))
    out_specs = [pl.BlockSpec((seq, ATTN_W), lambda b: (b, 0))]
    out_shape = [jax.ShapeDtypeStruct((batch * seq, ATTN_W), bf16)]
    if write_caches:
        depth = len(prev_zs) + 1
        cache = pl.BlockSpec((1, depth, N_HEADS, seq, HEAD_DIM), lambda b: (b, 0, 0, 0, 0))
        cache_shape = jax.ShapeDtypeStruct((batch, depth, N_HEADS, seq, HEAD_DIM), f32)
        out_specs += [cache, cache]
        out_shape += [cache_shape, cache_shape]
    else:
        prev_zs = ()
    prev_args = [a for zp in prev_zs for a in (zp, zp)]
    return pl.pallas_call(
        functools.partial(_attn_ctx_kernel, n_prev=len(prev_zs), write_caches=write_caches),
        grid=(batch,),
        in_specs=[col(0), col(1), col(2)] + [col(1), col(2)] * len(prev_zs),
        out_specs=out_specs,
        out_shape=out_shape,
        compiler_params=_cparams(("parallel",)),
        name="context_attention",
    )(z, z, z, *prev_args)


def _attn_lat_kernel(q_ref, k_ref, v_ref, ck_ref, cv_ref, t_ref, o_ref, bias_scr, k_scr, v_scr, ck_scr, cv_scr,
                     *, rows, wr):
    scale = HEAD_DIM ** -0.5
    n_loc = wr * GRID_W
    for h in range(2):
        for d in range(wr):
            for j in range(wr):
                bias_scr[h, d, :, j * GRID_W:(j + 1) * GRID_W] = t_ref[h, j - d + WIN_ROWS_MAX - 1]

    kb = k_ref[...].astype(bf16)
    vb = v_ref[...].astype(bf16)
    first = _first_head(kb.shape)
    zero = jnp.zeros_like(kb)
    pad = jnp.zeros(ck_ref.shape[3:], bf16)
    for h in range(2):
        keep = first if h == 0 else jnp.logical_not(first)
        k_scr[h] = jnp.where(keep, kb, zero)
        v_scr[h] = jnp.where(keep, vb, zero)
        ck, cv = ck_ref[0, 0, h].astype(bf16), cv_ref[0, 0, h].astype(bf16)
        ck_scr[h] = jnp.concatenate([ck, pad] if h == 0 else [pad, ck], axis=-1)
        cv_scr[h] = jnp.concatenate([cv, pad] if h == 0 else [pad, cv], axis=-1)

    nt = (((1,), (1,)), ((), ()))
    br = LAT_BLOCK_ROWS
    nq = br * GRID_W
    first_q = _first_head((nq, 2 * HEAD_DIM))

    def block(blk, carry):
        q_rows = pl.ds(pl.multiple_of(blk * nq, nq), nq)
        q = q_ref[q_rows, :].astype(bf16)
        windows, offsets = [], []
        for i in range(br):
            r = blk * br + i
            rs = jnp.clip(r - wr // 2, 0, rows - wr)
            windows.append(pl.ds(pl.multiple_of(rs * GRID_W, GRID_W), n_loc))
            offsets.append(r - rs)
        num, inv = None, []
        for h in range(2):
            s_ctx = lax.dot_general(q, ck_scr[h], nt, preferred_element_type=f32) * scale
            s_loc = [lax.dot_general(q[i * GRID_W:(i + 1) * GRID_W], k_scr[h, windows[i], :], nt,
                                     preferred_element_type=f32) * scale + bias_scr[h, offsets[i]]
                     for i in range(br)]
            s_loc = jnp.concatenate(s_loc, axis=0)
            m = jnp.maximum(jnp.max(s_loc, axis=-1, keepdims=True), jnp.max(s_ctx, axis=-1, keepdims=True))
            e_loc = jnp.exp(s_loc - m)
            e_ctx = jnp.exp(s_ctx - m)
            inv.append(1.0 / (jnp.sum(e_loc, axis=-1, keepdims=True) + jnp.sum(e_ctx, axis=-1, keepdims=True)))
            e_loc = e_loc.astype(bf16)
            o_loc = [jnp.dot(e_loc[i * GRID_W:(i + 1) * GRID_W], v_scr[h, windows[i], :],
                             preferred_element_type=f32) for i in range(br)]
            part = (jnp.dot(e_ctx.astype(bf16), cv_scr[h], preferred_element_type=f32)
                    + jnp.concatenate(o_loc, axis=0))
            num = part if num is None else num + part
        o_ref[q_rows, :] = (num * jnp.where(first_q, inv[0], inv[1])).astype(bf16)
        return carry

    lax.fori_loop(0, rows // br, block, 0)


def latent_bias_table(rpb):
    cols = np.arange(GRID_W)
    col_start = np.clip(cols - WIN_COLS // 2, 0, GRID_W - WIN_COLS)
    valid = (cols[None, :] >= col_start[:, None]) & (cols[None, :] < col_start[:, None] + WIN_COLS)
    dc = cols[None, :] - cols[:, None] + WIN_COLS - 1
    assert np.all((dc[valid] >= 0) & (dc[valid] < 2 * WIN_COLS - 1))
    onehot = ((np.arange(2 * WIN_COLS - 1)[:, None, None] == dc[None]) & valid[None]).astype(np.float32)
    t = jnp.einsum('hrd,dqk->hrqk', rpb.astype(f32), jnp.asarray(onehot), precision=lax.Precision.HIGHEST)
    return jnp.where(jnp.asarray(valid)[None, None], t, NEG_INF)


def latent_attention(z, row0, batch, seq, cache_k, cache_v, layer, tables):
    lanes = 2 * HEAD_DIM
    n_pairs = N_HEADS // 2
    rows = seq // GRID_W
    wr = min(WIN_ROWS_MAX, rows)
    past = cache_k.shape[3]
    blk0 = row0 // seq
    n_dr = 2 * WIN_ROWS_MAX - 1
    assert rows % LAT_BLOCK_ROWS == 0
    col = lambda off: pl.BlockSpec((seq, lanes), lambda b, p: (blk0 + b, off + p))
    ctx = pl.BlockSpec((1, 1, 2, past, HEAD_DIM), lambda b, p: (b, layer, p, 0, 0))
    return pl.pallas_call(
        functools.partial(_attn_lat_kernel, rows=rows, wr=wr),
        grid=(batch, n_pairs),
        in_specs=[col(0), col(n_pairs), col(2 * n_pairs), ctx, ctx,
                  pl.BlockSpec((None, 2, n_dr, GRID_W, GRID_W), lambda b, p: (layer, p, 0, 0, 0))],
        out_specs=pl.BlockSpec((seq, lanes), lambda b, p: (b, p)),
        out_shape=jax.ShapeDtypeStruct((batch * seq, ATTN_W), bf16),
        scratch_shapes=[pltpu.VMEM((2, wr, GRID_W, wr * GRID_W), f32),
                        pltpu.VMEM((2, seq, lanes), bf16), pltpu.VMEM((2, seq, lanes), bf16),
                        pltpu.VMEM((2, past, lanes), bf16), pltpu.VMEM((2, past, lanes), bf16)],
        compiler_params=_cparams(("parallel", "parallel")),
        name="latent_attention",
    )(z, z, z, cache_k, cache_v, tables)


def ssm_tables(a_re, a_im, log_dt, b_re, b_im, c_re, c_im):
    lc, p, m = SSM_CHUNK, SSM_STATE, SSM_GROUP
    eye2 = jnp.eye(2, dtype=f32)
    half = SSM_PW // 2

    def slab_row(t):
        t = jnp.broadcast_to(t.astype(f32).reshape(2, 1, SSM_PAIRS, 2, p), (2, 2, SSM_PAIRS, 2, p))
        return t.transpose(2, 0, 1, 3, 4).reshape(SSM_PAIRS, SSM_PW)

    def embed(first, second):
        t = jnp.stack([first, second], axis=1).astype(f32)
        x = t.shape[-1]
        t = t.reshape(2, 2, SSM_PAIRS, 2, p, x).transpose(2, 3, 5, 0, 1, 4)
        t = t[:, :, :, :, :, None, :] * eye2[None, :, None, None, None, :, None]
        return t.reshape(SSM_PAIRS, 2 * x, SSM_PW)

    lr, li = slab_row(a_re), slab_row(a_im)
    dt = slab_row(jnp.broadcast_to(jnp.exp(log_dt.astype(f32))[..., None], a_re.shape))
    sr, si = lr * dt, li * dt
    k = jnp.arange(lc + 1, dtype=f32)[:, None, None]
    mag = jnp.exp(k * sr[None])
    ar, ai = mag * jnp.cos(k * si[None]), mag * jnp.sin(k * si[None])
    den = lr * lr + li * li
    qr = ((ar[1] - 1.0) * lr + ai[1] * li) / den
    qi = (ai[1] * lr - (ar[1] - 1.0) * li) / den
    b4, b4s = embed(b_re, b_im), embed(-b_im, b_re)
    bb4 = (qr[:, None] * b4 + qi[:, None] * b4s)[:, None]
    bb4s = (qr[:, None] * b4s - qi[:, None] * b4)[:, None]
    rev = np.arange(lc)[::-1]
    fwd = np.arange(lc)

    def powers(t, ks):
        return jnp.stack([t[int(k)] for k in ks], axis=0)

    def per_step(t, k_fwd, k_bwd):
        t = jnp.concatenate([powers(t, k_fwd)[..., :half], powers(t, k_bwd)[..., half:]], axis=-1)
        return t.transpose(1, 0, 2)[:, :, None, :]

    bx = bb4 * per_step(ar, rev, fwd) + bb4s * per_step(ai, rev, fwd)
    bx = bx.reshape(SSM_PAIRS, SSM_PW, SSM_PW)
    ct_re, ct_im = c_re.transpose(0, 1, 3, 2), c_im.transpose(0, 1, 3, 2)
    c4 = embed(ct_re, -ct_im)[:, None]
    c4s = embed(-ct_im, -ct_re)[:, None]
    cyt = c4 * per_step(ar, fwd + 1, lc - fwd) + c4s * per_step(ai, fwd + 1, lc - fwd)
    cyt = cyt.reshape(SSM_PAIRS, SSM_PW, SSM_PW)

    j = np.arange(2 * lc - 1)
    live = jnp.asarray(np.concatenate([np.broadcast_to((j >= lc - 1)[:, None], (2 * lc - 1, half)),
                                       np.broadcast_to((j <= lc - 1)[:, None], (2 * lc - 1, half))], axis=-1), f32)
    kf, kb = np.maximum(j - (lc - 1), 0), np.maximum(lc - 1 - j, 0)
    kt = (c4 * per_step(ar, kf, kb) + c4s * per_step(ai, kf, kb)) * live[None, :, None, :]
    kt = kt.reshape(SSM_PAIRS, (2 * lc - 1) * 2 * m, SSM_PW)
    strip = lax.dot_general(bb4[:, 0], kt, (((2,), (2,)), ((0,), (0,))),
                            precision=lax.Precision.HIGH)
    tz = jnp.stack([strip[:, :, (lc - 1 - s) * 2 * m:(lc - 1 - s) * 2 * m + SSM_PW] for s in range(lc)], axis=1)
    tz = tz.reshape(SSM_PAIRS, SSM_PW, SSM_PW)

    sw = 2 * p
    a16p = jnp.concatenate([ar[lc][:, :sw], ai[lc][:, sw:half], ar[lc][:, half:half + sw], ai[lc][:, half + sw:]],
                           axis=-1)[:, None, :]
    return tz.astype(bf16), bx.astype(bf16), cyt.astype(bf16), a16p


def _ssm_kernel(u_ref, tz_ref, bx_ref, cyt_ref, a_ref, h0_ref, y_ref, fin_ref, x_scr, s_scr, y_scr, *, n_seq, n_chunks):
    r = n_seq * n_chunks
    pc = 2 * SSM_GROUP
    sw = 2 * SSM_STATE
    xs = [u_ref[pl.ds(s, r, stride=SSM_CHUNK), :] for s in range(SSM_CHUNK)]
    for kq in range(SSM_QUAD):
        u = jnp.concatenate([x[:, kq * pc:(kq + 1) * pc] for x in xs], axis=-1).astype(bf16)
        x = jnp.dot(u, bx_ref[kq], preferred_element_type=f32)
        for i in range(4):
            x_scr[4 * kq + i] = x[:, i * sw:(i + 1) * sw]
        y_scr[:, kq * SSM_PW:(kq + 1) * SSM_PW] = jnp.dot(u, tz_ref[kq], preferred_element_type=f32)

    def scan(direction):
        offs = [kq * SSM_PW + 2 * direction * sw for kq in range(SSM_QUAD)]
        slabs = [4 * kq + 2 * direction for kq in range(SSM_QUAD)]
        coef = [(a_ref[kq][:, 2 * direction * sw:2 * direction * sw + sw],
                 a_ref[kq][:, 2 * direction * sw + sw:2 * direction * sw + 2 * sw]) for kq in range(SSM_QUAD)]

        def body(i, carry):
            c = i if direction == 0 else n_chunks - 1 - i
            rows = pl.ds(c, n_seq, stride=n_chunks)
            out = []
            for kq in range(SSM_QUAD):
                sr, si = carry[2 * kq], carry[2 * kq + 1]
                ar, ai = coef[kq]
                re, im = slabs[kq], slabs[kq] + 1
                s_scr[re, rows, :] = sr
                s_scr[im, rows, :] = si
                xr = x_scr[re, rows, :]
                xi = x_scr[im, rows, :]
                out += [ar * sr - ai * si + xr, ar * si + ai * sr + xi]
            return tuple(out)

        init = []
        for lo in offs:
            init += [h0_ref[:, lo:lo + sw], h0_ref[:, lo + sw:lo + 2 * sw]]
        fin = lax.fori_loop(0, n_chunks, body, tuple(init))
        for kq, lo in enumerate(offs):
            fin_ref[:, lo:lo + sw] = fin[2 * kq]
            fin_ref[:, lo + sw:lo + 2 * sw] = fin[2 * kq + 1]

    scan(0)
    scan(1)
    ys = []
    for kq in range(SSM_QUAD):
        cols = slice(kq * SSM_PW, (kq + 1) * SSM_PW)
        s_in = jnp.concatenate([s_scr[4 * kq + i] for i in range(4)], axis=-1).astype(bf16)
        ys.append(y_scr[:, cols] + lax.dot_general(s_in, cyt_ref[kq], (((1,), (1,)), ((), ())),
                                                   preferred_element_type=f32))
    for t in range(SSM_CHUNK):
        y_ref[pl.ds(t, r, stride=SSM_CHUNK), :] = jnp.concatenate([y[:, t * pc:(t + 1) * pc] for y in ys], axis=-1)


def ssm_scan(z, row0, n_seq, seq, tables, layer, h0):
    tz, bx, cyt, a16p = tables
    n_chunks = seq // SSM_CHUNK
    rows = n_seq * seq
    r = n_seq * n_chunks
    lanes = SSM_QUAD * 2 * SSM_GROUP
    u_blk = (3 * ATTN_W) // lanes
    qw = SSM_QUAD * SSM_PW
    table = pl.BlockSpec((None, SSM_QUAD, SSM_PW, SSM_PW), lambda q: (layer, q, 0, 0))
    return pl.pallas_call(
        functools.partial(_ssm_kernel, n_seq=n_seq, n_chunks=n_chunks),
        grid=(SSM_PAIRS // SSM_QUAD,),
        in_specs=[pl.BlockSpec((rows, lanes), lambda q: (row0 // rows, u_blk + q)),
                  table, table, table,
                  pl.BlockSpec((None, SSM_QUAD, 1, SSM_PW), lambda q: (layer, q, 0, 0)),
                  pl.BlockSpec((n_seq, qw), lambda q: (0, q))],
        out_specs=[pl.BlockSpec((rows, lanes), lambda q: (0, q)),
                   pl.BlockSpec((n_seq, qw), lambda q: (0, q))],
        out_shape=[jax.ShapeDtypeStruct((rows, SSM_W), f32),
                   jax.ShapeDtypeStruct((n_seq, SSM_PAIRS * SSM_PW), f32)],
        scratch_shapes=[pltpu.VMEM((4 * SSM_QUAD, r, 2 * SSM_STATE), f32),
                        pltpu.VMEM((4 * SSM_QUAD, r, 2 * SSM_STATE), f32), pltpu.VMEM((r, qw), f32)],
        compiler_params=_cparams(("parallel",)),
        name="ssm_scan",
    )(z, tz, bx, cyt, a16p, h0)


def state_to_slabs(st):
    batch = st.shape[0]
    t = st.reshape(batch, 2, SSM_PAIRS, 2, SSM_STATE, 2).transpose(0, 2, 1, 5, 3, 4)
    return t.reshape(batch, SSM_PAIRS * SSM_PW)


def state_from_slabs(fin):
    batch = fin.shape[0]
    t = fin.reshape(batch, SSM_PAIRS, 2, 2, 2, SSM_STATE)
    return t.transpose(0, 2, 1, 4, 5, 3).reshape(batch, 2, SSM_GROUPS, SSM_STATE, 2)


def _glu_kernel(ya_ref, yb_ref, u_ref, d_ref, w_ref, b_ref, o_ref, *, tiles_a):
    y = _pick(pl.program_id(0) < tiles_a, ya_ref, yb_ref) + d_ref[...] * u_ref[...]
    z = jnp.dot(jax.nn.gelu(y).astype(bf16), w_ref[...], preferred_element_type=f32) + b_ref[...]
    o_ref[...] = (z[:, :SSM_W] * jax.nn.sigmoid(z[:, SSM_W:])).astype(bf16)


def ssm_glu(y_a, y_b, z, d_skip, w_glu, b_glu, layer, tm):
    tiles_a, tiles_b = y_a.shape[0] // tm, y_b.shape[0] // tm
    m_total = y_a.shape[0] + y_b.shape[0]
    u_blk = (3 * ATTN_W) // SSM_W
    return pl.pallas_call(
        functools.partial(_glu_kernel, tiles_a=tiles_a),
        grid=(tiles_a + tiles_b,),
        in_specs=[
            *_two_part(tm, SSM_W, tiles_a, tiles_b),
            pl.BlockSpec((tm, SSM_W), lambda i: (i, u_blk)),
            pl.BlockSpec((None, 1, SSM_W), lambda i: (layer, 0, 0)),
            pl.BlockSpec((None, SSM_W, 2 * SSM_W), lambda i: (layer, 0, 0)),
            pl.BlockSpec((None, 1, 2 * SSM_W), lambda i: (layer, 0, 0)),
        ],
        out_specs=pl.BlockSpec((tm, SSM_W), lambda i: (i, 0)),
        out_shape=jax.ShapeDtypeStruct((m_total, SSM_W), bf16),
        compiler_params=_cparams(("arbitrary",)),
        name="ssm_glu",
    )(y_a, y_b, z, d_skip, w_glu, b_glu)


def _pool_kernel(p_ref, w_ref, sc_ref, o_ref, pad_ref, *, seq):
    zeros = jnp.zeros((POOL_PAD, POOL_W), f32)
    pad_ref[0:POOL_PAD, :] = zeros
    pad_ref[POOL_PAD + seq:, :] = zeros
    pad_ref[POOL_PAD:POOL_PAD + seq, :] = p_ref[...]
    t = lax.broadcasted_iota(jnp.int32, (seq, 1), 0)
    for g, win in enumerate(POOL_WINDOWS):
        cols = slice(g * POOL_GROUP, (g + 1) * POOL_GROUP)
        total = jnp.zeros((seq, POOL_GROUP), f32)
        for d in range(-(win // 2), win - win // 2):
            total = total + pad_ref[POOL_PAD + d:POOL_PAD + d + seq, cols]
        lo = jnp.clip(t - win // 2, 0, seq)
        hi = jnp.clip(t - win // 2 + win, 0, seq)
        mixed = total / (hi - lo).astype(f32) - p_ref[:, cols]
        out = jnp.dot(mixed.astype(bf16), w_ref[g].astype(bf16), preferred_element_type=f32)
        o_ref[:, cols] = (out * sc_ref[:, cols]).astype(bf16)


def pool_mixer(z, row0, batch, seq, w_pool, pool_scale, layer):
    p_blk = (3 * ATTN_W + SSM_W) // POOL_W
    blk0 = row0 // seq
    return pl.pallas_call(
        functools.partial(_pool_kernel, seq=seq),
        grid=(batch,),
        in_specs=[
            pl.BlockSpec((seq, POOL_W), lambda b: (blk0 + b, p_blk)),
            pl.BlockSpec((None, len(POOL_WINDOWS), POOL_GROUP, POOL_GROUP), lambda b: (layer, 0, 0, 0)),
            pl.BlockSpec((None, 1, POOL_W), lambda b: (layer, 0, 0)),
        ],
        out_specs=pl.BlockSpec((seq, POOL_W), lambda b: (b, 0)),
        out_shape=jax.ShapeDtypeStruct((batch * seq, POOL_W), bf16),
        scratch_shapes=[pltpu.VMEM((seq + 2 * POOL_PAD, POOL_W), f32)],
        compiler_params=_cparams(("parallel",)),
        name="pool_mixer",
    )(z, w_pool, pool_scale)


def _out_kernel(aa_ref, ab_ref, s_ref, pa_ref, pb_ref, *refs, n_x, tiles_a, tn):
    x_refs = refs[:n_x]
    mod_ref, w_ref, o_ref = refs[n_x:]
    first = pl.program_id(0) < tiles_a
    a = _pick(first, aa_ref, ab_ref)
    s = s_ref[...]
    p = _pick(first, pa_ref, pb_ref)
    x = _rows(x_refs, tiles_a)
    gate = mod_ref[0][2:3]
    for c in range(w_ref.shape[1] // tn):
        cols = slice(c * tn, (c + 1) * tn)
        acc = jnp.dot(a, w_ref[0:ATTN_W, cols], preferred_element_type=f32)
        acc = acc + jnp.dot(s, w_ref[ATTN_W:ATTN_W + SSM_W, cols], preferred_element_type=f32)
        acc = acc + jnp.dot(p, w_ref[ATTN_W + SSM_W:, cols], preferred_element_type=f32)
        o_ref[:, cols] = x[:, cols] + gate[:, cols] * acc


def out_projection(a_a, a_b, s, p_a, p_b, xs, mod, w, layer, row_of_tile, tm, tn=512):
    m_total = sum(x.shape[0] for x in xs)
    d = xs[0].shape[1]
    tiles_a, tiles_b = a_a.shape[0] // tm, a_b.shape[0] // tm
    assert len(xs) == 1 or xs[0].shape[0] == a_a.shape[0]
    return pl.pallas_call(
        functools.partial(_out_kernel, n_x=len(xs), tiles_a=tiles_a, tn=tn),
        grid=(m_total // tm,),
        in_specs=[
            *_two_part(tm, ATTN_W, tiles_a, tiles_b),
            pl.BlockSpec((tm, SSM_W), lambda i: (i, 0)),
            *_two_part(tm, POOL_W, tiles_a, tiles_b),
            *_row_specs(xs, tm, d),
            pl.BlockSpec((None, 1, 6, d), lambda i: (layer, row_of_tile(i), 0, 0)),
            pl.BlockSpec((None, d, d), lambda i: (layer, 0, 0), pipeline_mode=pl.Buffered(1)),
        ],
        out_specs=pl.BlockSpec((tm, d), lambda i: (i, 0)),
        out_shape=jax.ShapeDtypeStruct((m_total, d), f32),
        compiler_params=_cparams(("arbitrary",)),
        name="out_projection",
    )(a_a, a_b, s, p_a, p_b, *xs, mod, w)


def _ffn_kernel(x_ref, mod_ref, g_ref, wg_ref, wu_ref, wo_ref, fg_ref, *refs, n_out, tiles_a, final_norm):
    o_refs = refs[:n_out]
    h_ref, acc_ref, act_ref = refs[n_out:]
    i, j = pl.program_id(0), pl.program_id(1)
    nh = pl.num_programs(1) - 1

    def down():
        acc_ref[...] += mod_ref[0][5:6] * jnp.dot(act_ref[...], wo_ref[...], preferred_element_type=f32)

    def gate_up():
        h = h_ref[...]
        gate = jnp.dot(h, wg_ref[...], preferred_element_type=f32)
        up = jnp.dot(h, wu_ref[...], preferred_element_type=f32)
        act_ref[...] = (gate * jax.nn.sigmoid(gate) * up).astype(bf16)

    @pl.when(j == 0)
    def _():
        m = mod_ref[0]
        x = x_ref[...]
        h_ref[...] = (_rms(x, g_ref[...]) * (1.0 + m[4:5]) + m[3:4]).astype(bf16)
        acc_ref[...] = x
        gate_up()

    @pl.when((j > 0) & (j < nh))
    def _():
        down()
        gate_up()

    def result():
        return _rms(acc_ref[...], fg_ref[...]) if final_norm else acc_ref[...]

    if n_out == 1:
        @pl.when(j == nh)
        def _():
            down()
            o_refs[0][...] = result()
    else:
        @pl.when((j == nh) & (i < tiles_a))
        def _():
            down()
            o_refs[0][...] = result()

        @pl.when((j == nh) & (i >= tiles_a))
        def _():
            down()
            o_refs[1][...] = result()


def ffn(x, mod, g, w_in, w_out, final_g, layer, row_of_tile, tm, th, final_norm, split_rows=None):
    m_total, d = x.shape
    hidden = w_out.shape[1]
    nh = hidden // th
    if split_rows is None:
        tiles_a = m_total // tm
        out_specs = [pl.BlockSpec((tm, d), lambda i, j: (i, 0))]
        out_shape = [jax.ShapeDtypeStruct((m_total, d), f32)]
    else:
        tiles_a = split_rows // tm
        out_specs = list(_two_part(tm, d, tiles_a, m_total // tm - tiles_a))
        out_shape = [jax.ShapeDtypeStruct((split_rows, d), f32), jax.ShapeDtypeStruct((m_total - split_rows, d), f32)]
    up_blk = lambda j: jnp.minimum(j, nh - 1)
    down_blk = lambda j: jnp.maximum(j - 1, 0)
    return pl.pallas_call(
        functools.partial(_ffn_kernel, n_out=len(out_specs), tiles_a=tiles_a, final_norm=final_norm),
        grid=(m_total // tm, nh + 1),
        in_specs=[
            pl.BlockSpec((tm, d), lambda i, j: (i, 0)),
            pl.BlockSpec((None, 1, 6, d), lambda i, j: (layer, row_of_tile(i), 0, 0)),
            pl.BlockSpec((None, 1, d), lambda i, j: (layer, 0, 0)),
            pl.BlockSpec((None, d, th), lambda i, j: (layer, 0, up_blk(j))),
            pl.BlockSpec((None, d, th), lambda i, j: (layer, 0, nh + up_blk(j))),
            pl.BlockSpec((None, th, d), lambda i, j: (layer, down_blk(j), 0)),
            pl.BlockSpec((1, d), lambda i, j: (0, 0)),
        ],
        out_specs=out_specs,
        out_shape=out_shape,
        scratch_shapes=[pltpu.VMEM((tm, d), bf16), pltpu.VMEM((tm, d), f32), pltpu.VMEM((tm, th), bf16)],
        compiler_params=_cparams(("arbitrary", "arbitrary")),
        name="ffn",
    )(x, mod, g, w_in, w_in, w_out, final_g)


def kernel(x_prompt, x_sample, c, cache_k, cache_v, state_ssm, c_ctx, w_mod, b_mod, norm1_g, norm2_g, w_in,
           attn_rpb, ssm_a_re, ssm_a_im, ssm_log_dt, ssm_b_re, ssm_b_im, ssm_c_re, ssm_c_im, ssm_d, ssm_w_glu,
           ssm_b_glu, pool_w, pool_scale, w_out, ffn_w_in, ffn_w_out, final_norm_g):
    batch, seq, d = x_prompt.shape
    dec_batch, dec_seq, _ = x_sample.shape
    depth = w_in.shape[0]
    m_ctx = batch * seq
    tm = 512
    assert m_ctx % tm == 0 and dec_seq % tm == 0 and dec_batch + 1 <= 8

    def row_of_tile(i):
        return jnp.where(i < m_ctx // tm, 0, 1 + (i - m_ctx // tm) // (dec_seq // tm))

    cond = jnp.concatenate([c_ctx[None, :], c, jnp.zeros((8 - 1 - dec_batch, d), f32)], axis=0)
    mod = modulation_all(cond, w_mod, b_mod)
    mod = mod[:, :1 + dec_batch].reshape(depth, 1 + dec_batch, 6, d)

    w_in_b, w_out_b = w_in.astype(bf16), w_out.astype(bf16)
    ffn_w_in_b, ffn_w_out_b, w_glu_b = ffn_w_in.astype(bf16), ffn_w_out.astype(bf16), ssm_w_glu.astype(bf16)
    row = lambda t: t[:, None, :]
    tables = jax.vmap(ssm_tables)(ssm_a_re, ssm_a_im, ssm_log_dt, ssm_b_re, ssm_b_im, ssm_c_re, ssm_c_im)
    bias_tables = jax.vmap(latent_bias_table)(attn_rpb)

    xs = (x_prompt.reshape(m_ctx, d), x_sample.reshape(dec_batch * dec_seq, d))
    h0_ctx = jnp.zeros((batch, SSM_PAIRS * SSM_PW), f32)

    zs, st_out = [], []
    for l in range(depth):
        last = l == depth - 1
        z = in_projection(xs, mod, row(norm1_g), w_in_b, l, row_of_tile, tm)

        a_ctx, *caches = context_attention(z, batch, seq, zs, write_caches=last)
        zs.append(z)
        a_lat = latent_attention(z, m_ctx, dec_batch, dec_seq, cache_k, cache_v, l, bias_tables)

        y_ctx, fin = ssm_scan(z, 0, batch, seq, tables, l, h0_ctx)
        y_lat, _ = ssm_scan(z, m_ctx, dec_batch, dec_seq, tables, l, state_to_slabs(state_ssm[:, l]))
        st_out.append(state_from_slabs(fin))
        s_out = ssm_glu(y_ctx, y_lat, z, row(ssm_d), w_glu_b, row(ssm_b_glu), l, tm)

        p_ctx = pool_mixer(z, 0, batch, seq, pool_w, row(pool_scale), l)
        p_lat = pool_mixer(z, m_ctx, dec_batch, dec_seq, pool_w, row(pool_scale), l)

        x = out_projection(a_ctx, a_lat, s_out, p_ctx, p_lat, xs, mod, w_out_b, l, row_of_tile, tm)
        xs = ffn(x, mod, row(norm2_g), ffn_w_in_b, ffn_w_out_b, final_norm_g[None], l, row_of_tile, tm, 512,
                 final_norm=last, split_rows=m_ctx if last else None)

    y_prompt = xs[0].reshape(batch, seq, d)
    y_sample = xs[1].reshape(dec_batch, dec_seq, d)
    return (y_prompt, y_sample, caches[0], caches[1], jnp.stack(st_out, axis=1))
```

```python
import functools

import jax
import jax.numpy as jnp
import numpy as np
from jax import lax
from jax.experimental import pallas as pl
from jax.experimental.pallas import tpu as pltpu

f32 = jnp.float32
bf16 = jnp.bfloat16

D_MODEL = 2048
N_HEADS = 16
HEAD_DIM = 64
ATTN_W = N_HEADS * HEAD_DIM
SSM_W = 512
SSM_GROUP = 16
SSM_GROUPS = 32
SSM_STATE = 64
POOL_W = 512
POOL_WINDOWS = (2, 4, 8, 16)
POOL_GROUP = 128
POOL_PAD = 16
IN_W = 3 * ATTN_W + SSM_W + POOL_W
GRID_W = 64
WIN_ROWS_MAX = 8
WIN_COLS = 16
RMS_EPS = 1e-6
NEG_INF = -1e30

LAT_BLOCK_ROWS = 4
SSM_CHUNK = 16
SSM_PAIRS = SSM_GROUPS // 2
SSM_QUAD = 4
SSM_PW = 2 * SSM_CHUNK * SSM_GROUP

VMEM_LIMIT = 56 * 1024 * 1024


def _cparams(sem):
    return pltpu.CompilerParams(dimension_semantics=sem, vmem_limit_bytes=VMEM_LIMIT)


def _rms(x, g):
    return x * lax.rsqrt(jnp.mean(x * x, axis=-1, keepdims=True) + RMS_EPS) * g


def _two_part(tm, width, tiles_a, tiles_b, col=lambda *_: 0):
    return (pl.BlockSpec((tm, width), lambda i, *r: (jnp.minimum(i, tiles_a - 1), col(*r))),
            pl.BlockSpec((tm, width), lambda i, *r: (jnp.clip(i - tiles_a, 0, tiles_b - 1), col(*r))))


def _row_specs(parts, tm, width, col=lambda *_: 0):
    if len(parts) == 1:
        return (pl.BlockSpec((tm, width), lambda i, *r: (i, col(*r))),)
    return _two_part(tm, width, parts[0].shape[0] // tm, parts[1].shape[0] // tm, col)


def _pick(first, a_ref, b_ref):
    return jnp.where(first, a_ref[...], b_ref[...])


def _rows(refs, tiles_a):
    if len(refs) == 1:
        return refs[0][...]
    return _pick(pl.program_id(0) < tiles_a, *refs)


def _mod_kernel(c_ref, w_ref, b_ref, o_ref):
    c = c_ref[...]
    s = (c * jax.nn.sigmoid(c)).astype(bf16)
    o_ref[0] = jnp.dot(s, w_ref[0].astype(bf16), preferred_element_type=f32) + b_ref[0]


def modulation_all(cond, w_mod, b_mod, tn=1024):
    n_layers, d, n = w_mod.shape
    return pl.pallas_call(
        _mod_kernel,
        grid=(n_layers, n // tn),
        in_specs=[
            pl.BlockSpec((8, d), lambda l, j: (0, 0)),
            pl.BlockSpec((1, d, tn), lambda l, j: (l, 0, j)),
            pl.BlockSpec((1, 1, tn), lambda l, j: (l, 0, j)),
        ],
        out_specs=pl.BlockSpec((1, 8, tn), lambda l, j: (l, 0, j)),
        out_shape=jax.ShapeDtypeStruct((n_layers, 8, n), f32),
        compiler_params=_cparams(("parallel", "parallel")),
        name="modulation",
    )(cond, w_mod, b_mod.reshape(n_layers, 1, n))


def _in_kernel(*refs, n_x, tiles_a, tn):
    x_refs = refs[:n_x]
    mod_ref, g_ref, w_ref, o_ref = refs[n_x:]
    m = mod_ref[0]
    y = _rms(_rows(x_refs, tiles_a), g_ref[...])
    h = (y * (1.0 + m[1:2]) + m[0:1]).astype(bf16)
    for c in range(w_ref.shape[1] // tn):
        cols = slice(c * tn, (c + 1) * tn)
        o_ref[:, cols] = jnp.dot(h, w_ref[:, cols], preferred_element_type=f32)


def in_projection(xs, mod, g, w, layer, row_of_tile, tm, tn=512):
    m_total = sum(x.shape[0] for x in xs)
    d = xs[0].shape[1]
    n = w.shape[2]
    return pl.pallas_call(
        functools.partial(_in_kernel, n_x=len(xs), tiles_a=xs[0].shape[0] // tm, tn=tn),
        grid=(m_total // tm,),
        in_specs=[
            *_row_specs(xs, tm, d),
            pl.BlockSpec((None, 1, 6, d), lambda i: (layer, row_of_tile(i), 0, 0)),
            pl.BlockSpec((None, 1, d), lambda i: (layer, 0, 0)),
            pl.BlockSpec((None, d, n), lambda i: (layer, 0, 0), pipeline_mode=pl.Buffered(1)),
        ],
        out_specs=pl.BlockSpec((tm, n), lambda i: (i, 0)),
        out_shape=jax.ShapeDtypeStruct((m_total, n), f32),
        compiler_params=_cparams(("arbitrary",)),
        name="in_projection",
    )(*xs, mod, g, w)


def _first_head(shape):
    return lax.broadcasted_iota(jnp.int32, shape, len(shape) - 1) < HEAD_DIM


def _attn_ctx_kernel(*refs, n_prev, write_caches):
    q_ref, k_ref, v_ref = refs[:3]
    prev = refs[3:3 + 2 * n_prev]
    outs = refs[3 + 2 * n_prev:]
    o_ref = outs[0]
    scale = HEAD_DIM ** -0.5
    nt = (((1,), (1,)), ((), ()))
    seq = q_ref.shape[0]
    lanes = 2 * HEAD_DIM
    first = _first_head((seq, lanes))
    for p in range(N_HEADS // 2):
        cols = slice(p * lanes, (p + 1) * lanes)
        q = q_ref[:, cols].astype(bf16)
        k = k_ref[:, cols]
        v = v_ref[:, cols]
        if write_caches:
            kc_ref, vc_ref = outs[1:]
            layers = [(prev[2 * l][:, cols], prev[2 * l + 1][:, cols]) for l in range(n_prev)] + [(k, v)]
            for l, (kl, vl) in enumerate(layers):
                for h in range(2):
                    kc_ref[0, l, 2 * p + h] = kl[:, h * HEAD_DIM:(h + 1) * HEAD_DIM]
                    vc_ref[0, l, 2 * p + h] = vl[:, h * HEAD_DIM:(h + 1) * HEAD_DIM]
        kb = k.astype(bf16)
        vb = v.astype(bf16)
        zero = jnp.zeros_like(kb)
        num, inv = None, []
        for h in range(2):
            keep = first if h == 0 else jnp.logical_not(first)
            s = lax.dot_general(q, jnp.where(keep, kb, zero), nt, preferred_element_type=f32) * scale
            e = jnp.exp(s - jnp.max(s, axis=-1, keepdims=True))
            inv.append(1.0 / jnp.sum(e, axis=-1, keepdims=True))
            part = jnp.dot(e.astype(bf16), jnp.where(keep, vb, zero), preferred_element_type=f32)
            num = part if num is None else num + part
        o_ref[:, cols] = (num * jnp.where(first, inv[0], inv[1])).astype(bf16)


def context_attention(z, batch, seq, prev_zs=(), write_caches=False):
    col = lambda blk: pl.BlockSpec((seq, ATTN_W), lambda b: (b, blk))
    out_specs = [pl.BlockSpec((seq, ATTN_W), lambda b: (b, 0))]
    out_shape = [jax.ShapeDtypeStruct((batch * seq, ATTN_W), bf16)]
    if write_caches:
        depth = len(prev_zs) + 1
        cache = pl.BlockSpec((1, depth, N_HEADS, seq, HEAD_DIM), lambda b: (b, 0, 0, 0, 0))
        cache_shape = jax.ShapeDtypeStruct((batch, depth, N_HEADS, seq, HEAD_DIM), f32)
        out_specs += [cache, cache]
        out_shape += [cache_shape, cache_shape]
    else:
        prev_zs = ()
    prev_args = [a for zp in prev_zs for a in (zp, zp)]
    return pl.pallas_call(
        functools.partial(_attn_ctx_kernel, n_prev=len(prev_zs), write_caches=write_caches),
        grid=(batch,),
        in_specs=[col(0), col(1), col(2)] + [col(1), col(2)] * len(prev_zs),
        out_specs=out_specs,
        out_shape=out_shape,
        compiler_params=_cparams(("parallel",)),
        name="context_attention",
    )(z, z, z, *prev_args)


def _attn_lat_kernel(q_ref, k_ref, v_ref, ck_ref, cv_ref, t_ref, o_ref, bias_scr, k_scr, v_scr, ck_scr, cv_scr,
                     *, rows, wr):
    scale = HEAD_DIM ** -0.5
    n_loc = wr * GRID_W
    for h in range(2):
        for d in range(wr):
            for j in range(wr):
                bias_scr[h, d, :, j * GRID_W:(j + 1) * GRID_W] = t_ref[h, j - d + WIN_ROWS_MAX - 1]

    kb = k_ref[...].astype(bf16)
    vb = v_ref[...].astype(bf16)
    first = _first_head(kb.shape)
    zero = jnp.zeros_like(kb)
    pad = jnp.zeros(ck_ref.shape[3:], bf16)
    for h in range(2):
        keep = first if h == 0 else jnp.logical_not(first)
        k_scr[h] = jnp.where(keep, kb, zero)
        v_scr[h] = jnp.where(keep, vb, zero)
        ck, cv = ck_ref[0, 0, h].astype(bf16), cv_ref[0, 0, h].astype(bf16)
        ck_scr[h] = jnp.concatenate([ck, pad] if h == 0 else [pad, ck], axis=-1)
        cv_scr[h] = jnp.concatenate([cv, pad] if h == 0 else [pad, cv], axis=-1)

    nt = (((1,), (1,)), ((), ()))
    br = LAT_BLOCK_ROWS
    nq = br * GRID_W
    first_q = _first_head((nq, 2 * HEAD_DIM))

    def block(blk, carry):
        q_rows = pl.ds(pl.multiple_of(blk * nq, nq), nq)
        q = q_ref[q_rows, :].astype(bf16)
        windows, offsets = [], []
        for i in range(br):
            r = blk * br + i
            rs = jnp.clip(r - wr // 2, 0, rows - wr)
            windows.append(pl.ds(pl.multiple_of(rs * GRID_W, GRID_W), n_loc))
            offsets.append(r - rs)
        scores = []
        for h in range(2):
            s_ctx = lax.dot_general(q, ck_scr[h], nt, preferred_element_type=f32) * scale
            s_loc = [lax.dot_general(q[i * GRID_W:(i + 1) * GRID_W], k_scr[h, windows[i], :], nt,
                                     preferred_element_type=f32) * scale + bias_scr[h, offsets[i]]
                     for i in range(br)]
            scores.append((jnp.concatenate(s_loc, axis=0), s_ctx))
        probs, inv = [], []
        for s_loc, s_ctx in scores:
            m = jnp.maximum(jnp.max(s_loc, axis=-1, keepdims=True), jnp.max(s_ctx, axis=-1, keepdims=True))
            e_loc = jnp.exp(s_loc - m)
            e_ctx = jnp.exp(s_ctx - m)
            inv.append(1.0 / (jnp.sum(e_loc, axis=-1, keepdims=True) + jnp.sum(e_ctx, axis=-1, keepdims=True)))
            probs.append((e_loc.astype(bf16), e_ctx.astype(bf16)))
        num = None
        for h, (e_loc, e_ctx) in enumerate(probs):
            o_loc = [jnp.dot(e_loc[i * GRID_W:(i + 1) * GRID_W], v_scr[h, windows[i], :],
                             preferred_element_type=f32) for i in range(br)]
            part = jnp.dot(e_ctx, cv_scr[h], preferred_element_type=f32) + jnp.concatenate(o_loc, axis=0)
            num = part if num is None else num + part
        o_ref[q_rows, :] = (num * jnp.where(first_q, inv[0], inv[1])).astype(bf16)
        return carry

    lax.fori_loop(0, rows // br, block, 0)


def latent_bias_table(rpb):
    cols = np.arange(GRID_W)
    col_start = np.clip(cols - WIN_COLS // 2, 0, GRID_W - WIN_COLS)
    valid = (cols[None, :] >= col_start[:, None]) & (cols[None, :] < col_start[:, None] + WIN_COLS)
    dc = cols[None, :] - cols[:, None] + WIN_COLS - 1
    assert np.all((dc[valid] >= 0) & (dc[valid] < 2 * WIN_COLS - 1))
    onehot = ((np.arange(2 * WIN_COLS - 1)[:, None, None] == dc[None]) & valid[None]).astype(np.float32)
    t = jnp.einsum('hrd,dqk->hrqk', rpb.astype(f32), jnp.asarray(onehot), precision=lax.Precision.HIGHEST)
    return jnp.where(jnp.asarray(valid)[None, None], t, NEG_INF)


def latent_attention(z, row0, batch, seq, cache_k, cache_v, layer, tables):
    lanes = 2 * HEAD_DIM
    n_pairs = N_HEADS // 2
    rows = seq // GRID_W
    wr = min(WIN_ROWS_MAX, rows)
    past = cache_k.shape[3]
    blk0 = row0 // seq
    n_dr = 2 * WIN_ROWS_MAX - 1
    assert rows % LAT_BLOCK_ROWS == 0
    col = lambda off: pl.BlockSpec((seq, lanes), lambda b, p: (blk0 + b, off + p))
    ctx = pl.BlockSpec((1, 1, 2, past, HEAD_DIM), lambda b, p: (b, layer, p, 0, 0))
    return pl.pallas_call(
        functools.partial(_attn_lat_kernel, rows=rows, wr=wr),
        grid=(batch, n_pairs),
        in_specs=[col(0), col(n_pairs), col(2 * n_pairs), ctx, ctx,
                  pl.BlockSpec((None, 2, n_dr, GRID_W, GRID_W), lambda b, p: (layer, p, 0, 0, 0))],
        out_specs=pl.BlockSpec((seq, lanes), lambda b, p: (b, p)),
        out_shape=jax.ShapeDtypeStruct((batch * seq, ATTN_W), bf16),
        scratch_shapes=[pltpu.VMEM((2, wr, GRID_W, wr * GRID_W), f32),
                        pltpu.VMEM((2, seq, lanes), bf16), pltpu.VMEM((2, seq, lanes), bf16),
                        pltpu.VMEM((2, past, lanes), bf16), pltpu.VMEM((2, past, lanes), bf16)],
        compiler_params=_cparams(("parallel", "parallel")),
        name="latent_attention",
    )(z, z, z, cache_k, cache_v, tables)


def ssm_tables(a_re, a_im, log_dt, b_re, b_im, c_re, c_im):
    lc, p, m = SSM_CHUNK, SSM_STATE, SSM_GROUP
    eye2 = jnp.eye(2, dtype=f32)
    half = SSM_PW // 2

    def slab_row(t):
        t = jnp.broadcast_to(t.astype(f32).reshape(2, 1, SSM_PAIRS, 2, p), (2, 2, SSM_PAIRS, 2, p))
        return t.transpose(2, 0, 1, 3, 4).reshape(SSM_PAIRS, SSM_PW)

    def embed(first, second):
        t = jnp.stack([first, second], axis=1).astype(f32)
        x = t.shape[-1]
        t = t.reshape(2, 2, SSM_PAIRS, 2, p, x).transpose(2, 3, 5, 0, 1, 4)
        t = t[:, :, :, :, :, None, :] * eye2[None, :, None, None, None, :, None]
        return t.reshape(SSM_PAIRS, 2 * x, SSM_PW)

    lr, li = slab_row(a_re), slab_row(a_im)
    dt = slab_row(jnp.broadcast_to(jnp.exp(log_dt.astype(f32))[..., None], a_re.shape))
    sr, si = lr * dt, li * dt
    k = jnp.arange(lc + 1, dtype=f32)[:, None, None]
    mag = jnp.exp(k * sr[None])
    ar, ai = mag * jnp.cos(k * si[None]), mag * jnp.sin(k * si[None])
    den = lr * lr + li * li
    qr = ((ar[1] - 1.0) * lr + ai[1] * li) / den
    qi = (ai[1] * lr - (ar[1] - 1.0) * li) / den
    b4, b4s = embed(b_re, b_im), embed(-b_im, b_re)
    bb4 = (qr[:, None] * b4 + qi[:, None] * b4s)[:, None]
    bb4s = (qr[:, None] * b4s - qi[:, None] * b4)[:, None]
    rev = np.arange(lc)[::-1]
    fwd = np.arange(lc)

    def powers(t, ks):
        return jnp.stack([t[int(k)] for k in ks], axis=0)

    def per_step(t, k_fwd, k_bwd):
        t = jnp.concatenate([powers(t, k_fwd)[..., :half], powers(t, k_bwd)[..., half:]], axis=-1)
        return t.transpose(1, 0, 2)[:, :, None, :]

    bx = bb4 * per_step(ar, rev, fwd) + bb4s * per_step(ai, rev, fwd)
    bx = bx.reshape(SSM_PAIRS, SSM_PW, SSM_PW)
    ct_re, ct_im = c_re.transpose(0, 1, 3, 2), c_im.transpose(0, 1, 3, 2)
    c4 = embed(ct_re, -ct_im)[:, None]
    c4s = embed(-ct_im, -ct_re)[:, None]
    cyt = c4 * per_step(ar, fwd + 1, lc - fwd) + c4s * per_step(ai, fwd + 1, lc - fwd)
    cyt = cyt.reshape(SSM_PAIRS, SSM_PW, SSM_PW)

    j = np.arange(2 * lc - 1)
    live = jnp.asarray(np.concatenate([np.broadcast_to((j >= lc - 1)[:, None], (2 * lc - 1, half)),
                                       np.broadcast_to((j <= lc - 1)[:, None], (2 * lc - 1, half))], axis=-1), f32)
    kf, kb = np.maximum(j - (lc - 1), 0), np.maximum(lc - 1 - j, 0)
    kt = (c4 * per_step(ar, kf, kb) + c4s * per_step(ai, kf, kb)) * live[None, :, None, :]
    kt = kt.reshape(SSM_PAIRS, (2 * lc - 1) * 2 * m, SSM_PW)
    strip = lax.dot_general(bb4[:, 0], kt, (((2,), (2,)), ((0,), (0,))),
                            precision=lax.Precision.HIGH)
    tz = jnp.stack([strip[:, :, (lc - 1 - s) * 2 * m:(lc - 1 - s) * 2 * m + SSM_PW] for s in range(lc)], axis=1)
    tz = tz.reshape(SSM_PAIRS, SSM_PW, SSM_PW)

    sw = 2 * p
    a16p = jnp.concatenate([ar[lc][:, :sw], ai[lc][:, sw:half], ar[lc][:, half:half + sw], ai[lc][:, half + sw:]],
                           axis=-1)[:, None, :]
    return tz.astype(bf16), bx.astype(bf16), cyt.astype(bf16), a16p


def _ssm_kernel(u_ref, tz_ref, bx_ref, cyt_ref, a_ref, h0_ref, y_ref, fin_ref, x_scr, s_scr, y_scr, *, n_seq, n_chunks):
    r = n_seq * n_chunks
    pc = 2 * SSM_GROUP
    sw = 2 * SSM_STATE
    xs = [u_ref[pl.ds(s, r, stride=SSM_CHUNK), :] for s in range(SSM_CHUNK)]
    for kq in range(SSM_QUAD):
        u = jnp.concatenate([x[:, kq * pc:(kq + 1) * pc] for x in xs], axis=-1).astype(bf16)
        x = jnp.dot(u, bx_ref[kq], preferred_element_type=f32)
        for i in range(4):
            x_scr[4 * kq + i] = x[:, i * sw:(i + 1) * sw]
        y_scr[:, kq * SSM_PW:(kq + 1) * SSM_PW] = jnp.dot(u, tz_ref[kq], preferred_element_type=f32)

    def scan(direction):
        offs = [kq * SSM_PW + 2 * direction * sw for kq in range(SSM_QUAD)]
        slabs = [4 * kq + 2 * direction for kq in range(SSM_QUAD)]
        coef = [(a_ref[kq][:, 2 * direction * sw:2 * direction * sw + sw],
                 a_ref[kq][:, 2 * direction * sw + sw:2 * direction * sw + 2 * sw]) for kq in range(SSM_QUAD)]

        def body(i, carry):
            c = i if direction == 0 else n_chunks - 1 - i
            rows = pl.ds(c, n_seq, stride=n_chunks)
            out = []
            for kq in range(SSM_QUAD):
                sr, si = carry[2 * kq], carry[2 * kq + 1]
                ar, ai = coef[kq]
                re, im = slabs[kq], slabs[kq] + 1
                s_scr[re, rows, :] = sr
                s_scr[im, rows, :] = si
                xr = x_scr[re, rows, :]
                xi = x_scr[im, rows, :]
                out += [ar * sr - ai * si + xr, ar * si + ai * sr + xi]
            return tuple(out)

        init = []
        for lo in offs:
            init += [h0_ref[:, lo:lo + sw], h0_ref[:, lo + sw:lo + 2 * sw]]
        fin = lax.fori_loop(0, n_chunks, body, tuple(init))
        for kq, lo in enumerate(offs):
            fin_ref[:, lo:lo + sw] = fin[2 * kq]
            fin_ref[:, lo + sw:lo + 2 * sw] = fin[2 * kq + 1]

    scan(0)
    scan(1)
    ys = []
    for kq in range(SSM_QUAD):
        cols = slice(kq * SSM_PW, (kq + 1) * SSM_PW)
        s_in = jnp.concatenate([s_scr[4 * kq + i] for i in range(4)], axis=-1).astype(bf16)
        ys.append(y_scr[:, cols] + lax.dot_general(s_in, cyt_ref[kq], (((1,), (1,)), ((), ())),
                                                   preferred_element_type=f32))
    for t in range(SSM_CHUNK):
        y_ref[pl.ds(t, r, stride=SSM_CHUNK), :] = jnp.concatenate([y[:, t * pc:(t + 1) * pc] for y in ys], axis=-1)


def ssm_scan(z, row0, n_seq, seq, tables, layer, h0):
    tz, bx, cyt, a16p = tables
    n_chunks = seq // SSM_CHUNK
    rows = n_seq * seq
    r = n_seq * n_chunks
    lanes = SSM_QUAD * 2 * SSM_GROUP
    u_blk = (3 * ATTN_W) // lanes
    qw = SSM_QUAD * SSM_PW
    table = pl.BlockSpec((None, SSM_QUAD, SSM_PW, SSM_PW), lambda q: (layer, q, 0, 0))
    return pl.pallas_call(
        functools.partial(_ssm_kernel, n_seq=n_seq, n_chunks=n_chunks),
        grid=(SSM_PAIRS // SSM_QUAD,),
        in_specs=[pl.BlockSpec((rows, lanes), lambda q: (row0 // rows, u_blk + q)),
                  table, table, table,
                  pl.BlockSpec((None, SSM_QUAD, 1, SSM_PW), lambda q: (layer, q, 0, 0)),
                  pl.BlockSpec((n_seq, qw), lambda q: (0, q))],
        out_specs=[pl.BlockSpec((rows, lanes), lambda q: (0, q)),
                   pl.BlockSpec((n_seq, qw), lambda q: (0, q))],
        out_shape=[jax.ShapeDtypeStruct((rows, SSM_W), f32),
                   jax.ShapeDtypeStruct((n_seq, SSM_PAIRS * SSM_PW), f32)],
        scratch_shapes=[pltpu.VMEM((4 * SSM_QUAD, r, 2 * SSM_STATE), f32),
                        pltpu.VMEM((4 * SSM_QUAD, r, 2 * SSM_STATE), f32), pltpu.VMEM((r, qw), f32)],
        compiler_params=_cparams(("parallel",)),
        name="ssm_scan",
    )(z, tz, bx, cyt, a16p, h0)


def state_to_slabs(st):
    batch = st.shape[0]
    t = st.reshape(batch, 2, SSM_PAIRS, 2, SSM_STATE, 2).transpose(0, 2, 1, 5, 3, 4)
    return t.reshape(batch, SSM_PAIRS * SSM_PW)


def state_from_slabs(fin):
    batch = fin.shape[0]
    t = fin.reshape(batch, SSM_PAIRS, 2, 2, 2, SSM_STATE)
    return t.transpose(0, 2, 1, 4, 5, 3).reshape(batch, 2, SSM_GROUPS, SSM_STATE, 2)


def _glu_kernel(ya_ref, yb_ref, u_ref, d_ref, w_ref, b_ref, o_ref, *, tiles_a):
    y = _pick(pl.program_id(0) < tiles_a, ya_ref, yb_ref) + d_ref[...] * u_ref[...]
    z = jnp.dot(jax.nn.gelu(y).astype(bf16), w_ref[...], preferred_element_type=f32) + b_ref[...]
    o_ref[...] = (z[:, :SSM_W] * jax.nn.sigmoid(z[:, SSM_W:])).astype(bf16)


def ssm_glu(y_a, y_b, z, d_skip, w_glu, b_glu, layer, tm):
    tiles_a, tiles_b = y_a.shape[0] // tm, y_b.shape[0] // tm
    m_total = y_a.shape[0] + y_b.shape[0]
    u_blk = (3 * ATTN_W) // SSM_W
    return pl.pallas_call(
        functools.partial(_glu_kernel, tiles_a=tiles_a),
        grid=(tiles_a + tiles_b,),
        in_specs=[
            *_two_part(tm, SSM_W, tiles_a, tiles_b),
            pl.BlockSpec((tm, SSM_W), lambda i: (i, u_blk)),
            pl.BlockSpec((None, 1, SSM_W), lambda i: (layer, 0, 0)),
            pl.BlockSpec((None, SSM_W, 2 * SSM_W), lambda i: (layer, 0, 0)),
            pl.BlockSpec((None, 1, 2 * SSM_W), lambda i: (layer, 0, 0)),
        ],
        out_specs=pl.BlockSpec((tm, SSM_W), lambda i: (i, 0)),
        out_shape=jax.ShapeDtypeStruct((m_total, SSM_W), bf16),
        compiler_params=_cparams(("arbitrary",)),
        name="ssm_glu",
    )(y_a, y_b, z, d_skip, w_glu, b_glu)


def _pool_kernel(p_ref, w_ref, sc_ref, o_ref, pad_ref, *, seq):
    zeros = jnp.zeros((POOL_PAD, POOL_W), f32)
    pad_ref[0:POOL_PAD, :] = zeros
    pad_ref[POOL_PAD + seq:, :] = zeros
    pad_ref[POOL_PAD:POOL_PAD + seq, :] = p_ref[...]
    t = lax.broadcasted_iota(jnp.int32, (seq, 1), 0)
    for g, win in enumerate(POOL_WINDOWS):
        cols = slice(g * POOL_GROUP, (g + 1) * POOL_GROUP)
        total = jnp.zeros((seq, POOL_GROUP), f32)
        for d in range(-(win // 2), win - win // 2):
            total = total + pad_ref[POOL_PAD + d:POOL_PAD + d + seq, cols]
        lo = jnp.clip(t - win // 2, 0, seq)
        hi = jnp.clip(t - win // 2 + win, 0, seq)
        mixed = total / (hi - lo).astype(f32) - p_ref[:, cols]
        out = jnp.dot(mixed.astype(bf16), w_ref[g].astype(bf16), preferred_element_type=f32)
        o_ref[:, cols] = (out * sc_ref[:, cols]).astype(bf16)


def pool_mixer(z, row0, batch, seq, w_pool, pool_scale, layer):
    p_blk = (3 * ATTN_W + SSM_W) // POOL_W
    blk0 = row0 // seq
    return pl.pallas_call(
        functools.partial(_pool_kernel, seq=seq),
        grid=(batch,),
        in_specs=[
            pl.BlockSpec((seq, POOL_W), lambda b: (blk0 + b, p_blk)),
            pl.BlockSpec((None, len(POOL_WINDOWS), POOL_GROUP, POOL_GROUP), lambda b: (layer, 0, 0, 0)),
            pl.BlockSpec((None, 1, POOL_W), lambda b: (layer, 0, 0)),
        ],
        out_specs=pl.BlockSpec((seq, POOL_W), lambda b: (b, 0)),
        out_shape=jax.ShapeDtypeStruct((batch * seq, POOL_W), bf16),
        scratch_shapes=[pltpu.VMEM((seq + 2 * POOL_PAD, POOL_W), f32)],
        compiler_params=_cparams(("parallel",)),
        name="pool_mixer",
    )(z, w_pool, pool_scale)


def _out_kernel(aa_ref, ab_ref, s_ref, pa_ref, pb_ref, *refs, n_x, tiles_a, tn):
    x_refs = refs[:n_x]
    mod_ref, w_ref, o_ref = refs[n_x:]
    first = pl.program_id(0) < tiles_a
    a = _pick(first, aa_ref, ab_ref)
    s = s_ref[...]
    p = _pick(first, pa_ref, pb_ref)
    x = _rows(x_refs, tiles_a)
    gate = mod_ref[0][2:3]
    for c in range(w_ref.shape[1] // tn):
        cols = slice(c * tn, (c + 1) * tn)
        acc = jnp.dot(a, w_ref[0:ATTN_W, cols], preferred_element_type=f32)
        acc = acc + jnp.dot(s, w_ref[ATTN_W:ATTN_W + SSM_W, cols], preferred_element_type=f32)
        acc = acc + jnp.dot(p, w_ref[ATTN_W + SSM_W:, cols], preferred_element_type=f32)
        o_ref[:, cols] = x[:, cols] + gate[:, cols] * acc


def out_projection(a_a, a_b, s, p_a, p_b, xs, mod, w, layer, row_of_tile, tm, tn=512):
    m_total = sum(x.shape[0] for x in xs)
    d = xs[0].shape[1]
    tiles_a, tiles_b = a_a.shape[0] // tm, a_b.shape[0] // tm
    assert len(xs) == 1 or xs[0].shape[0] == a_a.shape[0]
    return pl.pallas_call(
        functools.partial(_out_kernel, n_x=len(xs), tiles_a=tiles_a, tn=tn),
        grid=(m_total // tm,),
        in_specs=[
            *_two_part(tm, ATTN_W, tiles_a, tiles_b),
            pl.BlockSpec((tm, SSM_W), lambda i: (i, 0)),
            *_two_part(tm, POOL_W, tiles_a, tiles_b),
            *_row_specs(xs, tm, d),
            pl.BlockSpec((None, 1, 6, d), lambda i: (layer, row_of_tile(i), 0, 0)),
            pl.BlockSpec((None, d, d), lambda i: (layer, 0, 0), pipeline_mode=pl.Buffered(1)),
        ],
        out_specs=pl.BlockSpec((tm, d), lambda i: (i, 0)),
        out_shape=jax.ShapeDtypeStruct((m_total, d), f32),
        compiler_params=_cparams(("arbitrary",)),
        name="out_projection",
    )(a_a, a_b, s, p_a, p_b, *xs, mod, w)


def _ffn_kernel(x_ref, mod_ref, g_ref, wg_ref, wu_ref, wo_ref, fg_ref, *refs, n_out, tiles_a, final_norm):
    o_refs = refs[:n_out]
    h_ref, acc_ref = refs[n_out:]
    i, j = pl.program_id(0), pl.program_id(1)
    last = j == pl.num_programs(1) - 1

    @pl.when(j == 0)
    def _():
        m = mod_ref[0]
        x = x_ref[...]
        h_ref[...] = (_rms(x, g_ref[...]) * (1.0 + m[4:5]) + m[3:4]).astype(bf16)
        acc_ref[...] = x

    h = h_ref[...]
    gate = jnp.dot(h, wg_ref[...], preferred_element_type=f32)
    up = jnp.dot(h, wu_ref[...], preferred_element_type=f32)
    act = (gate * jax.nn.sigmoid(gate) * up).astype(bf16)
    acc_ref[...] += mod_ref[0][5:6] * jnp.dot(act, wo_ref[...], preferred_element_type=f32)

    def result():
        return _rms(acc_ref[...], fg_ref[...]) if final_norm else acc_ref[...]

    if n_out == 1:
        @pl.when(last)
        def _():
            o_refs[0][...] = result()
    else:
        @pl.when(last & (i < tiles_a))
        def _():
            o_refs[0][...] = result()

        @pl.when(last & (i >= tiles_a))
        def _():
            o_refs[1][...] = result()


def ffn(x, mod, g, w_in, w_out, final_g, layer, row_of_tile, tm, th, final_norm, split_rows=None):
    m_total, d = x.shape
    hidden = w_out.shape[1]
    nh = hidden // th
    if split_rows is None:
        tiles_a = m_total // tm
        out_specs = [pl.BlockSpec((tm, d), lambda i, j: (i, 0))]
        out_shape = [jax.ShapeDtypeStruct((m_total, d), f32)]
    else:
        tiles_a = split_rows // tm
        out_specs = list(_two_part(tm, d, tiles_a, m_total // tm - tiles_a))
        out_shape = [jax.ShapeDtypeStruct((split_rows, d), f32), jax.ShapeDtypeStruct((m_total - split_rows, d), f32)]
    return pl.pallas_call(
        functools.partial(_ffn_kernel, n_out=len(out_specs), tiles_a=tiles_a, final_norm=final_norm),
        grid=(m_total // tm, nh),
        in_specs=[
            pl.BlockSpec((tm, d), lambda i, j: (i, 0)),
            pl.BlockSpec((None, 1, 6, d), lambda i, j: (layer, row_of_tile(i), 0, 0)),
            pl.BlockSpec((None, 1, d), lambda i, j: (layer, 0, 0)),
            pl.BlockSpec((None, d, th), lambda i, j: (layer, 0, j)),
            pl.BlockSpec((None, d, th), lambda i, j: (layer, 0, nh + j)),
            pl.BlockSpec((None, th, d), lambda i, j: (layer, j, 0)),
            pl.BlockSpec((1, d), lambda i, j: (0, 0)),
        ],
        out_specs=out_specs,
        out_shape=out_shape,
        scratch_shapes=[pltpu.VMEM((tm, d), bf16), pltpu.VMEM((tm, d), f32)],
        compiler_params=_cparams(("arbitrary", "arbitrary")),
        name="ffn",
    )(x, mod, g, w_in, w_in, w_out, final_g)


def kernel(x_prompt, x_sample, c, cache_k, cache_v, state_ssm, c_ctx, w_mod, b_mod, norm1_g, norm2_g, w_in,
           attn_rpb, ssm_a_re, ssm_a_im, ssm_log_dt, ssm_b_re, ssm_b_im, ssm_c_re, ssm_c_im, ssm_d, ssm_w_glu,
           ssm_b_glu, pool_w, pool_scale, w_out, ffn_w_in, ffn_w_out, final_norm_g):
    batch, seq, d = x_prompt.shape
    dec_batch, dec_seq, _ = x_sample.shape
    depth = w_in.shape[0]
    m_ctx = batch * seq
    tm = 512
    assert m_ctx % tm == 0 and dec_seq % tm == 0 and dec_batch + 1 <= 8

    def row_of_tile(i):
        return jnp.where(i < m_ctx // tm, 0, 1 + (i - m_ctx // tm) // (dec_seq // tm))

    cond = jnp.concatenate([c_ctx[None, :], c, jnp.zeros((8 - 1 - dec_batch, d), f32)], axis=0)
    mod = modulation_all(cond, w_mod, b_mod)
    mod = mod[:, :1 + dec_batch].reshape(depth, 1 + dec_batch, 6, d)

    w_in_b, w_out_b = w_in.astype(bf16), w_out.astype(bf16)
    ffn_w_in_b, ffn_w_out_b, w_glu_b = ffn_w_in.astype(bf16), ffn_w_out.astype(bf16), ssm_w_glu.astype(bf16)
    row = lambda t: t[:, None, :]
    tables = jax.vmap(ssm_tables)(ssm_a_re, ssm_a_im, ssm_log_dt, ssm_b_re, ssm_b_im, ssm_c_re, ssm_c_im)
    bias_tables = jax.vmap(latent_bias_table)(attn_rpb)

    xs = (x_prompt.reshape(m_ctx, d), x_sample.reshape(dec_batch * dec_seq, d))
    h0_ctx = jnp.zeros((batch, SSM_PAIRS * SSM_PW), f32)

    zs, st_out = [], []
    for l in range(depth):
        last = l == depth - 1
        z = in_projection(xs, mod, row(norm1_g), w_in_b, l, row_of_tile, tm)

        a_ctx, *caches = context_attention(z, batch, seq, zs, write_caches=last)
        zs.append(z)
        a_lat = latent_attention(z, m_ctx, dec_batch, dec_seq, cache_k, cache_v, l, bias_tables)

        y_ctx, fin = ssm_scan(z, 0, batch, seq, tables, l, h0_ctx)
        y_lat, _ = ssm_scan(z, m_ctx, dec_batch, dec_seq, tables, l, state_to_slabs(state_ssm[:, l]))
        st_out.append(state_from_slabs(fin))
        s_out = ssm_glu(y_ctx, y_lat, z, row(ssm_d), w_glu_b, row(ssm_b_glu), l, tm)

        p_ctx = pool_mixer(z, 0, batch, seq, pool_w, row(pool_scale), l)
        p_lat = pool_mixer(z, m_ctx, dec_batch, dec_seq, pool_w, row(pool_scale), l)

        x = out_projection(a_ctx, a_lat, s_out, p_ctx, p_lat, xs, mod, w_out_b, l, row_of_tile, tm)
        xs = ffn(x, mod, row(norm2_g), ffn_w_in_b, ffn_w_out_b, final_norm_g[None], l, row_of_tile, tm, 512,
                 final_norm=last, split_rows=m_ctx if last else None)

    y_prompt = xs[0].reshape(batch, seq, d)
    y_sample = xs[1].reshape(dec_batch, dec_seq, d)
    return (y_prompt, y_sample, caches[0], caches[1], jnp.stack(st_out, axis=1))
```

```python
import functools

import jax
import jax.numpy as jnp
import numpy as np
from jax import lax
from jax.experimental import pallas as pl
from jax.experimental.pallas import tpu as pltpu

f32 = jnp.float32
bf16 = jnp.bfloat16

D_MODEL = 2048
N_HEADS = 16
HEAD_DIM = 64
ATTN_W = N_HEADS * HEAD_DIM
SSM_W = 512
SSM_GROUP = 16
SSM_GROUPS = 32
SSM_STATE = 64
POOL_W = 512
POOL_WINDOWS = (2, 4, 8, 16)
POOL_GROUP = 128
POOL_PAD = 16
IN_W = 3 * ATTN_W + SSM_W + POOL_W
GRID_W = 64
WIN_ROWS_MAX = 8
WIN_COLS = 16
RMS_EPS = 1e-6
NEG_INF = -1e30

LAT_BLOCK_ROWS = 4
SSM_CHUNK = 16
SSM_PAIRS = SSM_GROUPS // 2
SSM_QUAD = 4
SSM_PW = 2 * SSM_CHUNK * SSM_GROUP

VMEM_LIMIT = 56 * 1024 * 1024


def _cparams(sem):
    return pltpu.CompilerParams(dimension_semantics=sem, vmem_limit_bytes=VMEM_LIMIT)


def _rms(x, g):
    return x * lax.rsqrt(jnp.mean(x * x, axis=-1, keepdims=True) + RMS_EPS) * g


def _two_part(tm, width, tiles_a, tiles_b, col=lambda *_: 0):
    return (pl.BlockSpec((tm, width), lambda i, *r: (jnp.minimum(i, tiles_a - 1), col(*r))),
            pl.BlockSpec((tm, width), lambda i, *r: (jnp.clip(i - tiles_a, 0, tiles_b - 1), col(*r))))


def _row_specs(parts, tm, width, col=lambda *_: 0):
    if len(parts) == 1:
        return (pl.BlockSpec((tm, width), lambda i, *r: (i, col(*r))),)
    return _two_part(tm, width, parts[0].shape[0] // tm, parts[1].shape[0] // tm, col)


def _pick(first, a_ref, b_ref):
    return jnp.where(first, a_ref[...], b_ref[...])


def _rows(refs, tiles_a):
    if len(refs) == 1:
        return refs[0][...]
    return _pick(pl.program_id(0) < tiles_a, *refs)


def _mod_kernel(c_ref, w_ref, b_ref, o_ref):
    c = c_ref[...]
    s = (c * jax.nn.sigmoid(c)).astype(bf16)
    o_ref[0] = jnp.dot(s, w_ref[0].astype(bf16), preferred_element_type=f32) + b_ref[0]


def modulation_all(cond, w_mod, b_mod, tn=1024):
    n_layers, d, n = w_mod.shape
    return pl.pallas_call(
        _mod_kernel,
        grid=(n_layers, n // tn),
        in_specs=[
            pl.BlockSpec((8, d), lambda l, j: (0, 0)),
            pl.BlockSpec((1, d, tn), lambda l, j: (l, 0, j)),
            pl.BlockSpec((1, 1, tn), lambda l, j: (l, 0, j)),
        ],
        out_specs=pl.BlockSpec((1, 8, tn), lambda l, j: (l, 0, j)),
        out_shape=jax.ShapeDtypeStruct((n_layers, 8, n), f32),
        compiler_params=_cparams(("parallel", "parallel")),
        name="modulation",
    )(cond, w_mod, b_mod.reshape(n_layers, 1, n))


def _in_kernel(*refs, n_x, tiles_a, tn):
    x_refs = refs[:n_x]
    mod_ref, g_ref, w_ref, o_ref = refs[n_x:]
    m = mod_ref[0]
    y = _rms(_rows(x_refs, tiles_a), g_ref[...])
    h = (y * (1.0 + m[1:2]) + m[0:1]).astype(bf16)
    for c in range(w_ref.shape[1] // tn):
        cols = slice(c * tn, (c + 1) * tn)
        o_ref[:, cols] = jnp.dot(h, w_ref[:, cols], preferred_element_type=f32)


def in_projection(xs, mod, g, w, layer, row_of_tile, tm, tn=512):
    m_total = sum(x.shape[0] for x in xs)
    d = xs[0].shape[1]
    n = w.shape[2]
    return pl.pallas_call(
        functools.partial(_in_kernel, n_x=len(xs), tiles_a=xs[0].shape[0] // tm, tn=tn),
        grid=(m_total // tm,),
        in_specs=[
            *_row_specs(xs, tm, d),
            pl.BlockSpec((None, 1, 6, d), lambda i: (layer, row_of_tile(i), 0, 0)),
            pl.BlockSpec((None, 1, d), lambda i: (layer, 0, 0)),
            pl.BlockSpec((None, d, n), lambda i: (layer, 0, 0), pipeline_mode=pl.Buffered(1)),
        ],
        out_specs=pl.BlockSpec((tm, n), lambda i: (i, 0)),
        out_shape=jax.ShapeDtypeStruct((m_total, n), f32),
        compiler_params=_cparams(("arbitrary",)),
        name="in_projection",
    )(*xs, mod, g, w)


def _first_head(shape):
    return lax.broadcasted_iota(jnp.int32, shape, len(shape) - 1) < HEAD_DIM


def _attn_ctx_kernel(*refs, n_prev, write_caches):
    q_ref, k_ref, v_ref = refs[:3]
    prev = refs[3:3 + 2 * n_prev]
    outs = refs[3 + 2 * n_prev:]
    o_ref = outs[0]
    scale = HEAD_DIM ** -0.5
    nt = (((1,), (1,)), ((), ()))
    seq = q_ref.shape[0]
    lanes = 2 * HEAD_DIM
    first = _first_head((seq, lanes))
    keeps = (first, jnp.logical_not(first))

    def scores_of(p):
        cols = slice(p * lanes, (p + 1) * lanes)
        q = q_ref[:, cols].astype(bf16)
        k = k_ref[:, cols]
        v = v_ref[:, cols]
        if write_caches:
            kc_ref, vc_ref = outs[1:]
            layers = [(prev[2 * l][:, cols], prev[2 * l + 1][:, cols]) for l in range(n_prev)] + [(k, v)]
            for l, (kl, vl) in enumerate(layers):
                for h in range(2):
                    kc_ref[0, l, 2 * p + h] = kl[:, h * HEAD_DIM:(h + 1) * HEAD_DIM]
                    vc_ref[0, l, 2 * p + h] = vl[:, h * HEAD_DIM:(h + 1) * HEAD_DIM]
        kb = k.astype(bf16)
        zero = jnp.zeros_like(kb)
        return [lax.dot_general(q, jnp.where(keep, kb, zero), nt, preferred_element_type=f32) * scale
                for keep in keeps]

    def finish(p, scores):
        cols = slice(p * lanes, (p + 1) * lanes)
        vb = v_ref[:, cols].astype(bf16)
        zero = jnp.zeros_like(vb)
        num, inv = None, []
        for keep, s in zip(keeps, scores):
            e = jnp.exp(s - jnp.max(s, axis=-1, keepdims=True))
            inv.append(1.0 / jnp.sum(e, axis=-1, keepdims=True))
            part = jnp.dot(e.astype(bf16), jnp.where(keep, vb, zero), preferred_element_type=f32)
            num = part if num is None else num + part
        o_ref[:, cols] = (num * jnp.where(first, inv[0], inv[1])).astype(bf16)

    n_pairs = N_HEADS // 2
    pending = scores_of(0)
    for p in range(n_pairs):
        upcoming = scores_of(p + 1) if p + 1 < n_pairs else None
        finish(p, pending)
        pending = upcoming


def context_attention(z, batch, seq, prev_zs=(), write_caches=False):
    col = lambda blk: pl.BlockSpec((seq, ATTN_W), lambda b: (b, blk))
    out_specs = [pl.BlockSpec((seq, ATTN_W), lambda b: (b, 0))]
    out_shape = [jax.ShapeDtypeStruct((batch * seq, ATTN_W), bf16)]
    if write_caches:
        depth = len(prev_zs) + 1
        cache = pl.BlockSpec((1, depth, N_HEADS, seq, HEAD_DIM), lambda b: (b, 0, 0, 0, 0))
        cache_shape = jax.ShapeDtypeStruct((batch, depth, N_HEADS, seq, HEAD_DIM), f32)
        out_specs += [cache, cache]
        out_shape += [cache_shape, cache_shape]
    else:
        prev_zs = ()
    prev_args = [a for zp in prev_zs for a in (zp, zp)]
    return pl.pallas_call(
        functools.partial(_attn_ctx_kernel, n_prev=len(prev_zs), write_caches=write_caches),
        grid=(batch,),
        in_specs=[col(0), col(1), col(2)] + [col(1), col(2)] * len(prev_zs),
        out_specs=out_specs,
        out_shape=out_shape,
        compiler_params=_cparams(("parallel",)),
        name="context_attention",
    )(z, z, z, *prev_args)


def _attn_lat_kernel(q_ref, k_ref, v_ref, ck_ref, cv_ref, t_ref, o_ref, bias_scr, k_scr, v_scr, ck_scr, cv_scr,
                     *, rows, wr):
    scale = HEAD_DIM ** -0.5
    n_loc = wr * GRID_W
    for h in range(2):
        for d in range(wr):
            for j in range(wr):
                bias_scr[h, d, :, j * GRID_W:(j + 1) * GRID_W] = t_ref[h, j - d + WIN_ROWS_MAX - 1]

    kb = k_ref[...].astype(bf16)
    vb = v_ref[...].astype(bf16)
    first = _first_head(kb.shape)
    zero = jnp.zeros_like(kb)
    pad = jnp.zeros(ck_ref.shape[3:], bf16)
    for h in range(2):
        keep = first if h == 0 else jnp.logical_not(first)
        k_scr[h] = jnp.where(keep, kb, zero)
        v_scr[h] = jnp.where(keep, vb, zero)
        ck, cv = ck_ref[0, 0, h].astype(bf16), cv_ref[0, 0, h].astype(bf16)
        ck_scr[h] = jnp.concatenate([ck, pad] if h == 0 else [pad, ck], axis=-1)
        cv_scr[h] = jnp.concatenate([cv, pad] if h == 0 else [pad, cv], axis=-1)

    nt = (((1,), (1,)), ((), ()))
    br = LAT_BLOCK_ROWS
    nq = br * GRID_W
    first_q = _first_head((nq, 2 * HEAD_DIM))

    def block_rows(blk):
        return slice(blk * nq, (blk + 1) * nq)

    def windows_of(blk):
        out = []
        for i in range(br):
            r = blk * br + i
            rs = min(max(r - wr // 2, 0), rows - wr)
            out.append((slice(rs * GRID_W, rs * GRID_W + n_loc), r - rs))
        return out

    def scores_of(blk):
        q = q_ref[block_rows(blk), :].astype(bf16)
        out = []
        for h in range(2):
            s_ctx = lax.dot_general(q, ck_scr[h], nt, preferred_element_type=f32) * scale
            s_loc = [lax.dot_general(q[i * GRID_W:(i + 1) * GRID_W], k_scr[h, win, :], nt,
                                     preferred_element_type=f32) * scale + bias_scr[h, off]
                     for i, (win, off) in enumerate(windows_of(blk))]
            out.append((jnp.concatenate(s_loc, axis=0), s_ctx))
        return out

    def finish(blk, scores):
        probs, inv = [], []
        for s_loc, s_ctx in scores:
            m = jnp.maximum(jnp.max(s_loc, axis=-1, keepdims=True), jnp.max(s_ctx, axis=-1, keepdims=True))
            e_loc = jnp.exp(s_loc - m)
            e_ctx = jnp.exp(s_ctx - m)
            inv.append(1.0 / (jnp.sum(e_loc, axis=-1, keepdims=True) + jnp.sum(e_ctx, axis=-1, keepdims=True)))
            probs.append((e_loc.astype(bf16), e_ctx.astype(bf16)))
        num = None
        for h, (e_loc, e_ctx) in enumerate(probs):
            o_loc = [jnp.dot(e_loc[i * GRID_W:(i + 1) * GRID_W], v_scr[h, win, :], preferred_element_type=f32)
                     for i, (win, _) in enumerate(windows_of(blk))]
            part = jnp.dot(e_ctx, cv_scr[h], preferred_element_type=f32) + jnp.concatenate(o_loc, axis=0)
            num = part if num is None else num + part
        o_ref[block_rows(blk), :] = (num * jnp.where(first_q, inv[0], inv[1])).astype(bf16)

    pending = scores_of(0)
    for blk in range(rows // br):
        upcoming = scores_of(blk + 1) if blk + 1 < rows // br else None
        finish(blk, pending)
        pending = upcoming


def latent_bias_table(rpb):
    cols = np.arange(GRID_W)
    col_start = np.clip(cols - WIN_COLS // 2, 0, GRID_W - WIN_COLS)
    valid = (cols[None, :] >= col_start[:, None]) & (cols[None, :] < col_start[:, None] + WIN_COLS)
    dc = cols[None, :] - cols[:, None] + WIN_COLS - 1
    assert np.all((dc[valid] >= 0) & (dc[valid] < 2 * WIN_COLS - 1))
    onehot = ((np.arange(2 * WIN_COLS - 1)[:, None, None] == dc[None]) & valid[None]).astype(np.float32)
    t = jnp.einsum('hrd,dqk->hrqk', rpb.astype(f32), jnp.asarray(onehot), precision=lax.Precision.HIGHEST)
    return jnp.where(jnp.asarray(valid)[None, None], t, NEG_INF)


def latent_attention(z, row0, batch, seq, cache_k, cache_v, layer, tables):
    lanes = 2 * HEAD_DIM
    n_pairs = N_HEADS // 2
    rows = seq // GRID_W
    wr = min(WIN_ROWS_MAX, rows)
    past = cache_k.shape[3]
    blk0 = row0 // seq
    n_dr = 2 * WIN_ROWS_MAX - 1
    assert rows % LAT_BLOCK_ROWS == 0
    col = lambda off: pl.BlockSpec((seq, lanes), lambda b, p: (blk0 + b, off + p))
    ctx = pl.BlockSpec((1, 1, 2, past, HEAD_DIM), lambda b, p: (b, layer, p, 0, 0))
    return pl.pallas_call(
        functools.partial(_attn_lat_kernel, rows=rows, wr=wr),
        grid=(batch, n_pairs),
        in_specs=[col(0), col(n_pairs), col(2 * n_pairs), ctx, ctx,
                  pl.BlockSpec((None, 2, n_dr, GRID_W, GRID_W), lambda b, p: (layer, p, 0, 0, 0))],
        out_specs=pl.BlockSpec((seq, lanes), lambda b, p: (b, p)),
        out_shape=jax.ShapeDtypeStruct((batch * seq, ATTN_W), bf16),
        scratch_shapes=[pltpu.VMEM((2, wr, GRID_W, wr * GRID_W), f32),
                        pltpu.VMEM((2, seq, lanes), bf16), pltpu.VMEM((2, seq, lanes), bf16),
                        pltpu.VMEM((2, past, lanes), bf16), pltpu.VMEM((2, past, lanes), bf16)],
        compiler_params=_cparams(("parallel", "parallel")),
        name="latent_attention",
    )(z, z, z, cache_k, cache_v, tables)


def ssm_tables(a_re, a_im, log_dt, b_re, b_im, c_re, c_im):
    lc, p, m = SSM_CHUNK, SSM_STATE, SSM_GROUP
    eye2 = jnp.eye(2, dtype=f32)
    half = SSM_PW // 2

    def slab_row(t):
        t = jnp.broadcast_to(t.astype(f32).reshape(2, 1, SSM_PAIRS, 2, p), (2, 2, SSM_PAIRS, 2, p))
        return t.transpose(2, 0, 1, 3, 4).reshape(SSM_PAIRS, SSM_PW)

    def embed(first, second):
        t = jnp.stack([first, second], axis=1).astype(f32)
        x = t.shape[-1]
        t = t.reshape(2, 2, SSM_PAIRS, 2, p, x).transpose(2, 3, 5, 0, 1, 4)
        t = t[:, :, :, :, :, None, :] * eye2[None, :, None, None, None, :, None]
        return t.reshape(SSM_PAIRS, 2 * x, SSM_PW)

    lr, li = slab_row(a_re), slab_row(a_im)
    dt = slab_row(jnp.broadcast_to(jnp.exp(log_dt.astype(f32))[..., None], a_re.shape))
    sr, si = lr * dt, li * dt
    k = jnp.arange(lc + 1, dtype=f32)[:, None, None]
    mag = jnp.exp(k * sr[None])
    ar, ai = mag * jnp.cos(k * si[None]), mag * jnp.sin(k * si[None])
    den = lr * lr + li * li
    qr = ((ar[1] - 1.0) * lr + ai[1] * li) / den
    qi = (ai[1] * lr - (ar[1] - 1.0) * li) / den
    b4, b4s = embed(b_re, b_im), embed(-b_im, b_re)
    bb4 = (qr[:, None] * b4 + qi[:, None] * b4s)[:, None]
    bb4s = (qr[:, None] * b4s - qi[:, None] * b4)[:, None]
    rev = np.arange(lc)[::-1]
    fwd = np.arange(lc)

    def powers(t, ks):
        return jnp.stack([t[int(k)] for k in ks], axis=0)

    def per_step(t, k_fwd, k_bwd):
        t = jnp.concatenate([powers(t, k_fwd)[..., :half], powers(t, k_bwd)[..., half:]], axis=-1)
        return t.transpose(1, 0, 2)[:, :, None, :]

    bx = bb4 * per_step(ar, rev, fwd) + bb4s * per_step(ai, rev, fwd)
    bx = bx.reshape(SSM_PAIRS, SSM_PW, SSM_PW)
    ct_re, ct_im = c_re.transpose(0, 1, 3, 2), c_im.transpose(0, 1, 3, 2)
    c4 = embed(ct_re, -ct_im)[:, None]
    c4s = embed(-ct_im, -ct_re)[:, None]
    cyt = c4 * per_step(ar, fwd + 1, lc - fwd) + c4s * per_step(ai, fwd + 1, lc - fwd)
    cyt = cyt.reshape(SSM_PAIRS, SSM_PW, SSM_PW)

    j = np.arange(2 * lc - 1)
    live = jnp.asarray(np.concatenate([np.broadcast_to((j >= lc - 1)[:, None], (2 * lc - 1, half)),
                                       np.broadcast_to((j <= lc - 1)[:, None], (2 * lc - 1, half))], axis=-1), f32)
    kf, kb = np.maximum(j - (lc - 1), 0), np.maximum(lc - 1 - j, 0)
    kt = (c4 * per_step(ar, kf, kb) + c4s * per_step(ai, kf, kb)) * live[None, :, None, :]
    kt = kt.reshape(SSM_PAIRS, (2 * lc - 1) * 2 * m, SSM_PW)
    strip = lax.dot_general(bb4[:, 0], kt, (((2,), (2,)), ((0,), (0,))),
                            precision=lax.Precision.HIGH)
    tz = jnp.stack([strip[:, :, (lc - 1 - s) * 2 * m:(lc - 1 - s) * 2 * m + SSM_PW] for s in range(lc)], axis=1)
    tz = tz.reshape(SSM_PAIRS, SSM_PW, SSM_PW)

    sw = 2 * p
    a16p = jnp.concatenate([ar[lc][:, :sw], ai[lc][:, sw:half], ar[lc][:, half:half + sw], ai[lc][:, half + sw:]],
                           axis=-1)[:, None, :]
    return tz.astype(bf16), bx.astype(bf16), cyt.astype(bf16), a16p


def _ssm_kernel(u_ref, tz_ref, bx_ref, cyt_ref, a_ref, h0_ref, y_ref, fin_ref, x_scr, s_scr, y_scr, *, n_seq, n_chunks):
    r = n_seq * n_chunks
    pc = 2 * SSM_GROUP
    sw = 2 * SSM_STATE
    xs = [u_ref[pl.ds(s, r, stride=SSM_CHUNK), :] for s in range(SSM_CHUNK)]
    for kq in range(SSM_QUAD):
        u = jnp.concatenate([x[:, kq * pc:(kq + 1) * pc] for x in xs], axis=-1).astype(bf16)
        x = jnp.dot(u, bx_ref[kq], preferred_element_type=f32)
        for i in range(4):
            x_scr[4 * kq + i] = x[:, i * sw:(i + 1) * sw]
        y_scr[:, kq * SSM_PW:(kq + 1) * SSM_PW] = jnp.dot(u, tz_ref[kq], preferred_element_type=f32)

    def scan(direction):
        offs = [kq * SSM_PW + 2 * direction * sw for kq in range(SSM_QUAD)]
        slabs = [4 * kq + 2 * direction for kq in range(SSM_QUAD)]
        coef = [(a_ref[kq][:, 2 * direction * sw:2 * direction * sw + sw],
                 a_ref[kq][:, 2 * direction * sw + sw:2 * direction * sw + 2 * sw]) for kq in range(SSM_QUAD)]

        def body(i, carry):
            c = i if direction == 0 else n_chunks - 1 - i
            rows = pl.ds(c, n_seq, stride=n_chunks)
            out = []
            for kq in range(SSM_QUAD):
                sr, si = carry[2 * kq], carry[2 * kq + 1]
                ar, ai = coef[kq]
                re, im = slabs[kq], slabs[kq] + 1
                s_scr[re, rows, :] = sr
                s_scr[im, rows, :] = si
                xr = x_scr[re, rows, :]
                xi = x_scr[im, rows, :]
                out += [ar * sr - ai * si + xr, ar * si + ai * sr + xi]
            return tuple(out)

        init = []
        for lo in offs:
            init += [h0_ref[:, lo:lo + sw], h0_ref[:, lo + sw:lo + 2 * sw]]
        fin = lax.fori_loop(0, n_chunks, body, tuple(init))
        for kq, lo in enumerate(offs):
            fin_ref[:, lo:lo + sw] = fin[2 * kq]
            fin_ref[:, lo + sw:lo + 2 * sw] = fin[2 * kq + 1]

    scan(0)
    scan(1)
    ys = []
    for kq in range(SSM_QUAD):
        cols = slice(kq * SSM_PW, (kq + 1) * SSM_PW)
        s_in = jnp.concatenate([s_scr[4 * kq + i] for i in range(4)], axis=-1).astype(bf16)
        ys.append(y_scr[:, cols] + lax.dot_general(s_in, cyt_ref[kq], (((1,), (1,)), ((), ())),
                                                   preferred_element_type=f32))
    for t in range(SSM_CHUNK):
        y_ref[pl.ds(t, r, stride=SSM_CHUNK), :] = jnp.concatenate([y[:, t * pc:(t + 1) * pc] for y in ys], axis=-1)


def ssm_scan(z, row0, n_seq, seq, tables, layer, h0):
    tz, bx, cyt, a16p = tables
    n_chunks = seq // SSM_CHUNK
    rows = n_seq * seq
    r = n_seq * n_chunks
    lanes = SSM_QUAD * 2 * SSM_GROUP
    u_blk = (3 * ATTN_W) // lanes
    qw = SSM_QUAD * SSM_PW
    table = pl.BlockSpec((None, SSM_QUAD, SSM_PW, SSM_PW), lambda q: (layer, q, 0, 0))
    return pl.pallas_call(
        functools.partial(_ssm_kernel, n_seq=n_seq, n_chunks=n_chunks),
        grid=(SSM_PAIRS // SSM_QUAD,),
        in_specs=[pl.BlockSpec((rows, lanes), lambda q: (row0 // rows, u_blk + q)),
                  table, table, table,
                  pl.BlockSpec((None, SSM_QUAD, 1, SSM_PW), lambda q: (layer, q, 0, 0)),
                  pl.BlockSpec((n_seq, qw), lambda q: (0, q))],
        out_specs=[pl.BlockSpec((rows, lanes), lambda q: (0, q)),
                   pl.BlockSpec((n_seq, qw), lambda q: (0, q))],
        out_shape=[jax.ShapeDtypeStruct((rows, SSM_W), f32),
                   jax.ShapeDtypeStruct((n_seq, SSM_PAIRS * SSM_PW), f32)],
        scratch_shapes=[pltpu.VMEM((4 * SSM_QUAD, r, 2 * SSM_STATE), f32),
                        pltpu.VMEM((4 * SSM_QUAD, r, 2 * SSM_STATE), f32), pltpu.VMEM((r, qw), f32)],
        compiler_params=_cparams(("parallel",)),
        name="ssm_scan",
    )(z, tz, bx, cyt, a16p, h0)


def state_to_slabs(st):
    batch = st.shape[0]
    t = st.reshape(batch, 2, SSM_PAIRS, 2, SSM_STATE, 2).transpose(0, 2, 1, 5, 3, 4)
    return t.reshape(batch, SSM_PAIRS * SSM_PW)


def state_from_slabs(fin):
    batch = fin.shape[0]
    t = fin.reshape(batch, SSM_PAIRS, 2, 2, 2, SSM_STATE)
    return t.transpose(0, 2, 1, 4, 5, 3).reshape(batch, 2, SSM_GROUPS, SSM_STATE, 2)


def _glu_kernel(ya_ref, yb_ref, u_ref, d_ref, w_ref, b_ref, o_ref, *, tiles_a):
    y = _pick(pl.program_id(0) < tiles_a, ya_ref, yb_ref) + d_ref[...] * u_ref[...]
    z = jnp.dot(jax.nn.gelu(y).astype(bf16), w_ref[...], preferred_element_type=f32) + b_ref[...]
    o_ref[...] = (z[:, :SSM_W] * jax.nn.sigmoid(z[:, SSM_W:])).astype(bf16)


def ssm_glu(y_a, y_b, z, d_skip, w_glu, b_glu, layer, tm):
    tiles_a, tiles_b = y_a.shape[0] // tm, y_b.shape[0] // tm
    m_total = y_a.shape[0] + y_b.shape[0]
    u_blk = (3 * ATTN_W) // SSM_W
    return pl.pallas_call(
        functools.partial(_glu_kernel, tiles_a=tiles_a),
        grid=(tiles_a + tiles_b,),
        in_specs=[
            *_two_part(tm, SSM_W, tiles_a, tiles_b),
            pl.BlockSpec((tm, SSM_W), lambda i: (i, u_blk)),
            pl.BlockSpec((None, 1, SSM_W), lambda i: (layer, 0, 0)),
            pl.BlockSpec((None, SSM_W, 2 * SSM_W), lambda i: (layer, 0, 0)),
            pl.BlockSpec((None, 1, 2 * SSM_W), lambda i: (layer, 0, 0)),
        ],
        out_specs=pl.BlockSpec((tm, SSM_W), lambda i: (i, 0)),
        out_shape=jax.ShapeDtypeStruct((m_total, SSM_W), bf16),
        compiler_params=_cparams(("arbitrary",)),
        name="ssm_glu",
    )(y_a, y_b, z, d_skip, w_glu, b_glu)


def _pool_kernel(p_ref, w_ref, sc_ref, o_ref, pad_ref, *, seq):
    zeros = jnp.zeros((POOL_PAD, POOL_W), f32)
    pad_ref[0:POOL_PAD, :] = zeros
    pad_ref[POOL_PAD + seq:, :] = zeros
    pad_ref[POOL_PAD:POOL_PAD + seq, :] = p_ref[...]
    t = lax.broadcasted_iota(jnp.int32, (seq, 1), 0)
    for g, win in enumerate(POOL_WINDOWS):
        cols = slice(g * POOL_GROUP, (g + 1) * POOL_GROUP)
        total = jnp.zeros((seq, POOL_GROUP), f32)
        for d in range(-(win // 2), win - win // 2):
            total = total + pad_ref[POOL_PAD + d:POOL_PAD + d + seq, cols]
        lo = jnp.clip(t - win // 2, 0, seq)
        hi = jnp.clip(t - win // 2 + win, 0, seq)
        mixed = total / (hi - lo).astype(f32) - p_ref[:, cols]
        out = jnp.dot(mixed.astype(bf16), w_ref[g].astype(bf16), preferred_element_type=f32)
        o_ref[:, cols] = (out * sc_ref[:, cols]).astype(bf16)


def pool_mixer(z, row0, batch, seq, w_pool, pool_scale, layer):
    p_blk = (3 * ATTN_W + SSM_W) // POOL_W
    blk0 = row0 // seq
    return pl.pallas_call(
        functools.partial(_pool_kernel, seq=seq),
        grid=(batch,),
        in_specs=[
            pl.BlockSpec((seq, POOL_W), lambda b: (blk0 + b, p_blk)),
            pl.BlockSpec((None, len(POOL_WINDOWS), POOL_GROUP, POOL_GROUP), lambda b: (layer, 0, 0, 0)),
            pl.BlockSpec((None, 1, POOL_W), lambda b: (layer, 0, 0)),
        ],
        out_specs=pl.BlockSpec((seq, POOL_W), lambda b: (b, 0)),
        out_shape=jax.ShapeDtypeStruct((batch * seq, POOL_W), bf16),
        scratch_shapes=[pltpu.VMEM((seq + 2 * POOL_PAD, POOL_W), f32)],
        compiler_params=_cparams(("parallel",)),
        name="pool_mixer",
    )(z, w_pool, pool_scale)


def _out_kernel(aa_ref, ab_ref, s_ref, pa_ref, pb_ref, *refs, n_x, tiles_a, tn):
    x_refs = refs[:n_x]
    mod_ref, w_ref, o_ref = refs[n_x:]
    first = pl.program_id(0) < tiles_a
    a = _pick(first, aa_ref, ab_ref)
    s = s_ref[...]
    p = _pick(first, pa_ref, pb_ref)
    x = _rows(x_refs, tiles_a)
    gate = mod_ref[0][2:3]
    for c in range(w_ref.shape[1] // tn):
        cols = slice(c * tn, (c + 1) * tn)
        acc = jnp.dot(a, w_ref[0:ATTN_W, cols], preferred_element_type=f32)
        acc = acc + jnp.dot(s, w_ref[ATTN_W:ATTN_W + SSM_W, cols], preferred_element_type=f32)
        acc = acc + jnp.dot(p, w_ref[ATTN_W + SSM_W:, cols], preferred_element_type=f32)
        o_ref[:, cols] = x[:, cols] + gate[:, cols] * acc


def out_projection(a_a, a_b, s, p_a, p_b, xs, mod, w, layer, row_of_tile, tm, tn=512):
    m_total = sum(x.shape[0] for x in xs)
    d = xs[0].shape[1]
    tiles_a, tiles_b = a_a.shape[0] // tm, a_b.shape[0] // tm
    assert len(xs) == 1 or xs[0].shape[0] == a_a.shape[0]
    return pl.pallas_call(
        functools.partial(_out_kernel, n_x=len(xs), tiles_a=tiles_a, tn=tn),
        grid=(m_total // tm,),
        in_specs=[
            *_two_part(tm, ATTN_W, tiles_a, tiles_b),
            pl.BlockSpec((tm, SSM_W), lambda i: (i, 0)),
            *_two_part(tm, POOL_W, tiles_a, tiles_b),
            *_row_specs(xs, tm, d),
            pl.BlockSpec((None, 1, 6, d), lambda i: (layer, row_of_tile(i), 0, 0)),
            pl.BlockSpec((None, d, d), lambda i: (layer, 0, 0), pipeline_mode=pl.Buffered(1)),
        ],
        out_specs=pl.BlockSpec((tm, d), lambda i: (i, 0)),
        out_shape=jax.ShapeDtypeStruct((m_total, d), f32),
        compiler_params=_cparams(("arbitrary",)),
        name="out_projection",
    )(a_a, a_b, s, p_a, p_b, *xs, mod, w)


def _ffn_kernel(x_ref, mod_ref, g_ref, wg_ref, wu_ref, wo_ref, fg_ref, *refs, n_out, tiles_a, final_norm):
    o_refs = refs[:n_out]
    h_ref, acc_ref = refs[n_out:]
    i, j = pl.program_id(0), pl.program_id(1)
    last = j == pl.num_programs(1) - 1

    @pl.when(j == 0)
    def _():
        m = mod_ref[0]
        x = x_ref[...]
        h_ref[...] = (_rms(x, g_ref[...]) * (1.0 + m[4:5]) + m[3:4]).astype(bf16)
        acc_ref[...] = x

    h = h_ref[...]
    gate = jnp.dot(h, wg_ref[...], preferred_element_type=f32)
    up = jnp.dot(h, wu_ref[...], preferred_element_type=f32)
    act = (gate * jax.nn.sigmoid(gate) * up).astype(bf16)
    acc_ref[...] += mod_ref[0][5:6] * jnp.dot(act, wo_ref[...], preferred_element_type=f32)

    def result():
        return _rms(acc_ref[...], fg_ref[...]) if final_norm else acc_ref[...]

    if n_out == 1:
        @pl.when(last)
        def _():
            o_refs[0][...] = result()
    else:
        @pl.when(last & (i < tiles_a))
        def _():
            o_refs[0][...] = result()

        @pl.when(last & (i >= tiles_a))
        def _():
            o_refs[1][...] = result()


def ffn(x, mod, g, w_in, w_out, final_g, layer, row_of_tile, tm, th, final_norm, split_rows=None):
    m_total, d = x.shape
    hidden = w_out.shape[1]
    nh = hidden // th
    if split_rows is None:
        tiles_a = m_total // tm
        out_specs = [pl.BlockSpec((tm, d), lambda i, j: (i, 0))]
        out_shape = [jax.ShapeDtypeStruct((m_total, d), f32)]
    else:
        tiles_a = split_rows // tm
        out_specs = list(_two_part(tm, d, tiles_a, m_total // tm - tiles_a))
        out_shape = [jax.ShapeDtypeStruct((split_rows, d), f32), jax.ShapeDtypeStruct((m_total - split_rows, d), f32)]
    return pl.pallas_call(
        functools.partial(_ffn_kernel, n_out=len(out_specs), tiles_a=tiles_a, final_norm=final_norm),
        grid=(m_total // tm, nh),
        in_specs=[
            pl.BlockSpec((tm, d), lambda i, j: (i, 0)),
            pl.BlockSpec((None, 1, 6, d), lambda i, j: (layer, row_of_tile(i), 0, 0)),
            pl.BlockSpec((None, 1, d), lambda i, j: (layer, 0, 0)),
            pl.BlockSpec((None, d, th), lambda i, j: (layer, 0, j)),
            pl.BlockSpec((None, d, th), lambda i, j: (layer, 0, nh + j)),
            pl.BlockSpec((None, th, d), lambda i, j: (layer, j, 0)),
            pl.BlockSpec((1, d), lambda i, j: (0, 0)),
        ],
        out_specs=out_specs,
        out_shape=out_shape,
        scratch_shapes=[pltpu.VMEM((tm, d), bf16), pltpu.VMEM((tm, d), f32)],
        compiler_params=_cparams(("arbitrary", "arbitrary")),
        name="ffn",
    )(x, mod, g, w_in, w_in, w_out, final_g)


def kernel(x_prompt, x_sample, c, cache_k, cache_v, state_ssm, c_ctx, w_mod, b_mod, norm1_g, norm2_g, w_in,
           attn_rpb, ssm_a_re, ssm_a_im, ssm_log_dt, ssm_b_re, ssm_b_im, ssm_c_re, ssm_c_im, ssm_d, ssm_w_glu,
           ssm_b_glu, pool_w, pool_scale, w_out, ffn_w_in, ffn_w_out, final_norm_g):
    batch, seq, d = x_prompt.shape
    dec_batch, dec_seq, _ = x_sample.shape
    depth = w_in.shape[0]
    m_ctx = batch * seq
    tm = 512
    assert m_ctx % tm == 0 and dec_seq % tm == 0 and dec_batch + 1 <= 8

    def row_of_tile(i):
        return jnp.where(i < m_ctx // tm, 0, 1 + (i - m_ctx // tm) // (dec_seq // tm))

    cond = jnp.concatenate([c_ctx[None, :], c, jnp.zeros((8 - 1 - dec_batch, d), f32)], axis=0)
    mod = modulation_all(cond, w_mod, b_mod)
    mod = mod[:, :1 + dec_batch].reshape(depth, 1 + dec_batch, 6, d)

    w_in_b, w_out_b = w_in.astype(bf16), w_out.astype(bf16)
    ffn_w_in_b, ffn_w_out_b, w_glu_b = ffn_w_in.astype(bf16), ffn_w_out.astype(bf16), ssm_w_glu.astype(bf16)
    row = lambda t: t[:, None, :]
    tables = jax.vmap(ssm_tables)(ssm_a_re, ssm_a_im, ssm_log_dt, ssm_b_re, ssm_b_im, ssm_c_re, ssm_c_im)
    bias_tables = jax.vmap(latent_bias_table)(attn_rpb)

    xs = (x_prompt.reshape(m_ctx, d), x_sample.reshape(dec_batch * dec_seq, d))
    h0_ctx = jnp.zeros((batch, SSM_PAIRS * SSM_PW), f32)

    zs, st_out = [], []
    for l in range(depth):
        last = l == depth - 1
        z = in_projection(xs, mod, row(norm1_g), w_in_b, l, row_of_tile, tm)

        a_ctx, *caches = context_attention(z, batch, seq, zs, write_caches=last)
        zs.append(z)
        a_lat = latent_attention(z, m_ctx, dec_batch, dec_seq, cache_k, cache_v, l, bias_tables)

        y_ctx, fin = ssm_scan(z, 0, batch, seq, tables, l, h0_ctx)
        y_lat, _ = ssm_scan(z, m_ctx, dec_batch, dec_seq, tables, l, state_to_slabs(state_ssm[:, l]))
        st_out.append(state_from_slabs(fin))
        s_out = ssm_glu(y_ctx, y_lat, z, row(ssm_d), w_glu_b, row(ssm_b_glu), l, tm)

        p_ctx = pool_mixer(z, 0, batch, seq, pool_w, row(pool_scale), l)
        p_lat = pool_mixer(z, m_ctx, dec_batch, dec_seq, pool_w, row(pool_scale), l)

        x = out_projection(a_ctx, a_lat, s_out, p_ctx, p_lat, xs, mod, w_out_b, l, row_of_tile, tm)
        xs = ffn(x, mod, row(norm2_g), ffn_w_in_b, ffn_w_out_b, final_norm_g[None], l, row_of_tile, tm, 512,
                 final_norm=last, split_rows=m_ctx if last else None)

    y_prompt = xs[0].reshape(batch, seq, d)
    y_sample = xs[1].reshape(dec_batch, dec_seq, d)
    return (y_prompt, y_sample, caches[0], caches[1], jnp.stack(st_out, axis=1))
```

```python
import functools

import jax
import jax.numpy as jnp
import numpy as np
from jax import lax
from jax.experimental import pallas as pl
from jax.experimental.pallas import tpu as pltpu

f32 = jnp.float32
bf16 = jnp.bfloat16

D_MODEL = 2048
N_HEADS = 16
HEAD_DIM = 64
ATTN_W = N_HEADS * HEAD_DIM
SSM_W = 512
SSM_GROUP = 16
SSM_GROUPS = 32
SSM_STATE = 64
POOL_W = 512
POOL_WINDOWS = (2, 4, 8, 16)
POOL_GROUP = 128
POOL_PAD = 16
IN_W = 3 * ATTN_W + SSM_W + POOL_W
GRID_W = 64
WIN_ROWS_MAX = 8
WIN_COLS = 16
RMS_EPS = 1e-6
NEG_INF = -1e30

LAT_BLOCK_ROWS = 4
SSM_CHUNK = 16
SSM_PAIRS = SSM_GROUPS // 2
SSM_QUAD = 4
SSM_PW = 2 * SSM_CHUNK * SSM_GROUP

VMEM_LIMIT = 56 * 1024 * 1024


def _cparams(sem):
    return pltpu.CompilerParams(dimension_semantics=sem, vmem_limit_bytes=VMEM_LIMIT)


def _rms(x, g):
    return x * lax.rsqrt(jnp.mean(x * x, axis=-1, keepdims=True) + RMS_EPS) * g


def _two_part(tm, width, tiles_a, tiles_b, col=lambda *_: 0):
    return (pl.BlockSpec((tm, width), lambda i, *r: (jnp.minimum(i, tiles_a - 1), col(*r))),
            pl.BlockSpec((tm, width), lambda i, *r: (jnp.clip(i - tiles_a, 0, tiles_b - 1), col(*r))))


def _row_specs(parts, tm, width, col=lambda *_: 0):
    if len(parts) == 1:
        return (pl.BlockSpec((tm, width), lambda i, *r: (i, col(*r))),)
    return _two_part(tm, width, parts[0].shape[0] // tm, parts[1].shape[0] // tm, col)


def _pick(first, a_ref, b_ref):
    return jnp.where(first, a_ref[...], b_ref[...])


def _rows(refs, tiles_a):
    if len(refs) == 1:
        return refs[0][...]
    return _pick(pl.program_id(0) < tiles_a, *refs)


def _mod_kernel(c_ref, w_ref, b_ref, o_ref):
    c = c_ref[...]
    s = (c * jax.nn.sigmoid(c)).astype(bf16)
    o_ref[0] = jnp.dot(s, w_ref[0].astype(bf16), preferred_element_type=f32) + b_ref[0]


def modulation_all(cond, w_mod, b_mod, tn=1024):
    n_layers, d, n = w_mod.shape
    return pl.pallas_call(
        _mod_kernel,
        grid=(n_layers, n // tn),
        in_specs=[
            pl.BlockSpec((8, d), lambda l, j: (0, 0)),
            pl.BlockSpec((1, d, tn), lambda l, j: (l, 0, j)),
            pl.BlockSpec((1, 1, tn), lambda l, j: (l, 0, j)),
        ],
        out_specs=pl.BlockSpec((1, 8, tn), lambda l, j: (l, 0, j)),
        out_shape=jax.ShapeDtypeStruct((n_layers, 8, n), f32),
        compiler_params=_cparams(("parallel", "parallel")),
        name="modulation",
    )(cond, w_mod, b_mod.reshape(n_layers, 1, n))


def _in_kernel(*refs, n_x, tiles_a, tn):
    x_refs = refs[:n_x]
    mod_ref, g_ref, w_ref, o_ref = refs[n_x:]
    m = mod_ref[0]
    y = _rms(_rows(x_refs, tiles_a), g_ref[...])
    h = (y * (1.0 + m[1:2]) + m[0:1]).astype(bf16)
    for c in range(w_ref.shape[1] // tn):
        cols = slice(c * tn, (c + 1) * tn)
        o_ref[:, cols] = jnp.dot(h, w_ref[:, cols], preferred_element_type=f32)


def in_projection(xs, mod, g, w, layer, row_of_tile, tm, tn=512):
    m_total = sum(x.shape[0] for x in xs)
    d = xs[0].shape[1]
    n = w.shape[2]
    return pl.pallas_call(
        functools.partial(_in_kernel, n_x=len(xs), tiles_a=xs[0].shape[0] // tm, tn=tn),
        grid=(m_total // tm,),
        in_specs=[
            *_row_specs(xs, tm, d),
            pl.BlockSpec((None, 1, 6, d), lambda i: (layer, row_of_tile(i), 0, 0)),
            pl.BlockSpec((None, 1, d), lambda i: (layer, 0, 0)),
            pl.BlockSpec((None, d, n), lambda i: (layer, 0, 0), pipeline_mode=pl.Buffered(1)),
        ],
        out_specs=pl.BlockSpec((tm, n), lambda i: (i, 0)),
        out_shape=jax.ShapeDtypeStruct((m_total, n), f32),
        compiler_params=_cparams(("arbitrary",)),
        name="in_projection",
    )(*xs, mod, g, w)


def _first_head(shape):
    return lax.broadcasted_iota(jnp.int32, shape, len(shape) - 1) < HEAD_DIM


def _attn_ctx_kernel(*refs, n_prev, write_caches):
    q_ref, k_ref, v_ref = refs[:3]
    prev = refs[3:3 + 2 * n_prev]
    outs = refs[3 + 2 * n_prev:]
    o_ref = outs[0]
    scale = HEAD_DIM ** -0.5
    nt = (((1,), (1,)), ((), ()))
    seq = q_ref.shape[0]
    lanes = 2 * HEAD_DIM
    first = _first_head((seq, lanes))
    keeps = (first, jnp.logical_not(first))

    def scores_of(p):
        cols = slice(p * lanes, (p + 1) * lanes)
        q = q_ref[:, cols].astype(bf16)
        k = k_ref[:, cols]
        v = v_ref[:, cols]
        if write_caches:
            kc_ref, vc_ref = outs[1:]
            layers = [(prev[2 * l][:, cols], prev[2 * l + 1][:, cols]) for l in range(n_prev)] + [(k, v)]
            for l, (kl, vl) in enumerate(layers):
                for h in range(2):
                    kc_ref[0, l, 2 * p + h] = kl[:, h * HEAD_DIM:(h + 1) * HEAD_DIM]
                    vc_ref[0, l, 2 * p + h] = vl[:, h * HEAD_DIM:(h + 1) * HEAD_DIM]
        kb = k.astype(bf16)
        zero = jnp.zeros_like(kb)
        return [lax.dot_general(q, jnp.where(keep, kb, zero), nt, preferred_element_type=f32) * scale
                for keep in keeps]

    def finish(p, scores):
        cols = slice(p * lanes, (p + 1) * lanes)
        vb = v_ref[:, cols].astype(bf16)
        zero = jnp.zeros_like(vb)
        num, inv = None, []
        for keep, s in zip(keeps, scores):
            e = jnp.exp(s - jnp.max(s, axis=-1, keepdims=True))
            inv.append(1.0 / jnp.sum(e, axis=-1, keepdims=True))
            part = jnp.dot(e.astype(bf16), jnp.where(keep, vb, zero), preferred_element_type=f32)
            num = part if num is None else num + part
        o_ref[:, cols] = (num * jnp.where(first, inv[0], inv[1])).astype(bf16)

    n_pairs = N_HEADS // 2
    pending = scores_of(0)
    for p in range(n_pairs):
        upcoming = scores_of(p + 1) if p + 1 < n_pairs else None
        finish(p, pending)
        pending = upcoming


def context_attention(z, batch, seq, prev_zs=(), write_caches=False):
    col = lambda blk: pl.BlockSpec((seq, ATTN_W), lambda b: (b, blk))
    out_specs = [pl.BlockSpec((seq, ATTN_W), lambda b: (b, 0))]
    out_shape = [jax.ShapeDtypeStruct((batch * seq, ATTN_W), bf16)]
    if write_caches:
        depth = len(prev_zs) + 1
        cache = pl.BlockSpec((1, depth, N_HEADS, seq, HEAD_DIM), lambda b: (b, 0, 0, 0, 0))
        cache_shape = jax.ShapeDtypeStruct((batch, depth, N_HEADS, seq, HEAD_DIM), f32)
        out_specs += [cache, cache]
        out_shape += [cache_shape, cache_shape]
    else:
        prev_zs = ()
    prev_args = [a for zp in prev_zs for a in (zp, zp)]
    return pl.pallas_call(
        functools.partial(_attn_ctx_kernel, n_prev=len(prev_zs), write_caches=write_caches),
        grid=(batch,),
        in_specs=[col(0), col(1), col(2)] + [col(1), col(2)] * len(prev_zs),
        out_specs=out_specs,
        out_shape=out_shape,
        compiler_params=_cparams(("parallel",)),
        name="context_attention",
    )(z, z, z, *prev_args)


def _attn_lat_kernel(q_ref, k_ref, v_ref, ck_ref, cv_ref, t_ref, o_ref, bias_scr, k_scr, v_scr, ck_scr, cv_scr,
                     *, rows, wr):
    scale = HEAD_DIM ** -0.5
    n_loc = wr * GRID_W
    for h in range(2):
        for d in range(wr):
            for j in range(wr):
                bias_scr[h, d, :, j * GRID_W:(j + 1) * GRID_W] = t_ref[h, j - d + WIN_ROWS_MAX - 1]

    kb = k_ref[...].astype(bf16)
    vb = v_ref[...].astype(bf16)
    first = _first_head(kb.shape)
    zero = jnp.zeros_like(kb)
    pad = jnp.zeros(ck_ref.shape[3:], bf16)
    for h in range(2):
        keep = first if h == 0 else jnp.logical_not(first)
        k_scr[h] = jnp.where(keep, kb, zero)
        v_scr[h] = jnp.where(keep, vb, zero)
        ck, cv = ck_ref[0, 0, h].astype(bf16), cv_ref[0, 0, h].astype(bf16)
        ck_scr[h] = jnp.concatenate([ck, pad] if h == 0 else [pad, ck], axis=-1)
        cv_scr[h] = jnp.concatenate([cv, pad] if h == 0 else [pad, cv], axis=-1)

    nt = (((1,), (1,)), ((), ()))
    br = LAT_BLOCK_ROWS
    nq = br * GRID_W
    first_q = _first_head((nq, 2 * HEAD_DIM))

    def block_rows(blk):
        return slice(blk * nq, (blk + 1) * nq)

    def windows_of(blk):
        out = []
        for i in range(br):
            r = blk * br + i
            rs = min(max(r - wr // 2, 0), rows - wr)
            out.append((slice(rs * GRID_W, rs * GRID_W + n_loc), r - rs))
        return out

    def scores_of(blk):
        q = q_ref[block_rows(blk), :].astype(bf16)
        out = []
        for h in range(2):
            s_ctx = lax.dot_general(q, ck_scr[h], nt, preferred_element_type=f32) * scale
            s_loc = [lax.dot_general(q[i * GRID_W:(i + 1) * GRID_W], k_scr[h, win, :], nt,
                                     preferred_element_type=f32) * scale + bias_scr[h, off]
                     for i, (win, off) in enumerate(windows_of(blk))]
            out.append((jnp.concatenate(s_loc, axis=0), s_ctx))
        return out

    def finish(blk, scores):
        probs, inv = [], []
        for s_loc, s_ctx in scores:
            m = jnp.maximum(jnp.max(s_loc, axis=-1, keepdims=True), jnp.max(s_ctx, axis=-1, keepdims=True))
            e_loc = jnp.exp(s_loc - m)
            e_ctx = jnp.exp(s_ctx - m)
            inv.append(1.0 / (jnp.sum(e_loc, axis=-1, keepdims=True) + jnp.sum(e_ctx, axis=-1, keepdims=True)))
            probs.append((e_loc.astype(bf16), e_ctx.astype(bf16)))
        num = None
        for h, (e_loc, e_ctx) in enumerate(probs):
            o_loc = [jnp.dot(e_loc[i * GRID_W:(i + 1) * GRID_W], v_scr[h, win, :], preferred_element_type=f32)
                     for i, (win, _) in enumerate(windows_of(blk))]
            part = jnp.dot(e_ctx, cv_scr[h], preferred_element_type=f32) + jnp.concatenate(o_loc, axis=0)
            num = part if num is None else num + part
        o_ref[block_rows(blk), :] = (num * jnp.where(first_q, inv[0], inv[1])).astype(bf16)

    pending = scores_of(0)
    for blk in range(rows // br):
        upcoming = scores_of(blk + 1) if blk + 1 < rows // br else None
        finish(blk, pending)
        pending = upcoming


def latent_bias_table(rpb):
    cols = np.arange(GRID_W)
    col_start = np.clip(cols - WIN_COLS // 2, 0, GRID_W - WIN_COLS)
    valid = (cols[None, :] >= col_start[:, None]) & (cols[None, :] < col_start[:, None] + WIN_COLS)
    dc = cols[None, :] - cols[:, None] + WIN_COLS - 1
    assert np.all((dc[valid] >= 0) & (dc[valid] < 2 * WIN_COLS - 1))
    onehot = ((np.arange(2 * WIN_COLS - 1)[:, None, None] == dc[None]) & valid[None]).astype(np.float32)
    t = jnp.einsum('hrd,dqk->hrqk', rpb.astype(f32), jnp.asarray(onehot), precision=lax.Precision.HIGHEST)
    return jnp.where(jnp.asarray(valid)[None, None], t, NEG_INF)


def latent_attention(z, row0, batch, seq, cache_k, cache_v, layer, tables):
    lanes = 2 * HEAD_DIM
    n_pairs = N_HEADS // 2
    rows = seq // GRID_W
    wr = min(WIN_ROWS_MAX, rows)
    past = cache_k.shape[3]
    blk0 = row0 // seq
    n_dr = 2 * WIN_ROWS_MAX - 1
    assert rows % LAT_BLOCK_ROWS == 0
    col = lambda off: pl.BlockSpec((seq, lanes), lambda b, p: (blk0 + b, off + p))
    ctx = pl.BlockSpec((1, 1, 2, past, HEAD_DIM), lambda b, p: (b, layer, p, 0, 0))
    return pl.pallas_call(
        functools.partial(_attn_lat_kernel, rows=rows, wr=wr),
        grid=(batch, n_pairs),
        in_specs=[col(0), col(n_pairs), col(2 * n_pairs), ctx, ctx,
                  pl.BlockSpec((None, 2, n_dr, GRID_W, GRID_W), lambda b, p: (layer, p, 0, 0, 0))],
        out_specs=pl.BlockSpec((seq, lanes), lambda b, p: (b, p)),
        out_shape=jax.ShapeDtypeStruct((batch * seq, ATTN_W), bf16),
        scratch_shapes=[pltpu.VMEM((2, wr, GRID_W, wr * GRID_W), f32),
                        pltpu.VMEM((2, seq, lanes), bf16), pltpu.VMEM((2, seq, lanes), bf16),
                        pltpu.VMEM((2, past, lanes), bf16), pltpu.VMEM((2, past, lanes), bf16)],
        compiler_params=_cparams(("parallel", "parallel")),
        name="latent_attention",
    )(z, z, z, cache_k, cache_v, tables)


def ssm_tables(a_re, a_im, log_dt, b_re, b_im, c_re, c_im):
    lc, p, m = SSM_CHUNK, SSM_STATE, SSM_GROUP
    eye2 = jnp.eye(2, dtype=f32)
    half = SSM_PW // 2

    def slab_row(t):
        t = jnp.broadcast_to(t.astype(f32).reshape(2, 1, SSM_PAIRS, 2, p), (2, 2, SSM_PAIRS, 2, p))
        return t.transpose(2, 0, 1, 3, 4).reshape(SSM_PAIRS, SSM_PW)

    def embed(first, second):
        t = jnp.stack([first, second], axis=1).astype(f32)
        x = t.shape[-1]
        t = t.reshape(2, 2, SSM_PAIRS, 2, p, x).transpose(2, 3, 5, 0, 1, 4)
        t = t[:, :, :, :, :, None, :] * eye2[None, :, None, None, None, :, None]
        return t.reshape(SSM_PAIRS, 2 * x, SSM_PW)

    lr, li = slab_row(a_re), slab_row(a_im)
    dt = slab_row(jnp.broadcast_to(jnp.exp(log_dt.astype(f32))[..., None], a_re.shape))
    sr, si = lr * dt, li * dt
    k = jnp.arange(lc + 1, dtype=f32)[:, None, None]
    mag = jnp.exp(k * sr[None])
    ar, ai = mag * jnp.cos(k * si[None]), mag * jnp.sin(k * si[None])
    den = lr * lr + li * li
    qr = ((ar[1] - 1.0) * lr + ai[1] * li) / den
    qi = (ai[1] * lr - (ar[1] - 1.0) * li) / den
    b4, b4s = embed(b_re, b_im), embed(-b_im, b_re)
    bb4 = (qr[:, None] * b4 + qi[:, None] * b4s)[:, None]
    bb4s = (qr[:, None] * b4s - qi[:, None] * b4)[:, None]
    rev = np.arange(lc)[::-1]
    fwd = np.arange(lc)

    def powers(t, ks):
        return jnp.stack([t[int(k)] for k in ks], axis=0)

    def per_step(t, k_fwd, k_bwd):
        t = jnp.concatenate([powers(t, k_fwd)[..., :half], powers(t, k_bwd)[..., half:]], axis=-1)
        return t.transpose(1, 0, 2)[:, :, None, :]

    bx = bb4 * per_step(ar, rev, fwd) + bb4s * per_step(ai, rev, fwd)
    bx = bx.reshape(SSM_PAIRS, SSM_PW, SSM_PW)
    ct_re, ct_im = c_re.transpose(0, 1, 3, 2), c_im.transpose(0, 1, 3, 2)
    c4 = embed(ct_re, -ct_im)[:, None]
    c4s = embed(-ct_im, -ct_re)[:, None]
    cyt = c4 * per_step(ar, fwd + 1, lc - fwd) + c4s * per_step(ai, fwd + 1, lc - fwd)
    cyt = cyt.reshape(SSM_PAIRS, SSM_PW, SSM_PW)

    j = np.arange(2 * lc - 1)
    live = jnp.asarray(np.concatenate([np.broadcast_to((j >= lc - 1)[:, None], (2 * lc - 1, half)),
                                       np.broadcast_to((j <= lc - 1)[:, None], (2 * lc - 1, half))], axis=-1), f32)
    kf, kb = np.maximum(j - (lc - 1), 0), np.maximum(lc - 1 - j, 0)
    kt = (c4 * per_step(ar, kf, kb) + c4s * per_step(ai, kf, kb)) * live[None, :, None, :]
    kt = kt.reshape(SSM_PAIRS, (2 * lc - 1) * 2 * m, SSM_PW)
    strip = lax.dot_general(bb4[:, 0], kt, (((2,), (2,)), ((0,), (0,))),
                            precision=lax.Precision.HIGH)
    tz = jnp.stack([strip[:, :, (lc - 1 - s) * 2 * m:(lc - 1 - s) * 2 * m + SSM_PW] for s in range(lc)], axis=1)
    tz = tz.reshape(SSM_PAIRS, SSM_PW, SSM_PW)

    sw = 2 * p
    a16p = jnp.concatenate([ar[lc][:, :sw], ai[lc][:, sw:half], ar[lc][:, half:half + sw], ai[lc][:, half + sw:]],
                           axis=-1)[:, None, :]
    return tz.astype(bf16), bx.astype(bf16), cyt.astype(bf16), a16p


def _ssm_kernel(u_ref, tz_ref, bx_ref, cyt_ref, a_ref, h0_ref, y_ref, fin_ref, x_scr, s_scr, y_scr, *, n_seq, n_chunks):
    r = n_seq * n_chunks
    pc = 2 * SSM_GROUP
    sw = 2 * SSM_STATE
    xs = [u_ref[pl.ds(s, r, stride=SSM_CHUNK), :] for s in range(SSM_CHUNK)]
    for kq in range(SSM_QUAD):
        u = jnp.concatenate([x[:, kq * pc:(kq + 1) * pc] for x in xs], axis=-1).astype(bf16)
        x = jnp.dot(u, bx_ref[kq], preferred_element_type=f32)
        for i in range(4):
            x_scr[4 * kq + i] = x[:, i * sw:(i + 1) * sw]
        y_scr[:, kq * SSM_PW:(kq + 1) * SSM_PW] = jnp.dot(u, tz_ref[kq], preferred_element_type=f32)

    def scan(direction):
        offs = [kq * SSM_PW + 2 * direction * sw for kq in range(SSM_QUAD)]
        slabs = [4 * kq + 2 * direction for kq in range(SSM_QUAD)]
        coef = [(a_ref[kq][:, 2 * direction * sw:2 * direction * sw + sw],
                 a_ref[kq][:, 2 * direction * sw + sw:2 * direction * sw + 2 * sw]) for kq in range(SSM_QUAD)]

        def body(i, carry):
            c = i if direction == 0 else n_chunks - 1 - i
            rows = pl.ds(c, n_seq, stride=n_chunks)
            out = []
            for kq in range(SSM_QUAD):
                sr, si = carry[2 * kq], carry[2 * kq + 1]
                ar, ai = coef[kq]
                re, im = slabs[kq], slabs[kq] + 1
                s_scr[re, rows, :] = sr
                s_scr[im, rows, :] = si
                xr = x_scr[re, rows, :]
                xi = x_scr[im, rows, :]
                out += [ar * sr - ai * si + xr, ar * si + ai * sr + xi]
            return tuple(out)

        init = []
        for lo in offs:
            init += [h0_ref[:, lo:lo + sw], h0_ref[:, lo + sw:lo + 2 * sw]]
        fin = lax.fori_loop(0, n_chunks, body, tuple(init))
        for kq, lo in enumerate(offs):
            fin_ref[:, lo:lo + sw] = fin[2 * kq]
            fin_ref[:, lo + sw:lo + 2 * sw] = fin[2 * kq + 1]

    scan(0)
    scan(1)
    ys = []
    for kq in range(SSM_QUAD):
        cols = slice(kq * SSM_PW, (kq + 1) * SSM_PW)
        s_in = jnp.concatenate([s_scr[4 * kq + i] for i in range(4)], axis=-1).astype(bf16)
        ys.append(y_scr[:, cols] + lax.dot_general(s_in, cyt_ref[kq], (((1,), (1,)), ((), ())),
                                                   preferred_element_type=f32))
    for t in range(SSM_CHUNK):
        y_ref[pl.ds(t, r, stride=SSM_CHUNK), :] = jnp.concatenate([y[:, t * pc:(t + 1) * pc] for y in ys], axis=-1)


def ssm_scan(z, row0, n_seq, seq, tables, layer, h0):
    tz, bx, cyt, a16p = tables
    n_chunks = seq // SSM_CHUNK
    rows = n_seq * seq
    r = n_seq * n_chunks
    lanes = SSM_QUAD * 2 * SSM_GROUP
    u_blk = (3 * ATTN_W) // lanes
    qw = SSM_QUAD * SSM_PW
    table = pl.BlockSpec((None, SSM_QUAD, SSM_PW, SSM_PW), lambda q: (layer, q, 0, 0))
    return pl.pallas_call(
        functools.partial(_ssm_kernel, n_seq=n_seq, n_chunks=n_chunks),
        grid=(SSM_PAIRS // SSM_QUAD,),
        in_specs=[pl.BlockSpec((rows, lanes), lambda q: (row0 // rows, u_blk + q)),
                  table, table, table,
                  pl.BlockSpec((None, SSM_QUAD, 1, SSM_PW), lambda q: (layer, q, 0, 0)),
                  pl.BlockSpec((n_seq, qw), lambda q: (0, q))],
        out_specs=[pl.BlockSpec((rows, lanes), lambda q: (0, q)),
                   pl.BlockSpec((n_seq, qw), lambda q: (0, q))],
        out_shape=[jax.ShapeDtypeStruct((rows, SSM_W), f32),
                   jax.ShapeDtypeStruct((n_seq, SSM_PAIRS * SSM_PW), f32)],
        scratch_shapes=[pltpu.VMEM((4 * SSM_QUAD, r, 2 * SSM_STATE), f32),
                        pltpu.VMEM((4 * SSM_QUAD, r, 2 * SSM_STATE), f32), pltpu.VMEM((r, qw), f32)],
        compiler_params=_cparams(("parallel",)),
        name="ssm_scan",
    )(z, tz, bx, cyt, a16p, h0)


def state_to_slabs(st):
    batch = st.shape[0]
    t = st.reshape(batch, 2, SSM_PAIRS, 2, SSM_STATE, 2).transpose(0, 2, 1, 5, 3, 4)
    return t.reshape(batch, SSM_PAIRS * SSM_PW)


def state_from_slabs(fin):
    batch = fin.shape[0]
    t = fin.reshape(batch, SSM_PAIRS, 2, 2, 2, SSM_STATE)
    return t.transpose(0, 2, 1, 4, 5, 3).reshape(batch, 2, SSM_GROUPS, SSM_STATE, 2)


def _glu_kernel(ya_ref, yb_ref, u_ref, d_ref, w_ref, b_ref, o_ref, *, tiles_a):
    y = _pick(pl.program_id(0) < tiles_a, ya_ref, yb_ref) + d_ref[...] * u_ref[...]
    z = jnp.dot(jax.nn.gelu(y).astype(bf16), w_ref[...], preferred_element_type=f32) + b_ref[...]
    o_ref[...] = (z[:, :SSM_W] * jax.nn.sigmoid(z[:, SSM_W:])).astype(bf16)


def ssm_glu(y_a, y_b, z, d_skip, w_glu, b_glu, layer, tm):
    tiles_a, tiles_b = y_a.shape[0] // tm, y_b.shape[0] // tm
    m_total = y_a.shape[0] + y_b.shape[0]
    u_blk = (3 * ATTN_W) // SSM_W
    return pl.pallas_call(
        functools.partial(_glu_kernel, tiles_a=tiles_a),
        grid=(tiles_a + tiles_b,),
        in_specs=[
            *_two_part(tm, SSM_W, tiles_a, tiles_b),
            pl.BlockSpec((tm, SSM_W), lambda i: (i, u_blk)),
            pl.BlockSpec((None, 1, SSM_W), lambda i: (layer, 0, 0)),
            pl.BlockSpec((None, SSM_W, 2 * SSM_W), lambda i: (layer, 0, 0)),
            pl.BlockSpec((None, 1, 2 * SSM_W), lambda i: (layer, 0, 0)),
        ],
        out_specs=pl.BlockSpec((tm, SSM_W), lambda i: (i, 0)),
        out_shape=jax.ShapeDtypeStruct((m_total, SSM_W), bf16),
        compiler_params=_cparams(("arbitrary",)),
        name="ssm_glu",
    )(y_a, y_b, z, d_skip, w_glu, b_glu)


def _pool_kernel(p_ref, w_ref, sc_ref, o_ref, pad_ref, *, seq):
    zeros = jnp.zeros((POOL_PAD, POOL_W), f32)
    pad_ref[0:POOL_PAD, :] = zeros
    pad_ref[POOL_PAD + seq:, :] = zeros
    pad_ref[POOL_PAD:POOL_PAD + seq, :] = p_ref[...]
    t = lax.broadcasted_iota(jnp.int32, (seq, 1), 0)
    for g, win in enumerate(POOL_WINDOWS):
        cols = slice(g * POOL_GROUP, (g + 1) * POOL_GROUP)
        total = jnp.zeros((seq, POOL_GROUP), f32)
        for d in range(-(win // 2), win - win // 2):
            total = total + pad_ref[POOL_PAD + d:POOL_PAD + d + seq, cols]
        lo = jnp.clip(t - win // 2, 0, seq)
        hi = jnp.clip(t - win // 2 + win, 0, seq)
        mixed = total / (hi - lo).astype(f32) - p_ref[:, cols]
        out = jnp.dot(mixed.astype(bf16), w_ref[g].astype(bf16), preferred_element_type=f32)
        o_ref[:, cols] = (out * sc_ref[:, cols]).astype(bf16)


def pool_mixer(z, row0, batch, seq, w_pool, pool_scale, layer):
    p_blk = (3 * ATTN_W + SSM_W) // POOL_W
    blk0 = row0 // seq
    return pl.pallas_call(
        functools.partial(_pool_kernel, seq=seq),
        grid=(batch,),
        in_specs=[
            pl.BlockSpec((seq, POOL_W), lambda b: (blk0 + b, p_blk)),
            pl.BlockSpec((None, len(POOL_WINDOWS), POOL_GROUP, POOL_GROUP), lambda b: (layer, 0, 0, 0)),
            pl.BlockSpec((None, 1, POOL_W), lambda b: (layer, 0, 0)),
        ],
        out_specs=pl.BlockSpec((seq, POOL_W), lambda b: (b, 0)),
        out_shape=jax.ShapeDtypeStruct((batch * seq, POOL_W), bf16),
        scratch_shapes=[pltpu.VMEM((seq + 2 * POOL_PAD, POOL_W), f32)],
        compiler_params=_cparams(("parallel",)),
        name="pool_mixer",
    )(z, w_pool, pool_scale)


def _out_kernel(aa_ref, ab_ref, s_ref, pa_ref, pb_ref, *refs, n_x, tiles_a, tn):
    x_refs = refs[:n_x]
    mod_ref, w_ref, o_ref = refs[n_x:]
    first = pl.program_id(0) < tiles_a
    a = _pick(first, aa_ref, ab_ref)
    s = s_ref[...]
    p = _pick(first, pa_ref, pb_ref)
    x = _rows(x_refs, tiles_a)
    gate = mod_ref[0][2:3]
    for c in range(w_ref.shape[1] // tn):
        cols = slice(c * tn, (c + 1) * tn)
        acc = jnp.dot(a, w_ref[0:ATTN_W, cols], preferred_element_type=f32)
        acc = acc + jnp.dot(s, w_ref[ATTN_W:ATTN_W + SSM_W, cols], preferred_element_type=f32)
        acc = acc + jnp.dot(p, w_ref[ATTN_W + SSM_W:, cols], preferred_element_type=f32)
        o_ref[:, cols] = x[:, cols] + gate[:, cols] * acc


def out_projection(a_a, a_b, s, p_a, p_b, xs, mod, w, layer, row_of_tile, tm, tn=512):
    m_total = sum(x.shape[0] for x in xs)
    d = xs[0].shape[1]
    tiles_a, tiles_b = a_a.shape[0] // tm, a_b.shape[0] // tm
    assert len(xs) == 1 or xs[0].shape[0] == a_a.shape[0]
    return pl.pallas_call(
        functools.partial(_out_kernel, n_x=len(xs), tiles_a=tiles_a, tn=tn),
        grid=(m_total // tm,),
        in_specs=[
            *_two_part(tm, ATTN_W, tiles_a, tiles_b),
            pl.BlockSpec((tm, SSM_W), lambda i: (i, 0)),
            *_two_part(tm, POOL_W, tiles_a, tiles_b),
            *_row_specs(xs, tm, d),
            pl.BlockSpec((None, 1, 6, d), lambda i: (layer, row_of_tile(i), 0, 0)),
            pl.BlockSpec((None, d, d), lambda i: (layer, 0, 0), pipeline_mode=pl.Buffered(1)),
        ],
        out_specs=pl.BlockSpec((tm, d), lambda i: (i, 0)),
        out_shape=jax.ShapeDtypeStruct((m_total, d), f32),
        compiler_params=_cparams(("arbitrary",)),
        name="out_projection",
    )(a_a, a_b, s, p_a, p_b, *xs, mod, w)


def _ffn_kernel(x_ref, mod_ref, g_ref, *refs, n_out, tiles_a, final_norm, n_chunks, per_step):
    w_refs, (fg_ref, *rest) = refs[:3 * per_step], refs[3 * per_step:]
    o_refs = rest[:n_out]
    h_ref, acc_ref = rest[n_out:]
    i, j = pl.program_id(0), pl.program_id(1)
    last = j == pl.num_programs(1) - 1

    @pl.when(j == 0)
    def _():
        m = mod_ref[0]
        x = x_ref[...]
        h_ref[...] = (_rms(x, g_ref[...]) * (1.0 + m[4:5]) + m[3:4]).astype(bf16)
        acc_ref[...] = x

    def chunk(c):
        wg_ref, wu_ref, wo_ref = w_refs[3 * c:3 * c + 3]
        h = h_ref[...]
        gate = jnp.dot(h, wg_ref[...], preferred_element_type=f32)
        up = jnp.dot(h, wu_ref[...], preferred_element_type=f32)
        act = (gate * jax.nn.sigmoid(gate) * up).astype(bf16)
        return jnp.dot(act, wo_ref[...], preferred_element_type=f32)

    def group(n):
        def run():
            total = chunk(0)
            for c in range(1, n):
                total = total + chunk(c)
            acc_ref[...] += mod_ref[0][5:6] * total
        return run

    rem = n_chunks % per_step
    if rem == 0:
        group(per_step)()
    else:
        pl.when(jnp.logical_not(last))(group(per_step))
        pl.when(last)(group(rem))

    def result():
        return _rms(acc_ref[...], fg_ref[...]) if final_norm else acc_ref[...]

    if n_out == 1:
        @pl.when(last)
        def _():
            o_refs[0][...] = result()
    else:
        @pl.when(last & (i < tiles_a))
        def _():
            o_refs[0][...] = result()

        @pl.when(last & (i >= tiles_a))
        def _():
            o_refs[1][...] = result()


def ffn(x, mod, g, w_in, w_out, final_g, layer, row_of_tile, tm, th, final_norm, split_rows=None, per_step=1):
    m_total, d = x.shape
    hidden = w_out.shape[1]
    nh = hidden // th
    if split_rows is None:
        tiles_a = m_total // tm
        out_specs = [pl.BlockSpec((tm, d), lambda i, j: (i, 0))]
        out_shape = [jax.ShapeDtypeStruct((m_total, d), f32)]
    else:
        tiles_a = split_rows // tm
        out_specs = list(_two_part(tm, d, tiles_a, m_total // tm - tiles_a))
        out_shape = [jax.ShapeDtypeStruct((split_rows, d), f32), jax.ShapeDtypeStruct((m_total - split_rows, d), f32)]
    weights, w_specs = [], []
    for c in range(per_step):
        blk = lambda j, c=c: jnp.minimum(per_step * j + c, nh - 1)
        w_specs += [pl.BlockSpec((None, d, th), lambda i, j, blk=blk: (layer, 0, blk(j))),
                    pl.BlockSpec((None, d, th), lambda i, j, blk=blk: (layer, 0, nh + blk(j))),
                    pl.BlockSpec((None, th, d), lambda i, j, blk=blk: (layer, blk(j), 0))]
        weights += [w_in, w_in, w_out]
    return pl.pallas_call(
        functools.partial(_ffn_kernel, n_out=len(out_specs), tiles_a=tiles_a, final_norm=final_norm, n_chunks=nh,
                          per_step=per_step),
        grid=(m_total // tm, -(-nh // per_step)),
        in_specs=[
            pl.BlockSpec((tm, d), lambda i, j: (i, 0)),
            pl.BlockSpec((None, 1, 6, d), lambda i, j: (layer, row_of_tile(i), 0, 0)),
            pl.BlockSpec((None, 1, d), lambda i, j: (layer, 0, 0)),
            *w_specs,
            pl.BlockSpec((1, d), lambda i, j: (0, 0)),
        ],
        out_specs=out_specs,
        out_shape=out_shape,
        scratch_shapes=[pltpu.VMEM((tm, d), bf16), pltpu.VMEM((tm, d), f32)],
        compiler_params=_cparams(("arbitrary", "arbitrary")),
        name="ffn",
    )(x, mod, g, *weights, final_g)


def kernel(x_prompt, x_sample, c, cache_k, cache_v, state_ssm, c_ctx, w_mod, b_mod, norm1_g, norm2_g, w_in,
           attn_rpb, ssm_a_re, ssm_a_im, ssm_log_dt, ssm_b_re, ssm_b_im, ssm_c_re, ssm_c_im, ssm_d, ssm_w_glu,
           ssm_b_glu, pool_w, pool_scale, w_out, ffn_w_in, ffn_w_out, final_norm_g):
    batch, seq, d = x_prompt.shape
    dec_batch, dec_seq, _ = x_sample.shape
    depth = w_in.shape[0]
    m_ctx = batch * seq
    tm = 512
    assert m_ctx % tm == 0 and dec_seq % tm == 0 and dec_batch + 1 <= 8

    def row_of_tile(i):
        return jnp.where(i < m_ctx // tm, 0, 1 + (i - m_ctx // tm) // (dec_seq // tm))

    cond = jnp.concatenate([c_ctx[None, :], c, jnp.zeros((8 - 1 - dec_batch, d), f32)], axis=0)
    mod = modulation_all(cond, w_mod, b_mod)
    mod = mod[:, :1 + dec_batch].reshape(depth, 1 + dec_batch, 6, d)

    w_in_b, w_out_b = w_in.astype(bf16), w_out.astype(bf16)
    ffn_w_in_b, ffn_w_out_b, w_glu_b = ffn_w_in.astype(bf16), ffn_w_out.astype(bf16), ssm_w_glu.astype(bf16)
    row = lambda t: t[:, None, :]
    tables = jax.vmap(ssm_tables)(ssm_a_re, ssm_a_im, ssm_log_dt, ssm_b_re, ssm_b_im, ssm_c_re, ssm_c_im)
    bias_tables = jax.vmap(latent_bias_table)(attn_rpb)

    xs = (x_prompt.reshape(m_ctx, d), x_sample.reshape(dec_batch * dec_seq, d))
    h0_ctx = jnp.zeros((batch, SSM_PAIRS * SSM_PW), f32)

    zs, st_out = [], []
    for l in range(depth):
        last = l == depth - 1
        z = in_projection(xs, mod, row(norm1_g), w_in_b, l, row_of_tile, tm)

        a_ctx, *caches = context_attention(z, batch, seq, zs, write_caches=last)
        zs.append(z)
        a_lat = latent_attention(z, m_ctx, dec_batch, dec_seq, cache_k, cache_v, l, bias_tables)

        y_ctx, fin = ssm_scan(z, 0, batch, seq, tables, l, h0_ctx)
        y_lat, _ = ssm_scan(z, m_ctx, dec_batch, dec_seq, tables, l, state_to_slabs(state_ssm[:, l]))
        st_out.append(state_from_slabs(fin))
        s_out = ssm_glu(y_ctx, y_lat, z, row(ssm_d), w_glu_b, row(ssm_b_glu), l, tm)

        p_ctx = pool_mixer(z, 0, batch, seq, pool_w, row(pool_scale), l)
        p_lat = pool_mixer(z, m_ctx, dec_batch, dec_seq, pool_w, row(pool_scale), l)

        x = out_projection(a_ctx, a_lat, s_out, p_ctx, p_lat, xs, mod, w_out_b, l, row_of_tile, tm)
        xs = ffn(x, mod, row(norm2_g), ffn_w_in_b, ffn_w_out_b, final_norm_g[None], l, row_of_tile, tm, 512,
                 final_norm=last, split_rows=m_ctx if last else None, per_step=1 if last else 2)

    y_prompt = xs[0].reshape(batch, seq, d)
    y_sample = xs[1].reshape(dec_batch, dec_seq, d)
    return (y_prompt, y_sample, caches[0], caches[1], jnp.stack(st_out, axis=1))
```

```python
import functools

import jax
import jax.numpy as jnp
import numpy as np
from jax import lax
from jax.experimental import pallas as pl
from jax.experimental.pallas import tpu as pltpu

f32 = jnp.float32
bf16 = jnp.bfloat16

D_MODEL = 2048
N_HEADS = 16
HEAD_DIM = 64
ATTN_W = N_HEADS * HEAD_DIM
SSM_W = 512
SSM_GROUP = 16
SSM_GROUPS = 32
SSM_STATE = 64
POOL_W = 512
POOL_WINDOWS = (2, 4, 8, 16)
POOL_GROUP = 128
POOL_PAD = 16
IN_W = 3 * ATTN_W + SSM_W + POOL_W
GRID_W = 64
WIN_ROWS_MAX = 8
WIN_COLS = 16
RMS_EPS = 1e-6
NEG_INF = -1e30

LAT_BLOCK_ROWS = 4
SSM_CHUNK = 16
SSM_PAIRS = SSM_GROUPS // 2
SSM_QUAD = 4
SSM_PW = 2 * SSM_CHUNK * SSM_GROUP

VMEM_LIMIT = 56 * 1024 * 1024


def _cparams(sem):
    return pltpu.CompilerParams(dimension_semantics=sem, vmem_limit_bytes=VMEM_LIMIT)


def _rms(x, g):
    return x * lax.rsqrt(jnp.mean(x * x, axis=-1, keepdims=True) + RMS_EPS) * g


def _two_part(tm, width, tiles_a, tiles_b, col=lambda *_: 0):
    return (pl.BlockSpec((tm, width), lambda i, *r: (jnp.minimum(i, tiles_a - 1), col(*r))),
            pl.BlockSpec((tm, width), lambda i, *r: (jnp.clip(i - tiles_a, 0, tiles_b - 1), col(*r))))


def _row_specs(parts, tm, width, col=lambda *_: 0):
    if len(parts) == 1:
        return (pl.BlockSpec((tm, width), lambda i, *r: (i, col(*r))),)
    return _two_part(tm, width, parts[0].shape[0] // tm, parts[1].shape[0] // tm, col)


def _pick(first, a_ref, b_ref):
    return jnp.where(first, a_ref[...], b_ref[...])


def _rows(refs, tiles_a):
    if len(refs) == 1:
        return refs[0][...]
    return _pick(pl.program_id(0) < tiles_a, *refs)


def _mod_kernel(c_ref, w_ref, b_ref, o_ref):
    c = c_ref[...]
    s = (c * jax.nn.sigmoid(c)).astype(bf16)
    o_ref[0] = jnp.dot(s, w_ref[0].astype(bf16), preferred_element_type=f32) + b_ref[0]


def modulation_all(cond, w_mod, b_mod, tn=1024):
    n_layers, d, n = w_mod.shape
    return pl.pallas_call(
        _mod_kernel,
        grid=(n_layers, n // tn),
        in_specs=[
            pl.BlockSpec((8, d), lambda l, j: (0, 0)),
            pl.BlockSpec((1, d, tn), lambda l, j: (l, 0, j)),
            pl.BlockSpec((1, 1, tn), lambda l, j: (l, 0, j)),
        ],
        out_specs=pl.BlockSpec((1, 8, tn), lambda l, j: (l, 0, j)),
        out_shape=jax.ShapeDtypeStruct((n_layers, 8, n), f32),
        compiler_params=_cparams(("parallel", "parallel")),
        name="modulation",
    )(cond, w_mod, b_mod.reshape(n_layers, 1, n))


def _in_kernel(*refs, n_x, tiles_a, tn):
    x_refs = refs[:n_x]
    mod_ref, g_ref, w_ref, o_ref = refs[n_x:]
    m = mod_ref[0]
    y = _rms(_rows(x_refs, tiles_a), g_ref[...])
    h = (y * (1.0 + m[1:2]) + m[0:1]).astype(bf16)
    for c in range(w_ref.shape[1] // tn):
        cols = slice(c * tn, (c + 1) * tn)
        o_ref[:, cols] = jnp.dot(h, w_ref[:, cols], preferred_element_type=f32)


def in_projection(xs, mod, g, w, layer, row_of_tile, tm, tn=512):
    m_total = sum(x.shape[0] for x in xs)
    d = xs[0].shape[1]
    n = w.shape[2]
    return pl.pallas_call(
        functools.partial(_in_kernel, n_x=len(xs), tiles_a=xs[0].shape[0] // tm, tn=tn),
        grid=(m_total // tm,),
        in_specs=[
            *_row_specs(xs, tm, d),
            pl.BlockSpec((None, 1, 6, d), lambda i: (layer, row_of_tile(i), 0, 0)),
            pl.BlockSpec((None, 1, d), lambda i: (layer, 0, 0)),
            pl.BlockSpec((None, d, n), lambda i: (layer, 0, 0), pipeline_mode=pl.Buffered(1)),
        ],
        out_specs=pl.BlockSpec((tm, n), lambda i: (i, 0)),
        out_shape=jax.ShapeDtypeStruct((m_total, n), f32),
        compiler_params=_cparams(("arbitrary",)),
        name="in_projection",
    )(*xs, mod, g, w)


def _first_head(shape):
    return lax.broadcasted_iota(jnp.int32, shape, len(shape) - 1) < HEAD_DIM


def _attn_ctx_kernel(*refs, n_prev, write_caches):
    q_ref, k_ref, v_ref = refs[:3]
    prev = refs[3:3 + 2 * n_prev]
    outs = refs[3 + 2 * n_prev:]
    o_ref = outs[0]
    scale = HEAD_DIM ** -0.5
    nt = (((1,), (1,)), ((), ()))
    seq = q_ref.shape[0]
    lanes = 2 * HEAD_DIM
    first = _first_head((seq, lanes))
    keeps = (first, jnp.logical_not(first))

    def scores_of(p):
        cols = slice(p * lanes, (p + 1) * lanes)
        q = q_ref[:, cols].astype(bf16)
        k = k_ref[:, cols]
        v = v_ref[:, cols]
        if write_caches:
            kc_ref, vc_ref = outs[1:]
            layers = [(prev[2 * l][:, cols], prev[2 * l + 1][:, cols]) for l in range(n_prev)] + [(k, v)]
            for l, (kl, vl) in enumerate(layers):
                for h in range(2):
                    kc_ref[0, l, 2 * p + h] = kl[:, h * HEAD_DIM:(h + 1) * HEAD_DIM]
                    vc_ref[0, l, 2 * p + h] = vl[:, h * HEAD_DIM:(h + 1) * HEAD_DIM]
        kb = k.astype(bf16)
        zero = jnp.zeros_like(kb)
        return [lax.dot_general(q, jnp.where(keep, kb, zero), nt, preferred_element_type=f32) * scale
                for keep in keeps]

    def finish(p, scores):
        cols = slice(p * lanes, (p + 1) * lanes)
        vb = v_ref[:, cols].astype(bf16)
        zero = jnp.zeros_like(vb)
        num, inv = None, []
        for keep, s in zip(keeps, scores):
            e = jnp.exp(s - jnp.max(s, axis=-1, keepdims=True))
            inv.append(1.0 / jnp.sum(e, axis=-1, keepdims=True))
            part = jnp.dot(e.astype(bf16), jnp.where(keep, vb, zero), preferred_element_type=f32)
            num = part if num is None else num + part
        o_ref[:, cols] = (num * jnp.where(first, inv[0], inv[1])).astype(bf16)

    n_pairs = N_HEADS // 2
    pending = scores_of(0)
    for p in range(n_pairs):
        upcoming = scores_of(p + 1) if p + 1 < n_pairs else None
        finish(p, pending)
        pending = upcoming


def context_attention(z, batch, seq, prev_zs=(), write_caches=False):
    col = lambda blk: pl.BlockSpec((seq, ATTN_W), lambda b: (b, blk))
    out_specs = [pl.BlockSpec((seq, ATTN_W), lambda b: (b, 0))]
    out_shape = [jax.ShapeDtypeStruct((batch * seq, ATTN_W), bf16)]
    if write_caches:
        depth = len(prev_zs) + 1
        cache = pl.BlockSpec((1, depth, N_HEADS, seq, HEAD_DIM), lambda b: (b, 0, 0, 0, 0))
        cache_shape = jax.ShapeDtypeStruct((batch, depth, N_HEADS, seq, HEAD_DIM), f32)
        out_specs += [cache, cache]
        out_shape += [cache_shape, cache_shape]
    else:
        prev_zs = ()
    prev_args = [a for zp in prev_zs for a in (zp, zp)]
    return pl.pallas_call(
        functools.partial(_attn_ctx_kernel, n_prev=len(prev_zs), write_caches=write_caches),
        grid=(batch,),
        in_specs=[col(0), col(1), col(2)] + [col(1), col(2)] * len(prev_zs),
        out_specs=out_specs,
        out_shape=out_shape,
        compiler_params=_cparams(("parallel",)),
        name="context_attention",
    )(z, z, z, *prev_args)


def _attn_lat_kernel(q_ref, k_ref, v_ref, ck_ref, cv_ref, t_ref, o_ref, bias_scr, k_scr, v_scr, ck_scr, cv_scr,
                     *, rows, wr):
    scale = HEAD_DIM ** -0.5
    n_loc = wr * GRID_W
    for h in range(2):
        for d in range(wr):
            for j in range(wr):
                bias_scr[h, d, :, j * GRID_W:(j + 1) * GRID_W] = t_ref[h, j - d + WIN_ROWS_MAX - 1]

    kb = k_ref[...].astype(bf16)
    vb = v_ref[...].astype(bf16)
    first = _first_head(kb.shape)
    zero = jnp.zeros_like(kb)
    pad = jnp.zeros(ck_ref.shape[3:], bf16)
    for h in range(2):
        keep = first if h == 0 else jnp.logical_not(first)
        k_scr[h] = jnp.where(keep, kb, zero)
        v_scr[h] = jnp.where(keep, vb, zero)
        ck, cv = ck_ref[0, 0, h].astype(bf16), cv_ref[0, 0, h].astype(bf16)
        ck_scr[h] = jnp.concatenate([ck, pad] if h == 0 else [pad, ck], axis=-1)
        cv_scr[h] = jnp.concatenate([cv, pad] if h == 0 else [pad, cv], axis=-1)

    nt = (((1,), (1,)), ((), ()))
    br = LAT_BLOCK_ROWS
    nq = br * GRID_W
    first_q = _first_head((nq, 2 * HEAD_DIM))

    def block_rows(blk):
        return slice(blk * nq, (blk + 1) * nq)

    def windows_of(blk):
        out = []
        for i in range(br):
            r = blk * br + i
            rs = min(max(r - wr // 2, 0), rows - wr)
            out.append((slice(rs * GRID_W, rs * GRID_W + n_loc), r - rs))
        return out

    def scores_of(blk):
        q = q_ref[block_rows(blk), :].astype(bf16)
        out = []
        for h in range(2):
            s_ctx = lax.dot_general(q, ck_scr[h], nt, preferred_element_type=f32) * scale
            s_loc = [lax.dot_general(q[i * GRID_W:(i + 1) * GRID_W], k_scr[h, win, :], nt,
                                     preferred_element_type=f32) * scale + bias_scr[h, off]
                     for i, (win, off) in enumerate(windows_of(blk))]
            out.append((jnp.concatenate(s_loc, axis=0), s_ctx))
        return out

    def finish(blk, scores):
        probs, inv = [], []
        for s_loc, s_ctx in scores:
            m = jnp.maximum(jnp.max(s_loc, axis=-1, keepdims=True), jnp.max(s_ctx, axis=-1, keepdims=True))
            e_loc = jnp.exp(s_loc - m)
            e_ctx = jnp.exp(s_ctx - m)
            inv.append(1.0 / (jnp.sum(e_loc, axis=-1, keepdims=True) + jnp.sum(e_ctx, axis=-1, keepdims=True)))
            probs.append((e_loc.astype(bf16), e_ctx.astype(bf16)))
        num = None
        for h, (e_loc, e_ctx) in enumerate(probs):
            o_loc = [jnp.dot(e_loc[i * GRID_W:(i + 1) * GRID_W], v_scr[h, win, :], preferred_element_type=f32)
                     for i, (win, _) in enumerate(windows_of(blk))]
            part = jnp.dot(e_ctx, cv_scr[h], preferred_element_type=f32) + jnp.concatenate(o_loc, axis=0)
            num = part if num is None else num + part
        o_ref[block_rows(blk), :] = (num * jnp.where(first_q, inv[0], inv[1])).astype(bf16)

    pending = scores_of(0)
    for blk in range(rows // br):
        upcoming = scores_of(blk + 1) if blk + 1 < rows // br else None
        finish(blk, pending)
        pending = upcoming


def latent_bias_table(rpb):
    cols = np.arange(GRID_W)
    col_start = np.clip(cols - WIN_COLS // 2, 0, GRID_W - WIN_COLS)
    valid = (cols[None, :] >= col_start[:, None]) & (cols[None, :] < col_start[:, None] + WIN_COLS)
    dc = cols[None, :] - cols[:, None] + WIN_COLS - 1
    assert np.all((dc[valid] >= 0) & (dc[valid] < 2 * WIN_COLS - 1))
    onehot = ((np.arange(2 * WIN_COLS - 1)[:, None, None] == dc[None]) & valid[None]).astype(np.float32)
    t = jnp.einsum('hrd,dqk->hrqk', rpb.astype(f32), jnp.asarray(onehot), precision=lax.Precision.HIGHEST)
    return jnp.where(jnp.asarray(valid)[None, None], t, NEG_INF)


def latent_attention(z, row0, batch, seq, cache_k, cache_v, layer, tables):
    lanes = 2 * HEAD_DIM
    n_pairs = N_HEADS // 2
    rows = seq // GRID_W
    wr = min(WIN_ROWS_MAX, rows)
    past = cache_k.shape[3]
    blk0 = row0 // seq
    n_dr = 2 * WIN_ROWS_MAX - 1
    assert rows % LAT_BLOCK_ROWS == 0
    col = lambda off: pl.BlockSpec((seq, lanes), lambda b, p: (blk0 + b, off + p))
    ctx = pl.BlockSpec((1, 1, 2, past, HEAD_DIM), lambda b, p: (b, layer, p, 0, 0))
    return pl.pallas_call(
        functools.partial(_attn_lat_kernel, rows=rows, wr=wr),
        grid=(batch, n_pairs),
        in_specs=[col(0), col(n_pairs), col(2 * n_pairs), ctx, ctx,
                  pl.BlockSpec((None, 2, n_dr, GRID_W, GRID_W), lambda b, p: (layer, p, 0, 0, 0))],
        out_specs=pl.BlockSpec((seq, lanes), lambda b, p: (b, p)),
        out_shape=jax.ShapeDtypeStruct((batch * seq, ATTN_W), bf16),
        scratch_shapes=[pltpu.VMEM((2, wr, GRID_W, wr * GRID_W), f32),
                        pltpu.VMEM((2, seq, lanes), bf16), pltpu.VMEM((2, seq, lanes), bf16),
                        pltpu.VMEM((2, past, lanes), bf16), pltpu.VMEM((2, past, lanes), bf16)],
        compiler_params=_cparams(("parallel", "parallel")),
        name="latent_attention",
    )(z, z, z, cache_k, cache_v, tables)


def ssm_tables(a_re, a_im, log_dt, b_re, b_im, c_re, c_im):
    lc, p, m = SSM_CHUNK, SSM_STATE, SSM_GROUP
    eye2 = jnp.eye(2, dtype=f32)
    half = SSM_PW // 2

    def slab_row(t):
        t = jnp.broadcast_to(t.astype(f32).reshape(2, 1, SSM_PAIRS, 2, p), (2, 2, SSM_PAIRS, 2, p))
        return t.transpose(2, 0, 1, 3, 4).reshape(SSM_PAIRS, SSM_PW)

    def embed(first, second):
        t = jnp.stack([first, second], axis=1).astype(f32)
        x = t.shape[-1]
        t = t.reshape(2, 2, SSM_PAIRS, 2, p, x).transpose(2, 3, 5, 0, 1, 4)
        t = t[:, :, :, :, :, None, :] * eye2[None, :, None, None, None, :, None]
        return t.reshape(SSM_PAIRS, 2 * x, SSM_PW)

    lr, li = slab_row(a_re), slab_row(a_im)
    dt = slab_row(jnp.broadcast_to(jnp.exp(log_dt.astype(f32))[..., None], a_re.shape))
    sr, si = lr * dt, li * dt
    k = jnp.arange(lc + 1, dtype=f32)[:, None, None]
    mag = jnp.exp(k * sr[None])
    ar, ai = mag * jnp.cos(k * si[None]), mag * jnp.sin(k * si[None])
    den = lr * lr + li * li
    qr = ((ar[1] - 1.0) * lr + ai[1] * li) / den
    qi = (ai[1] * lr - (ar[1] - 1.0) * li) / den
    b4, b4s = embed(b_re, b_im), embed(-b_im, b_re)
    bb4 = (qr[:, None] * b4 + qi[:, None] * b4s)[:, None]
    bb4s = (qr[:, None] * b4s - qi[:, None] * b4)[:, None]
    rev = np.arange(lc)[::-1]
    fwd = np.arange(lc)

    def powers(t, ks):
        return jnp.stack([t[int(k)] for k in ks], axis=0)

    def per_step(t, k_fwd, k_bwd):
        t = jnp.concatenate([powers(t, k_fwd)[..., :half], powers(t, k_bwd)[..., half:]], axis=-1)
        return t.transpose(1, 0, 2)[:, :, None, :]

    bx = bb4 * per_step(ar, rev, fwd) + bb4s * per_step(ai, rev, fwd)
    bx = bx.reshape(SSM_PAIRS, SSM_PW, SSM_PW)
    ct_re, ct_im = c_re.transpose(0, 1, 3, 2), c_im.transpose(0, 1, 3, 2)
    c4 = embed(ct_re, -ct_im)[:, None]
    c4s = embed(-ct_im, -ct_re)[:, None]
    cyt = c4 * per_step(ar, fwd + 1, lc - fwd) + c4s * per_step(ai, fwd + 1, lc - fwd)
    cyt = cyt.reshape(SSM_PAIRS, SSM_PW, SSM_PW)

    j = np.arange(2 * lc - 1)
    live = jnp.asarray(np.concatenate([np.broadcast_to((j >= lc - 1)[:, None], (2 * lc - 1, half)),
                                       np.broadcast_to((j <= lc - 1)[:, None], (2 * lc - 1, half))], axis=-1), f32)
    kf, kb = np.maximum(j - (lc - 1), 0), np.maximum(lc - 1 - j, 0)
    kt = (c4 * per_step(ar, kf, kb) + c4s * per_step(ai, kf, kb)) * live[None, :, None, :]
    kt = kt.reshape(SSM_PAIRS, (2 * lc - 1) * 2 * m, SSM_PW)
    strip = lax.dot_general(bb4[:, 0], kt, (((2,), (2,)), ((0,), (0,))),
                            precision=lax.Precision.HIGH)
    tz = jnp.stack([strip[:, :, (lc - 1 - s) * 2 * m:(lc - 1 - s) * 2 * m + SSM_PW] for s in range(lc)], axis=1)
    tz = tz.reshape(SSM_PAIRS, SSM_PW, SSM_PW)

    sw = 2 * p
    a16p = jnp.concatenate([ar[lc][:, :sw], ai[lc][:, sw:half], ar[lc][:, half:half + sw], ai[lc][:, half + sw:]],
                           axis=-1)[:, None, :]
    return tz.astype(bf16), bx.astype(bf16), cyt.astype(bf16), a16p


def _ssm_kernel(u_ref, tz_ref, bx_ref, cyt_ref, a_ref, h0_ref, y_ref, fin_ref, x_scr, s_scr, y_scr, *, n_seq, n_chunks):
    r = n_seq * n_chunks
    pc = 2 * SSM_GROUP
    sw = 2 * SSM_STATE
    xs = [u_ref[pl.ds(s, r, stride=SSM_CHUNK), :] for s in range(SSM_CHUNK)]
    for kq in range(SSM_QUAD):
        u = jnp.concatenate([x[:, kq * pc:(kq + 1) * pc] for x in xs], axis=-1).astype(bf16)
        x = jnp.dot(u, bx_ref[kq], preferred_element_type=f32)
        for i in range(4):
            x_scr[4 * kq + i] = x[:, i * sw:(i + 1) * sw]
        y_scr[:, kq * SSM_PW:(kq + 1) * SSM_PW] = jnp.dot(u, tz_ref[kq], preferred_element_type=f32)

    def scan(direction):
        offs = [kq * SSM_PW + 2 * direction * sw for kq in range(SSM_QUAD)]
        slabs = [4 * kq + 2 * direction for kq in range(SSM_QUAD)]
        coef = [(a_ref[kq][:, 2 * direction * sw:2 * direction * sw + sw],
                 a_ref[kq][:, 2 * direction * sw + sw:2 * direction * sw + 2 * sw]) for kq in range(SSM_QUAD)]

        def body(i, carry):
            c = i if direction == 0 else n_chunks - 1 - i
            rows = pl.ds(c, n_seq, stride=n_chunks)
            out = []
            for kq in range(SSM_QUAD):
                sr, si = carry[2 * kq], carry[2 * kq + 1]
                ar, ai = coef[kq]
                re, im = slabs[kq], slabs[kq] + 1
                s_scr[re, rows, :] = sr
                s_scr[im, rows, :] = si
                xr = x_scr[re, rows, :]
                xi = x_scr[im, rows, :]
                out += [ar * sr - ai * si + xr, ar * si + ai * sr + xi]
            return tuple(out)

        init = []
        for lo in offs:
            init += [h0_ref[:, lo:lo + sw], h0_ref[:, lo + sw:lo + 2 * sw]]
        fin = lax.fori_loop(0, n_chunks, body, tuple(init))
        for kq, lo in enumerate(offs):
            fin_ref[:, lo:lo + sw] = fin[2 * kq]
            fin_ref[:, lo + sw:lo + 2 * sw] = fin[2 * kq + 1]

    scan(0)
    scan(1)
    ys = []
    for kq in range(SSM_QUAD):
        cols = slice(kq * SSM_PW, (kq + 1) * SSM_PW)
        s_in = jnp.concatenate([s_scr[4 * kq + i] for i in range(4)], axis=-1).astype(bf16)
        ys.append(y_scr[:, cols] + lax.dot_general(s_in, cyt_ref[kq], (((1,), (1,)), ((), ())),
                                                   preferred_element_type=f32))
    for t in range(SSM_CHUNK):
        y_ref[pl.ds(t, r, stride=SSM_CHUNK), :] = jnp.concatenate([y[:, t * pc:(t + 1) * pc] for y in ys], axis=-1)


def ssm_scan(z, row0, n_seq, seq, tables, layer, h0):
    tz, bx, cyt, a16p = tables
    n_chunks = seq // SSM_CHUNK
    rows = n_seq * seq
    r = n_seq * n_chunks
    lanes = SSM_QUAD * 2 * SSM_GROUP
    u_blk = (3 * ATTN_W) // lanes
    qw = SSM_QUAD * SSM_PW
    table = pl.BlockSpec((None, SSM_QUAD, SSM_PW, SSM_PW), lambda q: (layer, q, 0, 0))
    return pl.pallas_call(
        functools.partial(_ssm_kernel, n_seq=n_seq, n_chunks=n_chunks),
        grid=(SSM_PAIRS // SSM_QUAD,),
        in_specs=[pl.BlockSpec((rows, lanes), lambda q: (row0 // rows, u_blk + q)),
                  table, table, table,
                  pl.BlockSpec((None, SSM_QUAD, 1, SSM_PW), lambda q: (layer, q, 0, 0)),
                  pl.BlockSpec((n_seq, qw), lambda q: (0, q))],
        out_specs=[pl.BlockSpec((rows, lanes), lambda q: (0, q)),
                   pl.BlockSpec((n_seq, qw), lambda q: (0, q))],
        out_shape=[jax.ShapeDtypeStruct((rows, SSM_W), f32),
                   jax.ShapeDtypeStruct((n_seq, SSM_PAIRS * SSM_PW), f32)],
        scratch_shapes=[pltpu.VMEM((4 * SSM_QUAD, r, 2 * SSM_STATE), f32),
                        pltpu.VMEM((4 * SSM_QUAD, r, 2 * SSM_STATE), f32), pltpu.VMEM((r, qw), f32)],
        compiler_params=_cparams(("parallel",)),
        name="ssm_scan",
    )(z, tz, bx, cyt, a16p, h0)


def state_to_slabs(st):
    batch = st.shape[0]
    t = st.reshape(batch, 2, SSM_PAIRS, 2, SSM_STATE, 2).transpose(0, 2, 1, 5, 3, 4)
    return t.reshape(batch, SSM_PAIRS * SSM_PW)


def state_from_slabs(fin):
    batch = fin.shape[0]
    t = fin.reshape(batch, SSM_PAIRS, 2, 2, 2, SSM_STATE)
    return t.transpose(0, 2, 1, 4, 5, 3).reshape(batch, 2, SSM_GROUPS, SSM_STATE, 2)


def _glu_kernel(ya_ref, yb_ref, u_ref, d_ref, w_ref, b_ref, o_ref, *, tiles_a):
    y = _pick(pl.program_id(0) < tiles_a, ya_ref, yb_ref) + d_ref[...] * u_ref[...]
    z = jnp.dot(jax.nn.gelu(y).astype(bf16), w_ref[...], preferred_element_type=f32) + b_ref[...]
    o_ref[...] = (z[:, :SSM_W] * jax.nn.sigmoid(z[:, SSM_W:])).astype(bf16)


def ssm_glu(y_a, y_b, z, d_skip, w_glu, b_glu, layer, tm):
    tiles_a, tiles_b = y_a.shape[0] // tm, y_b.shape[0] // tm
    m_total = y_a.shape[0] + y_b.shape[0]
    u_blk = (3 * ATTN_W) // SSM_W
    return pl.pallas_call(
        functools.partial(_glu_kernel, tiles_a=tiles_a),
        grid=(tiles_a + tiles_b,),
        in_specs=[
            *_two_part(tm, SSM_W, tiles_a, tiles_b),
            pl.BlockSpec((tm, SSM_W), lambda i: (i, u_blk)),
            pl.BlockSpec((None, 1, SSM_W), lambda i: (layer, 0, 0)),
            pl.BlockSpec((None, SSM_W, 2 * SSM_W), lambda i: (layer, 0, 0)),
            pl.BlockSpec((None, 1, 2 * SSM_W), lambda i: (layer, 0, 0)),
        ],
        out_specs=pl.BlockSpec((tm, SSM_W), lambda i: (i, 0)),
        out_shape=jax.ShapeDtypeStruct((m_total, SSM_W), bf16),
        compiler_params=_cparams(("arbitrary",)),
        name="ssm_glu",
    )(y_a, y_b, z, d_skip, w_glu, b_glu)


def _pool_kernel(p_ref, w_ref, sc_ref, o_ref, pad_ref, *, seq):
    zeros = jnp.zeros((POOL_PAD, POOL_W), f32)
    pad_ref[0:POOL_PAD, :] = zeros
    pad_ref[POOL_PAD + seq:, :] = zeros
    pad_ref[POOL_PAD:POOL_PAD + seq, :] = p_ref[...]
    t = lax.broadcasted_iota(jnp.int32, (seq, 1), 0)
    for g, win in enumerate(POOL_WINDOWS):
        cols = slice(g * POOL_GROUP, (g + 1) * POOL_GROUP)
        total = jnp.zeros((seq, POOL_GROUP), f32)
        for d in range(-(win // 2), win - win // 2):
            total = total + pad_ref[POOL_PAD + d:POOL_PAD + d + seq, cols]
        lo = jnp.clip(t - win // 2, 0, seq)
        hi = jnp.clip(t - win // 2 + win, 0, seq)
        mixed = total / (hi - lo).astype(f32) - p_ref[:, cols]
        out = jnp.dot(mixed.astype(bf16), w_ref[g].astype(bf16), preferred_element_type=f32)
        o_ref[:, cols] = (out * sc_ref[:, cols]).astype(bf16)


def pool_mixer(z, row0, batch, seq, w_pool, pool_scale, layer):
    p_blk = (3 * ATTN_W + SSM_W) // POOL_W
    blk0 = row0 // seq
    return pl.pallas_call(
        functools.partial(_pool_kernel, seq=seq),
        grid=(batch,),
        in_specs=[
            pl.BlockSpec((seq, POOL_W), lambda b: (blk0 + b, p_blk)),
            pl.BlockSpec((None, len(POOL_WINDOWS), POOL_GROUP, POOL_GROUP), lambda b: (layer, 0, 0, 0)),
            pl.BlockSpec((None, 1, POOL_W), lambda b: (layer, 0, 0)),
        ],
        out_specs=pl.BlockSpec((seq, POOL_W), lambda b: (b, 0)),
        out_shape=jax.ShapeDtypeStruct((batch * seq, POOL_W), bf16),
        scratch_shapes=[pltpu.VMEM((seq + 2 * POOL_PAD, POOL_W), f32)],
        compiler_params=_cparams(("parallel",)),
        name="pool_mixer",
    )(z, w_pool, pool_scale)


def _out_kernel(aa_ref, ab_ref, s_ref, pa_ref, pb_ref, *refs, n_x, tiles_a, tn):
    x_refs = refs[:n_x]
    mod_ref, w_ref, o_ref = refs[n_x:]
    first = pl.program_id(0) < tiles_a
    a = _pick(first, aa_ref, ab_ref)
    s = s_ref[...]
    p = _pick(first, pa_ref, pb_ref)
    x = _rows(x_refs, tiles_a)
    gate = mod_ref[0][2:3]
    for c in range(w_ref.shape[1] // tn):
        cols = slice(c * tn, (c + 1) * tn)
        acc = jnp.dot(a, w_ref[0:ATTN_W, cols], preferred_element_type=f32)
        acc = acc + jnp.dot(s, w_ref[ATTN_W:ATTN_W + SSM_W, cols], preferred_element_type=f32)
        acc = acc + jnp.dot(p, w_ref[ATTN_W + SSM_W:, cols], preferred_element_type=f32)
        o_ref[:, cols] = x[:, cols] + gate[:, cols] * acc


def out_projection(a_a, a_b, s, p_a, p_b, xs, mod, w, layer, row_of_tile, tm, tn=512):
    m_total = sum(x.shape[0] for x in xs)
    d = xs[0].shape[1]
    tiles_a, tiles_b = a_a.shape[0] // tm, a_b.shape[0] // tm
    assert len(xs) == 1 or xs[0].shape[0] == a_a.shape[0]
    return pl.pallas_call(
        functools.partial(_out_kernel, n_x=len(xs), tiles_a=tiles_a, tn=tn),
        grid=(m_total // tm,),
        in_specs=[
            *_two_part(tm, ATTN_W, tiles_a, tiles_b),
            pl.BlockSpec((tm, SSM_W), lambda i: (i, 0)),
            *_two_part(tm, POOL_W, tiles_a, tiles_b),
            *_row_specs(xs, tm, d),
            pl.BlockSpec((None, 1, 6, d), lambda i: (layer, row_of_tile(i), 0, 0)),
            pl.BlockSpec((None, d, d), lambda i: (layer, 0, 0), pipeline_mode=pl.Buffered(1)),
        ],
        out_specs=pl.BlockSpec((tm, d), lambda i: (i, 0)),
        out_shape=jax.ShapeDtypeStruct((m_total, d), f32),
        compiler_params=_cparams(("arbitrary",)),
        name="out_projection",
    )(a_a, a_b, s, p_a, p_b, *xs, mod, w)


def _ffn_kernel(x_ref, mod_ref, g_ref, wg_ref, wu_ref, wo_ref, fg_ref, *refs, n_out, tiles_a, final_norm):
    o_refs = refs[:n_out]
    h_ref, acc_ref = refs[n_out:]
    i, j = pl.program_id(0), pl.program_id(1)
    last = j == pl.num_programs(1) - 1

    def chunk():
        h = h_ref[...]
        gate = jnp.dot(h, wg_ref[...], preferred_element_type=f32)
        up = jnp.dot(h, wu_ref[...], preferred_element_type=f32)
        act = (gate * jax.nn.sigmoid(gate) * up).astype(bf16)
        return mod_ref[0][5:6] * jnp.dot(act, wo_ref[...], preferred_element_type=f32)

    @pl.when(j == 0)
    def _():
        m = mod_ref[0]
        x = x_ref[...]
        h_ref[...] = (_rms(x, g_ref[...]) * (1.0 + m[4:5]) + m[3:4]).astype(bf16)
        acc_ref[...] = x + chunk()

    @pl.when(j > 0)
    def _():
        acc_ref[...] += chunk()

    def result():
        return _rms(acc_ref[...], fg_ref[...]) if final_norm else acc_ref[...]

    if n_out == 1:
        @pl.when(last)
        def _():
            o_refs[0][...] = result()
    else:
        @pl.when(last & (i < tiles_a))
        def _():
            o_refs[0][...] = result()

        @pl.when(last & (i >= tiles_a))
        def _():
            o_refs[1][...] = result()


def ffn(x, mod, g, w_in, w_out, final_g, layer, row_of_tile, tm, th, final_norm, split_rows=None):
    m_total, d = x.shape
    hidden = w_out.shape[1]
    nh = hidden // th
    if split_rows is None:
        tiles_a = m_total // tm
        out_specs = [pl.BlockSpec((tm, d), lambda i, j: (i, 0))]
        out_shape = [jax.ShapeDtypeStruct((m_total, d), f32)]
    else:
        tiles_a = split_rows // tm
        out_specs = list(_two_part(tm, d, tiles_a, m_total // tm - tiles_a))
        out_shape = [jax.ShapeDtypeStruct((split_rows, d), f32), jax.ShapeDtypeStruct((m_total - split_rows, d), f32)]
    return pl.pallas_call(
        functools.partial(_ffn_kernel, n_out=len(out_specs), tiles_a=tiles_a, final_norm=final_norm),
        grid=(m_total // tm, nh),
        in_specs=[
            pl.BlockSpec((tm, d), lambda i, j: (i, 0)),
            pl.BlockSpec((None, 1, 6, d), lambda i, j: (layer, row_of_tile(i), 0, 0)),
            pl.BlockSpec((None, 1, d), lambda i, j: (layer, 0, 0)),
            pl.BlockSpec((None, d, th), lambda i, j: (layer, 0, j)),
            pl.BlockSpec((None, d, th), lambda i, j: (layer, 0, nh + j)),
            pl.BlockSpec((None, th, d), lambda i, j: (layer, j, 0)),
            pl.BlockSpec((1, d), lambda i, j: (0, 0)),
        ],
        out_specs=out_specs,
        out_shape=out_shape,
        scratch_shapes=[pltpu.VMEM((tm, d), bf16), pltpu.VMEM((tm, d), f32)],
        compiler_params=_cparams(("arbitrary", "arbitrary")),
        name="ffn",
    )(x, mod, g, w_in, w_in, w_out, final_g)


def kernel(x_prompt, x_sample, c, cache_k, cache_v, state_ssm, c_ctx, w_mod, b_mod, norm1_g, norm2_g, w_in,
           attn_rpb, ssm_a_re, ssm_a_im, ssm_log_dt, ssm_b_re, ssm_b_im, ssm_c_re, ssm_c_im, ssm_d, ssm_w_glu,
           ssm_b_glu, pool_w, pool_scale, w_out, ffn_w_in, ffn_w_out, final_norm_g):
    batch, seq, d = x_prompt.shape
    dec_batch, dec_seq, _ = x_sample.shape
    depth = w_in.shape[0]
    m_ctx = batch * seq
    tm = 512
    assert m_ctx % tm == 0 and dec_seq % tm == 0 and dec_batch + 1 <= 8

    def row_of_tile(i):
        return jnp.where(i < m_ctx // tm, 0, 1 + (i - m_ctx // tm) // (dec_seq // tm))

    cond = jnp.concatenate([c_ctx[None, :], c, jnp.zeros((8 - 1 - dec_batch, d), f32)], axis=0)
    mod = modulation_all(cond, w_mod, b_mod)
    mod = mod[:, :1 + dec_batch].reshape(depth, 1 + dec_batch, 6, d)

    w_in_b, w_out_b = w_in.astype(bf16), w_out.astype(bf16)
    ffn_w_in_b, ffn_w_out_b, w_glu_b = ffn_w_in.astype(bf16), ffn_w_out.astype(bf16), ssm_w_glu.astype(bf16)
    row = lambda t: t[:, None, :]
    tables = jax.vmap(ssm_tables)(ssm_a_re, ssm_a_im, ssm_log_dt, ssm_b_re, ssm_b_im, ssm_c_re, ssm_c_im)
    bias_tables = jax.vmap(latent_bias_table)(attn_rpb)

    xs = (x_prompt.reshape(m_ctx, d), x_sample.reshape(dec_batch * dec_seq, d))
    h0_ctx = jnp.zeros((batch, SSM_PAIRS * SSM_PW), f32)

    zs, st_out = [], []
    for l in range(depth):
        last = l == depth - 1
        z = in_projection(xs, mod, row(norm1_g), w_in_b, l, row_of_tile, tm)

        a_ctx, *caches = context_attention(z, batch, seq, zs, write_caches=last)
        zs.append(z)
        a_lat = latent_attention(z, m_ctx, dec_batch, dec_seq, cache_k, cache_v, l, bias_tables)

        y_ctx, fin = ssm_scan(z, 0, batch, seq, tables, l, h0_ctx)
        y_lat, _ = ssm_scan(z, m_ctx, dec_batch, dec_seq, tables, l, state_to_slabs(state_ssm[:, l]))
        st_out.append(state_from_slabs(fin))
        s_out = ssm_glu(y_ctx, y_lat, z, row(ssm_d), w_glu_b, row(ssm_b_glu), l, tm)

        p_ctx = pool_mixer(z, 0, batch, seq, pool_w, row(pool_scale), l)
        p_lat = pool_mixer(z, m_ctx, dec_batch, dec_seq, pool_w, row(pool_scale), l)

        x = out_projection(a_ctx, a_lat, s_out, p_ctx, p_lat, xs, mod, w_out_b, l, row_of_tile, tm)
        xs = ffn(x, mod, row(norm2_g), ffn_w_in_b, ffn_w_out_b, final_norm_g[None], l, row_of_tile, tm, 512,
                 final_norm=last, split_rows=m_ctx if last else None)

    y_prompt = xs[0].reshape(batch, seq, d)
    y_sample = xs[1].reshape(dec_batch, dec_seq, d)
    return (y_prompt, y_sample, caches[0], caches[1], jnp.stack(st_out, axis=1))
```

```python
import functools

import jax
import jax.numpy as jnp
import numpy as np
from jax import lax
from jax.experimental import pallas as pl
from jax.experimental.pallas import tpu as pltpu

f32 = jnp.float32
bf16 = jnp.bfloat16

D_MODEL = 2048
N_HEADS = 16
HEAD_DIM = 64
ATTN_W = N_HEADS * HEAD_DIM
SSM_W = 512
SSM_GROUP = 16
SSM_GROUPS = 32
SSM_STATE = 64
POOL_W = 512
POOL_WINDOWS = (2, 4, 8, 16)
POOL_GROUP = 128
POOL_PAD = 16
IN_W = 3 * ATTN_W + SSM_W + POOL_W
GRID_W = 64
WIN_ROWS_MAX = 8
WIN_COLS = 16
RMS_EPS = 1e-6
NEG_INF = -1e30

LAT_BLOCK_ROWS = 4
SSM_CHUNK = 16
SSM_PAIRS = SSM_GROUPS // 2
SSM_QUAD = 4
SSM_PW = 2 * SSM_CHUNK * SSM_GROUP

VMEM_LIMIT = 56 * 1024 * 1024


def _cparams(sem):
    return pltpu.CompilerParams(dimension_semantics=sem, vmem_limit_bytes=VMEM_LIMIT)


def _rms(x, g):
    return x * lax.rsqrt(jnp.mean(x * x, axis=-1, keepdims=True) + RMS_EPS) * g


def _two_part(tm, width, tiles_a, tiles_b, col=lambda *_: 0):
    return (pl.BlockSpec((tm, width), lambda i, *r: (jnp.minimum(i, tiles_a - 1), col(*r))),
            pl.BlockSpec((tm, width), lambda i, *r: (jnp.clip(i - tiles_a, 0, tiles_b - 1), col(*r))))


def _row_specs(parts, tm, width, col=lambda *_: 0):
    if len(parts) == 1:
        return (pl.BlockSpec((tm, width), lambda i, *r: (i, col(*r))),)
    return _two_part(tm, width, parts[0].shape[0] // tm, parts[1].shape[0] // tm, col)


def _pick(first, a_ref, b_ref):
    return jnp.where(first, a_ref[...], b_ref[...])


def _rows(refs, tiles_a):
    if len(refs) == 1:
        return refs[0][...]
    return _pick(pl.program_id(0) < tiles_a, *refs)


def _mod_kernel(c_ref, w_ref, b_ref, o_ref):
    c = c_ref[...]
    s = (c * jax.nn.sigmoid(c)).astype(bf16)
    o_ref[0] = jnp.dot(s, w_ref[0].astype(bf16), preferred_element_type=f32) + b_ref[0]


def modulation_all(cond, w_mod, b_mod, tn=1024):
    n_layers, d, n = w_mod.shape
    return pl.pallas_call(
        _mod_kernel,
        grid=(n_layers, n // tn),
        in_specs=[
            pl.BlockSpec((8, d), lambda l, j: (0, 0)),
            pl.BlockSpec((1, d, tn), lambda l, j: (l, 0, j)),
            pl.BlockSpec((1, 1, tn), lambda l, j: (l, 0, j)),
        ],
        out_specs=pl.BlockSpec((1, 8, tn), lambda l, j: (l, 0, j)),
        out_shape=jax.ShapeDtypeStruct((n_layers, 8, n), f32),
        compiler_params=_cparams(("parallel", "parallel")),
        name="modulation",
    )(cond, w_mod, b_mod.reshape(n_layers, 1, n))


def _in_kernel(*refs, n_x, tiles_a, tn):
    x_refs = refs[:n_x]
    mod_ref, g_ref, w_ref, o_ref = refs[n_x:]
    m = mod_ref[0]
    y = _rms(_rows(x_refs, tiles_a), g_ref[...])
    h = (y * (1.0 + m[1:2]) + m[0:1]).astype(bf16)
    for c in range(w_ref.shape[1] // tn):
        cols = slice(c * tn, (c + 1) * tn)
        o_ref[:, cols] = jnp.dot(h, w_ref[:, cols], preferred_element_type=f32)


def in_projection(xs, mod, g, w, layer, row_of_tile, tm, tn=512):
    m_total = sum(x.shape[0] for x in xs)
    d = xs[0].shape[1]
    n = w.shape[2]
    return pl.pallas_call(
        functools.partial(_in_kernel, n_x=len(xs), tiles_a=xs[0].shape[0] // tm, tn=tn),
        grid=(m_total // tm,),
        in_specs=[
            *_row_specs(xs, tm, d),
            pl.BlockSpec((None, 1, 6, d), lambda i: (layer, row_of_tile(i), 0, 0)),
            pl.BlockSpec((None, 1, d), lambda i: (layer, 0, 0)),
            pl.BlockSpec((None, d, n), lambda i: (layer, 0, 0), pipeline_mode=pl.Buffered(1)),
        ],
        out_specs=pl.BlockSpec((tm, n), lambda i: (i, 0)),
        out_shape=jax.ShapeDtypeStruct((m_total, n), f32),
        compiler_params=_cparams(("arbitrary",)),
        name="in_projection",
    )(*xs, mod, g, w)


def _first_head(shape):
    return lax.broadcasted_iota(jnp.int32, shape, len(shape) - 1) < HEAD_DIM


def _attn_ctx_kernel(*refs, n_prev, write_caches):
    q_ref, k_ref, v_ref = refs[:3]
    prev = refs[3:3 + 2 * n_prev]
    outs = refs[3 + 2 * n_prev:]
    o_ref = outs[0]
    scale = HEAD_DIM ** -0.5
    nt = (((1,), (1,)), ((), ()))
    seq = q_ref.shape[0]
    lanes = 2 * HEAD_DIM
    first = _first_head((seq, lanes))
    keeps = (first, jnp.logical_not(first))

    def scores_of(p):
        cols = slice(p * lanes, (p + 1) * lanes)
        q = q_ref[:, cols].astype(bf16)
        k = k_ref[:, cols]
        v = v_ref[:, cols]
        if write_caches:
            kc_ref, vc_ref = outs[1:]
            layers = [(prev[2 * l][:, cols], prev[2 * l + 1][:, cols]) for l in range(n_prev)] + [(k, v)]
            for l, (kl, vl) in enumerate(layers):
                for h in range(2):
                    kc_ref[0, l, 2 * p + h] = kl[:, h * HEAD_DIM:(h + 1) * HEAD_DIM]
                    vc_ref[0, l, 2 * p + h] = vl[:, h * HEAD_DIM:(h + 1) * HEAD_DIM]
        kb = k.astype(bf16)
        zero = jnp.zeros_like(kb)
        return [lax.dot_general(q, jnp.where(keep, kb, zero), nt, preferred_element_type=f32) * scale
                for keep in keeps]

    def finish(p, scores):
        cols = slice(p * lanes, (p + 1) * lanes)
        vb = v_ref[:, cols].astype(bf16)
        zero = jnp.zeros_like(vb)
        num, inv = None, []
        for keep, s in zip(keeps, scores):
            e = jnp.exp(s - jnp.max(s, axis=-1, keepdims=True))
            inv.append(1.0 / jnp.sum(e, axis=-1, keepdims=True))
            part = jnp.dot(e.astype(bf16), jnp.where(keep, vb, zero), preferred_element_type=f32)
            num = part if num is None else num + part
        o_ref[:, cols] = (num * jnp.where(first, inv[0], inv[1])).astype(bf16)

    n_pairs = N_HEADS // 2
    pending = scores_of(0)
    for p in range(n_pairs):
        upcoming = scores_of(p + 1) if p + 1 < n_pairs else None
        finish(p, pending)
        pending = upcoming


def context_attention(z, batch, seq, prev_zs=(), write_caches=False):
    col = lambda blk: pl.BlockSpec((seq, ATTN_W), lambda b: (b, blk))
    out_specs = [pl.BlockSpec((seq, ATTN_W), lambda b: (b, 0))]
    out_shape = [jax.ShapeDtypeStruct((batch * seq, ATTN_W), bf16)]
    if write_caches:
        depth = len(prev_zs) + 1
        cache = pl.BlockSpec((1, depth, N_HEADS, seq, HEAD_DIM), lambda b: (b, 0, 0, 0, 0))
        cache_shape = jax.ShapeDtypeStruct((batch, depth, N_HEADS, seq, HEAD_DIM), f32)
        out_specs += [cache, cache]
        out_shape += [cache_shape, cache_shape]
    else:
        prev_zs = ()
    prev_args = [a for zp in prev_zs for a in (zp, zp)]
    return pl.pallas_call(
        functools.partial(_attn_ctx_kernel, n_prev=len(prev_zs), write_caches=write_caches),
        grid=(batch,),
        in_specs=[col(0), col(1), col(2)] + [col(1), col(2)] * len(prev_zs),
        out_specs=out_specs,
        out_shape=out_shape,
        compiler_params=_cparams(("parallel",)),
        name="context_attention",
    )(z, z, z, *prev_args)


def _attn_lat_kernel(q_ref, k_ref, v_ref, ck_ref, cv_ref, t_ref, o_ref, bias_scr, k_scr, v_scr, ck_scr, cv_scr,
                     *, rows, wr):
    scale = HEAD_DIM ** -0.5
    n_loc = wr * GRID_W
    for h in range(2):
        for d in range(wr):
            for j in range(wr):
                bias_scr[h, d, :, j * GRID_W:(j + 1) * GRID_W] = t_ref[h, j - d + WIN_ROWS_MAX - 1]

    kb = k_ref[...].astype(bf16)
    vb = v_ref[...].astype(bf16)
    first = _first_head(kb.shape)
    zero = jnp.zeros_like(kb)
    pad = jnp.zeros(ck_ref.shape[3:], bf16)
    for h in range(2):
        keep = first if h == 0 else jnp.logical_not(first)
        k_scr[h] = jnp.where(keep, kb, zero)
        v_scr[h] = jnp.where(keep, vb, zero)
        ck, cv = ck_ref[0, 0, h].astype(bf16), cv_ref[0, 0, h].astype(bf16)
        ck_scr[h] = jnp.concatenate([ck, pad] if h == 0 else [pad, ck], axis=-1)
        cv_scr[h] = jnp.concatenate([cv, pad] if h == 0 else [pad, cv], axis=-1)

    nt = (((1,), (1,)), ((), ()))
    br = LAT_BLOCK_ROWS
    nq = br * GRID_W
    first_q = _first_head((nq, 2 * HEAD_DIM))

    def block_rows(blk):
        return slice(blk * nq, (blk + 1) * nq)

    def windows_of(blk):
        out = []
        for i in range(br):
            r = blk * br + i
            rs = min(max(r - wr // 2, 0), rows - wr)
            out.append((slice(rs * GRID_W, rs * GRID_W + n_loc), r - rs))
        return out

    def scores_of(blk):
        q = q_ref[block_rows(blk), :].astype(bf16)
        out = []
        for h in range(2):
            s_ctx = lax.dot_general(q, ck_scr[h], nt, preferred_element_type=f32) * scale
            s_loc = [lax.dot_general(q[i * GRID_W:(i + 1) * GRID_W], k_scr[h, win, :], nt,
                                     preferred_element_type=f32) * scale + bias_scr[h, off]
                     for i, (win, off) in enumerate(windows_of(blk))]
            out.append((jnp.concatenate(s_loc, axis=0), s_ctx))
        return out

    def finish(blk, scores):
        probs, inv = [], []
        for s_loc, s_ctx in scores:
            m = jnp.maximum(jnp.max(s_loc, axis=-1, keepdims=True), jnp.max(s_ctx, axis=-1, keepdims=True))
            e_loc = jnp.exp(s_loc - m)
            e_ctx = jnp.exp(s_ctx - m)
            inv.append(1.0 / (jnp.sum(e_loc, axis=-1, keepdims=True) + jnp.sum(e_ctx, axis=-1, keepdims=True)))
            probs.append((e_loc.astype(bf16), e_ctx.astype(bf16)))
        num = None
        for h, (e_loc, e_ctx) in enumerate(probs):
            o_loc = [jnp.dot(e_loc[i * GRID_W:(i + 1) * GRID_W], v_scr[h, win, :], preferred_element_type=f32)
                     for i, (win, _) in enumerate(windows_of(blk))]
            part = jnp.dot(e_ctx, cv_scr[h], preferred_element_type=f32) + jnp.concatenate(o_loc, axis=0)
            num = part if num is None else num + part
        o_ref[block_rows(blk), :] = (num * jnp.where(first_q, inv[0], inv[1])).astype(bf16)

    pending = scores_of(0)
    for blk in range(rows // br):
        upcoming = scores_of(blk + 1) if blk + 1 < rows // br else None
        finish(blk, pending)
        pending = upcoming


def latent_bias_table(rpb):
    cols = np.arange(GRID_W)
    col_start = np.clip(cols - WIN_COLS // 2, 0, GRID_W - WIN_COLS)
    valid = (cols[None, :] >= col_start[:, None]) & (cols[None, :] < col_start[:, None] + WIN_COLS)
    dc = cols[None, :] - cols[:, None] + WIN_COLS - 1
    assert np.all((dc[valid] >= 0) & (dc[valid] < 2 * WIN_COLS - 1))
    onehot = ((np.arange(2 * WIN_COLS - 1)[:, None, None] == dc[None]) & valid[None]).astype(np.float32)
    t = jnp.einsum('hrd,dqk->hrqk', rpb.astype(f32), jnp.asarray(onehot), precision=lax.Precision.HIGHEST)
    return jnp.where(jnp.asarray(valid)[None, None], t, NEG_INF)


def latent_attention(z, row0, batch, seq, cache_k, cache_v, layer, tables):
    lanes = 2 * HEAD_DIM
    n_pairs = N_HEADS // 2
    rows = seq // GRID_W
    wr = min(WIN_ROWS_MAX, rows)
    past = cache_k.shape[3]
    blk0 = row0 // seq
    n_dr = 2 * WIN_ROWS_MAX - 1
    assert rows % LAT_BLOCK_ROWS == 0
    col = lambda off: pl.BlockSpec((seq, lanes), lambda b, p: (blk0 + b, off + p))
    ctx = pl.BlockSpec((1, 1, 2, past, HEAD_DIM), lambda b, p: (b, layer, p, 0, 0))
    return pl.pallas_call(
        functools.partial(_attn_lat_kernel, rows=rows, wr=wr),
        grid=(batch, n_pairs),
        in_specs=[col(0), col(n_pairs), col(2 * n_pairs), ctx, ctx,
                  pl.BlockSpec((None, 2, n_dr, GRID_W, GRID_W), lambda b, p: (layer, p, 0, 0, 0))],
        out_specs=pl.BlockSpec((seq, lanes), lambda b, p: (b, p)),
        out_shape=jax.ShapeDtypeStruct((batch * seq, ATTN_W), bf16),
        scratch_shapes=[pltpu.VMEM((2, wr, GRID_W, wr * GRID_W), f32),
                        pltpu.VMEM((2, seq, lanes), bf16), pltpu.VMEM((2, seq, lanes), bf16),
                        pltpu.VMEM((2, past, lanes), bf16), pltpu.VMEM((2, past, lanes), bf16)],
        compiler_params=_cparams(("parallel", "parallel")),
        name="latent_attention",
    )(z, z, z, cache_k, cache_v, tables)


def ssm_tables(a_re, a_im, log_dt, b_re, b_im, c_re, c_im):
    lc, p, m = SSM_CHUNK, SSM_STATE, SSM_GROUP
    half = SSM_PW // 2

    def slab_row(t):
        t = jnp.broadcast_to(t.astype(f32).reshape(2, 1, SSM_PAIRS, 2, p), (2, 2, SSM_PAIRS, 2, p))
        return t.transpose(2, 0, 1, 3, 4).reshape(SSM_PAIRS, SSM_PW)

    def embed(first, second):
        t = jnp.stack([first, second], axis=1).astype(f32)
        x = t.shape[-1]
        t = t.reshape(2, 2, SSM_PAIRS, 2, p, x).transpose(2, 3, 5, 0, 1, 4)
        keep = [(0, 0)] * 4
        t = jnp.stack([jnp.pad(t[:, 0], keep + [(0, p)]), jnp.pad(t[:, 1], keep + [(p, 0)])], axis=1)
        return t.reshape(SSM_PAIRS, 2 * x, SSM_PW)

    lr, li = slab_row(a_re), slab_row(a_im)
    dt = slab_row(jnp.broadcast_to(jnp.exp(log_dt.astype(f32))[..., None], a_re.shape))
    sr, si = lr * dt, li * dt
    k = jnp.arange(lc + 1, dtype=f32)[:, None, None]
    mag = jnp.exp(k * sr[None])
    ar, ai = mag * jnp.cos(k * si[None]), mag * jnp.sin(k * si[None])
    den = lr * lr + li * li
    qr = ((ar[1] - 1.0) * lr + ai[1] * li) / den
    qi = (ai[1] * lr - (ar[1] - 1.0) * li) / den
    b4, b4s = embed(b_re, b_im), embed(-b_im, b_re)
    bb4 = (qr[:, None] * b4 + qi[:, None] * b4s)[:, None]
    bb4s = (qr[:, None] * b4s - qi[:, None] * b4)[:, None]
    rev = np.arange(lc)[::-1]
    fwd = np.arange(lc)

    def powers(t, ks):
        ks = [int(v) for v in ks]
        parts, i = [], 0
        while i < len(ks):
            n = 1
            step = ks[i + 1] - ks[i] if i + 1 < len(ks) else 0
            assert abs(step) <= 1
            while i + n < len(ks) and ks[i + n] - ks[i + n - 1] == step:
                n += 1
            lo, hi = min(ks[i], ks[i + n - 1]), max(ks[i], ks[i + n - 1])
            run = t[lo:hi + 1]
            if step == 0:
                run = jnp.broadcast_to(run, (n,) + t.shape[1:])
            elif step < 0:
                run = run[::-1]
            parts.append(run)
            i += n
        return parts[0] if len(parts) == 1 else jnp.concatenate(parts, axis=0)

    def per_step(t, k_fwd, k_bwd):
        t = jnp.concatenate([powers(t, k_fwd)[..., :half], powers(t, k_bwd)[..., half:]], axis=-1)
        return t.transpose(1, 0, 2)[:, :, None, :]

    bx = bb4 * per_step(ar, rev, fwd) + bb4s * per_step(ai, rev, fwd)
    bx = bx.reshape(SSM_PAIRS, SSM_PW, SSM_PW)
    ct_re, ct_im = c_re.transpose(0, 1, 3, 2), c_im.transpose(0, 1, 3, 2)
    c4 = embed(ct_re, -ct_im)[:, None]
    c4s = embed(-ct_im, -ct_re)[:, None]
    cyt = c4 * per_step(ar, fwd + 1, lc - fwd) + c4s * per_step(ai, fwd + 1, lc - fwd)
    cyt = cyt.reshape(SSM_PAIRS, SSM_PW, SSM_PW)

    j = np.arange(2 * lc - 1)
    live = jnp.asarray(np.concatenate([np.broadcast_to((j >= lc - 1)[:, None], (2 * lc - 1, half)),
                                       np.broadcast_to((j <= lc - 1)[:, None], (2 * lc - 1, half))], axis=-1), f32)
    kf, kb = np.maximum(j - (lc - 1), 0), np.maximum(lc - 1 - j, 0)
    kt = (c4 * per_step(ar, kf, kb) + c4s * per_step(ai, kf, kb)) * live[None, :, None, :]
    kt = kt.reshape(SSM_PAIRS, (2 * lc - 1) * 2 * m, SSM_PW)
    strip = lax.dot_general(bb4[:, 0], kt, (((2,), (2,)), ((0,), (0,))),
                            precision=lax.Precision.HIGH)
    tz = jnp.stack([strip[:, :, (lc - 1 - s) * 2 * m:(lc - 1 - s) * 2 * m + SSM_PW] for s in range(lc)], axis=1)
    tz = tz.reshape(SSM_PAIRS, SSM_PW, SSM_PW)

    sw = 2 * p
    a16p = jnp.concatenate([ar[lc][:, :sw], ai[lc][:, sw:half], ar[lc][:, half:half + sw], ai[lc][:, half + sw:]],
                           axis=-1)[:, None, :]
    return tz.astype(bf16), bx.astype(bf16), cyt.astype(bf16), a16p


def _ssm_kernel(u_ref, tz_ref, bx_ref, cyt_ref, a_ref, h0_ref, y_ref, fin_ref, x_scr, s_scr, y_scr, *, n_seq, n_chunks):
    r = n_seq * n_chunks
    pc = 2 * SSM_GROUP
    sw = 2 * SSM_STATE
    xs = [u_ref[pl.ds(s, r, stride=SSM_CHUNK), :] for s in range(SSM_CHUNK)]
    for kq in range(SSM_QUAD):
        u = jnp.concatenate([x[:, kq * pc:(kq + 1) * pc] for x in xs], axis=-1).astype(bf16)
        x = jnp.dot(u, bx_ref[kq], preferred_element_type=f32)
        for i in range(4):
            x_scr[4 * kq + i] = x[:, i * sw:(i + 1) * sw]
        y_scr[:, kq * SSM_PW:(kq + 1) * SSM_PW] = jnp.dot(u, tz_ref[kq], preferred_element_type=f32)

    def scan(direction):
        offs = [kq * SSM_PW + 2 * direction * sw for kq in range(SSM_QUAD)]
        slabs = [4 * kq + 2 * direction for kq in range(SSM_QUAD)]
        coef = [(a_ref[kq][:, 2 * direction * sw:2 * direction * sw + sw],
                 a_ref[kq][:, 2 * direction * sw + sw:2 * direction * sw + 2 * sw]) for kq in range(SSM_QUAD)]

        def body(i, carry):
            c = i if direction == 0 else n_chunks - 1 - i
            rows = pl.ds(c, n_seq, stride=n_chunks)
            out = []
            for kq in range(SSM_QUAD):
                sr, si = carry[2 * kq], carry[2 * kq + 1]
                ar, ai = coef[kq]
                re, im = slabs[kq], slabs[kq] + 1
                s_scr[re, rows, :] = sr
                s_scr[im, rows, :] = si
                xr = x_scr[re, rows, :]
                xi = x_scr[im, rows, :]
                out += [ar * sr - ai * si + xr, ar * si + ai * sr + xi]
            return tuple(out)

        init = []
        for lo in offs:
            init += [h0_ref[:, lo:lo + sw], h0_ref[:, lo + sw:lo + 2 * sw]]
        fin = lax.fori_loop(0, n_chunks, body, tuple(init))
        for kq, lo in enumerate(offs):
            fin_ref[:, lo:lo + sw] = fin[2 * kq]
            fin_ref[:, lo + sw:lo + 2 * sw] = fin[2 * kq + 1]

    scan(0)
    scan(1)
    ys = []
    for kq in range(SSM_QUAD):
        cols = slice(kq * SSM_PW, (kq + 1) * SSM_PW)
        s_in = jnp.concatenate([s_scr[4 * kq + i] for i in range(4)], axis=-1).astype(bf16)
        ys.append(y_scr[:, cols] + lax.dot_general(s_in, cyt_ref[kq], (((1,), (1,)), ((), ())),
                                                   preferred_element_type=f32))
    for t in range(SSM_CHUNK):
        y_ref[pl.ds(t, r, stride=SSM_CHUNK), :] = jnp.concatenate([y[:, t * pc:(t + 1) * pc] for y in ys], axis=-1)


def ssm_scan(z, row0, n_seq, seq, tables, layer, h0):
    tz, bx, cyt, a16p = tables
    n_chunks = seq // SSM_CHUNK
    rows = n_seq * seq
    r = n_seq * n_chunks
    lanes = SSM_QUAD * 2 * SSM_GROUP
    u_blk = (3 * ATTN_W) // lanes
    qw = SSM_QUAD * SSM_PW
    table = pl.BlockSpec((None, SSM_QUAD, SSM_PW, SSM_PW), lambda q: (layer, q, 0, 0))
    return pl.pallas_call(
        functools.partial(_ssm_kernel, n_seq=n_seq, n_chunks=n_chunks),
        grid=(SSM_PAIRS // SSM_QUAD,),
        in_specs=[pl.BlockSpec((rows, lanes), lambda q: (row0 // rows, u_blk + q)),
                  table, table, table,
                  pl.BlockSpec((None, SSM_QUAD, 1, SSM_PW), lambda q: (layer, q, 0, 0)),
                  pl.BlockSpec((n_seq, qw), lambda q: (0, q))],
        out_specs=[pl.BlockSpec((rows, lanes), lambda q: (0, q)),
                   pl.BlockSpec((n_seq, qw), lambda q: (0, q))],
        out_shape=[jax.ShapeDtypeStruct((rows, SSM_W), f32),
                   jax.ShapeDtypeStruct((n_seq, SSM_PAIRS * SSM_PW), f32)],
        scratch_shapes=[pltpu.VMEM((4 * SSM_QUAD, r, 2 * SSM_STATE), f32),
                        pltpu.VMEM((4 * SSM_QUAD, r, 2 * SSM_STATE), f32), pltpu.VMEM((r, qw), f32)],
        compiler_params=_cparams(("parallel",)),
        name="ssm_scan",
    )(z, tz, bx, cyt, a16p, h0)


def state_to_slabs(st):
    batch = st.shape[0]
    t = st.reshape(batch, 2, SSM_PAIRS, 2, SSM_STATE, 2).transpose(0, 2, 1, 5, 3, 4)
    return t.reshape(batch, SSM_PAIRS * SSM_PW)


def state_from_slabs(fin):
    batch = fin.shape[0]
    t = fin.reshape(batch, SSM_PAIRS, 2, 2, 2, SSM_STATE)
    return t.transpose(0, 2, 1, 4, 5, 3).reshape(batch, 2, SSM_GROUPS, SSM_STATE, 2)


def _glu_kernel(ya_ref, yb_ref, u_ref, d_ref, w_ref, b_ref, o_ref, *, tiles_a):
    y = _pick(pl.program_id(0) < tiles_a, ya_ref, yb_ref) + d_ref[...] * u_ref[...]
    z = jnp.dot(jax.nn.gelu(y).astype(bf16), w_ref[...], preferred_element_type=f32) + b_ref[...]
    o_ref[...] = (z[:, :SSM_W] * jax.nn.sigmoid(z[:, SSM_W:])).astype(bf16)


def ssm_glu(y_a, y_b, z, d_skip, w_glu, b_glu, layer, tm):
    tiles_a, tiles_b = y_a.shape[0] // tm, y_b.shape[0] // tm
    m_total = y_a.shape[0] + y_b.shape[0]
    u_blk = (3 * ATTN_W) // SSM_W
    return pl.pallas_call(
        functools.partial(_glu_kernel, tiles_a=tiles_a),
        grid=(tiles_a + tiles_b,),
        in_specs=[
            *_two_part(tm, SSM_W, tiles_a, tiles_b),
            pl.BlockSpec((tm, SSM_W), lambda i: (i, u_blk)),
            pl.BlockSpec((None, 1, SSM_W), lambda i: (layer, 0, 0)),
            pl.BlockSpec((None, SSM_W, 2 * SSM_W), lambda i: (layer, 0, 0)),
            pl.BlockSpec((None, 1, 2 * SSM_W), lambda i: (layer, 0, 0)),
        ],
        out_specs=pl.BlockSpec((tm, SSM_W), lambda i: (i, 0)),
        out_shape=jax.ShapeDtypeStruct((m_total, SSM_W), bf16),
        compiler_params=_cparams(("arbitrary",)),
        name="ssm_glu",
    )(y_a, y_b, z, d_skip, w_glu, b_glu)


def _pool_kernel(p_ref, w_ref, sc_ref, o_ref, pad_ref, *, seq):
    zeros = jnp.zeros((POOL_PAD, POOL_W), f32)
    pad_ref[0:POOL_PAD, :] = zeros
    pad_ref[POOL_PAD + seq:, :] = zeros
    pad_ref[POOL_PAD:POOL_PAD + seq, :] = p_ref[...]
    t = lax.broadcasted_iota(jnp.int32, (seq, 1), 0)
    for g, win in enumerate(POOL_WINDOWS):
        cols = slice(g * POOL_GROUP, (g + 1) * POOL_GROUP)
        total = jnp.zeros((seq, POOL_GROUP), f32)
        for d in range(-(win // 2), win - win // 2):
            total = total + pad_ref[POOL_PAD + d:POOL_PAD + d + seq, cols]
        lo = jnp.clip(t - win // 2, 0, seq)
        hi = jnp.clip(t - win // 2 + win, 0, seq)
        mixed = total / (hi - lo).astype(f32) - p_ref[:, cols]
        out = jnp.dot(mixed.astype(bf16), w_ref[g].astype(bf16), preferred_element_type=f32)
        o_ref[:, cols] = (out * sc_ref[:, cols]).astype(bf16)


def pool_mixer(z, row0, batch, seq, w_pool, pool_scale, layer):
    p_blk = (3 * ATTN_W + SSM_W) // POOL_W
    blk0 = row0 // seq
    return pl.pallas_call(
        functools.partial(_pool_kernel, seq=seq),
        grid=(batch,),
        in_specs=[
            pl.BlockSpec((seq, POOL_W), lambda b: (blk0 + b, p_blk)),
            pl.BlockSpec((None, len(POOL_WINDOWS), POOL_GROUP, POOL_GROUP), lambda b: (layer, 0, 0, 0)),
            pl.BlockSpec((None, 1, POOL_W), lambda b: (layer, 0, 0)),
        ],
        out_specs=pl.BlockSpec((seq, POOL_W), lambda b: (b, 0)),
        out_shape=jax.ShapeDtypeStruct((batch * seq, POOL_W), bf16),
        scratch_shapes=[pltpu.VMEM((seq + 2 * POOL_PAD, POOL_W), f32)],
        compiler_params=_cparams(("parallel",)),
        name="pool_mixer",
    )(z, w_pool, pool_scale)


def _out_kernel(aa_ref, ab_ref, s_ref, pa_ref, pb_ref, *refs, n_x, tiles_a, tn):
    x_refs = refs[:n_x]
    mod_ref, w_ref, o_ref = refs[n_x:]
    first = pl.program_id(0) < tiles_a
    a = _pick(first, aa_ref, ab_ref)
    s = s_ref[...]
    p = _pick(first, pa_ref, pb_ref)
    x = _rows(x_refs, tiles_a)
    gate = mod_ref[0][2:3]
    for c in range(w_ref.shape[1] // tn):
        cols = slice(c * tn, (c + 1) * tn)
        acc = jnp.dot(a, w_ref[0:ATTN_W, cols], preferred_element_type=f32)
        acc = acc + jnp.dot(s, w_ref[ATTN_W:ATTN_W + SSM_W, cols], preferred_element_type=f32)
        acc = acc + jnp.dot(p, w_ref[ATTN_W + SSM_W:, cols], preferred_element_type=f32)
        o_ref[:, cols] = x[:, cols] + gate[:, cols] * acc


def out_projection(a_a, a_b, s, p_a, p_b, xs, mod, w, layer, row_of_tile, tm, tn=512):
    m_total = sum(x.shape[0] for x in xs)
    d = xs[0].shape[1]
    tiles_a, tiles_b = a_a.shape[0] // tm, a_b.shape[0] // tm
    assert len(xs) == 1 or xs[0].shape[0] == a_a.shape[0]
    return pl.pallas_call(
        functools.partial(_out_kernel, n_x=len(xs), tiles_a=tiles_a, tn=tn),
        grid=(m_total // tm,),
        in_specs=[
            *_two_part(tm, ATTN_W, tiles_a, tiles_b),
            pl.BlockSpec((tm, SSM_W), lambda i: (i, 0)),
            *_two_part(tm, POOL_W, tiles_a, tiles_b),
            *_row_specs(xs, tm, d),
            pl.BlockSpec((None, 1, 6, d), lambda i: (layer, row_of_tile(i), 0, 0)),
            pl.BlockSpec((None, d, d), lambda i: (layer, 0, 0), pipeline_mode=pl.Buffered(1)),
        ],
        out_specs=pl.BlockSpec((tm, d), lambda i: (i, 0)),
        out_shape=jax.ShapeDtypeStruct((m_total, d), f32),
        compiler_params=_cparams(("arbitrary",)),
        name="out_projection",
    )(a_a, a_b, s, p_a, p_b, *xs, mod, w)


def _ffn_kernel(x_ref, mod_ref, g_ref, wg_ref, wu_ref, wo_ref, fg_ref, *refs, n_out, tiles_a, final_norm):
    o_refs = refs[:n_out]
    h_ref, acc_ref = refs[n_out:]
    i, j = pl.program_id(0), pl.program_id(1)
    last = j == pl.num_programs(1) - 1

    def chunk():
        h = h_ref[...]
        gate = jnp.dot(h, wg_ref[...], preferred_element_type=f32)
        up = jnp.dot(h, wu_ref[...], preferred_element_type=f32)
        act = (gate * jax.nn.sigmoid(gate) * up).astype(bf16)
        return mod_ref[0][5:6] * jnp.dot(act, wo_ref[...], preferred_element_type=f32)

    @pl.when(j == 0)
    def _():
        m = mod_ref[0]
        x = x_ref[...]
        h_ref[...] = (_rms(x, g_ref[...]) * (1.0 + m[4:5]) + m[3:4]).astype(bf16)
        acc_ref[...] = x + chunk()

    @pl.when(j > 0)
    def _():
        acc_ref[...] += chunk()

    def result():
        return _rms(acc_ref[...], fg_ref[...]) if final_norm else acc_ref[...]

    if n_out == 1:
        @pl.when(last)
        def _():
            o_refs[0][...] = result()
    else:
        @pl.when(last & (i < tiles_a))
        def _():
            o_refs[0][...] = result()

        @pl.when(last & (i >= tiles_a))
        def _():
            o_refs[1][...] = result()


def ffn(x, mod, g, w_in, w_out, final_g, layer, row_of_tile, tm, th, final_norm, split_rows=None):
    m_total, d = x.shape
    hidden = w_out.shape[1]
    nh = hidden // th
    if split_rows is None:
        tiles_a = m_total // tm
        out_specs = [pl.BlockSpec((tm, d), lambda i, j: (i, 0))]
        out_shape = [jax.ShapeDtypeStruct((m_total, d), f32)]
    else:
        tiles_a = split_rows // tm
        out_specs = list(_two_part(tm, d, tiles_a, m_total // tm - tiles_a))
        out_shape = [jax.ShapeDtypeStruct((split_rows, d), f32), jax.ShapeDtypeStruct((m_total - split_rows, d), f32)]
    return pl.pallas_call(
        functools.partial(_ffn_kernel, n_out=len(out_specs), tiles_a=tiles_a, final_norm=final_norm),
        grid=(m_total // tm, nh),
        in_specs=[
            pl.BlockSpec((tm, d), lambda i, j: (i, 0)),
            pl.BlockSpec((None, 1, 6, d), lambda i, j: (layer, row_of_tile(i), 0, 0)),
            pl.BlockSpec((None, 1, d), lambda i, j: (layer, 0, 0)),
            pl.BlockSpec((None, d, th), lambda i, j: (layer, 0, j)),
            pl.BlockSpec((None, d, th), lambda i, j: (layer, 0, nh + j)),
            pl.BlockSpec((None, th, d), lambda i, j: (layer, j, 0)),
            pl.BlockSpec((1, d), lambda i, j: (0, 0)),
        ],
        out_specs=out_specs,
        out_shape=out_shape,
        scratch_shapes=[pltpu.VMEM((tm, d), bf16), pltpu.VMEM((tm, d), f32)],
        compiler_params=_cparams(("arbitrary", "arbitrary")),
        name="ffn",
    )(x, mod, g, w_in, w_in, w_out, final_g)


def kernel(x_prompt, x_sample, c, cache_k, cache_v, state_ssm, c_ctx, w_mod, b_mod, norm1_g, norm2_g, w_in,
           attn_rpb, ssm_a_re, ssm_a_im, ssm_log_dt, ssm_b_re, ssm_b_im, ssm_c_re, ssm_c_im, ssm_d, ssm_w_glu,
           ssm_b_glu, pool_w, pool_scale, w_out, ffn_w_in, ffn_w_out, final_norm_g):
    batch, seq, d = x_prompt.shape
    dec_batch, dec_seq, _ = x_sample.shape
    depth = w_in.shape[0]
    m_ctx = batch * seq
    tm = 512
    assert m_ctx % tm == 0 and dec_seq % tm == 0 and dec_batch + 1 <= 8

    def row_of_tile(i):
        return jnp.where(i < m_ctx // tm, 0, 1 + (i - m_ctx // tm) // (dec_seq // tm))

    cond = jnp.concatenate([c_ctx[None, :], c, jnp.zeros((8 - 1 - dec_batch, d), f32)], axis=0)
    mod = modulation_all(cond, w_mod, b_mod)
    mod = mod[:, :1 + dec_batch].reshape(depth, 1 + dec_batch, 6, d)

    w_in_b, w_out_b = w_in.astype(bf16), w_out.astype(bf16)
    ffn_w_in_b, ffn_w_out_b, w_glu_b = ffn_w_in.astype(bf16), ffn_w_out.astype(bf16), ssm_w_glu.astype(bf16)
    row = lambda t: t[:, None, :]
    tables = jax.vmap(ssm_tables)(ssm_a_re, ssm_a_im, ssm_log_dt, ssm_b_re, ssm_b_im, ssm_c_re, ssm_c_im)
    bias_tables = jax.vmap(latent_bias_table)(attn_rpb)

    xs = (x_prompt.reshape(m_ctx, d), x_sample.reshape(dec_batch * dec_seq, d))
    h0_ctx = jnp.zeros((batch, SSM_PAIRS * SSM_PW), f32)

    zs, st_out = [], []
    for l in range(depth):
        last = l == depth - 1
        z = in_projection(xs, mod, row(norm1_g), w_in_b, l, row_of_tile, tm)

        a_ctx, *caches = context_attention(z, batch, seq, zs, write_caches=last)
        zs.append(z)
        a_lat = latent_attention(z, m_ctx, dec_batch, dec_seq, cache_k, cache_v, l, bias_tables)

        y_ctx, fin = ssm_scan(z, 0, batch, seq, tables, l, h0_ctx)
        y_lat, _ = ssm_scan(z, m_ctx, dec_batch, dec_seq, tables, l, state_to_slabs(state_ssm[:, l]))
        st_out.append(state_from_slabs(fin))
        s_out = ssm_glu(y_ctx, y_lat, z, row(ssm_d), w_glu_b, row(ssm_b_glu), l, tm)

        p_ctx = pool_mixer(z, 0, batch, seq, pool_w, row(pool_scale), l)
        p_lat = pool_mixer(z, m_ctx, dec_batch, dec_seq, pool_w, row(pool_scale), l)

        x = out_projection(a_ctx, a_lat, s_out, p_ctx, p_lat, xs, mod, w_out_b, l, row_of_tile, tm)
        xs = ffn(x, mod, row(norm2_g), ffn_w_in_b, ffn_w_out_b, final_norm_g[None], l, row_of_tile, tm, 512,
                 final_norm=last, split_rows=m_ctx if last else None)

    y_prompt = xs[0].reshape(batch, seq, d)
    y_sample = xs[1].reshape(dec_batch, dec_seq, d)
    return (y_prompt, y_sample, caches[0], caches[1], jnp.stack(st_out, axis=1))
```

```python
import functools

import jax
import jax.numpy as jnp
import numpy as np
from jax import lax
from jax.experimental import pallas as pl
from jax.experimental.pallas import tpu as pltpu

f32 = jnp.float32
bf16 = jnp.bfloat16

D_MODEL = 2048
N_HEADS = 16
HEAD_DIM = 64
ATTN_W = N_HEADS * HEAD_DIM
SSM_W = 512
SSM_GROUP = 16
SSM_GROUPS = 32
SSM_STATE = 64
POOL_W = 512
POOL_WINDOWS = (2, 4, 8, 16)
POOL_GROUP = 128
POOL_PAD = 16
POOL_TAIL = 8
IN_W = 3 * ATTN_W + SSM_W + POOL_W
GRID_W = 64
WIN_ROWS_MAX = 8
WIN_COLS = 16
RMS_EPS = 1e-6
NEG_INF = -1e30

LAT_BLOCK_ROWS = 4
SSM_CHUNK = 16
SSM_PAIRS = SSM_GROUPS // 2
SSM_QUAD = 4
SSM_PW = 2 * SSM_CHUNK * SSM_GROUP

VMEM_LIMIT = 56 * 1024 * 1024


def _cparams(sem):
    return pltpu.CompilerParams(dimension_semantics=sem, vmem_limit_bytes=VMEM_LIMIT)


def _rms(x, g):
    return x * lax.rsqrt(jnp.mean(x * x, axis=-1, keepdims=True) + RMS_EPS) * g


def _two_part(tm, width, tiles_a, tiles_b, col=lambda *_: 0):
    return (pl.BlockSpec((tm, width), lambda i, *r: (jnp.minimum(i, tiles_a - 1), col(*r))),
            pl.BlockSpec((tm, width), lambda i, *r: (jnp.clip(i - tiles_a, 0, tiles_b - 1), col(*r))))


def _row_specs(parts, tm, width, col=lambda *_: 0):
    if len(parts) == 1:
        return (pl.BlockSpec((tm, width), lambda i, *r: (i, col(*r))),)
    return _two_part(tm, width, parts[0].shape[0] // tm, parts[1].shape[0] // tm, col)


def _pick(first, a_ref, b_ref):
    return jnp.where(first, a_ref[...], b_ref[...])


def _rows(refs, tiles_a):
    if len(refs) == 1:
        return refs[0][...]
    return _pick(pl.program_id(0) < tiles_a, *refs)


def _mod_kernel(c_ref, w_ref, b_ref, o_ref):
    c = c_ref[...]
    s = (c * jax.nn.sigmoid(c)).astype(bf16)
    o_ref[0] = jnp.dot(s, w_ref[0].astype(bf16), preferred_element_type=f32) + b_ref[0]


def modulation_all(cond, w_mod, b_mod, tn=2048):
    n_layers, d, n = w_mod.shape
    return pl.pallas_call(
        _mod_kernel,
        grid=(n_layers, n // tn),
        in_specs=[
            pl.BlockSpec((8, d), lambda l, j: (0, 0)),
            pl.BlockSpec((1, d, tn), lambda l, j: (l, 0, j)),
            pl.BlockSpec((1, 1, tn), lambda l, j: (l, 0, j)),
        ],
        out_specs=pl.BlockSpec((1, 8, tn), lambda l, j: (l, 0, j)),
        out_shape=jax.ShapeDtypeStruct((n_layers, 8, n), f32),
        compiler_params=_cparams(("parallel", "parallel")),
        name="modulation",
    )(cond, w_mod, b_mod.reshape(n_layers, 1, n))


def _in_kernel(*refs, n_x, tiles_a, tn):
    x_refs = refs[:n_x]
    mod_ref, g_ref, w_ref, o_ref = refs[n_x:]
    m = mod_ref[0]
    y = _rms(_rows(x_refs, tiles_a), g_ref[...])
    h = (y * (1.0 + m[1:2]) + m[0:1]).astype(bf16)
    for c in range(w_ref.shape[1] // tn):
        cols = slice(c * tn, (c + 1) * tn)
        o_ref[:, cols] = jnp.dot(h, w_ref[:, cols], preferred_element_type=f32)


def in_projection(xs, mod, g, w, layer, row_of_tile, tm, tn=512):
    m_total = sum(x.shape[0] for x in xs)
    d = xs[0].shape[1]
    n = w.shape[2]
    return pl.pallas_call(
        functools.partial(_in_kernel, n_x=len(xs), tiles_a=xs[0].shape[0] // tm, tn=tn),
        grid=(m_total // tm,),
        in_specs=[
            *_row_specs(xs, tm, d),
            pl.BlockSpec((None, 1, 6, d), lambda i: (layer, row_of_tile(i), 0, 0)),
            pl.BlockSpec((None, 1, d), lambda i: (layer, 0, 0)),
            pl.BlockSpec((None, d, n), lambda i: (layer, 0, 0), pipeline_mode=pl.Buffered(1)),
        ],
        out_specs=pl.BlockSpec((tm, n), lambda i: (i, 0)),
        out_shape=jax.ShapeDtypeStruct((m_total, n), f32),
        compiler_params=_cparams(("arbitrary",)),
        name="in_projection",
    )(*xs, mod, g, w)


def _first_head(shape):
    return lax.broadcasted_iota(jnp.int32, shape, len(shape) - 1) < HEAD_DIM


def _attn_ctx_kernel(*refs, n_prev, write_caches):
    q_ref, k_ref, v_ref = refs[:3]
    prev = refs[3:3 + 2 * n_prev]
    outs = refs[3 + 2 * n_prev:]
    o_ref = outs[0]
    scale = HEAD_DIM ** -0.5
    nt = (((1,), (1,)), ((), ()))
    seq = q_ref.shape[0]
    lanes = 2 * HEAD_DIM
    first = _first_head((seq, lanes))
    keeps = (first, jnp.logical_not(first))

    def scores_of(p):
        cols = slice(p * lanes, (p + 1) * lanes)
        q = q_ref[:, cols].astype(bf16)
        k = k_ref[:, cols]
        v = v_ref[:, cols]
        if write_caches:
            kc_ref, vc_ref = outs[1:]
            layers = [(prev[2 * l][:, cols], prev[2 * l + 1][:, cols]) for l in range(n_prev)] + [(k, v)]
            for l, (kl, vl) in enumerate(layers):
                for h in range(2):
                    kc_ref[0, l, 2 * p + h] = kl[:, h * HEAD_DIM:(h + 1) * HEAD_DIM]
                    vc_ref[0, l, 2 * p + h] = vl[:, h * HEAD_DIM:(h + 1) * HEAD_DIM]
        kb = k.astype(bf16)
        zero = jnp.zeros_like(kb)
        return [lax.dot_general(q, jnp.where(keep, kb, zero), nt, preferred_element_type=f32) * scale
                for keep in keeps]

    def finish(p, scores):
        cols = slice(p * lanes, (p + 1) * lanes)
        vb = v_ref[:, cols].astype(bf16)
        zero = jnp.zeros_like(vb)
        num, inv = None, []
        for keep, s in zip(keeps, scores):
            e = jnp.exp(s - jnp.max(s, axis=-1, keepdims=True))
            inv.append(1.0 / jnp.sum(e, axis=-1, keepdims=True))
            part = jnp.dot(e.astype(bf16), jnp.where(keep, vb, zero), preferred_element_type=f32)
            num = part if num is None else num + part
        o_ref[:, cols] = (num * jnp.where(first, inv[0], inv[1])).astype(bf16)

    n_pairs = N_HEADS // 2
    pending = scores_of(0)
    for p in range(n_pairs):
        upcoming = scores_of(p + 1) if p + 1 < n_pairs else None
        finish(p, pending)
        pending = upcoming


def context_attention(z, batch, seq, prev_zs=(), write_caches=False):
    col = lambda blk: pl.BlockSpec((seq, ATTN_W), lambda b: (b, blk))
    out_specs = [pl.BlockSpec((seq, ATTN_W), lambda b: (b, 0))]
    out_shape = [jax.ShapeDtypeStruct((batch * seq, ATTN_W), bf16)]
    if write_caches:
        depth = len(prev_zs) + 1
        cache = pl.BlockSpec((1, depth, N_HEADS, seq, HEAD_DIM), lambda b: (b, 0, 0, 0, 0))
        cache_shape = jax.ShapeDtypeStruct((batch, depth, N_HEADS, seq, HEAD_DIM), f32)
        out_specs += [cache, cache]
        out_shape += [cache_shape, cache_shape]
    else:
        prev_zs = ()
    prev_args = [a for zp in prev_zs for a in (zp, zp)]
    return pl.pallas_call(
        functools.partial(_attn_ctx_kernel, n_prev=len(prev_zs), write_caches=write_caches),
        grid=(batch,),
        in_specs=[col(0), col(1), col(2)] + [col(1), col(2)] * len(prev_zs),
        out_specs=out_specs,
        out_shape=out_shape,
        compiler_params=_cparams(("parallel",)),
        name="context_attention",
    )(z, z, z, *prev_args)


def _attn_lat_kernel(q_ref, k_ref, v_ref, ck_ref, cv_ref, t_ref, o_ref, bias_scr, k_scr, v_scr, ck_scr, cv_scr,
                     *, rows, wr):
    scale = HEAD_DIM ** -0.5
    n_loc = wr * GRID_W
    for h in range(2):
        for d in range(wr):
            for j in range(wr):
                bias_scr[h, d, :, j * GRID_W:(j + 1) * GRID_W] = t_ref[h, j - d + WIN_ROWS_MAX - 1]

    kb = k_ref[...].astype(bf16)
    vb = v_ref[...].astype(bf16)
    first = _first_head(kb.shape)
    zero = jnp.zeros_like(kb)
    pad = jnp.zeros(ck_ref.shape[3:], bf16)
    for h in range(2):
        keep = first if h == 0 else jnp.logical_not(first)
        k_scr[h] = jnp.where(keep, kb, zero)
        v_scr[h] = jnp.where(keep, vb, zero)
        ck, cv = ck_ref[0, 0, h].astype(bf16), cv_ref[0, 0, h].astype(bf16)
        ck_scr[h] = jnp.concatenate([ck, pad] if h == 0 else [pad, ck], axis=-1)
        cv_scr[h] = jnp.concatenate([cv, pad] if h == 0 else [pad, cv], axis=-1)

    nt = (((1,), (1,)), ((), ()))
    br = LAT_BLOCK_ROWS
    nq = br * GRID_W
    first_q = _first_head((nq, 2 * HEAD_DIM))

    def block_rows(blk):
        return slice(blk * nq, (blk + 1) * nq)

    def windows_of(blk):
        out = []
        for i in range(br):
            r = blk * br + i
            rs = min(max(r - wr // 2, 0), rows - wr)
            out.append((slice(rs * GRID_W, rs * GRID_W + n_loc), r - rs))
        return out

    def scores_of(blk):
        q = q_ref[block_rows(blk), :].astype(bf16)
        out = []
        for h in range(2):
            s_ctx = lax.dot_general(q, ck_scr[h], nt, preferred_element_type=f32) * scale
            s_loc = [lax.dot_general(q[i * GRID_W:(i + 1) * GRID_W], k_scr[h, win, :], nt,
                                     preferred_element_type=f32) * scale + bias_scr[h, off]
                     for i, (win, off) in enumerate(windows_of(blk))]
            out.append((jnp.concatenate(s_loc, axis=0), s_ctx))
        return out

    def finish(blk, scores):
        probs, inv = [], []
        for s_loc, s_ctx in scores:
            m = jnp.maximum(jnp.max(s_loc, axis=-1, keepdims=True), jnp.max(s_ctx, axis=-1, keepdims=True))
            e_loc = jnp.exp(s_loc - m)
            e_ctx = jnp.exp(s_ctx - m)
            inv.append(1.0 / (jnp.sum(e_loc, axis=-1, keepdims=True) + jnp.sum(e_ctx, axis=-1, keepdims=True)))
            probs.append((e_loc.astype(bf16), e_ctx.astype(bf16)))
        num = None
        for h, (e_loc, e_ctx) in enumerate(probs):
            o_loc = [jnp.dot(e_loc[i * GRID_W:(i + 1) * GRID_W], v_scr[h, win, :], preferred_element_type=f32)
                     for i, (win, _) in enumerate(windows_of(blk))]
            part = jnp.dot(e_ctx, cv_scr[h], preferred_element_type=f32) + jnp.concatenate(o_loc, axis=0)
            num = part if num is None else num + part
        o_ref[block_rows(blk), :] = (num * jnp.where(first_q, inv[0], inv[1])).astype(bf16)

    pending = scores_of(0)
    for blk in range(rows // br):
        upcoming = scores_of(blk + 1) if blk + 1 < rows // br else None
        finish(blk, pending)
        pending = upcoming


def latent_bias_table(rpb):
    cols = np.arange(GRID_W)
    col_start = np.clip(cols - WIN_COLS // 2, 0, GRID_W - WIN_COLS)
    valid = (cols[None, :] >= col_start[:, None]) & (cols[None, :] < col_start[:, None] + WIN_COLS)
    dc = cols[None, :] - cols[:, None] + WIN_COLS - 1
    assert np.all((dc[valid] >= 0) & (dc[valid] < 2 * WIN_COLS - 1))
    onehot = ((np.arange(2 * WIN_COLS - 1)[:, None, None] == dc[None]) & valid[None]).astype(np.float32)
    t = jnp.einsum('hrd,dqk->hrqk', rpb.astype(f32), jnp.asarray(onehot), precision=lax.Precision.HIGHEST)
    return jnp.where(jnp.asarray(valid)[None, None], t, NEG_INF)


def latent_attention(z, row0, batch, seq, cache_k, cache_v, layer, tables):
    lanes = 2 * HEAD_DIM
    n_pairs = N_HEADS // 2
    rows = seq // GRID_W
    wr = min(WIN_ROWS_MAX, rows)
    past = cache_k.shape[3]
    blk0 = row0 // seq
    n_dr = 2 * WIN_ROWS_MAX - 1
    assert rows % LAT_BLOCK_ROWS == 0
    col = lambda off: pl.BlockSpec((seq, lanes), lambda b, p: (blk0 + b, off + p))
    ctx = pl.BlockSpec((1, 1, 2, past, HEAD_DIM), lambda b, p: (b, layer, p, 0, 0))
    return pl.pallas_call(
        functools.partial(_attn_lat_kernel, rows=rows, wr=wr),
        grid=(batch, n_pairs),
        in_specs=[col(0), col(n_pairs), col(2 * n_pairs), ctx, ctx,
                  pl.BlockSpec((None, 2, n_dr, GRID_W, GRID_W), lambda b, p: (layer, p, 0, 0, 0))],
        out_specs=pl.BlockSpec((seq, lanes), lambda b, p: (b, p)),
        out_shape=jax.ShapeDtypeStruct((batch * seq, ATTN_W), bf16),
        scratch_shapes=[pltpu.VMEM((2, wr, GRID_W, wr * GRID_W), f32),
                        pltpu.VMEM((2, seq, lanes), bf16), pltpu.VMEM((2, seq, lanes), bf16),
                        pltpu.VMEM((2, past, lanes), bf16), pltpu.VMEM((2, past, lanes), bf16)],
        compiler_params=_cparams(("parallel", "parallel")),
        name="latent_attention",
    )(z, z, z, cache_k, cache_v, tables)


def ssm_tables(a_re, a_im, log_dt, b_re, b_im, c_re, c_im):
    lc, p, m = SSM_CHUNK, SSM_STATE, SSM_GROUP
    half = SSM_PW // 2

    def slab_row(t):
        t = jnp.broadcast_to(t.astype(f32).reshape(2, 1, SSM_PAIRS, 2, p), (2, 2, SSM_PAIRS, 2, p))
        return t.transpose(2, 0, 1, 3, 4).reshape(SSM_PAIRS, SSM_PW)

    def embed(first, second):
        t = jnp.stack([first, second], axis=1).astype(f32)
        x = t.shape[-1]
        t = t.reshape(2, 2, SSM_PAIRS, 2, p, x).transpose(2, 3, 5, 0, 1, 4)
        keep = [(0, 0)] * 4
        t = jnp.stack([jnp.pad(t[:, 0], keep + [(0, p)]), jnp.pad(t[:, 1], keep + [(p, 0)])], axis=1)
        return t.reshape(SSM_PAIRS, 2 * x, SSM_PW)

    lr, li = slab_row(a_re), slab_row(a_im)
    dt = slab_row(jnp.broadcast_to(jnp.exp(log_dt.astype(f32))[..., None], a_re.shape))
    sr, si = lr * dt, li * dt
    k = jnp.arange(lc + 1, dtype=f32)[:, None, None]
    mag = jnp.exp(k * sr[None])
    ar, ai = mag * jnp.cos(k * si[None]), mag * jnp.sin(k * si[None])
    den = lr * lr + li * li
    qr = ((ar[1] - 1.0) * lr + ai[1] * li) / den
    qi = (ai[1] * lr - (ar[1] - 1.0) * li) / den
    b4, b4s = embed(b_re, b_im), embed(-b_im, b_re)
    bb4 = (qr[:, None] * b4 + qi[:, None] * b4s)[:, None]
    bb4s = (qr[:, None] * b4s - qi[:, None] * b4)[:, None]
    rev = np.arange(lc)[::-1]
    fwd = np.arange(lc)

    def powers(t, ks):
        ks = [int(v) for v in ks]
        parts, i = [], 0
        while i < len(ks):
            n = 1
            step = ks[i + 1] - ks[i] if i + 1 < len(ks) else 0
            assert abs(step) <= 1
            while i + n < len(ks) and ks[i + n] - ks[i + n - 1] == step:
                n += 1
            lo, hi = min(ks[i], ks[i + n - 1]), max(ks[i], ks[i + n - 1])
            run = t[lo:hi + 1]
            if step == 0:
                run = jnp.broadcast_to(run, (n,) + t.shape[1:])
            elif step < 0:
                run = run[::-1]
            parts.append(run)
            i += n
        return parts[0] if len(parts) == 1 else jnp.concatenate(parts, axis=0)

    def per_step(t, k_fwd, k_bwd):
        t = jnp.concatenate([powers(t, k_fwd)[..., :half], powers(t, k_bwd)[..., half:]], axis=-1)
        return t.transpose(1, 0, 2)[:, :, None, :]

    bx = bb4 * per_step(ar, rev, fwd) + bb4s * per_step(ai, rev, fwd)
    bx = bx.reshape(SSM_PAIRS, SSM_PW, SSM_PW)
    ct_re, ct_im = c_re.transpose(0, 1, 3, 2), c_im.transpose(0, 1, 3, 2)
    c4 = embed(ct_re, -ct_im)[:, None]
    c4s = embed(-ct_im, -ct_re)[:, None]
    cyt = c4 * per_step(ar, fwd + 1, lc - fwd) + c4s * per_step(ai, fwd + 1, lc - fwd)
    cyt = cyt.reshape(SSM_PAIRS, SSM_PW, SSM_PW)

    kt = c4 * per_step(ar, fwd, rev) + c4s * per_step(ai, fwd, rev)
    kt = kt.reshape(SSM_PAIRS, SSM_PW, SSM_PW)
    nt = (((2,), (2,)), ((0,), (0,)))
    k_fwd = lax.dot_general(bb4[:, 0, :, :half], kt[:, :, :half], nt, precision=lax.Precision.HIGH)
    k_bwd = lax.dot_general(bb4[:, 0, :, half:], kt[:, :, half:], nt, precision=lax.Precision.HIGH)
    edge = (lc - 1) * 2 * m
    strip = jnp.concatenate([k_bwd[:, :, :edge], k_bwd[:, :, edge:] + k_fwd[:, :, :2 * m], k_fwd[:, :, 2 * m:]], axis=-1)
    tz = jnp.stack([strip[:, :, (lc - 1 - s) * 2 * m:(lc - 1 - s) * 2 * m + SSM_PW] for s in range(lc)], axis=1)
    tz = tz.reshape(SSM_PAIRS, SSM_PW, SSM_PW)

    sw = 2 * p
    a16p = jnp.concatenate([ar[lc][:, :sw], ai[lc][:, sw:half], ar[lc][:, half:half + sw], ai[lc][:, half + sw:]],
                           axis=-1)[:, None, :]
    return tz.astype(bf16), bx.astype(bf16), cyt.astype(bf16), a16p


def _ssm_kernel(u_ref, tz_ref, bx_ref, cyt_ref, a_ref, h0_ref, y_ref, fin_ref, x_scr, s_scr, y_scr, *, n_seq, n_chunks):
    r = n_seq * n_chunks
    pc = 2 * SSM_GROUP
    sw = 2 * SSM_STATE
    xs = [u_ref[pl.ds(s, r, stride=SSM_CHUNK), :] for s in range(SSM_CHUNK)]
    for kq in range(SSM_QUAD):
        u = jnp.concatenate([x[:, kq * pc:(kq + 1) * pc] for x in xs], axis=-1).astype(bf16)
        x = jnp.dot(u, bx_ref[kq], preferred_element_type=f32)
        for i in range(4):
            x_scr[4 * kq + i] = x[:, i * sw:(i + 1) * sw]
        y_scr[:, kq * SSM_PW:(kq + 1) * SSM_PW] = jnp.dot(u, tz_ref[kq], preferred_element_type=f32)

    def scan(direction):
        offs = [kq * SSM_PW + 2 * direction * sw for kq in range(SSM_QUAD)]
        slabs = [4 * kq + 2 * direction for kq in range(SSM_QUAD)]
        coef = [(a_ref[kq][:, 2 * direction * sw:2 * direction * sw + sw],
                 a_ref[kq][:, 2 * direction * sw + sw:2 * direction * sw + 2 * sw]) for kq in range(SSM_QUAD)]

        def body(i, carry):
            c = i if direction == 0 else n_chunks - 1 - i
            rows = pl.ds(c, n_seq, stride=n_chunks)
            out = []
            for kq in range(SSM_QUAD):
                sr, si = carry[2 * kq], carry[2 * kq + 1]
                ar, ai = coef[kq]
                re, im = slabs[kq], slabs[kq] + 1
                s_scr[re, rows, :] = sr
                s_scr[im, rows, :] = si
                xr = x_scr[re, rows, :]
                xi = x_scr[im, rows, :]
                out += [ar * sr - ai * si + xr, ar * si + ai * sr + xi]
            return tuple(out)

        init = []
        for lo in offs:
            init += [h0_ref[:, lo:lo + sw], h0_ref[:, lo + sw:lo + 2 * sw]]
        fin = lax.fori_loop(0, n_chunks, body, tuple(init))
        for kq, lo in enumerate(offs):
            fin_ref[:, lo:lo + sw] = fin[2 * kq]
            fin_ref[:, lo + sw:lo + 2 * sw] = fin[2 * kq + 1]

    scan(0)
    scan(1)
    ys = []
    for kq in range(SSM_QUAD):
        cols = slice(kq * SSM_PW, (kq + 1) * SSM_PW)
        s_in = jnp.concatenate([s_scr[4 * kq + i] for i in range(4)], axis=-1).astype(bf16)
        ys.append(y_scr[:, cols] + lax.dot_general(s_in, cyt_ref[kq], (((1,), (1,)), ((), ())),
                                                   preferred_element_type=f32))
    for t in range(SSM_CHUNK):
        y_ref[pl.ds(t, r, stride=SSM_CHUNK), :] = jnp.concatenate([y[:, t * pc:(t + 1) * pc] for y in ys], axis=-1)


def ssm_scan(z, row0, n_seq, seq, tables, layer, h0):
    tz, bx, cyt, a16p = tables
    n_chunks = seq // SSM_CHUNK
    rows = n_seq * seq
    r = n_seq * n_chunks
    lanes = SSM_QUAD * 2 * SSM_GROUP
    u_blk = (3 * ATTN_W) // lanes
    qw = SSM_QUAD * SSM_PW
    table = pl.BlockSpec((None, SSM_QUAD, SSM_PW, SSM_PW), lambda q: (layer, q, 0, 0))
    return pl.pallas_call(
        functools.partial(_ssm_kernel, n_seq=n_seq, n_chunks=n_chunks),
        grid=(SSM_PAIRS // SSM_QUAD,),
        in_specs=[pl.BlockSpec((rows, lanes), lambda q: (row0 // rows, u_blk + q)),
                  table, table, table,
                  pl.BlockSpec((None, SSM_QUAD, 1, SSM_PW), lambda q: (layer, q, 0, 0)),
                  pl.BlockSpec((n_seq, qw), lambda q: (0, q))],
        out_specs=[pl.BlockSpec((rows, lanes), lambda q: (0, q)),
                   pl.BlockSpec((n_seq, qw), lambda q: (0, q))],
        out_shape=[jax.ShapeDtypeStruct((rows, SSM_W), f32),
                   jax.ShapeDtypeStruct((n_seq, SSM_PAIRS * SSM_PW), f32)],
        scratch_shapes=[pltpu.VMEM((4 * SSM_QUAD, r, 2 * SSM_STATE), f32),
                        pltpu.VMEM((4 * SSM_QUAD, r, 2 * SSM_STATE), f32), pltpu.VMEM((r, qw), f32)],
        compiler_params=_cparams(("parallel",)),
        name="ssm_scan",
    )(z, tz, bx, cyt, a16p, h0)


def state_to_slabs(st):
    batch = st.shape[0]
    t = st.reshape(batch, 2, SSM_PAIRS, 2, SSM_STATE, 2).transpose(0, 2, 1, 5, 3, 4)
    return t.reshape(batch, SSM_PAIRS * SSM_PW)


def state_from_slabs(fin):
    batch = fin.shape[0]
    t = fin.reshape(batch, SSM_PAIRS, 2, 2, 2, SSM_STATE)
    return t.transpose(0, 2, 1, 4, 5, 3).reshape(batch, 2, SSM_GROUPS, SSM_STATE, 2)


def _glu_kernel(ya_ref, yb_ref, u_ref, d_ref, w_ref, b_ref, o_ref, *, tiles_a):
    y = _pick(pl.program_id(0) < tiles_a, ya_ref, yb_ref) + d_ref[...] * u_ref[...]
    z = jnp.dot(jax.nn.gelu(y).astype(bf16), w_ref[...], preferred_element_type=f32) + b_ref[...]
    o_ref[...] = (z[:, :SSM_W] * jax.nn.sigmoid(z[:, SSM_W:])).astype(bf16)


def ssm_glu(y_a, y_b, z, d_skip, w_glu, b_glu, layer, tm):
    tiles_a, tiles_b = y_a.shape[0] // tm, y_b.shape[0] // tm
    m_total = y_a.shape[0] + y_b.shape[0]
    u_blk = (3 * ATTN_W) // SSM_W
    return pl.pallas_call(
        functools.partial(_glu_kernel, tiles_a=tiles_a),
        grid=(tiles_a + tiles_b,),
        in_specs=[
            *_two_part(tm, SSM_W, tiles_a, tiles_b),
            pl.BlockSpec((tm, SSM_W), lambda i: (i, u_blk)),
            pl.BlockSpec((None, 1, SSM_W), lambda i: (layer, 0, 0)),
            pl.BlockSpec((None, SSM_W, 2 * SSM_W), lambda i: (layer, 0, 0)),
            pl.BlockSpec((None, 1, 2 * SSM_W), lambda i: (layer, 0, 0)),
        ],
        out_specs=pl.BlockSpec((tm, SSM_W), lambda i: (i, 0)),
        out_shape=jax.ShapeDtypeStruct((m_total, SSM_W), bf16),
        compiler_params=_cparams(("arbitrary",)),
        name="ssm_glu",
    )(y_a, y_b, z, d_skip, w_glu, b_glu)


def _pool_kernel(p_ref, w_ref, sc_ref, o_ref, pad_ref, *, seq):
    zeros = jnp.zeros((POOL_PAD, POOL_GROUP), f32)
    body = slice(POOL_PAD, POOL_PAD + seq)
    ext = slice(POOL_PAD, POOL_PAD + seq + POOL_TAIL)
    t = lax.broadcasted_iota(jnp.int32, (seq, 1), 0)
    for g, win in enumerate(POOL_WINDOWS):
        cols = slice(g * POOL_GROUP, (g + 1) * POOL_GROUP)
        pad_ref[0:POOL_PAD, cols] = zeros
        pad_ref[POOL_PAD + seq:, cols] = zeros
        x = p_ref[:, cols]
        pad_ref[body, cols] = x
        w = 1
        while w < win:
            pad_ref[ext, cols] = pad_ref[ext, cols] + pad_ref[POOL_PAD - w:POOL_PAD - w + seq + POOL_TAIL, cols]
            w *= 2
        ahead = win // 2 - 1
        total = pad_ref[POOL_PAD + ahead:POOL_PAD + ahead + seq, cols]
        lo = jnp.clip(t - win // 2, 0, seq)
        hi = jnp.clip(t - win // 2 + win, 0, seq)
        mixed = total / (hi - lo).astype(f32) - x
        out = jnp.dot(mixed.astype(bf16), w_ref[g].astype(bf16), preferred_element_type=f32)
        o_ref[:, cols] = (out * sc_ref[:, cols]).astype(bf16)


def pool_mixer(z, row0, batch, seq, w_pool, pool_scale, layer):
    p_blk = (3 * ATTN_W + SSM_W) // POOL_W
    blk0 = row0 // seq
    return pl.pallas_call(
        functools.partial(_pool_kernel, seq=seq),
        grid=(batch,),
        in_specs=[
            pl.BlockSpec((seq, POOL_W), lambda b: (blk0 + b, p_blk)),
            pl.BlockSpec((None, len(POOL_WINDOWS), POOL_GROUP, POOL_GROUP), lambda b: (layer, 0, 0, 0)),
            pl.BlockSpec((None, 1, POOL_W), lambda b: (layer, 0, 0)),
        ],
        out_specs=pl.BlockSpec((seq, POOL_W), lambda b: (b, 0)),
        out_shape=jax.ShapeDtypeStruct((batch * seq, POOL_W), bf16),
        scratch_shapes=[pltpu.VMEM((seq + 2 * POOL_PAD, POOL_W), f32)],
        compiler_params=_cparams(("parallel",)),
        name="pool_mixer",
    )(z, w_pool, pool_scale)


def _out_kernel(aa_ref, ab_ref, s_ref, pa_ref, pb_ref, *refs, n_x, tiles_a, tn):
    x_refs = refs[:n_x]
    mod_ref, w_ref, o_ref = refs[n_x:]
    first = pl.program_id(0) < tiles_a
    a = _pick(first, aa_ref, ab_ref)
    s = s_ref[...]
    p = _pick(first, pa_ref, pb_ref)
    x = _rows(x_refs, tiles_a)
    gate = mod_ref[0][2:3]
    for c in range(w_ref.shape[1] // tn):
        cols = slice(c * tn, (c + 1) * tn)
        acc = jnp.dot(a, w_ref[0:ATTN_W, cols], preferred_element_type=f32)
        acc = acc + jnp.dot(s, w_ref[ATTN_W:ATTN_W + SSM_W, cols], preferred_element_type=f32)
        acc = acc + jnp.dot(p, w_ref[ATTN_W + SSM_W:, cols], preferred_element_type=f32)
        o_ref[:, cols] = x[:, cols] + gate[:, cols] * acc


def out_projection(a_a, a_b, s, p_a, p_b, xs, mod, w, layer, row_of_tile, tm, tn=512):
    m_total = sum(x.shape[0] for x in xs)
    d = xs[0].shape[1]
    tiles_a, tiles_b = a_a.shape[0] // tm, a_b.shape[0] // tm
    assert len(xs) == 1 or xs[0].shape[0] == a_a.shape[0]
    return pl.pallas_call(
        functools.partial(_out_kernel, n_x=len(xs), tiles_a=tiles_a, tn=tn),
        grid=(m_total // tm,),
        in_specs=[
            *_two_part(tm, ATTN_W, tiles_a, tiles_b),
            pl.BlockSpec((tm, SSM_W), lambda i: (i, 0)),
            *_two_part(tm, POOL_W, tiles_a, tiles_b),
            *_row_specs(xs, tm, d),
            pl.BlockSpec((None, 1, 6, d), lambda i: (layer, row_of_tile(i), 0, 0)),
            pl.BlockSpec((None, d, d), lambda i: (layer, 0, 0), pipeline_mode=pl.Buffered(1)),
        ],
        out_specs=pl.BlockSpec((tm, d), lambda i: (i, 0)),
        out_shape=jax.ShapeDtypeStruct((m_total, d), f32),
        compiler_params=_cparams(("arbitrary",)),
        name="out_projection",
    )(a_a, a_b, s, p_a, p_b, *xs, mod, w)


def _ffn_kernel(x_ref, mod_ref, g_ref, wg_ref, wu_ref, wo_ref, fg_ref, *refs, n_out, tiles_a, final_norm):
    o_refs = refs[:n_out]
    h_ref, acc_ref = refs[n_out:]
    i, j = pl.program_id(0), pl.program_id(1)
    last = j == pl.num_programs(1) - 1

    def chunk():
        h = h_ref[...]
        gate = jnp.dot(h, wg_ref[...], preferred_element_type=f32)
        up = jnp.dot(h, wu_ref[...], preferred_element_type=f32)
        act = (gate * jax.nn.sigmoid(gate) * up).astype(bf16)
        return mod_ref[0][5:6] * jnp.dot(act, wo_ref[...], preferred_element_type=f32)

    @pl.when(j == 0)
    def _():
        m = mod_ref[0]
        x = x_ref[...]
        h_ref[...] = (_rms(x, g_ref[...]) * (1.0 + m[4:5]) + m[3:4]).astype(bf16)
        acc_ref[...] = x + chunk()

    @pl.when(j > 0)
    def _():
        acc_ref[...] += chunk()

    def result():
        return _rms(acc_ref[...], fg_ref[...]) if final_norm else acc_ref[...]

    if n_out == 1:
        @pl.when(last)
        def _():
            o_refs[0][...] = result()
    else:
        @pl.when(last & (i < tiles_a))
        def _():
            o_refs[0][...] = result()

        @pl.when(last & (i >= tiles_a))
        def _():
            o_refs[1][...] = result()


def ffn(x, mod, g, w_in, w_out, final_g, layer, row_of_tile, tm, th, final_norm, split_rows=None):
    m_total, d = x.shape
    hidden = w_out.shape[1]
    nh = hidden // th
    if split_rows is None:
        tiles_a = m_total // tm
        out_specs = [pl.BlockSpec((tm, d), lambda i, j: (i, 0))]
        out_shape = [jax.ShapeDtypeStruct((m_total, d), f32)]
    else:
        tiles_a = split_rows // tm
        out_specs = list(_two_part(tm, d, tiles_a, m_total // tm - tiles_a))
        out_shape = [jax.ShapeDtypeStruct((split_rows, d), f32), jax.ShapeDtypeStruct((m_total - split_rows, d), f32)]
    return pl.pallas_call(
        functools.partial(_ffn_kernel, n_out=len(out_specs), tiles_a=tiles_a, final_norm=final_norm),
        grid=(m_total // tm, nh),
        in_specs=[
            pl.BlockSpec((tm, d), lambda i, j: (i, 0)),
            pl.BlockSpec((None, 1, 6, d), lambda i, j: (layer, row_of_tile(i), 0, 0)),
            pl.BlockSpec((None, 1, d), lambda i, j: (layer, 0, 0)),
            pl.BlockSpec((None, d, th), lambda i, j: (layer, 0, j)),
            pl.BlockSpec((None, d, th), lambda i, j: (layer, 0, nh + j)),
            pl.BlockSpec((None, th, d), lambda i, j: (layer, j, 0)),
            pl.BlockSpec((1, d), lambda i, j: (0, 0)),
        ],
        out_specs=out_specs,
        out_shape=out_shape,
        scratch_shapes=[pltpu.VMEM((tm, d), bf16), pltpu.VMEM((tm, d), f32)],
        compiler_params=_cparams(("arbitrary", "arbitrary")),
        name="ffn",
    )(x, mod, g, w_in, w_in, w_out, final_g)


def kernel(x_prompt, x_sample, c, cache_k, cache_v, state_ssm, c_ctx, w_mod, b_mod, norm1_g, norm2_g, w_in,
           attn_rpb, ssm_a_re, ssm_a_im, ssm_log_dt, ssm_b_re, ssm_b_im, ssm_c_re, ssm_c_im, ssm_d, ssm_w_glu,
           ssm_b_glu, pool_w, pool_scale, w_out, ffn_w_in, ffn_w_out, final_norm_g):
    batch, seq, d = x_prompt.shape
    dec_batch, dec_seq, _ = x_sample.shape
    depth = w_in.shape[0]
    m_ctx = batch * seq
    tm = 512
    assert m_ctx % tm == 0 and dec_seq % tm == 0 and dec_batch + 1 <= 8

    def row_of_tile(i):
        return jnp.where(i < m_ctx // tm, 0, 1 + (i - m_ctx // tm) // (dec_seq // tm))

    cond = jnp.concatenate([c_ctx[None, :], c, jnp.zeros((8 - 1 - dec_batch, d), f32)], axis=0)
    mod = modulation_all(cond, w_mod, b_mod)
    mod = mod[:, :1 + dec_batch].reshape(depth, 1 + dec_batch, 6, d)

    w_in_b, w_out_b = w_in.astype(bf16), w_out.astype(bf16)
    ffn_w_in_b, ffn_w_out_b, w_glu_b = ffn_w_in.astype(bf16), ffn_w_out.astype(bf16), ssm_w_glu.astype(bf16)
    row = lambda t: t[:, None, :]
    tables = jax.vmap(ssm_tables)(ssm_a_re, ssm_a_im, ssm_log_dt, ssm_b_re, ssm_b_im, ssm_c_re, ssm_c_im)
    bias_tables = jax.vmap(latent_bias_table)(attn_rpb)

    xs = (x_prompt.reshape(m_ctx, d), x_sample.reshape(dec_batch * dec_seq, d))
    h0_ctx = jnp.zeros((batch, SSM_PAIRS * SSM_PW), f32)

    zs, st_out = [], []
    for l in range(depth):
        last = l == depth - 1
        z = in_projection(xs, mod, row(norm1_g), w_in_b, l, row_of_tile, tm)

        a_ctx, *caches = context_attention(z, batch, seq, zs, write_caches=last)
        zs.append(z)
        a_lat = latent_attention(z, m_ctx, dec_batch, dec_seq, cache_k, cache_v, l, bias_tables)

        y_ctx, fin = ssm_scan(z, 0, batch, seq, tables, l, h0_ctx)
        y_lat, _ = ssm_scan(z, m_ctx, dec_batch, dec_seq, tables, l, state_to_slabs(state_ssm[:, l]))
        st_out.append(state_from_slabs(fin))
        s_out = ssm_glu(y_ctx, y_lat, z, row(ssm_d), w_glu_b, row(ssm_b_glu), l, tm)

        p_ctx = pool_mixer(z, 0, batch, seq, pool_w, row(pool_scale), l)
        p_lat = pool_mixer(z, m_ctx, dec_batch, dec_seq, pool_w, row(pool_scale), l)

        x = out_projection(a_ctx, a_lat, s_out, p_ctx, p_lat, xs, mod, w_out_b, l, row_of_tile, tm)
        xs = ffn(x, mod, row(norm2_g), ffn_w_in_b, ffn_w_out_b, final_norm_g[None], l, row_of_tile, tm, 512,
                 final_norm=last, split_rows=m_ctx if last else None)

    y_prompt = xs[0].reshape(batch, seq, d)
    y_sample = xs[1].reshape(dec_batch, dec_seq, d)
    return (y_prompt, y_sample, caches[0], caches[1], jnp.stack(st_out, axis=1))
```

```python
import functools

import jax
import jax.numpy as jnp
import numpy as np
from jax import lax
from jax.experimental import pallas as pl
from jax.experimental.pallas import tpu as pltpu

f32 = jnp.float32
bf16 = jnp.bfloat16

D_MODEL = 2048
N_HEADS = 16
HEAD_DIM = 64
ATTN_W = N_HEADS * HEAD_DIM
SSM_W = 512
SSM_GROUP = 16
SSM_GROUPS = 32
SSM_STATE = 64
POOL_W = 512
POOL_WINDOWS = (2, 4, 8, 16)
POOL_GROUP = 128
POOL_PAD = 16
POOL_TAIL = 8
IN_W = 3 * ATTN_W + SSM_W + POOL_W
GRID_W = 64
WIN_ROWS_MAX = 8
WIN_COLS = 16
RMS_EPS = 1e-6
NEG_INF = -1e30

LAT_BLOCK_ROWS = 4
SSM_CHUNK = 16
SSM_PAIRS = SSM_GROUPS // 2
SSM_QUAD = 4
SSM_PW = 2 * SSM_CHUNK * SSM_GROUP

VMEM_LIMIT = 56 * 1024 * 1024


def _cparams(sem):
    return pltpu.CompilerParams(dimension_semantics=sem, vmem_limit_bytes=VMEM_LIMIT)


def _rms(x, g):
    return x * lax.rsqrt(jnp.mean(x * x, axis=-1, keepdims=True) + RMS_EPS) * g


def _two_part(tm, width, tiles_a, tiles_b, col=lambda *_: 0):
    return (pl.BlockSpec((tm, width), lambda i, *r: (jnp.minimum(i, tiles_a - 1), col(*r))),
            pl.BlockSpec((tm, width), lambda i, *r: (jnp.clip(i - tiles_a, 0, tiles_b - 1), col(*r))))


def _row_specs(parts, tm, width, col=lambda *_: 0):
    if len(parts) == 1:
        return (pl.BlockSpec((tm, width), lambda i, *r: (i, col(*r))),)
    return _two_part(tm, width, parts[0].shape[0] // tm, parts[1].shape[0] // tm, col)


def _pick(first, a_ref, b_ref):
    return jnp.where(first, a_ref[...], b_ref[...])


def _rows(refs, tiles_a):
    if len(refs) == 1:
        return refs[0][...]
    return _pick(pl.program_id(0) < tiles_a, *refs)


def _mod_kernel(c_ref, w_ref, b_ref, o_ref):
    c = c_ref[...]
    s = (c * jax.nn.sigmoid(c)).astype(bf16)
    o_ref[0] = jnp.dot(s, w_ref[0].astype(bf16), preferred_element_type=f32) + b_ref[0]


def modulation_all(cond, w_mod, b_mod, tn=2048):
    n_layers, d, n = w_mod.shape
    return pl.pallas_call(
        _mod_kernel,
        grid=(n_layers, n // tn),
        in_specs=[
            pl.BlockSpec((8, d), lambda l, j: (0, 0)),
            pl.BlockSpec((1, d, tn), lambda l, j: (l, 0, j)),
            pl.BlockSpec((1, 1, tn), lambda l, j: (l, 0, j)),
        ],
        out_specs=pl.BlockSpec((1, 8, tn), lambda l, j: (l, 0, j)),
        out_shape=jax.ShapeDtypeStruct((n_layers, 8, n), f32),
        compiler_params=_cparams(("parallel", "parallel")),
        name="modulation",
    )(cond, w_mod, b_mod.reshape(n_layers, 1, n))


def _in_kernel(*refs, n_x, tiles_a, tn):
    x_refs = refs[:n_x]
    mod_ref, g_ref, w_ref, o_ref = refs[n_x:]
    m = mod_ref[0]
    y = _rms(_rows(x_refs, tiles_a), g_ref[...])
    h = (y * (1.0 + m[1:2]) + m[0:1]).astype(bf16)
    for c in range(w_ref.shape[1] // tn):
        cols = slice(c * tn, (c + 1) * tn)
        o_ref[:, cols] = jnp.dot(h, w_ref[:, cols], preferred_element_type=f32)


def in_projection(xs, mod, g, w, layer, row_of_tile, tm, tn=512):
    m_total = sum(x.shape[0] for x in xs)
    d = xs[0].shape[1]
    n = w.shape[2]
    return pl.pallas_call(
        functools.partial(_in_kernel, n_x=len(xs), tiles_a=xs[0].shape[0] // tm, tn=tn),
        grid=(m_total // tm,),
        in_specs=[
            *_row_specs(xs, tm, d),
            pl.BlockSpec((None, 1, 6, d), lambda i: (layer, row_of_tile(i), 0, 0)),
            pl.BlockSpec((None, 1, d), lambda i: (layer, 0, 0)),
            pl.BlockSpec((None, d, n), lambda i: (layer, 0, 0), pipeline_mode=pl.Buffered(1)),
        ],
        out_specs=pl.BlockSpec((tm, n), lambda i: (i, 0)),
        out_shape=jax.ShapeDtypeStruct((m_total, n), f32),
        compiler_params=_cparams(("arbitrary",)),
        name="in_projection",
    )(*xs, mod, g, w)


def _first_head(shape):
    return lax.broadcasted_iota(jnp.int32, shape, len(shape) - 1) < HEAD_DIM


def _attn_ctx_kernel(*refs, n_prev, write_caches):
    q_ref, k_ref, v_ref = refs[:3]
    prev = refs[3:3 + 2 * n_prev]
    outs = refs[3 + 2 * n_prev:]
    o_ref = outs[0]
    scale = HEAD_DIM ** -0.5
    nt = (((1,), (1,)), ((), ()))
    seq = q_ref.shape[0]
    lanes = 2 * HEAD_DIM
    first = _first_head((seq, lanes))
    keeps = (first, jnp.logical_not(first))

    def scores_of(p):
        cols = slice(p * lanes, (p + 1) * lanes)
        q = q_ref[:, cols].astype(bf16)
        k = k_ref[:, cols]
        v = v_ref[:, cols]
        if write_caches:
            kc_ref, vc_ref = outs[1:]
            layers = [(prev[2 * l][:, cols], prev[2 * l + 1][:, cols]) for l in range(n_prev)] + [(k, v)]
            for l, (kl, vl) in enumerate(layers):
                for h in range(2):
                    kc_ref[0, l, 2 * p + h] = kl[:, h * HEAD_DIM:(h + 1) * HEAD_DIM]
                    vc_ref[0, l, 2 * p + h] = vl[:, h * HEAD_DIM:(h + 1) * HEAD_DIM]
        kb = k.astype(bf16)
        zero = jnp.zeros_like(kb)
        return [lax.dot_general(q, jnp.where(keep, kb, zero), nt, preferred_element_type=f32) * scale
                for keep in keeps]

    def finish(p, scores):
        cols = slice(p * lanes, (p + 1) * lanes)
        vb = v_ref[:, cols].astype(bf16)
        zero = jnp.zeros_like(vb)
        num, inv = None, []
        for keep, s in zip(keeps, scores):
            e = jnp.exp(s - jnp.max(s, axis=-1, keepdims=True))
            inv.append(1.0 / jnp.sum(e, axis=-1, keepdims=True))
            part = jnp.dot(e.astype(bf16), jnp.where(keep, vb, zero), preferred_element_type=f32)
            num = part if num is None else num + part
        o_ref[:, cols] = (num * jnp.where(first, inv[0], inv[1])).astype(bf16)

    n_pairs = N_HEADS // 2
    pending = scores_of(0)
    for p in range(n_pairs):
        upcoming = scores_of(p + 1) if p + 1 < n_pairs else None
        finish(p, pending)
        pending = upcoming


def context_attention(z, batch, seq, prev_zs=(), write_caches=False):
    col = lambda blk: pl.BlockSpec((seq, ATTN_W), lambda b: (b, blk))
    out_specs = [pl.BlockSpec((seq, ATTN_W), lambda b: (b, 0))]
    out_shape = [jax.ShapeDtypeStruct((batch * seq, ATTN_W), bf16)]
    if write_caches:
        depth = len(prev_zs) + 1
        cache = pl.BlockSpec((1, depth, N_HEADS, seq, HEAD_DIM), lambda b: (b, 0, 0, 0, 0))
        cache_shape = jax.ShapeDtypeStruct((batch, depth, N_HEADS, seq, HEAD_DIM), f32)
        out_specs += [cache, cache]
        out_shape += [cache_shape, cache_shape]
    else:
        prev_zs = ()
    prev_args = [a for zp in prev_zs for a in (zp, zp)]
    return pl.pallas_call(
        functools.partial(_attn_ctx_kernel, n_prev=len(prev_zs), write_caches=write_caches),
        grid=(batch,),
        in_specs=[col(0), col(1), col(2)] + [col(1), col(2)] * len(prev_zs),
        out_specs=out_specs,
        out_shape=out_shape,
        compiler_params=_cparams(("parallel",)),
        name="context_attention",
    )(z, z, z, *prev_args)


def _attn_lat_kernel(q_ref, k_ref, v_ref, ck_ref, cv_ref, t_ref, o_ref, bias_scr, k_scr, v_scr, ck_scr, cv_scr,
                     *, rows, wr):
    scale = HEAD_DIM ** -0.5
    n_loc = wr * GRID_W
    for h in range(2):
        for d in range(wr):
            for j in range(wr):
                bias_scr[h, d, :, j * GRID_W:(j + 1) * GRID_W] = t_ref[h, j - d + WIN_ROWS_MAX - 1]

    kb = k_ref[...].astype(bf16)
    vb = v_ref[...].astype(bf16)
    first = _first_head(kb.shape)
    zero = jnp.zeros_like(kb)
    pad = jnp.zeros(ck_ref.shape[3:], bf16)
    for h in range(2):
        keep = first if h == 0 else jnp.logical_not(first)
        k_scr[h] = jnp.where(keep, kb, zero)
        v_scr[h] = jnp.where(keep, vb, zero)
        ck, cv = ck_ref[0, 0, h].astype(bf16), cv_ref[0, 0, h].astype(bf16)
        ck_scr[h] = jnp.concatenate([ck, pad] if h == 0 else [pad, ck], axis=-1)
        cv_scr[h] = jnp.concatenate([cv, pad] if h == 0 else [pad, cv], axis=-1)

    nt = (((1,), (1,)), ((), ()))
    br = LAT_BLOCK_ROWS
    nq = br * GRID_W
    first_q = _first_head((nq, 2 * HEAD_DIM))

    def block_rows(blk):
        return slice(blk * nq, (blk + 1) * nq)

    def windows_of(blk):
        out = []
        for i in range(br):
            r = blk * br + i
            rs = min(max(r - wr // 2, 0), rows - wr)
            out.append((slice(rs * GRID_W, rs * GRID_W + n_loc), r - rs))
        return out

    def scores_of(blk):
        q = q_ref[block_rows(blk), :].astype(bf16)
        out = []
        for h in range(2):
            s_ctx = lax.dot_general(q, ck_scr[h], nt, preferred_element_type=f32) * scale
            s_loc = [lax.dot_general(q[i * GRID_W:(i + 1) * GRID_W], k_scr[h, win, :], nt,
                                     preferred_element_type=f32) * scale + bias_scr[h, off]
                     for i, (win, off) in enumerate(windows_of(blk))]
            out.append((jnp.concatenate(s_loc, axis=0), s_ctx))
        return out

    def finish(blk, scores):
        probs, inv = [], []
        for s_loc, s_ctx in scores:
            m = jnp.maximum(jnp.max(s_loc, axis=-1, keepdims=True), jnp.max(s_ctx, axis=-1, keepdims=True))
            e_loc = jnp.exp(s_loc - m)
            e_ctx = jnp.exp(s_ctx - m)
            inv.append(1.0 / (jnp.sum(e_loc, axis=-1, keepdims=True) + jnp.sum(e_ctx, axis=-1, keepdims=True)))
            probs.append((e_loc.astype(bf16), e_ctx.astype(bf16)))
        num = None
        for h, (e_loc, e_ctx) in enumerate(probs):
            o_loc = [jnp.dot(e_loc[i * GRID_W:(i + 1) * GRID_W], v_scr[h, win, :], preferred_element_type=f32)
                     for i, (win, _) in enumerate(windows_of(blk))]
            part = jnp.dot(e_ctx, cv_scr[h], preferred_element_type=f32) + jnp.concatenate(o_loc, axis=0)
            num = part if num is None else num + part
        o_ref[block_rows(blk), :] = (num * jnp.where(first_q, inv[0], inv[1])).astype(bf16)

    pending = scores_of(0)
    for blk in range(rows // br):
        upcoming = scores_of(blk + 1) if blk + 1 < rows // br else None
        finish(blk, pending)
        pending = upcoming


def latent_bias_table(rpb):
    cols = np.arange(GRID_W)
    col_start = np.clip(cols - WIN_COLS // 2, 0, GRID_W - WIN_COLS)
    valid = (cols[None, :] >= col_start[:, None]) & (cols[None, :] < col_start[:, None] + WIN_COLS)
    dc = cols[None, :] - cols[:, None] + WIN_COLS - 1
    assert np.all((dc[valid] >= 0) & (dc[valid] < 2 * WIN_COLS - 1))
    onehot = ((np.arange(2 * WIN_COLS - 1)[:, None, None] == dc[None]) & valid[None]).astype(np.float32)
    t = jnp.einsum('hrd,dqk->hrqk', rpb.astype(f32), jnp.asarray(onehot), precision=lax.Precision.HIGHEST)
    return jnp.where(jnp.asarray(valid)[None, None], t, NEG_INF)


def latent_attention(z, row0, batch, seq, cache_k, cache_v, layer, tables):
    lanes = 2 * HEAD_DIM
    n_pairs = N_HEADS // 2
    rows = seq // GRID_W
    wr = min(WIN_ROWS_MAX, rows)
    past = cache_k.shape[3]
    blk0 = row0 // seq
    n_dr = 2 * WIN_ROWS_MAX - 1
    assert rows % LAT_BLOCK_ROWS == 0
    col = lambda off: pl.BlockSpec((seq, lanes), lambda b, p: (blk0 + b, off + p))
    ctx = pl.BlockSpec((1, 1, 2, past, HEAD_DIM), lambda b, p: (b, layer, p, 0, 0))
    return pl.pallas_call(
        functools.partial(_attn_lat_kernel, rows=rows, wr=wr),
        grid=(batch, n_pairs),
        in_specs=[col(0), col(n_pairs), col(2 * n_pairs), ctx, ctx,
                  pl.BlockSpec((None, 2, n_dr, GRID_W, GRID_W), lambda b, p: (layer, p, 0, 0, 0))],
        out_specs=pl.BlockSpec((seq, lanes), lambda b, p: (b, p)),
        out_shape=jax.ShapeDtypeStruct((batch * seq, ATTN_W), bf16),
        scratch_shapes=[pltpu.VMEM((2, wr, GRID_W, wr * GRID_W), f32),
                        pltpu.VMEM((2, seq, lanes), bf16), pltpu.VMEM((2, seq, lanes), bf16),
                        pltpu.VMEM((2, past, lanes), bf16), pltpu.VMEM((2, past, lanes), bf16)],
        compiler_params=_cparams(("parallel", "parallel")),
        name="latent_attention",
    )(z, z, z, cache_k, cache_v, tables)


def ssm_tables(a_re, a_im, log_dt, b_re, b_im, c_re, c_im):
    lc, p, m = SSM_CHUNK, SSM_STATE, SSM_GROUP
    half = SSM_PW // 2

    def slab_row(t):
        t = jnp.broadcast_to(t.astype(f32).reshape(2, 1, SSM_PAIRS, 2, p), (2, 2, SSM_PAIRS, 2, p))
        return t.transpose(2, 0, 1, 3, 4).reshape(SSM_PAIRS, SSM_PW)

    def embed(first, second):
        t = jnp.stack([first, second], axis=1).astype(f32)
        x = t.shape[-1]
        t = t.reshape(2, 2, SSM_PAIRS, 2, p, x).transpose(2, 3, 5, 0, 1, 4)
        keep = [(0, 0)] * 4
        t = jnp.stack([jnp.pad(t[:, 0], keep + [(0, p)]), jnp.pad(t[:, 1], keep + [(p, 0)])], axis=1)
        return t.reshape(SSM_PAIRS, 2 * x, SSM_PW)

    lr, li = slab_row(a_re), slab_row(a_im)
    dt = slab_row(jnp.broadcast_to(jnp.exp(log_dt.astype(f32))[..., None], a_re.shape))
    sr, si = lr * dt, li * dt
    k = jnp.arange(lc + 1, dtype=f32)[:, None, None]
    mag = jnp.exp(k * sr[None])
    ar, ai = mag * jnp.cos(k * si[None]), mag * jnp.sin(k * si[None])
    den = lr * lr + li * li
    qr = ((ar[1] - 1.0) * lr + ai[1] * li) / den
    qi = (ai[1] * lr - (ar[1] - 1.0) * li) / den
    b4, b4s = embed(b_re, b_im), embed(-b_im, b_re)
    bb4 = (qr[:, None] * b4 + qi[:, None] * b4s)[:, None]
    bb4s = (qr[:, None] * b4s - qi[:, None] * b4)[:, None]
    rev = np.arange(lc)[::-1]
    fwd = np.arange(lc)

    def powers(t, ks):
        ks = [int(v) for v in ks]
        parts, i = [], 0
        while i < len(ks):
            n = 1
            step = ks[i + 1] - ks[i] if i + 1 < len(ks) else 0
            assert abs(step) <= 1
            while i + n < len(ks) and ks[i + n] - ks[i + n - 1] == step:
                n += 1
            lo, hi = min(ks[i], ks[i + n - 1]), max(ks[i], ks[i + n - 1])
            run = t[lo:hi + 1]
            if step == 0:
                run = jnp.broadcast_to(run, (n,) + t.shape[1:])
            elif step < 0:
                run = run[::-1]
            parts.append(run)
            i += n
        return parts[0] if len(parts) == 1 else jnp.concatenate(parts, axis=0)

    def per_step(t, k_fwd, k_bwd):
        t = jnp.concatenate([powers(t, k_fwd)[..., :half], powers(t, k_bwd)[..., half:]], axis=-1)
        return t.transpose(1, 0, 2)[:, :, None, :]

    bx = bb4 * per_step(ar, rev, fwd) + bb4s * per_step(ai, rev, fwd)
    bx = bx.reshape(SSM_PAIRS, SSM_PW, SSM_PW)
    ct_re, ct_im = c_re.transpose(0, 1, 3, 2), c_im.transpose(0, 1, 3, 2)
    c4 = embed(ct_re, -ct_im)[:, None]
    c4s = embed(-ct_im, -ct_re)[:, None]
    cyt = c4 * per_step(ar, fwd + 1, lc - fwd) + c4s * per_step(ai, fwd + 1, lc - fwd)
    cyt = cyt.reshape(SSM_PAIRS, SSM_PW, SSM_PW)

    kt = c4 * per_step(ar, fwd, rev) + c4s * per_step(ai, fwd, rev)
    kt = kt.reshape(SSM_PAIRS, SSM_PW, SSM_PW)
    nt = (((2,), (2,)), ((0,), (0,)))
    k_fwd = lax.dot_general(bb4[:, 0, :, :half], kt[:, :, :half], nt, precision=lax.Precision.HIGH)
    k_bwd = lax.dot_general(bb4[:, 0, :, half:], kt[:, :, half:], nt, precision=lax.Precision.HIGH)
    edge = (lc - 1) * 2 * m
    strip = jnp.concatenate([k_bwd[:, :, :edge], k_bwd[:, :, edge:] + k_fwd[:, :, :2 * m], k_fwd[:, :, 2 * m:]], axis=-1)
    tz = jnp.stack([strip[:, :, (lc - 1 - s) * 2 * m:(lc - 1 - s) * 2 * m + SSM_PW] for s in range(lc)], axis=1)
    tz = tz.reshape(SSM_PAIRS, SSM_PW, SSM_PW)

    sw = 2 * p
    a16p = jnp.concatenate([ar[lc][:, :sw], ai[lc][:, sw:half], ar[lc][:, half:half + sw], ai[lc][:, half + sw:]],
                           axis=-1)[:, None, :]
    return tz.astype(bf16), bx.astype(bf16), cyt.astype(bf16), a16p


def _ssm_kernel(u_ref, tz_ref, bx_ref, cyt_ref, a_ref, h0_ref, y_ref, fin_ref, x_scr, s_scr, y_scr, *, n_seq, n_chunks):
    r = n_seq * n_chunks
    pc = 2 * SSM_GROUP
    sw = 2 * SSM_STATE
    xs = [u_ref[pl.ds(s, r, stride=SSM_CHUNK), :] for s in range(SSM_CHUNK)]
    for kq in range(SSM_QUAD):
        u = jnp.concatenate([x[:, kq * pc:(kq + 1) * pc] for x in xs], axis=-1).astype(bf16)
        x = jnp.dot(u, bx_ref[kq], preferred_element_type=f32)
        for i in range(4):
            x_scr[4 * kq + i] = x[:, i * sw:(i + 1) * sw]
        y_scr[:, kq * SSM_PW:(kq + 1) * SSM_PW] = jnp.dot(u, tz_ref[kq], preferred_element_type=f32)

    def scan(direction):
        offs = [kq * SSM_PW + 2 * direction * sw for kq in range(SSM_QUAD)]
        slabs = [4 * kq + 2 * direction for kq in range(SSM_QUAD)]
        coef = [(a_ref[kq][:, 2 * direction * sw:2 * direction * sw + sw],
                 a_ref[kq][:, 2 * direction * sw + sw:2 * direction * sw + 2 * sw]) for kq in range(SSM_QUAD)]

        def body(i, carry):
            c = i if direction == 0 else n_chunks - 1 - i
            rows = pl.ds(c, n_seq, stride=n_chunks)
            out = []
            for kq in range(SSM_QUAD):
                sr, si = carry[2 * kq], carry[2 * kq + 1]
                ar, ai = coef[kq]
                re, im = slabs[kq], slabs[kq] + 1
                s_scr[re, rows, :] = sr
                s_scr[im, rows, :] = si
                xr = x_scr[re, rows, :]
                xi = x_scr[im, rows, :]
                out += [ar * sr - ai * si + xr, ar * si + ai * sr + xi]
            return tuple(out)

        init = []
        for lo in offs:
            init += [h0_ref[:, lo:lo + sw], h0_ref[:, lo + sw:lo + 2 * sw]]
        fin = lax.fori_loop(0, n_chunks, body, tuple(init))
        for kq, lo in enumerate(offs):
            fin_ref[:, lo:lo + sw] = fin[2 * kq]
            fin_ref[:, lo + sw:lo + 2 * sw] = fin[2 * kq + 1]

    scan(0)
    scan(1)
    ys = []
    for kq in range(SSM_QUAD):
        cols = slice(kq * SSM_PW, (kq + 1) * SSM_PW)
        s_in = jnp.concatenate([s_scr[4 * kq + i] for i in range(4)], axis=-1).astype(bf16)
        ys.append(y_scr[:, cols] + lax.dot_general(s_in, cyt_ref[kq], (((1,), (1,)), ((), ())),
                                                   preferred_element_type=f32))
    for t in range(SSM_CHUNK):
        y_ref[pl.ds(t, r, stride=SSM_CHUNK), :] = jnp.concatenate([y[:, t * pc:(t + 1) * pc] for y in ys], axis=-1)


def ssm_scan(z, row0, n_seq, seq, tables, layer, h0):
    tz, bx, cyt, a16p = tables
    n_chunks = seq // SSM_CHUNK
    rows = n_seq * seq
    r = n_seq * n_chunks
    lanes = SSM_QUAD * 2 * SSM_GROUP
    u_blk = (3 * ATTN_W) // lanes
    qw = SSM_QUAD * SSM_PW
    table = pl.BlockSpec((None, SSM_QUAD, SSM_PW, SSM_PW), lambda q: (layer, q, 0, 0))
    return pl.pallas_call(
        functools.partial(_ssm_kernel, n_seq=n_seq, n_chunks=n_chunks),
        grid=(SSM_PAIRS // SSM_QUAD,),
        in_specs=[pl.BlockSpec((rows, lanes), lambda q: (row0 // rows, u_blk + q)),
                  table, table, table,
                  pl.BlockSpec((None, SSM_QUAD, 1, SSM_PW), lambda q: (layer, q, 0, 0)),
                  pl.BlockSpec((n_seq, qw), lambda q: (0, q))],
        out_specs=[pl.BlockSpec((rows, lanes), lambda q: (0, q)),
                   pl.BlockSpec((n_seq, qw), lambda q: (0, q))],
        out_shape=[jax.ShapeDtypeStruct((rows, SSM_W), f32),
                   jax.ShapeDtypeStruct((n_seq, SSM_PAIRS * SSM_PW), f32)],
        scratch_shapes=[pltpu.VMEM((4 * SSM_QUAD, r, 2 * SSM_STATE), f32),
                        pltpu.VMEM((4 * SSM_QUAD, r, 2 * SSM_STATE), f32), pltpu.VMEM((r, qw), f32)],
        compiler_params=_cparams(("parallel",)),
        name="ssm_scan",
    )(z, tz, bx, cyt, a16p, h0)


def state_to_slabs(st):
    batch = st.shape[0]
    t = st.reshape(batch, 2, SSM_PAIRS, 2, SSM_STATE, 2).transpose(0, 2, 1, 5, 3, 4)
    return t.reshape(batch, SSM_PAIRS * SSM_PW)


def state_from_slabs(fin):
    batch = fin.shape[0]
    t = fin.reshape(batch, SSM_PAIRS, 2, 2, 2, SSM_STATE)
    return t.transpose(0, 2, 1, 4, 5, 3).reshape(batch, 2, SSM_GROUPS, SSM_STATE, 2)


def _glu_kernel(ya_ref, yb_ref, u_ref, d_ref, w_ref, b_ref, o_ref, *, tiles_a):
    y = _pick(pl.program_id(0) < tiles_a, ya_ref, yb_ref) + d_ref[...] * u_ref[...]
    z = jnp.dot(jax.nn.gelu(y).astype(bf16), w_ref[...], preferred_element_type=f32) + b_ref[...]
    o_ref[...] = (z[:, :SSM_W] * jax.nn.sigmoid(z[:, SSM_W:])).astype(bf16)


def ssm_glu(y_a, y_b, z, d_skip, w_glu, b_glu, layer, tm):
    tiles_a, tiles_b = y_a.shape[0] // tm, y_b.shape[0] // tm
    m_total = y_a.shape[0] + y_b.shape[0]
    u_blk = (3 * ATTN_W) // SSM_W
    return pl.pallas_call(
        functools.partial(_glu_kernel, tiles_a=tiles_a),
        grid=(tiles_a + tiles_b,),
        in_specs=[
            *_two_part(tm, SSM_W, tiles_a, tiles_b),
            pl.BlockSpec((tm, SSM_W), lambda i: (i, u_blk)),
            pl.BlockSpec((None, 1, SSM_W), lambda i: (layer, 0, 0)),
            pl.BlockSpec((None, SSM_W, 2 * SSM_W), lambda i: (layer, 0, 0)),
            pl.BlockSpec((None, 1, 2 * SSM_W), lambda i: (layer, 0, 0)),
        ],
        out_specs=pl.BlockSpec((tm, SSM_W), lambda i: (i, 0)),
        out_shape=jax.ShapeDtypeStruct((m_total, SSM_W), bf16),
        compiler_params=_cparams(("arbitrary",)),
        name="ssm_glu",
    )(y_a, y_b, z, d_skip, w_glu, b_glu)


def _pool_kernel(p_ref, w_ref, sc_ref, o_ref, pad_ref, *, seq):
    zeros = jnp.zeros((POOL_PAD, POOL_GROUP), f32)
    body = slice(POOL_PAD, POOL_PAD + seq)
    ext = slice(POOL_PAD, POOL_PAD + seq + POOL_TAIL)
    t = lax.broadcasted_iota(jnp.int32, (seq, 1), 0)
    for g, win in enumerate(POOL_WINDOWS):
        cols = slice(g * POOL_GROUP, (g + 1) * POOL_GROUP)
        pad_ref[0:POOL_PAD, cols] = zeros
        pad_ref[POOL_PAD + seq:, cols] = zeros
        x = p_ref[:, cols]
        pad_ref[body, cols] = x
        w = 1
        while w < win:
            pad_ref[ext, cols] = pad_ref[ext, cols] + pad_ref[POOL_PAD - w:POOL_PAD - w + seq + POOL_TAIL, cols]
            w *= 2
        ahead = win // 2 - 1
        total = pad_ref[POOL_PAD + ahead:POOL_PAD + ahead + seq, cols]
        lo = jnp.clip(t - win // 2, 0, seq)
        hi = jnp.clip(t - win // 2 + win, 0, seq)
        mixed = total / (hi - lo).astype(f32) - x
        out = jnp.dot(mixed.astype(bf16), w_ref[g].astype(bf16), preferred_element_type=f32)
        o_ref[:, cols] = (out * sc_ref[:, cols]).astype(bf16)


def pool_mixer(z, row0, batch, seq, w_pool, pool_scale, layer):
    p_blk = (3 * ATTN_W + SSM_W) // POOL_W
    blk0 = row0 // seq
    return pl.pallas_call(
        functools.partial(_pool_kernel, seq=seq),
        grid=(batch,),
        in_specs=[
            pl.BlockSpec((seq, POOL_W), lambda b: (blk0 + b, p_blk)),
            pl.BlockSpec((None, len(POOL_WINDOWS), POOL_GROUP, POOL_GROUP), lambda b: (layer, 0, 0, 0)),
            pl.BlockSpec((None, 1, POOL_W), lambda b: (layer, 0, 0)),
        ],
        out_specs=pl.BlockSpec((seq, POOL_W), lambda b: (b, 0)),
        out_shape=jax.ShapeDtypeStruct((batch * seq, POOL_W), bf16),
        scratch_shapes=[pltpu.VMEM((seq + 2 * POOL_PAD, POOL_W), f32)],
        compiler_params=_cparams(("parallel",)),
        name="pool_mixer",
    )(z, w_pool, pool_scale)


def _out_kernel(aa_ref, ab_ref, s_ref, pa_ref, pb_ref, *refs, n_x, tiles_a, tn):
    x_refs = refs[:n_x]
    mod_ref, w_ref, o_ref = refs[n_x:]
    first = pl.program_id(0) < tiles_a
    a = _pick(first, aa_ref, ab_ref)
    s = s_ref[...]
    p = _pick(first, pa_ref, pb_ref)
    x = _rows(x_refs, tiles_a)
    gate = mod_ref[0][2:3]
    for c in range(w_ref.shape[1] // tn):
        cols = slice(c * tn, (c + 1) * tn)
        acc = jnp.dot(a, w_ref[0:ATTN_W, cols], preferred_element_type=f32)
        acc = acc + jnp.dot(s, w_ref[ATTN_W:ATTN_W + SSM_W, cols], preferred_element_type=f32)
        acc = acc + jnp.dot(p, w_ref[ATTN_W + SSM_W:, cols], preferred_element_type=f32)
        o_ref[:, cols] = x[:, cols] + gate[:, cols] * acc


def out_projection(a_a, a_b, s, p_a, p_b, xs, mod, w, layer, row_of_tile, tm, tn=512):
    m_total = sum(x.shape[0] for x in xs)
    d = xs[0].shape[1]
    tiles_a, tiles_b = a_a.shape[0] // tm, a_b.shape[0] // tm
    assert len(xs) == 1 or xs[0].shape[0] == a_a.shape[0]
    return pl.pallas_call(
        functools.partial(_out_kernel, n_x=len(xs), tiles_a=tiles_a, tn=tn),
        grid=(m_total // tm,),
        in_specs=[
            *_two_part(tm, ATTN_W, tiles_a, tiles_b),
            pl.BlockSpec((tm, SSM_W), lambda i: (i, 0)),
            *_two_part(tm, POOL_W, tiles_a, tiles_b),
            *_row_specs(xs, tm, d),
            pl.BlockSpec((None, 1, 6, d), lambda i: (layer, row_of_tile(i), 0, 0)),
            pl.BlockSpec((None, d, d), lambda i: (layer, 0, 0), pipeline_mode=pl.Buffered(1)),
        ],
        out_specs=pl.BlockSpec((tm, d), lambda i: (i, 0)),
        out_shape=jax.ShapeDtypeStruct((m_total, d), f32),
        compiler_params=_cparams(("arbitrary",)),
        name="out_projection",
    )(a_a, a_b, s, p_a, p_b, *xs, mod, w)


def _ffn_kernel(x_ref, mod_ref, g_ref, wg_ref, wu_ref, wo_ref, fg_ref, *refs, n_out, tiles_a, final_norm):
    o_refs = refs[:n_out]
    h_ref, acc_ref = refs[n_out:]
    i, j = pl.program_id(0), pl.program_id(1)
    last = j == pl.num_programs(1) - 1

    def chunk():
        h = h_ref[...]
        half = wg_ref.shape[1] // 2
        acts = []
        for c in range(2):
            cs = slice(c * half, (c + 1) * half)
            gate = jnp.dot(h, wg_ref[:, cs], preferred_element_type=f32)
            up = jnp.dot(h, wu_ref[:, cs], preferred_element_type=f32)
            acts.append((gate * jax.nn.sigmoid(gate) * up).astype(bf16))
        down = (jnp.dot(acts[0], wo_ref[:half, :], preferred_element_type=f32)
                + jnp.dot(acts[1], wo_ref[half:, :], preferred_element_type=f32))
        return mod_ref[0][5:6] * down

    @pl.when(j == 0)
    def _():
        m = mod_ref[0]
        x = x_ref[...]
        h_ref[...] = (_rms(x, g_ref[...]) * (1.0 + m[4:5]) + m[3:4]).astype(bf16)
        acc_ref[...] = x + chunk()

    @pl.when(j > 0)
    def _():
        acc_ref[...] += chunk()

    def result():
        return _rms(acc_ref[...], fg_ref[...]) if final_norm else acc_ref[...]

    if n_out == 1:
        @pl.when(last)
        def _():
            o_refs[0][...] = result()
    else:
        @pl.when(last & (i < tiles_a))
        def _():
            o_refs[0][...] = result()

        @pl.when(last & (i >= tiles_a))
        def _():
            o_refs[1][...] = result()


def ffn(x, mod, g, w_in, w_out, final_g, layer, row_of_tile, tm, th, final_norm, split_rows=None):
    m_total, d = x.shape
    hidden = w_out.shape[1]
    nh = hidden // th
    if split_rows is None:
        tiles_a = m_total // tm
        out_specs = [pl.BlockSpec((tm, d), lambda i, j: (i, 0))]
        out_shape = [jax.ShapeDtypeStruct((m_total, d), f32)]
    else:
        tiles_a = split_rows // tm
        out_specs = list(_two_part(tm, d, tiles_a, m_total // tm - tiles_a))
        out_shape = [jax.ShapeDtypeStruct((split_rows, d), f32), jax.ShapeDtypeStruct((m_total - split_rows, d), f32)]
    return pl.pallas_call(
        functools.partial(_ffn_kernel, n_out=len(out_specs), tiles_a=tiles_a, final_norm=final_norm),
        grid=(m_total // tm, nh),
        in_specs=[
            pl.BlockSpec((tm, d), lambda i, j: (i, 0)),
            pl.BlockSpec((None, 1, 6, d), lambda i, j: (layer, row_of_tile(i), 0, 0)),
            pl.BlockSpec((None, 1, d), lambda i, j: (layer, 0, 0)),
            pl.BlockSpec((None, d, th), lambda i, j: (layer, 0, j)),
            pl.BlockSpec((None, d, th), lambda i, j: (layer, 0, nh + j)),
            pl.BlockSpec((None, th, d), lambda i, j: (layer, j, 0)),
            pl.BlockSpec((1, d), lambda i, j: (0, 0)),
        ],
        out_specs=out_specs,
        out_shape=out_shape,
        scratch_shapes=[pltpu.VMEM((tm, d), bf16), pltpu.VMEM((tm, d), f32)],
        compiler_params=_cparams(("arbitrary", "arbitrary")),
        name="ffn",
    )(x, mod, g, w_in, w_in, w_out, final_g)


def kernel(x_prompt, x_sample, c, cache_k, cache_v, state_ssm, c_ctx, w_mod, b_mod, norm1_g, norm2_g, w_in,
           attn_rpb, ssm_a_re, ssm_a_im, ssm_log_dt, ssm_b_re, ssm_b_im, ssm_c_re, ssm_c_im, ssm_d, ssm_w_glu,
           ssm_b_glu, pool_w, pool_scale, w_out, ffn_w_in, ffn_w_out, final_norm_g):
    batch, seq, d = x_prompt.shape
    dec_batch, dec_seq, _ = x_sample.shape
    depth = w_in.shape[0]
    m_ctx = batch * seq
    tm = 512
    assert m_ctx % tm == 0 and dec_seq % tm == 0 and dec_batch + 1 <= 8

    def row_of_tile(i):
        return jnp.where(i < m_ctx // tm, 0, 1 + (i - m_ctx // tm) // (dec_seq // tm))

    cond = jnp.concatenate([c_ctx[None, :], c, jnp.zeros((8 - 1 - dec_batch, d), f32)], axis=0)
    mod = modulation_all(cond, w_mod, b_mod)
    mod = mod[:, :1 + dec_batch].reshape(depth, 1 + dec_batch, 6, d)

    w_in_b, w_out_b = w_in.astype(bf16), w_out.astype(bf16)
    ffn_w_in_b, ffn_w_out_b, w_glu_b = ffn_w_in.astype(bf16), ffn_w_out.astype(bf16), ssm_w_glu.astype(bf16)
    row = lambda t: t[:, None, :]
    tables = jax.vmap(ssm_tables)(ssm_a_re, ssm_a_im, ssm_log_dt, ssm_b_re, ssm_b_im, ssm_c_re, ssm_c_im)
    bias_tables = jax.vmap(latent_bias_table)(attn_rpb)

    xs = (x_prompt.reshape(m_ctx, d), x_sample.reshape(dec_batch * dec_seq, d))
    h0_ctx = jnp.zeros((batch, SSM_PAIRS * SSM_PW), f32)

    zs, st_out = [], []
    for l in range(depth):
        last = l == depth - 1
        z = in_projection(xs, mod, row(norm1_g), w_in_b, l, row_of_tile, tm)

        a_ctx, *caches = context_attention(z, batch, seq, zs, write_caches=last)
        zs.append(z)
        a_lat = latent_attention(z, m_ctx, dec_batch, dec_seq, cache_k, cache_v, l, bias_tables)

        y_ctx, fin = ssm_scan(z, 0, batch, seq, tables, l, h0_ctx)
        y_lat, _ = ssm_scan(z, m_ctx, dec_batch, dec_seq, tables, l, state_to_slabs(state_ssm[:, l]))
        st_out.append(state_from_slabs(fin))
        s_out = ssm_glu(y_ctx, y_lat, z, row(ssm_d), w_glu_b, row(ssm_b_glu), l, tm)

        p_ctx = pool_mixer(z, 0, batch, seq, pool_w, row(pool_scale), l)
        p_lat = pool_mixer(z, m_ctx, dec_batch, dec_seq, pool_w, row(pool_scale), l)

        x = out_projection(a_ctx, a_lat, s_out, p_ctx, p_lat, xs, mod, w_out_b, l, row_of_tile, tm)
        xs = ffn(x, mod, row(norm2_g), ffn_w_in_b, ffn_w_out_b, final_norm_g[None], l, row_of_tile, tm, 512,
                 final_norm=last, split_rows=m_ctx if last else None)

    y_prompt = xs[0].reshape(batch, seq, d)
    y_sample = xs[1].reshape(dec_batch, dec_seq, d)
    return (y_prompt, y_sample, caches[0], caches[1], jnp.stack(st_out, axis=1))
```

```python
import functools

import jax
import jax.numpy as jnp
import numpy as np
from jax import lax
from jax.experimental import pallas as pl
from jax.experimental.pallas import tpu as pltpu

f32 = jnp.float32
bf16 = jnp.bfloat16

D_MODEL = 2048
N_HEADS = 16
HEAD_DIM = 64
ATTN_W = N_HEADS * HEAD_DIM
SSM_W = 512
SSM_GROUP = 16
SSM_GROUPS = 32
SSM_STATE = 64
POOL_W = 512
POOL_WINDOWS = (2, 4, 8, 16)
POOL_GROUP = 128
POOL_PAD = 16
POOL_TAIL = 8
IN_W = 3 * ATTN_W + SSM_W + POOL_W
GRID_W = 64
WIN_ROWS_MAX = 8
WIN_COLS = 16
RMS_EPS = 1e-6
NEG_INF = -1e30

LAT_BLOCK_ROWS = 4
SSM_CHUNK = 16
SSM_PAIRS = SSM_GROUPS // 2
SSM_QUAD = 4
SSM_PW = 2 * SSM_CHUNK * SSM_GROUP

VMEM_LIMIT = 56 * 1024 * 1024


def _cparams(sem):
    return pltpu.CompilerParams(dimension_semantics=sem, vmem_limit_bytes=VMEM_LIMIT)


def _rms(x, g):
    return x * lax.rsqrt(jnp.mean(x * x, axis=-1, keepdims=True) + RMS_EPS) * g


def _two_part(tm, width, tiles_a, tiles_b, col=lambda *_: 0):
    return (pl.BlockSpec((tm, width), lambda i, *r: (jnp.minimum(i, tiles_a - 1), col(*r))),
            pl.BlockSpec((tm, width), lambda i, *r: (jnp.clip(i - tiles_a, 0, tiles_b - 1), col(*r))))


def _row_specs(parts, tm, width, col=lambda *_: 0):
    if len(parts) == 1:
        return (pl.BlockSpec((tm, width), lambda i, *r: (i, col(*r))),)
    return _two_part(tm, width, parts[0].shape[0] // tm, parts[1].shape[0] // tm, col)


def _pick(first, a_ref, b_ref):
    return jnp.where(first, a_ref[...], b_ref[...])


def _rows(refs, tiles_a):
    if len(refs) == 1:
        return refs[0][...]
    return _pick(pl.program_id(0) < tiles_a, *refs)


def _mod_kernel(c_ref, w_ref, b_ref, o_ref):
    c = c_ref[...]
    s = (c * jax.nn.sigmoid(c)).astype(bf16)
    o_ref[0] = jnp.dot(s, w_ref[0].astype(bf16), preferred_element_type=f32) + b_ref[0]


def modulation_all(cond, w_mod, b_mod, tn=2048):
    n_layers, d, n = w_mod.shape
    return pl.pallas_call(
        _mod_kernel,
        grid=(n_layers, n // tn),
        in_specs=[
            pl.BlockSpec((8, d), lambda l, j: (0, 0)),
            pl.BlockSpec((1, d, tn), lambda l, j: (l, 0, j)),
            pl.BlockSpec((1, 1, tn), lambda l, j: (l, 0, j)),
        ],
        out_specs=pl.BlockSpec((1, 8, tn), lambda l, j: (l, 0, j)),
        out_shape=jax.ShapeDtypeStruct((n_layers, 8, n), f32),
        compiler_params=_cparams(("parallel", "parallel")),
        name="modulation",
    )(cond, w_mod, b_mod.reshape(n_layers, 1, n))


def _in_kernel(*refs, n_x, tiles_a, tn):
    x_refs = refs[:n_x]
    mod_ref, g_ref, w_ref, o_ref = refs[n_x:]
    m = mod_ref[0]
    y = _rms(_rows(x_refs, tiles_a), g_ref[...])
    h = (y * (1.0 + m[1:2]) + m[0:1]).astype(bf16)
    for c in range(w_ref.shape[1] // tn):
        cols = slice(c * tn, (c + 1) * tn)
        o_ref[:, cols] = jnp.dot(h, w_ref[:, cols], preferred_element_type=f32)


def in_projection(xs, mod, g, w, layer, row_of_tile, tm, tn=512):
    m_total = sum(x.shape[0] for x in xs)
    d = xs[0].shape[1]
    n = w.shape[2]
    return pl.pallas_call(
        functools.partial(_in_kernel, n_x=len(xs), tiles_a=xs[0].shape[0] // tm, tn=tn),
        grid=(m_total // tm,),
        in_specs=[
            *_row_specs(xs, tm, d),
            pl.BlockSpec((None, 1, 6, d), lambda i: (layer, row_of_tile(i), 0, 0)),
            pl.BlockSpec((None, 1, d), lambda i: (layer, 0, 0)),
            pl.BlockSpec((None, d, n), lambda i: (layer, 0, 0), pipeline_mode=pl.Buffered(1)),
        ],
        out_specs=pl.BlockSpec((tm, n), lambda i: (i, 0)),
        out_shape=jax.ShapeDtypeStruct((m_total, n), f32),
        compiler_params=_cparams(("arbitrary",)),
        name="in_projection",
    )(*xs, mod, g, w)


def _first_head(shape):
    return lax.broadcasted_iota(jnp.int32, shape, len(shape) - 1) < HEAD_DIM


def _attn_ctx_kernel(*refs, n_prev, write_caches):
    q_ref, k_ref, v_ref = refs[:3]
    prev = refs[3:3 + 2 * n_prev]
    outs = refs[3 + 2 * n_prev:]
    o_ref = outs[0]
    scale = HEAD_DIM ** -0.5
    nt = (((1,), (1,)), ((), ()))
    seq = q_ref.shape[0]
    lanes = 2 * HEAD_DIM
    first = _first_head((seq, lanes))
    keeps = (first, jnp.logical_not(first))

    def scores_of(p):
        cols = slice(p * lanes, (p + 1) * lanes)
        q = q_ref[:, cols].astype(bf16)
        k = k_ref[:, cols]
        v = v_ref[:, cols]
        if write_caches:
            kc_ref, vc_ref = outs[1:]
            layers = [(prev[2 * l][:, cols], prev[2 * l + 1][:, cols]) for l in range(n_prev)] + [(k, v)]
            for l, (kl, vl) in enumerate(layers):
                for h in range(2):
                    kc_ref[0, l, 2 * p + h] = kl[:, h * HEAD_DIM:(h + 1) * HEAD_DIM]
                    vc_ref[0, l, 2 * p + h] = vl[:, h * HEAD_DIM:(h + 1) * HEAD_DIM]
        kb = k.astype(bf16)
        zero = jnp.zeros_like(kb)
        return [lax.dot_general(q, jnp.where(keep, kb, zero), nt, preferred_element_type=f32) * scale
                for keep in keeps]

    def finish(p, scores):
        cols = slice(p * lanes, (p + 1) * lanes)
        vb = v_ref[:, cols].astype(bf16)
        zero = jnp.zeros_like(vb)
        num, inv = None, []
        for keep, s in zip(keeps, scores):
            e = jnp.exp(s - jnp.max(s, axis=-1, keepdims=True))
            inv.append(1.0 / jnp.sum(e, axis=-1, keepdims=True))
            part = jnp.dot(e.astype(bf16), jnp.where(keep, vb, zero), preferred_element_type=f32)
            num = part if num is None else num + part
        o_ref[:, cols] = (num * jnp.where(first, inv[0], inv[1])).astype(bf16)

    n_pairs = N_HEADS // 2
    pending = scores_of(0)
    for p in range(n_pairs):
        upcoming = scores_of(p + 1) if p + 1 < n_pairs else None
        finish(p, pending)
        pending = upcoming


def context_attention(z, batch, seq, prev_zs=(), write_caches=False):
    col = lambda blk: pl.BlockSpec((seq, ATTN_W), lambda b: (b, blk))
    out_specs = [pl.BlockSpec((seq, ATTN_W), lambda b: (b, 0))]
    out_shape = [jax.ShapeDtypeStruct((batch * seq, ATTN_W), bf16)]
    if write_caches:
        depth = len(prev_zs) + 1
        cache = pl.BlockSpec((1, depth, N_HEADS, seq, HEAD_DIM), lambda b: (b, 0, 0, 0, 0))
        cache_shape = jax.ShapeDtypeStruct((batch, depth, N_HEADS, seq, HEAD_DIM), f32)
        out_specs += [cache, cache]
        out_shape += [cache_shape, cache_shape]
    else:
        prev_zs = ()
    prev_args = [a for zp in prev_zs for a in (zp, zp)]
    return pl.pallas_call(
        functools.partial(_attn_ctx_kernel, n_prev=len(prev_zs), write_caches=write_caches),
        grid=(batch,),
        in_specs=[col(0), col(1), col(2)] + [col(1), col(2)] * len(prev_zs),
        out_specs=out_specs,
        out_shape=out_shape,
        compiler_params=_cparams(("parallel",)),
        name="context_attention",
    )(z, z, z, *prev_args)


def _attn_lat_kernel(q_ref, k_ref, v_ref, ck_ref, cv_ref, t_ref, o_ref, bias_scr, k_scr, v_scr, ck_scr, cv_scr,
                     *, rows, wr):
    scale = HEAD_DIM ** -0.5
    n_loc = wr * GRID_W
    for h in range(2):
        for d in range(wr):
            for j in range(wr):
                bias_scr[h, d, :, j * GRID_W:(j + 1) * GRID_W] = t_ref[h, j - d + WIN_ROWS_MAX - 1]

    kb = k_ref[...].astype(bf16)
    vb = v_ref[...].astype(bf16)
    first = _first_head(kb.shape)
    zero = jnp.zeros_like(kb)
    pad = jnp.zeros(ck_ref.shape[3:], bf16)
    for h in range(2):
        keep = first if h == 0 else jnp.logical_not(first)
        k_scr[h] = jnp.where(keep, kb, zero)
        v_scr[h] = jnp.where(keep, vb, zero)
        ck, cv = ck_ref[0, 0, h].astype(bf16), cv_ref[0, 0, h].astype(bf16)
        ck_scr[h] = jnp.concatenate([ck, pad] if h == 0 else [pad, ck], axis=-1)
        cv_scr[h] = jnp.concatenate([cv, pad] if h == 0 else [pad, cv], axis=-1)

    nt = (((1,), (1,)), ((), ()))
    br = LAT_BLOCK_ROWS
    nq = br * GRID_W
    first_q = _first_head((nq, 2 * HEAD_DIM))

    def block_rows(blk):
        return slice(blk * nq, (blk + 1) * nq)

    def windows_of(blk):
        out = []
        for i in range(br):
            r = blk * br + i
            rs = min(max(r - wr // 2, 0), rows - wr)
            out.append((slice(rs * GRID_W, rs * GRID_W + n_loc), r - rs))
        return out

    def scores_of(blk):
        q = q_ref[block_rows(blk), :].astype(bf16)
        out = []
        for h in range(2):
            s_ctx = lax.dot_general(q, ck_scr[h], nt, preferred_element_type=f32) * scale
            s_loc = [lax.dot_general(q[i * GRID_W:(i + 1) * GRID_W], k_scr[h, win, :], nt,
                                     preferred_element_type=f32) * scale + bias_scr[h, off]
                     for i, (win, off) in enumerate(windows_of(blk))]
            out.append((jnp.concatenate(s_loc, axis=0), s_ctx))
        return out

    def finish(blk, scores):
        probs, inv = [], []
        for s_loc, s_ctx in scores:
            m = jnp.maximum(jnp.max(s_loc, axis=-1, keepdims=True), jnp.max(s_ctx, axis=-1, keepdims=True))
            e_loc = jnp.exp(s_loc - m)
            e_ctx = jnp.exp(s_ctx - m)
            inv.append(1.0 / (jnp.sum(e_loc, axis=-1, keepdims=True) + jnp.sum(e_ctx, axis=-1, keepdims=True)))
            probs.append((e_loc.astype(bf16), e_ctx.astype(bf16)))
        num = None
        for h, (e_loc, e_ctx) in enumerate(probs):
            o_loc = [jnp.dot(e_loc[i * GRID_W:(i + 1) * GRID_W], v_scr[h, win, :], preferred_element_type=f32)
                     for i, (win, _) in enumerate(windows_of(blk))]
            part = jnp.dot(e_ctx, cv_scr[h], preferred_element_type=f32) + jnp.concatenate(o_loc, axis=0)
            num = part if num is None else num + part
        o_ref[block_rows(blk), :] = (num * jnp.where(first_q, inv[0], inv[1])).astype(bf16)

    pending = scores_of(0)
    for blk in range(rows // br):
        upcoming = scores_of(blk + 1) if blk + 1 < rows // br else None
        finish(blk, pending)
        pending = upcoming


def latent_bias_table(rpb):
    cols = np.arange(GRID_W)
    col_start = np.clip(cols - WIN_COLS // 2, 0, GRID_W - WIN_COLS)
    valid = (cols[None, :] >= col_start[:, None]) & (cols[None, :] < col_start[:, None] + WIN_COLS)
    dc = cols[None, :] - cols[:, None] + WIN_COLS - 1
    assert np.all((dc[valid] >= 0) & (dc[valid] < 2 * WIN_COLS - 1))
    onehot = ((np.arange(2 * WIN_COLS - 1)[:, None, None] == dc[None]) & valid[None]).astype(np.float32)
    t = jnp.einsum('hrd,dqk->hrqk', rpb.astype(f32), jnp.asarray(onehot), precision=lax.Precision.HIGHEST)
    return jnp.where(jnp.asarray(valid)[None, None], t, NEG_INF)


def latent_attention(z, row0, batch, seq, cache_k, cache_v, layer, tables):
    lanes = 2 * HEAD_DIM
    n_pairs = N_HEADS // 2
    rows = seq // GRID_W
    wr = min(WIN_ROWS_MAX, rows)
    past = cache_k.shape[3]
    blk0 = row0 // seq
    n_dr = 2 * WIN_ROWS_MAX - 1
    assert rows % LAT_BLOCK_ROWS == 0
    col = lambda off: pl.BlockSpec((seq, lanes), lambda b, p: (blk0 + b, off + p))
    ctx = pl.BlockSpec((1, 1, 2, past, HEAD_DIM), lambda b, p: (b, layer, p, 0, 0))
    return pl.pallas_call(
        functools.partial(_attn_lat_kernel, rows=rows, wr=wr),
        grid=(batch, n_pairs),
        in_specs=[col(0), col(n_pairs), col(2 * n_pairs), ctx, ctx,
                  pl.BlockSpec((None, 2, n_dr, GRID_W, GRID_W), lambda b, p: (layer, p, 0, 0, 0))],
        out_specs=pl.BlockSpec((seq, lanes), lambda b, p: (b, p)),
        out_shape=jax.ShapeDtypeStruct((batch * seq, ATTN_W), bf16),
        scratch_shapes=[pltpu.VMEM((2, wr, GRID_W, wr * GRID_W), f32),
                        pltpu.VMEM((2, seq, lanes), bf16), pltpu.VMEM((2, seq, lanes), bf16),
                        pltpu.VMEM((2, past, lanes), bf16), pltpu.VMEM((2, past, lanes), bf16)],
        compiler_params=_cparams(("parallel", "parallel")),
        name="latent_attention",
    )(z, z, z, cache_k, cache_v, tables)


def ssm_tables(a_re, a_im, log_dt, b_re, b_im, c_re, c_im):
    lc, p, m = SSM_CHUNK, SSM_STATE, SSM_GROUP
    half = SSM_PW // 2

    def slab_row(t):
        t = jnp.broadcast_to(t.astype(f32).reshape(2, 1, SSM_PAIRS, 2, p), (2, 2, SSM_PAIRS, 2, p))
        return t.transpose(2, 0, 1, 3, 4).reshape(SSM_PAIRS, SSM_PW)

    def embed(first, second):
        t = jnp.stack([first, second], axis=1).astype(f32)
        x = t.shape[-1]
        t = t.reshape(2, 2, SSM_PAIRS, 2, p, x).transpose(2, 3, 5, 0, 1, 4)
        keep = [(0, 0)] * 4
        t = jnp.stack([jnp.pad(t[:, 0], keep + [(0, p)]), jnp.pad(t[:, 1], keep + [(p, 0)])], axis=1)
        return t.reshape(SSM_PAIRS, 2 * x, SSM_PW)

    lr, li = slab_row(a_re), slab_row(a_im)
    dt = slab_row(jnp.broadcast_to(jnp.exp(log_dt.astype(f32))[..., None], a_re.shape))
    sr, si = lr * dt, li * dt
    k = jnp.arange(lc + 1, dtype=f32)[:, None, None]
    mag = jnp.exp(k * sr[None])
    ar, ai = mag * jnp.cos(k * si[None]), mag * jnp.sin(k * si[None])
    den = lr * lr + li * li
    qr = ((ar[1] - 1.0) * lr + ai[1] * li) / den
    qi = (ai[1] * lr - (ar[1] - 1.0) * li) / den
    b4, b4s = embed(b_re, b_im), embed(-b_im, b_re)
    bb4 = (qr[:, None] * b4 + qi[:, None] * b4s)[:, None]
    bb4s = (qr[:, None] * b4s - qi[:, None] * b4)[:, None]
    rev = np.arange(lc)[::-1]
    fwd = np.arange(lc)

    def powers(t, ks):
        ks = [int(v) for v in ks]
        parts, i = [], 0
        while i < len(ks):
            n = 1
            step = ks[i + 1] - ks[i] if i + 1 < len(ks) else 0
            assert abs(step) <= 1
            while i + n < len(ks) and ks[i + n] - ks[i + n - 1] == step:
                n += 1
            lo, hi = min(ks[i], ks[i + n - 1]), max(ks[i], ks[i + n - 1])
            run = t[lo:hi + 1]
            if step == 0:
                run = jnp.broadcast_to(run, (n,) + t.shape[1:])
            elif step < 0:
                run = run[::-1]
            parts.append(run)
            i += n
        return parts[0] if len(parts) == 1 else jnp.concatenate(parts, axis=0)

    def per_step(t, k_fwd, k_bwd):
        t = jnp.concatenate([powers(t, k_fwd)[..., :half], powers(t, k_bwd)[..., half:]], axis=-1)
        return t.transpose(1, 0, 2)[:, :, None, :]

    bx = bb4 * per_step(ar, rev, fwd) + bb4s * per_step(ai, rev, fwd)
    bx = bx.reshape(SSM_PAIRS, SSM_PW, SSM_PW)
    ct_re, ct_im = c_re.transpose(0, 1, 3, 2), c_im.transpose(0, 1, 3, 2)
    c4 = embed(ct_re, -ct_im)[:, None]
    c4s = embed(-ct_im, -ct_re)[:, None]
    cyt = c4 * per_step(ar, fwd + 1, lc - fwd) + c4s * per_step(ai, fwd + 1, lc - fwd)
    cyt = cyt.reshape(SSM_PAIRS, SSM_PW, SSM_PW)

    kt = c4 * per_step(ar, fwd, rev) + c4s * per_step(ai, fwd, rev)
    kt = kt.reshape(SSM_PAIRS, SSM_PW, SSM_PW)
    nt = (((2,), (2,)), ((0,), (0,)))
    k_fwd = lax.dot_general(bb4[:, 0, :, :half], kt[:, :, :half], nt, precision=lax.Precision.HIGH)
    k_bwd = lax.dot_general(bb4[:, 0, :, half:], kt[:, :, half:], nt, precision=lax.Precision.HIGH)
    edge = (lc - 1) * 2 * m
    strip = jnp.concatenate([k_bwd[:, :, :edge], k_bwd[:, :, edge:] + k_fwd[:, :, :2 * m], k_fwd[:, :, 2 * m:]], axis=-1)
    tz = jnp.stack([strip[:, :, (lc - 1 - s) * 2 * m:(lc - 1 - s) * 2 * m + SSM_PW] for s in range(lc)], axis=1)
    tz = tz.reshape(SSM_PAIRS, SSM_PW, SSM_PW)

    sw = 2 * p
    a16p = jnp.concatenate([ar[lc][:, :sw], ai[lc][:, sw:half], ar[lc][:, half:half + sw], ai[lc][:, half + sw:]],
                           axis=-1)[:, None, :]
    return tz.astype(bf16), bx.astype(bf16), cyt.astype(bf16), a16p


def _ssm_kernel(u_ref, tz_ref, bx_ref, cyt_ref, a_ref, h0_ref, y_ref, fin_ref, x_scr, s_scr, y_scr, *, n_seq, n_chunks):
    r = n_seq * n_chunks
    pc = 2 * SSM_GROUP
    sw = 2 * SSM_STATE
    xs = [u_ref[pl.ds(s, r, stride=SSM_CHUNK), :] for s in range(SSM_CHUNK)]
    for kq in range(SSM_QUAD):
        u = jnp.concatenate([x[:, kq * pc:(kq + 1) * pc] for x in xs], axis=-1).astype(bf16)
        x = jnp.dot(u, bx_ref[kq], preferred_element_type=f32)
        for i in range(4):
            x_scr[4 * kq + i] = x[:, i * sw:(i + 1) * sw]
        y_scr[:, kq * SSM_PW:(kq + 1) * SSM_PW] = jnp.dot(u, tz_ref[kq], preferred_element_type=f32)

    def scan(direction):
        offs = [kq * SSM_PW + 2 * direction * sw for kq in range(SSM_QUAD)]
        slabs = [4 * kq + 2 * direction for kq in range(SSM_QUAD)]
        coef = [(a_ref[kq][:, 2 * direction * sw:2 * direction * sw + sw],
                 a_ref[kq][:, 2 * direction * sw + sw:2 * direction * sw + 2 * sw]) for kq in range(SSM_QUAD)]

        def body(i, carry):
            c = i if direction == 0 else n_chunks - 1 - i
            rows = pl.ds(c, n_seq, stride=n_chunks)
            out = []
            for kq in range(SSM_QUAD):
                sr, si = carry[2 * kq], carry[2 * kq + 1]
                ar, ai = coef[kq]
                re, im = slabs[kq], slabs[kq] + 1
                s_scr[re, rows, :] = sr
                s_scr[im, rows, :] = si
                xr = x_scr[re, rows, :]
                xi = x_scr[im, rows, :]
                out += [ar * sr - ai * si + xr, ar * si + ai * sr + xi]
            return tuple(out)

        init = []
        for lo in offs:
            init += [h0_ref[:, lo:lo + sw], h0_ref[:, lo + sw:lo + 2 * sw]]
        fin = lax.fori_loop(0, n_chunks, body, tuple(init))
        for kq, lo in enumerate(offs):
            fin_ref[:, lo:lo + sw] = fin[2 * kq]
            fin_ref[:, lo + sw:lo + 2 * sw] = fin[2 * kq + 1]

    scan(0)
    scan(1)
    ys = []
    for kq in range(SSM_QUAD):
        cols = slice(kq * SSM_PW, (kq + 1) * SSM_PW)
        s_in = jnp.concatenate([s_scr[4 * kq + i] for i in range(4)], axis=-1).astype(bf16)
        ys.append(y_scr[:, cols] + lax.dot_general(s_in, cyt_ref[kq], (((1,), (1,)), ((), ())),
                                                   preferred_element_type=f32))
    for t in range(SSM_CHUNK):
        y_ref[pl.ds(t, r, stride=SSM_CHUNK), :] = jnp.concatenate([y[:, t * pc:(t + 1) * pc] for y in ys], axis=-1)


def ssm_scan(z, row0, n_seq, seq, tables, layer, h0):
    tz, bx, cyt, a16p = tables
    n_chunks = seq // SSM_CHUNK
    rows = n_seq * seq
    r = n_seq * n_chunks
    lanes = SSM_QUAD * 2 * SSM_GROUP
    u_blk = (3 * ATTN_W) // lanes
    qw = SSM_QUAD * SSM_PW
    table = pl.BlockSpec((None, SSM_QUAD, SSM_PW, SSM_PW), lambda q: (layer, q, 0, 0))
    return pl.pallas_call(
        functools.partial(_ssm_kernel, n_seq=n_seq, n_chunks=n_chunks),
        grid=(SSM_PAIRS // SSM_QUAD,),
        in_specs=[pl.BlockSpec((rows, lanes), lambda q: (row0 // rows, u_blk + q)),
                  table, table, table,
                  pl.BlockSpec((None, SSM_QUAD, 1, SSM_PW), lambda q: (layer, q, 0, 0)),
                  pl.BlockSpec((n_seq, qw), lambda q: (0, q))],
        out_specs=[pl.BlockSpec((rows, lanes), lambda q: (0, q)),
                   pl.BlockSpec((n_seq, qw), lambda q: (0, q))],
        out_shape=[jax.ShapeDtypeStruct((rows, SSM_W), f32),
                   jax.ShapeDtypeStruct((n_seq, SSM_PAIRS * SSM_PW), f32)],
        scratch_shapes=[pltpu.VMEM((4 * SSM_QUAD, r, 2 * SSM_STATE), f32),
                        pltpu.VMEM((4 * SSM_QUAD, r, 2 * SSM_STATE), f32), pltpu.VMEM((r, qw), f32)],
        compiler_params=_cparams(("parallel",)),
        name="ssm_scan",
    )(z, tz, bx, cyt, a16p, h0)


def state_to_slabs(st):
    batch = st.shape[0]
    t = st.reshape(batch, 2, SSM_PAIRS, 2, SSM_STATE, 2).transpose(0, 2, 1, 5, 3, 4)
    return t.reshape(batch, SSM_PAIRS * SSM_PW)


def state_from_slabs(fin):
    batch = fin.shape[0]
    t = fin.reshape(batch, SSM_PAIRS, 2, 2, 2, SSM_STATE)
    return t.transpose(0, 2, 1, 4, 5, 3).reshape(batch, 2, SSM_GROUPS, SSM_STATE, 2)


def _glu_kernel(ya_ref, yb_ref, u_ref, d_ref, w_ref, b_ref, o_ref, *, tiles_a):
    y = _pick(pl.program_id(0) < tiles_a, ya_ref, yb_ref) + d_ref[...] * u_ref[...]
    z = jnp.dot(jax.nn.gelu(y).astype(bf16), w_ref[...], preferred_element_type=f32) + b_ref[...]
    o_ref[...] = (z[:, :SSM_W] * jax.nn.sigmoid(z[:, SSM_W:])).astype(bf16)


def ssm_glu(y_a, y_b, z, d_skip, w_glu, b_glu, layer, tm):
    tiles_a, tiles_b = y_a.shape[0] // tm, y_b.shape[0] // tm
    m_total = y_a.shape[0] + y_b.shape[0]
    u_blk = (3 * ATTN_W) // SSM_W
    return pl.pallas_call(
        functools.partial(_glu_kernel, tiles_a=tiles_a),
        grid=(tiles_a + tiles_b,),
        in_specs=[
            *_two_part(tm, SSM_W, tiles_a, tiles_b),
            pl.BlockSpec((tm, SSM_W), lambda i: (i, u_blk)),
            pl.BlockSpec((None, 1, SSM_W), lambda i: (layer, 0, 0)),
            pl.BlockSpec((None, SSM_W, 2 * SSM_W), lambda i: (layer, 0, 0)),
            pl.BlockSpec((None, 1, 2 * SSM_W), lambda i: (layer, 0, 0)),
        ],
        out_specs=pl.BlockSpec((tm, SSM_W), lambda i: (i, 0)),
        out_shape=jax.ShapeDtypeStruct((m_total, SSM_W), bf16),
        compiler_params=_cparams(("arbitrary",)),
        name="ssm_glu",
    )(y_a, y_b, z, d_skip, w_glu, b_glu)


def _pool_kernel(p_ref, w_ref, sc_ref, o_ref, pad_ref, *, seq):
    zeros = jnp.zeros((POOL_PAD, POOL_GROUP), f32)
    body = slice(POOL_PAD, POOL_PAD + seq)
    ext = slice(POOL_PAD, POOL_PAD + seq + POOL_TAIL)
    t = lax.broadcasted_iota(jnp.int32, (seq, 1), 0)
    for g, win in enumerate(POOL_WINDOWS):
        cols = slice(g * POOL_GROUP, (g + 1) * POOL_GROUP)
        pad_ref[0:POOL_PAD, cols] = zeros
        pad_ref[POOL_PAD + seq:, cols] = zeros
        x = p_ref[:, cols]
        pad_ref[body, cols] = x
        w = 1
        while w < win:
            pad_ref[ext, cols] = pad_ref[ext, cols] + pad_ref[POOL_PAD - w:POOL_PAD - w + seq + POOL_TAIL, cols]
            w *= 2
        ahead = win // 2 - 1
        total = pad_ref[POOL_PAD + ahead:POOL_PAD + ahead + seq, cols]
        lo = jnp.clip(t - win // 2, 0, seq)
        hi = jnp.clip(t - win // 2 + win, 0, seq)
        mixed = total / (hi - lo).astype(f32) - x
        out = jnp.dot(mixed.astype(bf16), w_ref[g].astype(bf16), preferred_element_type=f32)
        o_ref[:, cols] = (out * sc_ref[:, cols]).astype(bf16)


def pool_mixer(z, row0, batch, seq, w_pool, pool_scale, layer):
    p_blk = (3 * ATTN_W + SSM_W) // POOL_W
    blk0 = row0 // seq
    return pl.pallas_call(
        functools.partial(_pool_kernel, seq=seq),
        grid=(batch,),
        in_specs=[
            pl.BlockSpec((seq, POOL_W), lambda b: (blk0 + b, p_blk)),
            pl.BlockSpec((None, len(POOL_WINDOWS), POOL_GROUP, POOL_GROUP), lambda b: (layer, 0, 0, 0)),
            pl.BlockSpec((None, 1, POOL_W), lambda b: (layer, 0, 0)),
        ],
        out_specs=pl.BlockSpec((seq, POOL_W), lambda b: (b, 0)),
        out_shape=jax.ShapeDtypeStruct((batch * seq, POOL_W), bf16),
        scratch_shapes=[pltpu.VMEM((seq + 2 * POOL_PAD, POOL_W), f32)],
        compiler_params=_cparams(("parallel",)),
        name="pool_mixer",
    )(z, w_pool, pool_scale)


def _out_kernel(aa_ref, ab_ref, s_ref, pa_ref, pb_ref, *refs, n_x, tiles_a, tn):
    x_refs = refs[:n_x]
    mod_ref, w_ref, o_ref = refs[n_x:]
    first = pl.program_id(0) < tiles_a
    a = _pick(first, aa_ref, ab_ref)
    s = s_ref[...]
    p = _pick(first, pa_ref, pb_ref)
    x = _rows(x_refs, tiles_a)
    gate = mod_ref[0][2:3]
    for c in range(w_ref.shape[1] // tn):
        cols = slice(c * tn, (c + 1) * tn)
        acc = jnp.dot(a, w_ref[0:ATTN_W, cols], preferred_element_type=f32)
        acc = acc + jnp.dot(s, w_ref[ATTN_W:ATTN_W + SSM_W, cols], preferred_element_type=f32)
        acc = acc + jnp.dot(p, w_ref[ATTN_W + SSM_W:, cols], preferred_element_type=f32)
        o_ref[:, cols] = x[:, cols] + gate[:, cols] * acc


def out_projection(a_a, a_b, s, p_a, p_b, xs, mod, w, layer, row_of_tile, tm, tn=512):
    m_total = sum(x.shape[0] for x in xs)
    d = xs[0].shape[1]
    tiles_a, tiles_b = a_a.shape[0] // tm, a_b.shape[0] // tm
    assert len(xs) == 1 or xs[0].shape[0] == a_a.shape[0]
    return pl.pallas_call(
        functools.partial(_out_kernel, n_x=len(xs), tiles_a=tiles_a, tn=tn),
        grid=(m_total // tm,),
        in_specs=[
            *_two_part(tm, ATTN_W, tiles_a, tiles_b),
            pl.BlockSpec((tm, SSM_W), lambda i: (i, 0)),
            *_two_part(tm, POOL_W, tiles_a, tiles_b),
            *_row_specs(xs, tm, d),
            pl.BlockSpec((None, 1, 6, d), lambda i: (layer, row_of_tile(i), 0, 0)),
            pl.BlockSpec((None, d, d), lambda i: (layer, 0, 0), pipeline_mode=pl.Buffered(1)),
        ],
        out_specs=pl.BlockSpec((tm, d), lambda i: (i, 0)),
        out_shape=jax.ShapeDtypeStruct((m_total, d), f32),
        compiler_params=_cparams(("arbitrary",)),
        name="out_projection",
    )(a_a, a_b, s, p_a, p_b, *xs, mod, w)


def _ffn_kernel(x_ref, mod_ref, g_ref, wg_ref, wu_ref, wo_ref, fg_ref, *refs, n_out, tiles_a, final_norm):
    o_refs = refs[:n_out]
    h_ref, acc_ref = refs[n_out:]
    i, j = pl.program_id(0), pl.program_id(1)
    last = j == pl.num_programs(1) - 1

    def chunk():
        h = h_ref[...]
        half = wg_ref.shape[1] // 2
        acts = []
        for c in range(2):
            cs = slice(c * half, (c + 1) * half)
            gate = jnp.dot(h, wg_ref[:, cs], preferred_element_type=f32)
            up = jnp.dot(h, wu_ref[:, cs], preferred_element_type=f32)
            acts.append((gate * jax.nn.sigmoid(gate) * up).astype(bf16))
        down = (jnp.dot(acts[0], wo_ref[:half, :], preferred_element_type=f32)
                + jnp.dot(acts[1], wo_ref[half:, :], preferred_element_type=f32))
        return mod_ref[0][5:6] * down

    @pl.when(j == 0)
    def _():
        m = mod_ref[0]
        x = x_ref[...]
        h_ref[...] = (_rms(x, g_ref[...]) * (1.0 + m[4:5]) + m[3:4]).astype(bf16)
        acc_ref[...] = x + chunk()

    @pl.when((j > 0) & jnp.logical_not(last))
    def _():
        acc_ref[...] += chunk()

    def finish(o_ref):
        def run():
            y = acc_ref[...] + chunk()
            o_ref[...] = _rms(y, fg_ref[...]) if final_norm else y
        return run

    if n_out == 1:
        pl.when(last)(finish(o_refs[0]))
    else:
        pl.when(last & (i < tiles_a))(finish(o_refs[0]))
        pl.when(last & (i >= tiles_a))(finish(o_refs[1]))


def ffn(x, mod, g, w_in, w_out, final_g, layer, row_of_tile, tm, th, final_norm, split_rows=None):
    m_total, d = x.shape
    hidden = w_out.shape[1]
    nh = hidden // th
    if split_rows is None:
        tiles_a = m_total // tm
        out_specs = [pl.BlockSpec((tm, d), lambda i, j: (i, 0))]
        out_shape = [jax.ShapeDtypeStruct((m_total, d), f32)]
    else:
        tiles_a = split_rows // tm
        out_specs = list(_two_part(tm, d, tiles_a, m_total // tm - tiles_a))
        out_shape = [jax.ShapeDtypeStruct((split_rows, d), f32), jax.ShapeDtypeStruct((m_total - split_rows, d), f32)]
    return pl.pallas_call(
        functools.partial(_ffn_kernel, n_out=len(out_specs), tiles_a=tiles_a, final_norm=final_norm),
        grid=(m_total // tm, nh),
        in_specs=[
            pl.BlockSpec((tm, d), lambda i, j: (i, 0)),
            pl.BlockSpec((None, 1, 6, d), lambda i, j: (layer, row_of_tile(i), 0, 0)),
            pl.BlockSpec((None, 1, d), lambda i, j: (layer, 0, 0)),
            pl.BlockSpec((None, d, th), lambda i, j: (layer, 0, j)),
            pl.BlockSpec((None, d, th), lambda i, j: (layer, 0, nh + j)),
            pl.BlockSpec((None, th, d), lambda i, j: (layer, j, 0)),
            pl.BlockSpec((1, d), lambda i, j: (0, 0)),
        ],
        out_specs=out_specs,
        out_shape=out_shape,
        scratch_shapes=[pltpu.VMEM((tm, d), bf16), pltpu.VMEM((tm, d), f32)],
        compiler_params=_cparams(("arbitrary", "arbitrary")),
        name="ffn",
    )(x, mod, g, w_in, w_in, w_out, final_g)


def kernel(x_prompt, x_sample, c, cache_k, cache_v, state_ssm, c_ctx, w_mod, b_mod, norm1_g, norm2_g, w_in,
           attn_rpb, ssm_a_re, ssm_a_im, ssm_log_dt, ssm_b_re, ssm_b_im, ssm_c_re, ssm_c_im, ssm_d, ssm_w_glu,
           ssm_b_glu, pool_w, pool_scale, w_out, ffn_w_in, ffn_w_out, final_norm_g):
    batch, seq, d = x_prompt.shape
    dec_batch, dec_seq, _ = x_sample.shape
    depth = w_in.shape[0]
    m_ctx = batch * seq
    tm = 512
    assert m_ctx % tm == 0 and dec_seq % tm == 0 and dec_batch + 1 <= 8

    def row_of_tile(i):
        return jnp.where(i < m_ctx // tm, 0, 1 + (i - m_ctx // tm) // (dec_seq // tm))

    cond = jnp.concatenate([c_ctx[None, :], c, jnp.zeros((8 - 1 - dec_batch, d), f32)], axis=0)
    mod = modulation_all(cond, w_mod, b_mod)
    mod = mod[:, :1 + dec_batch].reshape(depth, 1 + dec_batch, 6, d)

    w_in_b, w_out_b = w_in.astype(bf16), w_out.astype(bf16)
    ffn_w_in_b, ffn_w_out_b, w_glu_b = ffn_w_in.astype(bf16), ffn_w_out.astype(bf16), ssm_w_glu.astype(bf16)
    row = lambda t: t[:, None, :]
    tables = jax.vmap(ssm_tables)(ssm_a_re, ssm_a_im, ssm_log_dt, ssm_b_re, ssm_b_im, ssm_c_re, ssm_c_im)
    bias_tables = jax.vmap(latent_bias_table)(attn_rpb)

    xs = (x_prompt.reshape(m_ctx, d), x_sample.reshape(dec_batch * dec_seq, d))
    h0_ctx = jnp.zeros((batch, SSM_PAIRS * SSM_PW), f32)

    zs, st_out = [], []
    for l in range(depth):
        last = l == depth - 1
        z = in_projection(xs, mod, row(norm1_g), w_in_b, l, row_of_tile, tm)

        a_ctx, *caches = context_attention(z, batch, seq, zs, write_caches=last)
        zs.append(z)
        a_lat = latent_attention(z, m_ctx, dec_batch, dec_seq, cache_k, cache_v, l, bias_tables)

        y_ctx, fin = ssm_scan(z, 0, batch, seq, tables, l, h0_ctx)
        y_lat, _ = ssm_scan(z, m_ctx, dec_batch, dec_seq, tables, l, state_to_slabs(state_ssm[:, l]))
        st_out.append(state_from_slabs(fin))
        s_out = ssm_glu(y_ctx, y_lat, z, row(ssm_d), w_glu_b, row(ssm_b_glu), l, tm)

        p_ctx = pool_mixer(z, 0, batch, seq, pool_w, row(pool_scale), l)
        p_lat = pool_mixer(z, m_ctx, dec_batch, dec_seq, pool_w, row(pool_scale), l)

        x = out_projection(a_ctx, a_lat, s_out, p_ctx, p_lat, xs, mod, w_out_b, l, row_of_tile, tm)
        xs = ffn(x, mod, row(norm2_g), ffn_w_in_b, ffn_w_out_b, final_norm_g[None], l, row_of_tile, tm, 512,
                 final_norm=last, split_rows=m_ctx if last else None)

    y_prompt = xs[0].reshape(batch, seq, d)
    y_sample = xs[1].reshape(dec_batch, dec_seq, d)
    return (y_prompt, y_sample, caches[0], caches[1], jnp.stack(st_out, axis=1))
```

```python
import functools

import jax
import jax.numpy as jnp
import numpy as np
from jax import lax
from jax.experimental import pallas as pl
from jax.experimental.pallas import tpu as pltpu

f32 = jnp.float32
bf16 = jnp.bfloat16

D_MODEL = 2048
N_HEADS = 16
HEAD_DIM = 64
ATTN_W = N_HEADS * HEAD_DIM
SSM_W = 512
SSM_GROUP = 16
SSM_GROUPS = 32
SSM_STATE = 64
POOL_W = 512
POOL_WINDOWS = (2, 4, 8, 16)
POOL_GROUP = 128
POOL_PAD = 16
POOL_TAIL = 8
IN_W = 3 * ATTN_W + SSM_W + POOL_W
GRID_W = 64
WIN_ROWS_MAX = 8
WIN_COLS = 16
RMS_EPS = 1e-6
NEG_INF = -1e30

LAT_BLOCK_ROWS = 4
SSM_CHUNK = 16
SSM_PAIRS = SSM_GROUPS // 2
SSM_QUAD = 4
SSM_PW = 2 * SSM_CHUNK * SSM_GROUP
SSM_SCAN_UNROLL = 16

VMEM_LIMIT = 56 * 1024 * 1024


def _cparams(sem):
    return pltpu.CompilerParams(dimension_semantics=sem, vmem_limit_bytes=VMEM_LIMIT)


def _rms(x, g):
    return x * lax.rsqrt(jnp.mean(x * x, axis=-1, keepdims=True) + RMS_EPS) * g


def _two_part(tm, width, tiles_a, tiles_b, col=lambda *_: 0):
    return (pl.BlockSpec((tm, width), lambda i, *r: (jnp.minimum(i, tiles_a - 1), col(*r))),
            pl.BlockSpec((tm, width), lambda i, *r: (jnp.clip(i - tiles_a, 0, tiles_b - 1), col(*r))))


def _row_specs(parts, tm, width, col=lambda *_: 0):
    if len(parts) == 1:
        return (pl.BlockSpec((tm, width), lambda i, *r: (i, col(*r))),)
    return _two_part(tm, width, parts[0].shape[0] // tm, parts[1].shape[0] // tm, col)


def _pick(first, a_ref, b_ref):
    return jnp.where(first, a_ref[...], b_ref[...])


def _rows(refs, tiles_a):
    if len(refs) == 1:
        return refs[0][...]
    return _pick(pl.program_id(0) < tiles_a, *refs)


def _mod_kernel(c_ref, w_ref, b_ref, o_ref):
    c = c_ref[...]
    s = (c * jax.nn.sigmoid(c)).astype(bf16)
    o_ref[0] = jnp.dot(s, w_ref[0].astype(bf16), preferred_element_type=f32) + b_ref[0]


def modulation_all(cond, w_mod, b_mod, tn=2048):
    n_layers, d, n = w_mod.shape
    return pl.pallas_call(
        _mod_kernel,
        grid=(n_layers, n // tn),
        in_specs=[
            pl.BlockSpec((8, d), lambda l, j: (0, 0)),
            pl.BlockSpec((1, d, tn), lambda l, j: (l, 0, j)),
            pl.BlockSpec((1, 1, tn), lambda l, j: (l, 0, j)),
        ],
        out_specs=pl.BlockSpec((1, 8, tn), lambda l, j: (l, 0, j)),
        out_shape=jax.ShapeDtypeStruct((n_layers, 8, n), f32),
        compiler_params=_cparams(("parallel", "parallel")),
        name="modulation",
    )(cond, w_mod, b_mod.reshape(n_layers, 1, n))


def _in_kernel(*refs, n_x, tiles_a, tn):
    x_refs = refs[:n_x]
    mod_ref, g_ref, w_ref, o_ref = refs[n_x:]
    m = mod_ref[0]
    y = _rms(_rows(x_refs, tiles_a), g_ref[...])
    h = (y * (1.0 + m[1:2]) + m[0:1]).astype(bf16)
    for c in range(w_ref.shape[1] // tn):
        cols = slice(c * tn, (c + 1) * tn)
        o_ref[:, cols] = jnp.dot(h, w_ref[:, cols], preferred_element_type=f32)


def in_projection(xs, mod, g, w, layer, row_of_tile, tm, tn=512):
    m_total = sum(x.shape[0] for x in xs)
    d = xs[0].shape[1]
    n = w.shape[2]
    return pl.pallas_call(
        functools.partial(_in_kernel, n_x=len(xs), tiles_a=xs[0].shape[0] // tm, tn=tn),
        grid=(m_total // tm,),
        in_specs=[
            *_row_specs(xs, tm, d),
            pl.BlockSpec((None, 1, 6, d), lambda i: (layer, row_of_tile(i), 0, 0)),
            pl.BlockSpec((None, 1, d), lambda i: (layer, 0, 0)),
            pl.BlockSpec((None, d, n), lambda i: (layer, 0, 0), pipeline_mode=pl.Buffered(1)),
        ],
        out_specs=pl.BlockSpec((tm, n), lambda i: (i, 0)),
        out_shape=jax.ShapeDtypeStruct((m_total, n), f32),
        compiler_params=_cparams(("arbitrary",)),
        name="in_projection",
    )(*xs, mod, g, w)


def _first_head(shape):
    return lax.broadcasted_iota(jnp.int32, shape, len(shape) - 1) < HEAD_DIM


def _attn_ctx_kernel(*refs, n_prev, write_caches):
    q_ref, k_ref, v_ref = refs[:3]
    prev = refs[3:3 + 2 * n_prev]
    outs = refs[3 + 2 * n_prev:]
    o_ref = outs[0]
    scale = HEAD_DIM ** -0.5
    nt = (((1,), (1,)), ((), ()))
    seq = q_ref.shape[0]
    lanes = 2 * HEAD_DIM
    first = _first_head((seq, lanes))
    keeps = (first, jnp.logical_not(first))

    def scores_of(p):
        cols = slice(p * lanes, (p + 1) * lanes)
        q = q_ref[:, cols].astype(bf16)
        k = k_ref[:, cols]
        v = v_ref[:, cols]
        if write_caches:
            kc_ref, vc_ref = outs[1:]
            layers = [(prev[2 * l][:, cols], prev[2 * l + 1][:, cols]) for l in range(n_prev)] + [(k, v)]
            for l, (kl, vl) in enumerate(layers):
                for h in range(2):
                    kc_ref[0, l, 2 * p + h] = kl[:, h * HEAD_DIM:(h + 1) * HEAD_DIM]
                    vc_ref[0, l, 2 * p + h] = vl[:, h * HEAD_DIM:(h + 1) * HEAD_DIM]
        kb = k.astype(bf16)
        zero = jnp.zeros_like(kb)
        return [lax.dot_general(q, jnp.where(keep, kb, zero), nt, preferred_element_type=f32) * scale
                for keep in keeps]

    def finish(p, scores):
        cols = slice(p * lanes, (p + 1) * lanes)
        vb = v_ref[:, cols].astype(bf16)
        zero = jnp.zeros_like(vb)
        num, inv = None, []
        for keep, s in zip(keeps, scores):
            e = jnp.exp(s - jnp.max(s, axis=-1, keepdims=True))
            inv.append(1.0 / jnp.sum(e, axis=-1, keepdims=True))
            part = jnp.dot(e.astype(bf16), jnp.where(keep, vb, zero), preferred_element_type=f32)
            num = part if num is None else num + part
        o_ref[:, cols] = (num * jnp.where(first, inv[0], inv[1])).astype(bf16)

    n_pairs = N_HEADS // 2
    pending = scores_of(0)
    for p in range(n_pairs):
        upcoming = scores_of(p + 1) if p + 1 < n_pairs else None
        finish(p, pending)
        pending = upcoming


def context_attention(z, batch, seq, prev_zs=(), write_caches=False):
    col = lambda blk: pl.BlockSpec((seq, ATTN_W), lambda b: (b, blk))
    out_specs = [pl.BlockSpec((seq, ATTN_W), lambda b: (b, 0))]
    out_shape = [jax.ShapeDtypeStruct((batch * seq, ATTN_W), bf16)]
    if write_caches:
        depth = len(prev_zs) + 1
        cache = pl.BlockSpec((1, depth, N_HEADS, seq, HEAD_DIM), lambda b: (b, 0, 0, 0, 0))
        cache_shape = jax.ShapeDtypeStruct((batch, depth, N_HEADS, seq, HEAD_DIM), f32)
        out_specs += [cache, cache]
        out_shape += [cache_shape, cache_shape]
    else:
        prev_zs = ()
    prev_args = [a for zp in prev_zs for a in (zp, zp)]
    return pl.pallas_call(
        functools.partial(_attn_ctx_kernel, n_prev=len(prev_zs), write_caches=write_caches),
        grid=(batch,),
        in_specs=[col(0), col(1), col(2)] + [col(1), col(2)] * len(prev_zs),
        out_specs=out_specs,
        out_shape=out_shape,
        compiler_params=_cparams(("parallel",)),
        name="context_attention",
    )(z, z, z, *prev_args)


def _attn_lat_kernel(q_ref, k_ref, v_ref, ck_ref, cv_ref, t_ref, o_ref, bias_scr, k_scr, v_scr, ck_scr, cv_scr,
                     *, rows, wr):
    scale = HEAD_DIM ** -0.5
    n_loc = wr * GRID_W
    for h in range(2):
        for d in range(wr):
            for j in range(wr):
                bias_scr[h, d, :, j * GRID_W:(j + 1) * GRID_W] = t_ref[h, j - d + WIN_ROWS_MAX - 1]

    kb = k_ref[...].astype(bf16)
    vb = v_ref[...].astype(bf16)
    first = _first_head(kb.shape)
    zero = jnp.zeros_like(kb)
    pad = jnp.zeros(ck_ref.shape[3:], bf16)
    for h in range(2):
        keep = first if h == 0 else jnp.logical_not(first)
        k_scr[h] = jnp.where(keep, kb, zero)
        v_scr[h] = jnp.where(keep, vb, zero)
        ck, cv = ck_ref[0, 0, h].astype(bf16), cv_ref[0, 0, h].astype(bf16)
        ck_scr[h] = jnp.concatenate([ck, pad] if h == 0 else [pad, ck], axis=-1)
        cv_scr[h] = jnp.concatenate([cv, pad] if h == 0 else [pad, cv], axis=-1)

    nt = (((1,), (1,)), ((), ()))
    br = LAT_BLOCK_ROWS
    nq = br * GRID_W
    first_q = _first_head((nq, 2 * HEAD_DIM))

    def block_rows(blk):
        return slice(blk * nq, (blk + 1) * nq)

    def windows_of(blk):
        out = []
        for i in range(br):
            r = blk * br + i
            rs = min(max(r - wr // 2, 0), rows - wr)
            out.append((slice(rs * GRID_W, rs * GRID_W + n_loc), r - rs))
        return out

    def scores_of(blk):
        q = q_ref[block_rows(blk), :].astype(bf16)
        out = []
        for h in range(2):
            s_ctx = lax.dot_general(q, ck_scr[h], nt, preferred_element_type=f32) * scale
            s_loc = [lax.dot_general(q[i * GRID_W:(i + 1) * GRID_W], k_scr[h, win, :], nt,
                                     preferred_element_type=f32) * scale + bias_scr[h, off]
                     for i, (win, off) in enumerate(windows_of(blk))]
            out.append((jnp.concatenate(s_loc, axis=0), s_ctx))
        return out

    def finish(blk, scores):
        probs, inv = [], []
        for s_loc, s_ctx in scores:
            m = jnp.maximum(jnp.max(s_loc, axis=-1, keepdims=True), jnp.max(s_ctx, axis=-1, keepdims=True))
            e_loc = jnp.exp(s_loc - m)
            e_ctx = jnp.exp(s_ctx - m)
            inv.append(1.0 / (jnp.sum(e_loc, axis=-1, keepdims=True) + jnp.sum(e_ctx, axis=-1, keepdims=True)))
            probs.append((e_loc.astype(bf16), e_ctx.astype(bf16)))
        num = None
        for h, (e_loc, e_ctx) in enumerate(probs):
            o_loc = [jnp.dot(e_loc[i * GRID_W:(i + 1) * GRID_W], v_scr[h, win, :], preferred_element_type=f32)
                     for i, (win, _) in enumerate(windows_of(blk))]
            part = jnp.dot(e_ctx, cv_scr[h], preferred_element_type=f32) + jnp.concatenate(o_loc, axis=0)
            num = part if num is None else num + part
        o_ref[block_rows(blk), :] = (num * jnp.where(first_q, inv[0], inv[1])).astype(bf16)

    pending = scores_of(0)
    for blk in range(rows // br):
        upcoming = scores_of(blk + 1) if blk + 1 < rows // br else None
        finish(blk, pending)
        pending = upcoming


def latent_bias_table(rpb):
    cols = np.arange(GRID_W)
    col_start = np.clip(cols - WIN_COLS // 2, 0, GRID_W - WIN_COLS)
    valid = (cols[None, :] >= col_start[:, None]) & (cols[None, :] < col_start[:, None] + WIN_COLS)
    dc = cols[None, :] - cols[:, None] + WIN_COLS - 1
    assert np.all((dc[valid] >= 0) & (dc[valid] < 2 * WIN_COLS - 1))
    onehot = ((np.arange(2 * WIN_COLS - 1)[:, None, None] == dc[None]) & valid[None]).astype(np.float32)
    t = jnp.einsum('hrd,dqk->hrqk', rpb.astype(f32), jnp.asarray(onehot), precision=lax.Precision.HIGHEST)
    return jnp.where(jnp.asarray(valid)[None, None], t, NEG_INF)


def latent_attention(z, row0, batch, seq, cache_k, cache_v, layer, tables):
    lanes = 2 * HEAD_DIM
    n_pairs = N_HEADS // 2
    rows = seq // GRID_W
    wr = min(WIN_ROWS_MAX, rows)
    past = cache_k.shape[3]
    blk0 = row0 // seq
    n_dr = 2 * WIN_ROWS_MAX - 1
    assert rows % LAT_BLOCK_ROWS == 0
    col = lambda off: pl.BlockSpec((seq, lanes), lambda b, p: (blk0 + b, off + p))
    ctx = pl.BlockSpec((1, 1, 2, past, HEAD_DIM), lambda b, p: (b, layer, p, 0, 0))
    return pl.pallas_call(
        functools.partial(_attn_lat_kernel, rows=rows, wr=wr),
        grid=(batch, n_pairs),
        in_specs=[col(0), col(n_pairs), col(2 * n_pairs), ctx, ctx,
                  pl.BlockSpec((None, 2, n_dr, GRID_W, GRID_W), lambda b, p: (layer, p, 0, 0, 0))],
        out_specs=pl.BlockSpec((seq, lanes), lambda b, p: (b, p)),
        out_shape=jax.ShapeDtypeStruct((batch * seq, ATTN_W), bf16),
        scratch_shapes=[pltpu.VMEM((2, wr, GRID_W, wr * GRID_W), f32),
                        pltpu.VMEM((2, seq, lanes), bf16), pltpu.VMEM((2, seq, lanes), bf16),
                        pltpu.VMEM((2, past, lanes), bf16), pltpu.VMEM((2, past, lanes), bf16)],
        compiler_params=_cparams(("parallel", "parallel")),
        name="latent_attention",
    )(z, z, z, cache_k, cache_v, tables)


def ssm_tables(a_re, a_im, log_dt, b_re, b_im, c_re, c_im):
    lc, p, m = SSM_CHUNK, SSM_STATE, SSM_GROUP
    half = SSM_PW // 2

    def slab_row(t):
        t = jnp.broadcast_to(t.astype(f32).reshape(2, 1, SSM_PAIRS, 2, p), (2, 2, SSM_PAIRS, 2, p))
        return t.transpose(2, 0, 1, 3, 4).reshape(SSM_PAIRS, SSM_PW)

    def embed(first, second):
        t = jnp.stack([first, second], axis=1).astype(f32)
        x = t.shape[-1]
        t = t.reshape(2, 2, SSM_PAIRS, 2, p, x).transpose(2, 3, 5, 0, 1, 4)
        keep = [(0, 0)] * 4
        t = jnp.stack([jnp.pad(t[:, 0], keep + [(0, p)]), jnp.pad(t[:, 1], keep + [(p, 0)])], axis=1)
        return t.reshape(SSM_PAIRS, 2 * x, SSM_PW)

    lr, li = slab_row(a_re), slab_row(a_im)
    dt = slab_row(jnp.broadcast_to(jnp.exp(log_dt.astype(f32))[..., None], a_re.shape))
    sr, si = lr * dt, li * dt
    k = jnp.arange(lc + 1, dtype=f32)[:, None, None]
    mag = jnp.exp(k * sr[None])
    ar, ai = mag * jnp.cos(k * si[None]), mag * jnp.sin(k * si[None])
    den = lr * lr + li * li
    qr = ((ar[1] - 1.0) * lr + ai[1] * li) / den
    qi = (ai[1] * lr - (ar[1] - 1.0) * li) / den
    b4, b4s = embed(b_re, b_im), embed(-b_im, b_re)
    bb4 = (qr[:, None] * b4 + qi[:, None] * b4s)[:, None]
    bb4s = (qr[:, None] * b4s - qi[:, None] * b4)[:, None]
    rev = np.arange(lc)[::-1]
    fwd = np.arange(lc)

    def powers(t, ks):
        ks = [int(v) for v in ks]
        parts, i = [], 0
        while i < len(ks):
            n = 1
            step = ks[i + 1] - ks[i] if i + 1 < len(ks) else 0
            assert abs(step) <= 1
            while i + n < len(ks) and ks[i + n] - ks[i + n - 1] == step:
                n += 1
            lo, hi = min(ks[i], ks[i + n - 1]), max(ks[i], ks[i + n - 1])
            run = t[lo:hi + 1]
            if step == 0:
                run = jnp.broadcast_to(run, (n,) + t.shape[1:])
            elif step < 0:
                run = run[::-1]
            parts.append(run)
            i += n
        return parts[0] if len(parts) == 1 else jnp.concatenate(parts, axis=0)

    def per_step(t, k_fwd, k_bwd):
        t = jnp.concatenate([powers(t, k_fwd)[..., :half], powers(t, k_bwd)[..., half:]], axis=-1)
        return t.transpose(1, 0, 2)[:, :, None, :]

    bx = bb4 * per_step(ar, rev, fwd) + bb4s * per_step(ai, rev, fwd)
    bx = bx.reshape(SSM_PAIRS, SSM_PW, SSM_PW)
    ct_re, ct_im = c_re.transpose(0, 1, 3, 2), c_im.transpose(0, 1, 3, 2)
    c4 = embed(ct_re, -ct_im)[:, None]
    c4s = embed(-ct_im, -ct_re)[:, None]
    cyt = c4 * per_step(ar, fwd + 1, lc - fwd) + c4s * per_step(ai, fwd + 1, lc - fwd)
    cyt = cyt.reshape(SSM_PAIRS, SSM_PW, SSM_PW)

    kt = c4 * per_step(ar, fwd, rev) + c4s * per_step(ai, fwd, rev)
    kt = kt.reshape(SSM_PAIRS, SSM_PW, SSM_PW)
    nt = (((2,), (2,)), ((0,), (0,)))
    k_fwd = lax.dot_general(bb4[:, 0, :, :half], kt[:, :, :half], nt, precision=lax.Precision.HIGH)
    k_bwd = lax.dot_general(bb4[:, 0, :, half:], kt[:, :, half:], nt, precision=lax.Precision.HIGH)
    edge = (lc - 1) * 2 * m
    strip = jnp.concatenate([k_bwd[:, :, :edge], k_bwd[:, :, edge:] + k_fwd[:, :, :2 * m], k_fwd[:, :, 2 * m:]], axis=-1)
    tz = jnp.stack([strip[:, :, (lc - 1 - s) * 2 * m:(lc - 1 - s) * 2 * m + SSM_PW] for s in range(lc)], axis=1)
    tz = tz.reshape(SSM_PAIRS, SSM_PW, SSM_PW)

    sw = 2 * p
    a16p = jnp.concatenate([ar[lc][:, :sw], ai[lc][:, sw:half], ar[lc][:, half:half + sw], ai[lc][:, half + sw:]],
                           axis=-1)[:, None, :]
    return tz.astype(bf16), bx.astype(bf16), cyt.astype(bf16), a16p


def _ssm_kernel(u_ref, tz_ref, bx_ref, cyt_ref, a_ref, h0_ref, y_ref, fin_ref, x_scr, s_scr, y_scr, *, n_seq, n_chunks):
    r = n_seq * n_chunks
    pc = 2 * SSM_GROUP
    sw = 2 * SSM_STATE
    xs = [u_ref[pl.ds(s, r, stride=SSM_CHUNK), :] for s in range(SSM_CHUNK)]
    for kq in range(SSM_QUAD):
        u = jnp.concatenate([x[:, kq * pc:(kq + 1) * pc] for x in xs], axis=-1).astype(bf16)
        x = jnp.dot(u, bx_ref[kq], preferred_element_type=f32)
        for i in range(4):
            x_scr[4 * kq + i] = x[:, i * sw:(i + 1) * sw]
        y_scr[:, kq * SSM_PW:(kq + 1) * SSM_PW] = jnp.dot(u, tz_ref[kq], preferred_element_type=f32)

    def scan(direction):
        offs = [kq * SSM_PW + 2 * direction * sw for kq in range(SSM_QUAD)]
        slabs = [4 * kq + 2 * direction for kq in range(SSM_QUAD)]
        coef = [(a_ref[kq][:, 2 * direction * sw:2 * direction * sw + sw],
                 a_ref[kq][:, 2 * direction * sw + sw:2 * direction * sw + 2 * sw]) for kq in range(SSM_QUAD)]

        def body(i, carry):
            c = i if direction == 0 else n_chunks - 1 - i
            rows = pl.ds(c, n_seq, stride=n_chunks)
            out = []
            for kq in range(SSM_QUAD):
                sr, si = carry[2 * kq], carry[2 * kq + 1]
                ar, ai = coef[kq]
                re, im = slabs[kq], slabs[kq] + 1
                s_scr[re, rows, :] = sr
                s_scr[im, rows, :] = si
                xr = x_scr[re, rows, :]
                xi = x_scr[im, rows, :]
                out += [ar * sr - ai * si + xr, ar * si + ai * sr + xi]
            return tuple(out)

        init = []
        for lo in offs:
            init += [h0_ref[:, lo:lo + sw], h0_ref[:, lo + sw:lo + 2 * sw]]
        fin = lax.fori_loop(0, n_chunks, body, tuple(init), unroll=min(n_chunks, SSM_SCAN_UNROLL))
        for kq, lo in enumerate(offs):
            fin_ref[:, lo:lo + sw] = fin[2 * kq]
            fin_ref[:, lo + sw:lo + 2 * sw] = fin[2 * kq + 1]

    scan(0)
    scan(1)
    ys = []
    for kq in range(SSM_QUAD):
        cols = slice(kq * SSM_PW, (kq + 1) * SSM_PW)
        s_in = jnp.concatenate([s_scr[4 * kq + i] for i in range(4)], axis=-1).astype(bf16)
        ys.append(y_scr[:, cols] + lax.dot_general(s_in, cyt_ref[kq], (((1,), (1,)), ((), ())),
                                                   preferred_element_type=f32))
    for t in range(SSM_CHUNK):
        y_ref[pl.ds(t, r, stride=SSM_CHUNK), :] = jnp.concatenate([y[:, t * pc:(t + 1) * pc] for y in ys], axis=-1)


def ssm_scan(z, row0, n_seq, seq, tables, layer, h0):
    tz, bx, cyt, a16p = tables
    n_chunks = seq // SSM_CHUNK
    rows = n_seq * seq
    r = n_seq * n_chunks
    lanes = SSM_QUAD * 2 * SSM_GROUP
    u_blk = (3 * ATTN_W) // lanes
    qw = SSM_QUAD * SSM_PW
    table = pl.BlockSpec((None, SSM_QUAD, SSM_PW, SSM_PW), lambda q: (layer, q, 0, 0))
    return pl.pallas_call(
        functools.partial(_ssm_kernel, n_seq=n_seq, n_chunks=n_chunks),
        grid=(SSM_PAIRS // SSM_QUAD,),
        in_specs=[pl.BlockSpec((rows, lanes), lambda q: (row0 // rows, u_blk + q)),
                  table, table, table,
                  pl.BlockSpec((None, SSM_QUAD, 1, SSM_PW), lambda q: (layer, q, 0, 0)),
                  pl.BlockSpec((n_seq, qw), lambda q: (0, q))],
        out_specs=[pl.BlockSpec((rows, lanes), lambda q: (0, q)),
                   pl.BlockSpec((n_seq, qw), lambda q: (0, q))],
        out_shape=[jax.ShapeDtypeStruct((rows, SSM_W), f32),
                   jax.ShapeDtypeStruct((n_seq, SSM_PAIRS * SSM_PW), f32)],
        scratch_shapes=[pltpu.VMEM((4 * SSM_QUAD, r, 2 * SSM_STATE), f32),
                        pltpu.VMEM((4 * SSM_QUAD, r, 2 * SSM_STATE), f32), pltpu.VMEM((r, qw), f32)],
        compiler_params=_cparams(("parallel",)),
        name="ssm_scan",
    )(z, tz, bx, cyt, a16p, h0)


def state_to_slabs(st):
    batch = st.shape[0]
    t = st.reshape(batch, 2, SSM_PAIRS, 2, SSM_STATE, 2).transpose(0, 2, 1, 5, 3, 4)
    return t.reshape(batch, SSM_PAIRS * SSM_PW)


def state_from_slabs(fin):
    batch = fin.shape[0]
    t = fin.reshape(batch, SSM_PAIRS, 2, 2, 2, SSM_STATE)
    return t.transpose(0, 2, 1, 4, 5, 3).reshape(batch, 2, SSM_GROUPS, SSM_STATE, 2)


def _glu_kernel(ya_ref, yb_ref, u_ref, d_ref, w_ref, b_ref, o_ref, *, tiles_a):
    y = _pick(pl.program_id(0) < tiles_a, ya_ref, yb_ref) + d_ref[...] * u_ref[...]
    z = jnp.dot(jax.nn.gelu(y).astype(bf16), w_ref[...], preferred_element_type=f32) + b_ref[...]
    o_ref[...] = (z[:, :SSM_W] * jax.nn.sigmoid(z[:, SSM_W:])).astype(bf16)


def ssm_glu(y_a, y_b, z, d_skip, w_glu, b_glu, layer, tm):
    tiles_a, tiles_b = y_a.shape[0] // tm, y_b.shape[0] // tm
    m_total = y_a.shape[0] + y_b.shape[0]
    u_blk = (3 * ATTN_W) // SSM_W
    return pl.pallas_call(
        functools.partial(_glu_kernel, tiles_a=tiles_a),
        grid=(tiles_a + tiles_b,),
        in_specs=[
            *_two_part(tm, SSM_W, tiles_a, tiles_b),
            pl.BlockSpec((tm, SSM_W), lambda i: (i, u_blk)),
            pl.BlockSpec((None, 1, SSM_W), lambda i: (layer, 0, 0)),
            pl.BlockSpec((None, SSM_W, 2 * SSM_W), lambda i: (layer, 0, 0)),
            pl.BlockSpec((None, 1, 2 * SSM_W), lambda i: (layer, 0, 0)),
        ],
        out_specs=pl.BlockSpec((tm, SSM_W), lambda i: (i, 0)),
        out_shape=jax.ShapeDtypeStruct((m_total, SSM_W), bf16),
        compiler_params=_cparams(("arbitrary",)),
        name="ssm_glu",
    )(y_a, y_b, z, d_skip, w_glu, b_glu)


def _pool_kernel(p_ref, w_ref, sc_ref, o_ref, pad_ref, *, seq):
    zeros = jnp.zeros((POOL_PAD, POOL_GROUP), f32)
    body = slice(POOL_PAD, POOL_PAD + seq)
    ext = slice(POOL_PAD, POOL_PAD + seq + POOL_TAIL)
    t = lax.broadcasted_iota(jnp.int32, (seq, 1), 0)
    for g, win in enumerate(POOL_WINDOWS):
        cols = slice(g * POOL_GROUP, (g + 1) * POOL_GROUP)
        pad_ref[0:POOL_PAD, cols] = zeros
        pad_ref[POOL_PAD + seq:, cols] = zeros
        x = p_ref[:, cols]
        pad_ref[body, cols] = x
        w = 1
        while w < win:
            pad_ref[ext, cols] = pad_ref[ext, cols] + pad_ref[POOL_PAD - w:POOL_PAD - w + seq + POOL_TAIL, cols]
            w *= 2
        ahead = win // 2 - 1
        total = pad_ref[POOL_PAD + ahead:POOL_PAD + ahead + seq, cols]
        lo = jnp.clip(t - win // 2, 0, seq)
        hi = jnp.clip(t - win // 2 + win, 0, seq)
        mixed = total / (hi - lo).astype(f32) - x
        out = jnp.dot(mixed.astype(bf16), w_ref[g].astype(bf16), preferred_element_type=f32)
        o_ref[:, cols] = (out * sc_ref[:, cols]).astype(bf16)


def pool_mixer(z, row0, batch, seq, w_pool, pool_scale, layer):
    p_blk = (3 * ATTN_W + SSM_W) // POOL_W
    blk0 = row0 // seq
    return pl.pallas_call(
        functools.partial(_pool_kernel, seq=seq),
        grid=(batch,),
        in_specs=[
            pl.BlockSpec((seq, POOL_W), lambda b: (blk0 + b, p_blk)),
            pl.BlockSpec((None, len(POOL_WINDOWS), POOL_GROUP, POOL_GROUP), lambda b: (layer, 0, 0, 0)),
            pl.BlockSpec((None, 1, POOL_W), lambda b: (layer, 0, 0)),
        ],
        out_specs=pl.BlockSpec((seq, POOL_W), lambda b: (b, 0)),
        out_shape=jax.ShapeDtypeStruct((batch * seq, POOL_W), bf16),
        scratch_shapes=[pltpu.VMEM((seq + 2 * POOL_PAD, POOL_W), f32)],
        compiler_params=_cparams(("parallel",)),
        name="pool_mixer",
    )(z, w_pool, pool_scale)


def _out_kernel(aa_ref, ab_ref, s_ref, pa_ref, pb_ref, *refs, n_x, tiles_a, tn):
    x_refs = refs[:n_x]
    mod_ref, w_ref, o_ref = refs[n_x:]
    first = pl.program_id(0) < tiles_a
    a = _pick(first, aa_ref, ab_ref)
    s = s_ref[...]
    p = _pick(first, pa_ref, pb_ref)
    x = _rows(x_refs, tiles_a)
    gate = mod_ref[0][2:3]
    for c in range(w_ref.shape[1] // tn):
        cols = slice(c * tn, (c + 1) * tn)
        acc = jnp.dot(a, w_ref[0:ATTN_W, cols], preferred_element_type=f32)
        acc = acc + jnp.dot(s, w_ref[ATTN_W:ATTN_W + SSM_W, cols], preferred_element_type=f32)
        acc = acc + jnp.dot(p, w_ref[ATTN_W + SSM_W:, cols], preferred_element_type=f32)
        o_ref[:, cols] = x[:, cols] + gate[:, cols] * acc


def out_projection(a_a, a_b, s, p_a, p_b, xs, mod, w, layer, row_of_tile, tm, tn=512):
    m_total = sum(x.shape[0] for x in xs)
    d = xs[0].shape[1]
    tiles_a, tiles_b = a_a.shape[0] // tm, a_b.shape[0] // tm
    assert len(xs) == 1 or xs[0].shape[0] == a_a.shape[0]
    return pl.pallas_call(
        functools.partial(_out_kernel, n_x=len(xs), tiles_a=tiles_a, tn=tn),
        grid=(m_total // tm,),
        in_specs=[
            *_two_part(tm, ATTN_W, tiles_a, tiles_b),
            pl.BlockSpec((tm, SSM_W), lambda i: (i, 0)),
            *_two_part(tm, POOL_W, tiles_a, tiles_b),
            *_row_specs(xs, tm, d),
            pl.BlockSpec((None, 1, 6, d), lambda i: (layer, row_of_tile(i), 0, 0)),
            pl.BlockSpec((None, d, d), lambda i: (layer, 0, 0), pipeline_mode=pl.Buffered(1)),
        ],
        out_specs=pl.BlockSpec((tm, d), lambda i: (i, 0)),
        out_shape=jax.ShapeDtypeStruct((m_total, d), f32),
        compiler_params=_cparams(("arbitrary",)),
        name="out_projection",
    )(a_a, a_b, s, p_a, p_b, *xs, mod, w)


def _ffn_kernel(x_ref, mod_ref, g_ref, wg_ref, wu_ref, wo_ref, fg_ref, *refs, n_out, tiles_a, final_norm):
    o_refs = refs[:n_out]
    h_ref, acc_ref = refs[n_out:]
    i, j = pl.program_id(0), pl.program_id(1)
    last = j == pl.num_programs(1) - 1

    def chunk():
        h = h_ref[...]
        half = wg_ref.shape[1] // 2
        acts = []
        for c in range(2):
            cs = slice(c * half, (c + 1) * half)
            gate = jnp.dot(h, wg_ref[:, cs], preferred_element_type=f32)
            up = jnp.dot(h, wu_ref[:, cs], preferred_element_type=f32)
            acts.append((gate * jax.nn.sigmoid(gate) * up).astype(bf16))
        down = (jnp.dot(acts[0], wo_ref[:half, :], preferred_element_type=f32)
                + jnp.dot(acts[1], wo_ref[half:, :], preferred_element_type=f32))
        return mod_ref[0][5:6] * down

    @pl.when(j == 0)
    def _():
        m = mod_ref[0]
        x = x_ref[...]
        h_ref[...] = (_rms(x, g_ref[...]) * (1.0 + m[4:5]) + m[3:4]).astype(bf16)
        acc_ref[...] = x + chunk()

    @pl.when(j > 0)
    def _():
        acc_ref[...] += chunk()

    def result():
        return _rms(acc_ref[...], fg_ref[...]) if final_norm else acc_ref[...]

    if n_out == 1:
        @pl.when(last)
        def _():
            o_refs[0][...] = result()
    else:
        @pl.when(last & (i < tiles_a))
        def _():
            o_refs[0][...] = result()

        @pl.when(last & (i >= tiles_a))
        def _():
            o_refs[1][...] = result()


def ffn(x, mod, g, w_in, w_out, final_g, layer, row_of_tile, tm, th, final_norm, split_rows=None):
    m_total, d = x.shape
    hidden = w_out.shape[1]
    nh = hidden // th
    if split_rows is None:
        tiles_a = m_total // tm
        out_specs = [pl.BlockSpec((tm, d), lambda i, j: (i, 0))]
        out_shape = [jax.ShapeDtypeStruct((m_total, d), f32)]
    else:
        tiles_a = split_rows // tm
        out_specs = list(_two_part(tm, d, tiles_a, m_total // tm - tiles_a))
        out_shape = [jax.ShapeDtypeStruct((split_rows, d), f32), jax.ShapeDtypeStruct((m_total - split_rows, d), f32)]
    return pl.pallas_call(
        functools.partial(_ffn_kernel, n_out=len(out_specs), tiles_a=tiles_a, final_norm=final_norm),
        grid=(m_total // tm, nh),
        in_specs=[
            pl.BlockSpec((tm, d), lambda i, j: (i, 0)),
            pl.BlockSpec((None, 1, 6, d), lambda i, j: (layer, row_of_tile(i), 0, 0)),
            pl.BlockSpec((None, 1, d), lambda i, j: (layer, 0, 0)),
            pl.BlockSpec((None, d, th), lambda i, j: (layer, 0, j)),
            pl.BlockSpec((None, d, th), lambda i, j: (layer, 0, nh + j)),
            pl.BlockSpec((None, th, d), lambda i, j: (layer, j, 0)),
            pl.BlockSpec((1, d), lambda i, j: (0, 0)),
        ],
        out_specs=out_specs,
        out_shape=out_shape,
        scratch_shapes=[pltpu.VMEM((tm, d), bf16), pltpu.VMEM((tm, d), f32)],
        compiler_params=_cparams(("arbitrary", "arbitrary")),
        name="ffn",
    )(x, mod, g, w_in, w_in, w_out, final_g)


def kernel(x_prompt, x_sample, c, cache_k, cache_v, state_ssm, c_ctx, w_mod, b_mod, norm1_g, norm2_g, w_in,
           attn_rpb, ssm_a_re, ssm_a_im, ssm_log_dt, ssm_b_re, ssm_b_im, ssm_c_re, ssm_c_im, ssm_d, ssm_w_glu,
           ssm_b_glu, pool_w, pool_scale, w_out, ffn_w_in, ffn_w_out, final_norm_g):
    batch, seq, d = x_prompt.shape
    dec_batch, dec_seq, _ = x_sample.shape
    depth = w_in.shape[0]
    m_ctx = batch * seq
    tm = 512
    assert m_ctx % tm == 0 and dec_seq % tm == 0 and dec_batch + 1 <= 8

    def row_of_tile(i):
        return jnp.where(i < m_ctx // tm, 0, 1 + (i - m_ctx // tm) // (dec_seq // tm))

    cond = jnp.concatenate([c_ctx[None, :], c, jnp.zeros((8 - 1 - dec_batch, d), f32)], axis=0)
    mod = modulation_all(cond, w_mod, b_mod)
    mod = mod[:, :1 + dec_batch].reshape(depth, 1 + dec_batch, 6, d)

    w_in_b, w_out_b = w_in.astype(bf16), w_out.astype(bf16)
    ffn_w_in_b, ffn_w_out_b, w_glu_b = ffn_w_in.astype(bf16), ffn_w_out.astype(bf16), ssm_w_glu.astype(bf16)
    row = lambda t: t[:, None, :]
    tables = jax.vmap(ssm_tables)(ssm_a_re, ssm_a_im, ssm_log_dt, ssm_b_re, ssm_b_im, ssm_c_re, ssm_c_im)
    bias_tables = jax.vmap(latent_bias_table)(attn_rpb)

    xs = (x_prompt.reshape(m_ctx, d), x_sample.reshape(dec_batch * dec_seq, d))
    h0_ctx = jnp.zeros((batch, SSM_PAIRS * SSM_PW), f32)

    zs, st_out = [], []
    for l in range(depth):
        last = l == depth - 1
        z = in_projection(xs, mod, row(norm1_g), w_in_b, l, row_of_tile, tm)

        a_ctx, *caches = context_attention(z, batch, seq, zs, write_caches=last)
        zs.append(z)
        a_lat = latent_attention(z, m_ctx, dec_batch, dec_seq, cache_k, cache_v, l, bias_tables)

        y_ctx, fin = ssm_scan(z, 0, batch, seq, tables, l, h0_ctx)
        y_lat, _ = ssm_scan(z, m_ctx, dec_batch, dec_seq, tables, l, state_to_slabs(state_ssm[:, l]))
        st_out.append(state_from_slabs(fin))
        s_out = ssm_glu(y_ctx, y_lat, z, row(ssm_d), w_glu_b, row(ssm_b_glu), l, tm)

        p_ctx = pool_mixer(z, 0, batch, seq, pool_w, row(pool_scale), l)
        p_lat = pool_mixer(z, m_ctx, dec_batch, dec_seq, pool_w, row(pool_scale), l)

        x = out_projection(a_ctx, a_lat, s_out, p_ctx, p_lat, xs, mod, w_out_b, l, row_of_tile, tm)
        xs = ffn(x, mod, row(norm2_g), ffn_w_in_b, ffn_w_out_b, final_norm_g[None], l, row_of_tile, tm, 512,
                 final_norm=last, split_rows=m_ctx if last else None)

    y_prompt = xs[0].reshape(batch, seq, d)
    y_sample = xs[1].reshape(dec_batch, dec_seq, d)
    return (y_prompt, y_sample, caches[0], caches[1], jnp.stack(st_out, axis=1))
```
